```python
import functools
import jax, jax.numpy as jnp
from jax import lax
import numpy as np

D_MODEL = 1024
BATCH = 2
SEQ = 8192
DEPTH = 1

GRID_W = 64
CTX_LEN = 256
EPS = 1e-6

GLA_HEADS = 4
GLA_DK = D_MODEL // 2 // GLA_HEADS
GLA_DV = D_MODEL // GLA_HEADS
GLA_QK = GLA_HEADS * GLA_DK
GLA_V = GLA_HEADS * GLA_DV
GLA_RANK = 16
GLA_TAU = 16.0
GLA_CHUNK = 64

SSM_INNER = 2 * D_MODEL
SSM_HEADDIM = 64
SSM_HEADS = SSM_INNER // SSM_HEADDIM
SSM_GROUPS = 4
SSM_HPG = SSM_HEADS // SSM_GROUPS
SSM_STATE = 128
SSM_BC = SSM_GROUPS * SSM_STATE
SSM_CONV = 4
CONV_LEFT = 2
SSM_CONV_DIM = SSM_INNER + 2 * SSM_BC
SSM_CHUNK = 128

D_FF = ((8 * D_MODEL // 3 + 255) // 256) * 256

IN_WIDTHS = (GLA_QK, GLA_QK, GLA_V, GLA_V, GLA_RANK, GLA_RANK,
             SSM_INNER, SSM_INNER, SSM_BC, SSM_BC, SSM_HEADS, SSM_HEADS,
             D_MODEL, D_MODEL)
D_IN = sum(IN_WIDTHS)

kernel_name = "hybrid_gla_ssd_prefix_dit"


def rms_norm(x, w):
    xf = x.astype(jnp.float32)
    y = xf * lax.rsqrt(jnp.mean(xf * xf, axis=-1, keepdims=True) + EPS)
    return (y * w.astype(jnp.float32)).astype(x.dtype)


def modulate(h, shift, scale):
    return h * (1 + scale) + shift


def _split_in(proj):
    idx = np.cumsum(IN_WIDTHS)[:-1].tolist()
    return jnp.split(proj, idx, axis=-1)


def _chunks(t, size):
    Bn, L = t.shape[:2]
    return jnp.moveaxis(t.reshape(Bn, L // size, size, *t.shape[2:]), 1, 0)


def _unchunks(t):
    n, Bn, C = t.shape[:3]
    return jnp.moveaxis(t, 0, 1).reshape(Bn, n * C, *t.shape[3:])


def _dwconv_centred(u, w, b):
    W = u.shape[1]
    up = jnp.pad(u, ((0, 0), (CONV_LEFT, SSM_CONV - 1 - CONV_LEFT), (0, 0)))
    out = b + up[:, 0:W] * w[0]
    for j in range(1, SSM_CONV):
        out = out + up[:, j:j + W] * w[j]
    return out


def _seq_conv(u, w, b):
    return _dwconv_centred(u, w, b)


def _row_conv(u, w, b, rows):
    Bn, L, Cc = u.shape
    return _dwconv_centred(u.reshape(Bn * rows, GRID_W, Cc), w, b).reshape(Bn, L, Cc)


def _gla_scan(q, k, v, log_g, s0):
    out_dtype = v.dtype
    f32 = jnp.float32
    qc, kc, vc, gc = (_chunks(t.astype(f32), GLA_CHUNK) for t in (q, k, v, log_g))
    mask = jnp.tril(jnp.ones((GLA_CHUNK, GLA_CHUNK), bool))[None, :, :, None, None]

    def step(S, inp):
        qi, ki, vi, gi = inp
        b = jnp.cumsum(gi, axis=1)
        b_last = b[:, -1]
        diff = b[:, :, None] - b[:, None, :]
        decay = jnp.exp(jnp.where(mask, diff, -jnp.inf))
        att = jnp.einsum('bthk,bshk,btshk->bhts', qi, ki, decay)
        o = jnp.einsum('bhts,bshv->bthv', att, vi) + jnp.einsum('bthk,bhkv->bthv', qi * jnp.exp(b), S)
        S_new = jnp.exp(b_last)[..., None] * S + jnp.einsum(
            'bshk,bshv->bhkv', ki * jnp.exp(b_last[:, None] - b), vi)
        return S_new, o

    S_fin, o = lax.scan(step, s0, (qc, kc, vc, gc))
    return _unchunks(o).astype(out_dtype), S_fin


def _gla_state(k, v, log_g):
    b = jnp.cumsum(log_g.astype(jnp.float32), axis=1)
    return jnp.einsum('blhk,blhv->bhkv', k.astype(jnp.float32) * jnp.exp(b[:, -1:] - b), v.astype(jnp.float32))


def _ssd_scan(x, bm, cm, a, s0):
    out_dtype = bm.dtype
    f32 = jnp.float32
    xc, bc, cc, ac = (_chunks(t.astype(f32), SSM_CHUNK) for t in (x, bm, cm, a))
    mask = jnp.tril(jnp.ones((SSM_CHUNK, SSM_CHUNK), bool))[None, :, :, None, None]

    def step(S, inp):
        xi, bi, ci, ai = inp
        cum = jnp.cumsum(ai, axis=1)
        seg = cum[:, :, None] - cum[:, None, :]
        Lm = jnp.exp(jnp.where(mask, seg, -jnp.inf))
        cb = jnp.einsum('btgn,bsgn->btsg', ci, bi)
        y = jnp.einsum('btsg,btsge,bsgep->btgep', cb, Lm, xi)
        y = y + jnp.einsum('btgn,bgenp->btgep', ci, S) * jnp.exp(cum)[..., None]
        S_new = jnp.exp(cum[:, -1])[..., None, None] * S + jnp.einsum(
            'bsgn,bsge,bsgep->bgenp', bi, jnp.exp(cum[:, -1:] - cum), xi)
        return S_new, y

    S_fin, y = lax.scan(step, s0, (xc, bc, cc, ac))
    return _unchunks(y).astype(out_dtype), S_fin


def _ssd_state(x, bm, a):
    cum = jnp.cumsum(a.astype(jnp.float32), axis=1)
    return jnp.einsum('blgn,blge,blgep->bgenp', bm.astype(jnp.float32),
                      jnp.exp(cum[:, -1:] - cum), x.astype(jnp.float32))


def _flip(t):
    return jnp.flip(t, axis=1)


def _bidir(scan_fn, fwd_in, bwd_in, s_f, s_b):
    o_f, st_f = scan_fn(*fwd_in, s_f)
    o_b, st_b = scan_fn(*[_flip(t) for t in bwd_in], s_b)
    return o_f + _flip(o_b), st_f, st_b


def _gla_features(q, k, v, lr_f, lr_b, lp):
    Bn, L, _ = q.shape
    heads = lambda t, d: t.reshape(Bn, L, GLA_HEADS, d)
    q = heads(q, GLA_DK) * (GLA_DK ** -0.5)
    k = heads(k, GLA_DK)
    v = heads(v, GLA_DV)

    def log_gate(lr, up, bias):
        pre = (lr @ up + bias).astype(jnp.float32)
        return heads(jax.nn.log_sigmoid(pre) / GLA_TAU, GLA_DK)

    return (q, k, v, log_gate(lr_f, lp['gla_up_f'], lp['gla_bias_f']),
            log_gate(lr_b, lp['gla_up_b'], lp['gla_bias_b']))


def _ssm_features(xs, bm, cm, dt_f, dt_b, lp, conv_fn):
    xbc = jax.nn.silu(conv_fn(jnp.concatenate([xs, bm, cm], axis=-1), lp['conv_w'], lp['conv_b']))
    xs, bm, cm = jnp.split(xbc, [SSM_INNER, SSM_INNER + SSM_BC], axis=-1)
    Bn, L, _ = xs.shape
    xh = xs.reshape(Bn, L, SSM_GROUPS, SSM_HPG, SSM_HEADDIM)
    bm = bm.reshape(Bn, L, SSM_GROUPS, SSM_STATE)
    cm = cm.reshape(Bn, L, SSM_GROUPS, SSM_STATE)

    def direction(dt_raw, dt_bias, a_log):
        dt = jax.nn.softplus((dt_raw + dt_bias).astype(jnp.float32)).reshape(Bn, L, SSM_GROUPS, SSM_HPG)
        a = -jnp.exp(a_log.astype(jnp.float32)).reshape(SSM_GROUPS, SSM_HPG) * dt
        return xh * dt[..., None], a

    return (xh, bm, cm, direction(dt_f, lp['dt_bias_f'], lp['a_log_f']),
            direction(dt_b, lp['dt_bias_b'], lp['a_log_b']))


def _mixer(parts, lp, init, conv_fn):
    q, k, v, r, lr_f, lr_b, z, xs, bm, cm, dt_f, dt_b, gate_a, gate_b = parts
    Bn, L, _ = q.shape
    gq, gk, gv, lg_f, lg_b = _gla_features(q, k, v, lr_f, lr_b, lp)
    o_a, st_af, st_ab = _bidir(_gla_scan, (gq, gk, gv, lg_f), (gq, gk, gv, lg_b), init[0], init[1])
    o_a = rms_norm(o_a, lp['gla_norm_w']) * jax.nn.silu(r).reshape(Bn, L, GLA_HEADS, GLA_DV)
    y_a = o_a.reshape(Bn, L, GLA_V) @ lp['w_pa']
    xh, sb, sc, (x_f, a_f), (x_b, a_b) = _ssm_features(xs, bm, cm, dt_f, dt_b, lp, conv_fn)
    o_b, st_bf, st_bb = _bidir(_ssd_scan, (x_f, sb, sc, a_f), (x_b, sb, sc, a_b), init[2], init[3])
    o_b = o_b + lp['d_skip'].reshape(SSM_GROUPS, SSM_HPG, 1) * xh
    o_b = o_b.reshape(Bn, L, SSM_INNER) * jax.nn.silu(z)
    o_b = rms_norm(o_b.reshape(Bn, L, SSM_GROUPS, SSM_INNER // SSM_GROUPS),
                   lp['ssm_norm_w'].reshape(SSM_GROUPS, SSM_INNER // SSM_GROUPS)).reshape(Bn, L, SSM_INNER)
    y_b = o_b @ lp['w_pb']
    merged = jax.nn.sigmoid(gate_a) * y_a + jax.nn.sigmoid(gate_b) * y_b
    return merged @ lp['w_out'], (st_af, st_ab, st_bf, st_bb)


def _context_states(parts, lp):
    k, v, lr_f, lr_b = parts[1], parts[2], parts[4], parts[5]
    xs, bm, cm, dt_f, dt_b = parts[7], parts[8], parts[9], parts[10], parts[11]
    _, gk, gv, lg_f, lg_b = _gla_features(parts[0], k, v, lr_f, lr_b, lp)
    _, sb, _, (x_f, a_f), (x_b, a_b) = _ssm_features(xs, bm, cm, dt_f, dt_b, lp, _seq_conv)
    return (_gla_state(gk, gv, lg_f), _gla_state(_flip(gk), _flip(gv), _flip(lg_b)),
            _ssd_state(x_f, sb, a_f), _ssd_state(_flip(x_b), _flip(sb), _flip(a_b)))


def _zero_states(Bn):
    f32 = jnp.float32
    g = jnp.zeros((Bn, GLA_HEADS, GLA_DK, GLA_DV), f32)
    s = jnp.zeros((Bn, SSM_GROUPS, SSM_HPG, SSM_STATE, SSM_HEADDIM), f32)
    return (g, g, s, s)


def _swiglu(h, w_gate, w_up, w_down):
    return (jax.nn.silu(h @ w_gate) * (h @ w_up)) @ w_down


def setup_inputs(seed: int = 0) -> dict:
    key = jax.random.key(seed)
    ks = iter(jax.random.split(key, 40))
    nrm = lambda shape, s: jax.random.normal(next(ks), shape, jnp.float32) * s
    L_ = DEPTH
    dt = jnp.exp(jax.random.uniform(next(ks), (2, L_, SSM_HEADS), jnp.float32,
                                    np.log(1e-3), np.log(1e-1)))
    dt_bias = dt + jnp.log(-jnp.expm1(-dt))
    a_log = jnp.log(jax.random.uniform(next(ks), (2, L_, SSM_HEADS), jnp.float32, 1.0, 16.0))
    return {
        'x': nrm((BATCH, SEQ, D_MODEL), 1.0),
        'c': nrm((BATCH, D_MODEL), 1.0),
        'ctx': nrm((BATCH, CTX_LEN, D_MODEL), 1.0),
        'c_ctx': nrm((D_MODEL,), 1.0),
        'w_ada': nrm((L_, D_MODEL, 6 * D_MODEL), 0.02),
        'b_ada': nrm((L_, 6 * D_MODEL), 0.01),
        'norm1_w': 1.0 + nrm((L_, D_MODEL), 0.02),
        'w_in': nrm((L_, D_MODEL, D_IN), D_MODEL ** -0.5),
        'gla_up_f': nrm((L_, GLA_RANK, GLA_QK), GLA_RANK ** -0.5),
        'gla_bias_f': nrm((L_, GLA_QK), 0.1),
        'gla_up_b': nrm((L_, GLA_RANK, GLA_QK), GLA_RANK ** -0.5),
        'gla_bias_b': nrm((L_, GLA_QK), 0.1),
        'gla_norm_w': 1.0 + nrm((L_, GLA_DV), 0.02),
        'conv_w': nrm((L_, SSM_CONV, SSM_CONV_DIM), SSM_CONV ** -0.5),
        'conv_b': nrm((L_, SSM_CONV_DIM), 0.02),
        'dt_bias_f': dt_bias[0],
        'dt_bias_b': dt_bias[1],
        'a_log_f': a_log[0],
        'a_log_b': a_log[1],
        'd_skip': 1.0 + nrm((L_, SSM_HEADS), 0.02),
        'ssm_norm_w': 1.0 + nrm((L_, SSM_INNER), 0.02),
        'w_pa': nrm((L_, GLA_V, D_MODEL), GLA_V ** -0.5),
        'w_pb': nrm((L_, SSM_INNER, D_MODEL), SSM_INNER ** -0.5),
        'w_out': nrm((L_, D_MODEL, D_MODEL), D_MODEL ** -0.5),
        'norm2_w': 1.0 + nrm((L_, D_MODEL), 0.02),
        'w_gate': nrm((L_, D_MODEL, D_FF), D_MODEL ** -0.5),
        'w_up': nrm((L_, D_MODEL, D_FF), D_MODEL ** -0.5),
        'w_down': nrm((L_, D_FF, D_MODEL), D_FF ** -0.5),
        'final_norm_w': 1.0 + nrm((D_MODEL,), 0.02),
    }


def reference(x, c, ctx, c_ctx, w_ada, b_ada, norm1_w, w_in, gla_up_f, gla_bias_f, gla_up_b, gla_bias_b,
              gla_norm_w, conv_w, conv_b, dt_bias_f, dt_bias_b, a_log_f, a_log_b, d_skip, ssm_norm_w,
              w_pa, w_pb, w_out, norm2_w, w_gate, w_up, w_down, final_norm_w):
    rows = x.shape[1] // GRID_W
    latent_conv = functools.partial(_row_conv, rows=rows)
    h_lat, h_ctx = x, ctx
    for layer in range(DEPTH):
        lp = {
            'gla_up_f': gla_up_f[layer], 'gla_bias_f': gla_bias_f[layer],
            'gla_up_b': gla_up_b[layer], 'gla_bias_b': gla_bias_b[layer],
            'gla_norm_w': gla_norm_w[layer], 'conv_w': conv_w[layer], 'conv_b': conv_b[layer],
            'dt_bias_f': dt_bias_f[layer], 'dt_bias_b': dt_bias_b[layer],
            'a_log_f': a_log_f[layer], 'a_log_b': a_log_b[layer], 'd_skip': d_skip[layer],
            'ssm_norm_w': ssm_norm_w[layer], 'w_pa': w_pa[layer], 'w_pb': w_pb[layer], 'w_out': w_out[layer],
        }
        ada = jax.nn.silu(c)[:, None, :] @ w_ada[layer] + b_ada[layer]
        sh1, sc1, g1, sh2, sc2, g2 = jnp.split(ada, 6, axis=-1)
        ada_c = jax.nn.silu(c_ctx) @ w_ada[layer] + b_ada[layer]
        csh1, csc1, cg1, csh2, csc2, cg2 = jnp.split(ada_c, 6, axis=-1)

        parts_c = _split_in(modulate(rms_norm(h_ctx, norm1_w[layer]), csh1, csc1) @ w_in[layer])
        if layer == DEPTH - 1:
            states = _context_states(parts_c, lp)
        else:
            mix_c, states = _mixer(parts_c, lp, _zero_states(h_ctx.shape[0]), _seq_conv)
            h_ctx = h_ctx + cg1 * mix_c
            h_ctx = h_ctx + cg2 * _swiglu(modulate(rms_norm(h_ctx, norm2_w[layer]), csh2, csc2),
                                          w_gate[layer], w_up[layer], w_down[layer])

        parts = _split_in(modulate(rms_norm(h_lat, norm1_w[layer]), sh1, sc1) @ w_in[layer])
        mix, _ = _mixer(parts, lp, states, latent_conv)
        h_lat = h_lat + g1 * mix
        h_lat = h_lat + g2 * _swiglu(modulate(rms_norm(h_lat, norm2_w[layer]), sh2, sc2),
                                     w_gate[layer], w_up[layer], w_down[layer])
    return rms_norm(h_lat, final_norm_w)
```

```python
import functools

import numpy as np
import jax
import jax.numpy as jnp
from jax import lax
from jax.experimental import pallas as pl
from jax.experimental.pallas import tpu as pltpu

F32 = jnp.float32
BF16 = jnp.bfloat16

D_MODEL = 1024
GRID_W = 64
EPS = 1e-6

GLA_HEADS = 4
GLA_DK = 128
GLA_DV = 256
GLA_QK = GLA_HEADS * GLA_DK
GLA_V = GLA_HEADS * GLA_DV
GLA_RANK = 16
GLA_TAU = 16.0

SSM_INNER = 2 * D_MODEL
SSM_HEADDIM = 64
SSM_HEADS = SSM_INNER // SSM_HEADDIM
SSM_GROUPS = 4
SSM_HPG = SSM_HEADS // SSM_GROUPS
SSM_STATE = 128
SSM_BC = SSM_GROUPS * SSM_STATE
SSM_CONV = 4
CONV_LEFT = 2
SSM_GW = SSM_HPG * SSM_HEADDIM

D_FF = ((8 * D_MODEL // 3 + 255) // 256) * 256

_IN_WIDTHS = (GLA_QK, GLA_QK, GLA_V, GLA_V, GLA_RANK, GLA_RANK,
              SSM_INNER, SSM_INNER, SSM_BC, SSM_BC, SSM_HEADS, SSM_HEADS, D_MODEL, D_MODEL)
_IN_OFF = np.concatenate([[0], np.cumsum(_IN_WIDTHS)]).tolist()

P_Q, P_K, P_V, P_R = 0, 512, 1024, 2048
P_Z, P_XS, P_BM, P_CM = 3072, 5120, 7168, 7680
P_GA, P_GB = 8192, 9216
P_MAIN = 10240
S_LRF, S_LRB, S_DTF, S_DTB = 0, 16, 32, 64
P_SMALL = 128

GLA_CHUNK = 64
SSD_CHUNK = 128

VMEM_LIMIT = 56 * 1024 * 1024


def _sigmoid(x):
    return 1.0 / (1.0 + jnp.exp(-x))


def _silu(x):
    return x * _sigmoid(x)


def _softplus(x):
    return jnp.maximum(x, 0.0) + jnp.log1p(jnp.exp(-jnp.abs(x)))


def _split3(x):
    hi = x.astype(BF16)
    r1 = x - hi.astype(F32)
    mid = r1.astype(BF16)
    lo = (r1 - mid.astype(F32)).astype(BF16)
    return hi, mid, lo


def _dot(a, b):
    return jnp.dot(a, b, preferred_element_type=F32)


def _dot_nt(a, b):
    return lax.dot_general(a, b, (((1,), (1,)), ((), ())), preferred_element_type=F32)


def _dot_tn(a, b):
    return lax.dot_general(a, b, (((0,), (0,)), ((), ())), preferred_element_type=F32)


def _dot01(m01, x):
    hi, mid, lo = _split3(x)
    return _dot(m01, hi) + _dot(m01, mid) + _dot(m01, lo)


def _dot01r(x, m01):
    hi, mid, lo = _split3(x)
    return _dot(hi, m01) + _dot(mid, m01) + _dot(lo, m01)


def _params(sem):
    return pltpu.CompilerParams(dimension_semantics=sem, vmem_limit_bytes=VMEM_LIMIT)


def _const_spec(shape):
    n = len(shape)
    return pl.BlockSpec(shape, lambda *_: (0,) * n)


def _ada_kernel(c_ref, w_ref, b_ref, o_ref):
    s = _silu(c_ref[...])
    o_ref[...] = jnp.dot(s, w_ref[...], preferred_element_type=F32,
                         precision=lax.Precision.HIGHEST) + b_ref[...]


def _ada(cc, w, b):
    rows = cc.shape[0]
    n = w.shape[1]
    tn = 1536
    return pl.pallas_call(
        _ada_kernel,
        grid=(n // tn,),
        in_specs=[pl.BlockSpec((rows, D_MODEL), lambda j: (0, 0)),
                  pl.BlockSpec((D_MODEL, tn), lambda j: (0, j)),
                  pl.BlockSpec((1, tn), lambda j: (0, j))],
        out_specs=pl.BlockSpec((rows, tn), lambda j: (0, j)),
        out_shape=jax.ShapeDtypeStruct((rows, n), F32),
        compiler_params=_params(("arbitrary",)),
        name="ada",
    )(cc, w, b)


def _inproj_kernel(x_ref, mod_ref, nw_ref, wm_ref, ws_ref, om_ref, os_ref, xn_ref):
    @pl.when(pl.program_id(2) == 0)
    def _():
        x = x_ref[0]
        y = x * lax.rsqrt(jnp.mean(x * x, axis=-1, keepdims=True) + EPS) * nw_ref[...]
        y = y * (1.0 + mod_ref[0, 1:2, :]) + mod_ref[0, 0:1, :]
        xn = y.astype(BF16)
        xn_ref[...] = xn
        os_ref[0] = _dot(xn, ws_ref[...])

    om_ref[0] = _dot(xn_ref[...], wm_ref[...]).astype(BF16)


def _inproj(x, mod, nw, w_main, w_small, tm):
    B, L, _ = x.shape
    tn = 1024
    return pl.pallas_call(
        _inproj_kernel,
        grid=(B, L // tm, P_MAIN // tn),
        in_specs=[pl.BlockSpec((1, tm, D_MODEL), lambda b, i, j: (b, i, 0)),
                  pl.BlockSpec((1, 2, D_MODEL), lambda b, i, j: (b, 0, 0)),
                  pl.BlockSpec((1, D_MODEL), lambda b, i, j: (0, 0)),
                  pl.BlockSpec((D_MODEL, tn), lambda b, i, j: (0, j)),
                  pl.BlockSpec((D_MODEL, P_SMALL), lambda b, i, j: (0, 0))],
        out_specs=[pl.BlockSpec((1, tm, tn), lambda b, i, j: (b, i, j)),
                   pl.BlockSpec((1, tm, P_SMALL), lambda b, i, j: (b, i, 0))],
        out_shape=[jax.ShapeDtypeStruct((B, L, P_MAIN), BF16),
                   jax.ShapeDtypeStruct((B, L, P_SMALL), F32)],
        scratch_shapes=[pltpu.VMEM((tm, D_MODEL), BF16)],
        compiler_params=_params(("parallel", "parallel", "arbitrary")),
        name="inproj",
    )(x, mod, nw, w_main, w_small)


def _gla_consts(C, reverse):
    NL = int(np.log2(C))
    idx = np.arange(C)
    tri = (idx[None, :] <= idx[:, None]).astype(np.float32)
    mats = [tri]
    masks = [np.eye(C, dtype=np.float32)]
    for lev in range(NL):
        h = 1 << lev
        blk = idx // (2 * h)
        half = (idx // h) % 2
        ref = blk * 2 * h + h - 1
        d = tri - tri[ref]
        d[half == 0] *= -1.0
        mats.append(d)
        masks.append(((blk[:, None] == blk[None, :]) & (half[:, None] == 1)
                      & (half[None, :] == 0)).astype(np.float32))
    mats.append(1.0 - tri)
    if reverse:
        mats = [m[::-1, ::-1] for m in mats]
        masks = [m[::-1, ::-1] for m in masks]
    dmat = np.concatenate(mats, axis=0)
    return jnp.asarray(dmat, BF16), jnp.asarray(np.stack(masks), F32)


def _gla_kernel(q_ref, k_ref, v_ref, lr_ref, up_ref, bias_ref, dm_ref, mk_ref, s0_ref, *rest,
                C, TB, reverse, finalize, lr_off):
    if finalize:
        r_ref, op_ref, nw_ref, o_ref, sf_ref, st_ref = rest
    else:
        o_ref, sf_ref, st_ref = rest
    NL = int(np.log2(C))
    nchunk = TB // C
    blk = pl.program_id(2)

    @pl.when(blk == 0)
    def _():
        st_ref[...] = s0_ref[0, 0]

    def body(i, carry):
        ci = (nchunk - 1 - i) if reverse else i
        rows = pl.ds(pl.multiple_of(ci * C, C), C)
        q = q_ref[0, rows, :].astype(F32) * (GLA_DK ** -0.5)
        k = k_ref[0, rows, :].astype(F32)
        v = v_ref[0, rows, :]
        lr = lr_ref[0, rows, :][:, lr_off:lr_off + GLA_RANK]
        pre = jnp.dot(lr, up_ref[...], preferred_element_type=F32,
                      precision=lax.Precision.HIGHEST) + bias_ref[...]
        g = -_softplus(-pre) * (1.0 / GLA_TAU)
        e_all = jnp.exp(_dot01(dm_ref[...], g))
        e_b = e_all[0:C]
        e_last = e_all[(NL + 1) * C:(NL + 2) * C]
        att = mk_ref[0] * _dot_nt(q.astype(BF16), k.astype(BF16))
        for lev in range(NL):
            e_l = e_all[(lev + 1) * C:(lev + 2) * C]
            att = att + mk_ref[lev + 1] * _dot_nt((q * e_l).astype(BF16), (k * e_l).astype(BF16))
        st = st_ref[...]
        o = _dot(att.astype(BF16), v) + _dot_nt((q * e_b).astype(BF16), st.astype(BF16))
        dec = e_b[0:1, :] if reverse else e_b[C - 1:C, :]
        st_ref[...] = dec * st + _dot_tn(v, (k * e_last).astype(BF16))
        if finalize:
            o = o + op_ref[0, rows, :]
            y = o * lax.rsqrt(jnp.mean(o * o, axis=-1, keepdims=True) + EPS) * nw_ref[...]
            r = r_ref[0, rows, :].astype(F32)
            o_ref[0, rows, :] = (y * _silu(r)).astype(o_ref.dtype)
        else:
            o_ref[0, rows, :] = o.astype(o_ref.dtype)
        return carry

    lax.fori_loop(0, nchunk, body, 0)

    @pl.when(blk == pl.num_programs(2) - 1)
    def _():
        sf_ref[0, 0] = st_ref[...]


def _gla_pass(pm, ps, up, bias, s0, *, reverse, finalize, TB, o_prev=None, norm_w=None):
    B, L, _ = pm.shape
    C = GLA_CHUNK
    nb = L // TB
    dmat, masks = _gla_consts(C, reverse)
    tmap = (lambda i: nb - 1 - i) if reverse else (lambda i: i)
    qb, kb = P_Q // GLA_DK, P_K // GLA_DK
    vb, rb = P_V // GLA_DV, P_R // GLA_DV
    in_specs = [
        pl.BlockSpec((1, TB, GLA_DK), lambda b, h, i: (b, tmap(i), qb + h)),
        pl.BlockSpec((1, TB, GLA_DK), lambda b, h, i: (b, tmap(i), kb + h)),
        pl.BlockSpec((1, TB, GLA_DV), lambda b, h, i: (b, tmap(i), vb + h)),
        pl.BlockSpec((1, TB, P_SMALL), lambda b, h, i: (b, tmap(i), 0)),
        pl.BlockSpec((GLA_RANK, GLA_DK), lambda b, h, i: (0, h)),
        pl.BlockSpec((1, GLA_DK), lambda b, h, i: (0, h)),
        _const_spec(dmat.shape),
        _const_spec(masks.shape),
        pl.BlockSpec((1, 1, GLA_DV, GLA_DK), lambda b, h, i: (b, h, 0, 0)),
    ]
    args = [pm, pm, pm, ps, up, bias, dmat, masks, s0]
    if finalize:
        in_specs += [
            pl.BlockSpec((1, TB, GLA_DV), lambda b, h, i: (b, tmap(i), rb + h)),
            pl.BlockSpec((1, TB, GLA_DV), lambda b, h, i: (b, tmap(i), h)),
            _const_spec((1, GLA_DV)),
        ]
        args += [pm, o_prev, norm_w]
    kern = functools.partial(_gla_kernel, C=C, TB=TB, reverse=reverse, finalize=finalize,
                             lr_off=S_LRB if reverse else S_LRF)
    return pl.pallas_call(
        kern,
        grid=(B, GLA_HEADS, nb),
        in_specs=in_specs,
        out_specs=[pl.BlockSpec((1, TB, GLA_DV), lambda b, h, i: (b, tmap(i), h)),
                   pl.BlockSpec((1, 1, GLA_DV, GLA_DK), lambda b, h, i: (b, h, 0, 0))],
        out_shape=[jax.ShapeDtypeStruct((B, L, GLA_V), BF16 if finalize else F32),
                   jax.ShapeDtypeStruct((B, GLA_HEADS, GLA_DV, GLA_DK), F32)],
        scratch_shapes=[pltpu.VMEM((GLA_DV, GLA_DK), F32)],
        compiler_params=_params(("parallel", "parallel", "arbitrary")),
        name="gla_" + ("rev" if reverse else "fwd") + ("_fin" if finalize else ""),
    )(*args)


def _conv_silu(u, w, b, pos, rowlen):
    T = u.shape[0]
    out = b + u * w[CONV_LEFT:CONV_LEFT + 1, :]
    for j in range(SSM_CONV):
        off = j - CONV_LEFT
        if off == 0:
            continue
        shifted = pltpu.roll(u, (-off) % T, 0)
        ok = (pos + off >= 0) & (pos + off < rowlen)
        out = out + jnp.where(ok, shifted, 0.0) * w[j:j + 1, :]
    return _silu(out)


def _ssd_kernel(xs_ref, bm_ref, cm_ref, ps_ref, sel_ref, wx_ref, wb_ref, wc_ref, bx_ref, bb_ref, bc_ref,
                dtb_ref, alog_ref, tri_ref, s0_ref, *rest, C, TB, rowlen, reverse, finalize):
    if finalize:
        z_ref, yp_ref, dsk_ref, nw_ref, y_ref, sf_ref, st_ref, xc_s, bc_s, cc_s, ya_s, xw_s, dec_s = rest
    else:
        y_ref, sf_ref, st_ref, xc_s, bc_s, cc_s, ya_s, xw_s, dec_s = rest
    nchunk = TB // C
    blk = pl.program_id(2)

    @pl.when(blk == 0)
    def _():
        st_ref[...] = s0_ref[0, 0]

    pos = lax.broadcasted_iota(jnp.int32, (TB, 1), 0) % rowlen
    xc_s[...] = _conv_silu(xs_ref[0].astype(F32), wx_ref[...], bx_ref[...], pos, rowlen)
    bc_s[...] = _conv_silu(bm_ref[0].astype(F32), wb_ref[...], bb_ref[...], pos, rowlen)
    cc_s[...] = _conv_silu(cm_ref[0].astype(F32), wc_ref[...], bc_ref[...], pos, rowlen)
    neg_a = -jnp.exp(alog_ref[0])
    ti = lax.broadcasted_iota(jnp.int32, (C, C), 0)
    si = lax.broadcasted_iota(jnp.int32, (C, C), 1)
    keep = (si >= ti) if reverse else (si <= ti)
    last = 0 if reverse else C - 1

    def body(i, carry):
        ci = (nchunk - 1 - i) if reverse else i
        rows = pl.ds(pl.multiple_of(ci * C, C), C)
        raw = _dot01r(ps_ref[0, rows, :], sel_ref[0])
        dt = _softplus(raw + dtb_ref[0])
        a = dt * neg_a
        cum = _dot01(tri_ref[...], a)
        cum_t = jnp.transpose(cum)
        xc = xc_s[rows, :]
        bmc = bc_s[rows, :].astype(BF16)
        cmc = cc_s[rows, :].astype(BF16)
        cb = _dot_nt(cmc, bmc)
        st = st_ref[...]
        cs = _dot(cmc, st.astype(BF16))
        for e in range(SSM_HPG):
            lanes = slice(e * SSM_HEADDIM, (e + 1) * SSM_HEADDIM)
            ccol = cum[:, e:e + 1]
            seg = ccol - cum_t[e:e + 1, :]
            lm = jnp.where(keep, jnp.exp(jnp.minimum(seg, 0.0)), 0.0)
            xdt = xc[:, lanes] * dt[:, e:e + 1]
            y = _dot((cb * lm).astype(BF16), xdt.astype(BF16))
            y = y + cs[:, lanes] * jnp.exp(ccol)
            ya_s[rows, lanes] = y
            clast = cum[last:last + 1, e:e + 1]
            xw_s[:, lanes] = xdt * jnp.exp(clast - ccol)
            dec_s[:, lanes] = jnp.broadcast_to(jnp.exp(clast), (8, SSM_HEADDIM))
        st_ref[...] = dec_s[0:1, :] * st + _dot_tn(bmc, xw_s[...].astype(BF16))
        return carry

    lax.fori_loop(0, nchunk, body, 0)

    if finalize:
        y = ya_s[...] + yp_ref[0] + dsk_ref[...] * xc_s[...]
        y = y * _silu(z_ref[0].astype(F32))
        y = y * lax.rsqrt(jnp.mean(y * y, axis=-1, keepdims=True) + EPS) * nw_ref[...]
        y_ref[0] = y.astype(y_ref.dtype)
    else:
        y_ref[0] = ya_s[...]

    @pl.when(blk == pl.num_programs(2) - 1)
    def _():
        sf_ref[0, 0] = st_ref[...]


def _ssd_pass(pm, ps, conv_w, conv_b, dt_bias, a_log, s0, *, reverse, finalize, TB, rowlen,
              y_prev=None, d_skip_x=None, norm_w=None):
    B, L, _ = pm.shape
    C = SSD_CHUNK
    nb = L // TB
    G = SSM_GROUPS
    tmap = (lambda i: nb - 1 - i) if reverse else (lambda i: i)
    idx = np.arange(C)
    tri = (idx[None, :] >= idx[:, None]) if reverse else (idx[None, :] <= idx[:, None])
    tri = jnp.asarray(tri.astype(np.float32), BF16)
    sel = np.zeros((G, P_SMALL, 128), np.float32)
    off = S_DTB if reverse else S_DTF
    for g in range(G):
        sel[g, off + g * SSM_HPG + np.arange(SSM_HPG), np.arange(SSM_HPG)] = 1.0
    sel = jnp.asarray(sel, BF16)
    xsb, bmb, cmb, zb = P_XS // SSM_GW, P_BM // SSM_STATE, P_CM // SSM_STATE, P_Z // SSM_GW
    wbb, wcb = SSM_INNER // SSM_STATE, (SSM_INNER + SSM_BC) // SSM_STATE
    in_specs = [
        pl.BlockSpec((1, TB, SSM_GW), lambda b, g, i: (b, tmap(i), xsb + g)),
        pl.BlockSpec((1, TB, SSM_STATE), lambda b, g, i: (b, tmap(i), bmb + g)),
        pl.BlockSpec((1, TB, SSM_STATE), lambda b, g, i: (b, tmap(i), cmb + g)),
        pl.BlockSpec((1, TB, P_SMALL), lambda b, g, i: (b, tmap(i), 0)),
        pl.BlockSpec((1, P_SMALL, 128), lambda b, g, i: (g, 0, 0)),
        pl.BlockSpec((SSM_CONV, SSM_GW), lambda b, g, i: (0, g)),
        pl.BlockSpec((SSM_CONV, SSM_STATE), lambda b, g, i: (0, wbb + g)),
        pl.BlockSpec((SSM_CONV, SSM_STATE), lambda b, g, i: (0, wcb + g)),
        pl.BlockSpec((1, SSM_GW), lambda b, g, i: (0, g)),
        pl.BlockSpec((1, SSM_STATE), lambda b, g, i: (0, wbb + g)),
        pl.BlockSpec((1, SSM_STATE), lambda b, g, i: (0, wcb + g)),
        pl.BlockSpec((1, 1, 128), lambda b, g, i: (g, 0, 0)),
        pl.BlockSpec((1, 1, 128), lambda b, g, i: (g, 0, 0)),
        _const_spec((C, C)),
        pl.BlockSpec((1, 1, SSM_STATE, SSM_GW), lambda b, g, i: (b, g, 0, 0)),
    ]
    args = [pm, pm, pm, ps, sel, conv_w, conv_w, conv_w, conv_b, conv_b, conv_b, dt_bias, a_log, tri, s0]
    if finalize:
        in_specs += [
            pl.BlockSpec((1, TB, SSM_GW), lambda b, g, i: (b, tmap(i), zb + g)),
            pl.BlockSpec((1, TB, SSM_GW), lambda b, g, i: (b, tmap(i), g)),
            pl.BlockSpec((1, SSM_GW), lambda b, g, i: (0, g)),
            pl.BlockSpec((1, SSM_GW), lambda b, g, i: (0, g)),
        ]
        args += [pm, y_prev, d_skip_x, norm_w]
    kern = functools.partial(_ssd_kernel, C=C, TB=TB, rowlen=rowlen, reverse=reverse, finalize=finalize)
    return pl.pallas_call(
        kern,
        grid=(B, G, nb),
        in_specs=in_specs,
        out_specs=[pl.BlockSpec((1, TB, SSM_GW), lambda b, g, i: (b, tmap(i), g)),
                   pl.BlockSpec((1, 1, SSM_STATE, SSM_GW), lambda b, g, i: (b, g, 0, 0))],
        out_shape=[jax.ShapeDtypeStruct((B, L, SSM_INNER), BF16 if finalize else F32),
                   jax.ShapeDtypeStruct((B, G, SSM_STATE, SSM_GW), F32)],
        scratch_shapes=[pltpu.VMEM((SSM_STATE, SSM_GW), F32),
                        pltpu.VMEM((TB, SSM_GW), F32),
                        pltpu.VMEM((TB, SSM_STATE), F32),
                        pltpu.VMEM((TB, SSM_STATE), F32),
                        pltpu.VMEM((TB, SSM_GW), F32),
                        pltpu.VMEM((C, SSM_GW), F32),
                        pltpu.VMEM((8, SSM_GW), F32)],
        compiler_params=_params(("parallel", "parallel", "arbitrary")),
        name="ssd_" + ("rev" if reverse else "fwd") + ("_fin" if finalize else ""),
    )(*args)


def _merge_kernel(oa_ref, ob_ref, ga_ref, gb_ref, x_ref, g1_ref, wpa_ref, wpb_ref, wout_ref, h_ref):
    ya = _dot(oa_ref[0], wpa_ref[...])
    yb = _dot(ob_ref[0], wpb_ref[...])
    m = _sigmoid(ga_ref[0].astype(F32)) * ya + _sigmoid(gb_ref[0].astype(F32)) * yb
    mix = _dot(m.astype(BF16), wout_ref[...])
    h_ref[0] = x_ref[0] + g1_ref[0] * mix


def _merge(o_a, o_b, pm, x, g1, w_pa, w_pb, w_out, tm):
    B, L, _ = x.shape
    gab, gbb = P_GA // D_MODEL, P_GB // D_MODEL
    return pl.pallas_call(
        _merge_kernel,
        grid=(B, L // tm),
        in_specs=[pl.BlockSpec((1, tm, GLA_V), lambda b, i: (b, i, 0)),
                  pl.BlockSpec((1, tm, SSM_INNER), lambda b, i: (b, i, 0)),
                  pl.BlockSpec((1, tm, D_MODEL), lambda b, i: (b, i, gab)),
                  pl.BlockSpec((1, tm, D_MODEL), lambda b, i: (b, i, gbb)),
                  pl.BlockSpec((1, tm, D_MODEL), lambda b, i: (b, i, 0)),
                  pl.BlockSpec((1, 1, D_MODEL), lambda b, i: (b, 0, 0)),
                  _const_spec(w_pa.shape), _const_spec(w_pb.shape), _const_spec(w_out.shape)],
        out_specs=pl.BlockSpec((1, tm, D_MODEL), lambda b, i: (b, i, 0)),
        out_shape=jax.ShapeDtypeStruct((B, L, D_MODEL), F32),
        compiler_params=_params(("parallel", "parallel")),
        name="merge",
    )(o_a, o_b, pm, pm, x, g1, w_pa, w_pb, w_out)


FFN_CHUNK = 256


def _ffn_kernel(h_ref, mod_ref, n2_ref, fw_ref, wg_ref, wu_ref, wd_ref, o_ref):
    h = h_ref[0]
    hn = h * lax.rsqrt(jnp.mean(h * h, axis=-1, keepdims=True) + EPS) * n2_ref[...]
    hn = (hn * (1.0 + mod_ref[0, 1:2, :]) + mod_ref[0, 0:1, :]).astype(BF16)
    acc = jnp.zeros(h.shape, F32)
    for c in range(D_FF // FFN_CHUNK):
        cols = slice(c * FFN_CHUNK, (c + 1) * FFN_CHUNK)
        gt = _dot(hn, wg_ref[:, cols])
        up = _dot(hn, wu_ref[:, cols])
        acc = acc + _dot((_silu(gt) * up).astype(BF16), wd_ref[cols, :])
    h2 = h + mod_ref[0, 2:3, :] * acc
    o_ref[0] = h2 * lax.rsqrt(jnp.mean(h2 * h2, axis=-1, keepdims=True) + EPS) * fw_ref[...]


def _ffn(h, mod, n2w, fw, w_gate, w_up, w_down, tm):
    B, L, _ = h.shape
    return pl.pallas_call(
        _ffn_kernel,
        grid=(B, L // tm),
        in_specs=[pl.BlockSpec((1, tm, D_MODEL), lambda b, i: (b, i, 0)),
                  pl.BlockSpec((1, 3, D_MODEL), lambda b, i: (b, 0, 0)),
                  _const_spec((1, D_MODEL)), _const_spec((1, D_MODEL)),
                  pl.BlockSpec(w_gate.shape, lambda b, i: (0, 0), pipeline_mode=pl.Buffered(1)),
                  pl.BlockSpec(w_up.shape, lambda b, i: (0, 0), pipeline_mode=pl.Buffered(1)),
                  pl.BlockSpec(w_down.shape, lambda b, i: (0, 0), pipeline_mode=pl.Buffered(1))],
        out_specs=pl.BlockSpec((1, tm, D_MODEL), lambda b, i: (b, i, 0)),
        out_shape=jax.ShapeDtypeStruct((B, L, D_MODEL), F32),
        compiler_params=_params(("parallel", "parallel")),
        name="ffn",
    )(h, mod, n2w, fw, w_gate, w_up, w_down)


def _pick_block(L, pref):
    tb = min(L, pref)
    assert L % tb == 0
    return tb


def kernel(x, c, ctx, c_ctx, w_ada, b_ada, norm1_w, w_in, gla_up_f, gla_bias_f, gla_up_b, gla_bias_b,
           gla_norm_w, conv_w, conv_b, dt_bias_f, dt_bias_b, a_log_f, a_log_b, d_skip, ssm_norm_w,
           w_pa, w_pb, w_out, norm2_w, w_gate, w_up, w_down, final_norm_w):
    B, L, D = x.shape
    Lc = ctx.shape[1]
    depth = w_ada.shape[0]
    assert depth == 1 and D == D_MODEL
    assert L % GRID_W == 0 and L % SSD_CHUNK == 0 and Lc % SSD_CHUNK == 0
    lay = 0

    nrow = -(-(B + 1) // 8) * 8
    cc = jnp.zeros((nrow, D), F32).at[:B].set(c).at[B].set(c_ctx)
    ada = _ada(cc, w_ada[lay], b_ada[lay][None, :])
    sh1, sc1, g1, sh2, sc2, g2 = [ada[:, i * D:(i + 1) * D] for i in range(6)]
    mod1 = jnp.stack([sh1[:B], sc1[:B]], axis=1)
    mod1_c = jnp.broadcast_to(jnp.stack([sh1[B], sc1[B]])[None], (B, 2, D))
    mod2 = jnp.stack([sh2[:B], sc2[:B], g2[:B]], axis=1)
    g1_l = g1[:B, None, :]

    wi = w_in[lay]
    o = _IN_OFF
    w_main = jnp.concatenate([wi[:, o[0]:o[4]], wi[:, o[6]:o[10]], wi[:, o[12]:o[14]]], axis=1).astype(BF16)
    w_small = jnp.concatenate([wi[:, o[4]:o[6]], wi[:, o[10]:o[12]],
                               jnp.zeros((D, P_SMALL - 2 * GLA_RANK - 2 * SSM_HEADS), F32)], axis=1).astype(BF16)
    nw1 = norm1_w[lay][None, :]

    grp = lambda p: jnp.pad(p[lay].reshape(SSM_GROUPS, 1, SSM_HPG), ((0, 0), (0, 0), (0, 128 - SSM_HPG)))
    dtb_f, dtb_b, al_f, al_b = grp(dt_bias_f), grp(dt_bias_b), grp(a_log_f), grp(a_log_b)
    cw, cb_ = conv_w[lay], conv_b[lay][None, :]
    up_f, up_b = gla_up_f[lay], gla_up_b[lay]
    bi_f, bi_b = gla_bias_f[lay][None, :], gla_bias_b[lay][None, :]

    gla_zero = jnp.zeros((B, GLA_HEADS, GLA_DV, GLA_DK), F32)
    ssd_zero = jnp.zeros((B, SSM_GROUPS, SSM_STATE, SSM_GW), F32)

    pm_c, ps_c = _inproj(ctx, mod1_c, nw1, w_main, w_small, _pick_block(Lc, 256))
    tbc = _pick_block(Lc, 256)
    _, sg_f = _gla_pass(pm_c, ps_c, up_f, bi_f, gla_zero, reverse=False, finalize=False, TB=tbc)
    _, sg_b = _gla_pass(pm_c, ps_c, up_b, bi_b, gla_zero, reverse=True, finalize=False, TB=tbc)
    _, ss_f = _ssd_pass(pm_c, ps_c, cw, cb_, dtb_f, al_f, ssd_zero, reverse=False, finalize=False,
                        TB=Lc, rowlen=Lc)
    _, ss_b = _ssd_pass(pm_c, ps_c, cw, cb_, dtb_b, al_b, ssd_zero, reverse=True, finalize=False,
                        TB=Lc, rowlen=Lc)

    pm, ps = _inproj(x, mod1, nw1, w_main, w_small, _pick_block(L, 512))
    tbg = _pick_block(L, 512)
    og_f, _ = _gla_pass(pm, ps, up_f, bi_f, sg_f, reverse=False, finalize=False, TB=tbg)
    o_a, _ = _gla_pass(pm, ps, up_b, bi_b, sg_b, reverse=True, finalize=True, TB=tbg,
                       o_prev=og_f, norm_w=gla_norm_w[lay][None, :])
    tbs = _pick_block(L, 256)
    ys_f, _ = _ssd_pass(pm, ps, cw, cb_, dtb_f, al_f, ss_f, reverse=False, finalize=False,
                        TB=tbs, rowlen=GRID_W)
    o_b, _ = _ssd_pass(pm, ps, cw, cb_, dtb_b, al_b, ss_b, reverse=True, finalize=True,
                       TB=tbs, rowlen=GRID_W, y_prev=ys_f,
                       d_skip_x=jnp.repeat(d_skip[lay], SSM_HEADDIM)[None, :],
                       norm_w=ssm_norm_w[lay][None, :])

    h = _merge(o_a, o_b, pm, x, g1_l, w_pa[lay].astype(BF16), w_pb[lay].astype(BF16),
               w_out[lay].astype(BF16), _pick_block(L, 256))
    return _ffn(h, mod2, norm2_w[lay][None, :], final_norm_w[None, :], w_gate[lay].astype(BF16),
                w_up[lay].astype(BF16), w_down[lay].astype(BF16), _pick_block(L, 256))
```

```python
import functools

import numpy as np
import jax
import jax.numpy as jnp
from jax import lax
from jax.experimental import pallas as pl
from jax.experimental.pallas import tpu as pltpu

F32 = jnp.float32
BF16 = jnp.bfloat16

D_MODEL = 1024
GRID_W = 64
EPS = 1e-6

GLA_HEADS = 4
GLA_DK = 128
GLA_DV = 256
GLA_QK = GLA_HEADS * GLA_DK
GLA_V = GLA_HEADS * GLA_DV
GLA_RANK = 16
GLA_TAU = 16.0

SSM_INNER = 2 * D_MODEL
SSM_HEADDIM = 64
SSM_HEADS = SSM_INNER // SSM_HEADDIM
SSM_GROUPS = 4
SSM_HPG = SSM_HEADS // SSM_GROUPS
SSM_STATE = 128
SSM_BC = SSM_GROUPS * SSM_STATE
SSM_CONV = 4
CONV_LEFT = 2
SSM_GW = SSM_HPG * SSM_HEADDIM

D_FF = ((8 * D_MODEL // 3 + 255) // 256) * 256

_IN_WIDTHS = (GLA_QK, GLA_QK, GLA_V, GLA_V, GLA_RANK, GLA_RANK,
              SSM_INNER, SSM_INNER, SSM_BC, SSM_BC, SSM_HEADS, SSM_HEADS, D_MODEL, D_MODEL)
_IN_OFF = np.concatenate([[0], np.cumsum(_IN_WIDTHS)]).tolist()

P_Q, P_K, P_V, P_R = 0, 512, 1024, 2048
P_Z, P_XS, P_BM, P_CM = 3072, 5120, 7168, 7680
P_GA, P_GB = 8192, 9216
P_MAIN = 10240
S_LRF, S_LRB, S_DTF, S_DTB = 0, 16, 32, 64
P_SMALL = 128

GLA_CHUNK = 128
SSD_CHUNK = 128

VMEM_LIMIT = 56 * 1024 * 1024


def _sigmoid(x):
    return 1.0 / (1.0 + jnp.exp(-x))


def _silu(x):
    return x * _sigmoid(x)


def _softplus(x):
    return jnp.maximum(x, 0.0) + jnp.log1p(jnp.exp(-jnp.abs(x)))


def _split3(x):
    hi = x.astype(BF16)
    r1 = x - hi.astype(F32)
    mid = r1.astype(BF16)
    lo = (r1 - mid.astype(F32)).astype(BF16)
    return hi, mid, lo


def _dot(a, b):
    return jnp.dot(a, b, preferred_element_type=F32)


def _dot_nt(a, b):
    return lax.dot_general(a, b, (((1,), (1,)), ((), ())), preferred_element_type=F32)


def _dot_tn(a, b):
    return lax.dot_general(a, b, (((0,), (0,)), ((), ())), preferred_element_type=F32)


def _dot01(m01, x):
    hi, mid, lo = _split3(x)
    return _dot(m01, hi) + _dot(m01, mid) + _dot(m01, lo)


def _dot01r(x, m01):
    hi, mid, lo = _split3(x)
    return _dot(hi, m01) + _dot(mid, m01) + _dot(lo, m01)


def _params(sem):
    return pltpu.CompilerParams(dimension_semantics=sem, vmem_limit_bytes=VMEM_LIMIT)


def _const_spec(shape):
    n = len(shape)
    return pl.BlockSpec(shape, lambda *_: (0,) * n)


def _ada_kernel(c_ref, w_ref, b_ref, o_ref):
    s = _silu(c_ref[...])
    o_ref[...] = jnp.dot(s, w_ref[...], preferred_element_type=F32,
                         precision=lax.Precision.HIGHEST) + b_ref[...]


def _ada(cc, w, b):
    rows = cc.shape[0]
    n = w.shape[1]
    tn = 1536
    return pl.pallas_call(
        _ada_kernel,
        grid=(n // tn,),
        in_specs=[pl.BlockSpec((rows, D_MODEL), lambda j: (0, 0)),
                  pl.BlockSpec((D_MODEL, tn), lambda j: (0, j)),
                  pl.BlockSpec((1, tn), lambda j: (0, j))],
        out_specs=pl.BlockSpec((rows, tn), lambda j: (0, j)),
        out_shape=jax.ShapeDtypeStruct((rows, n), F32),
        compiler_params=_params(("arbitrary",)),
        name="ada",
    )(cc, w, b)


def _inproj_kernel(x_ref, mod_ref, nw_ref, wm_ref, ws_ref, om_ref, os_ref, xn_ref):
    @pl.when(pl.program_id(2) == 0)
    def _():
        x = x_ref[0]
        y = x * lax.rsqrt(jnp.mean(x * x, axis=-1, keepdims=True) + EPS) * nw_ref[...]
        y = y * (1.0 + mod_ref[0, 1:2, :]) + mod_ref[0, 0:1, :]
        xn = y.astype(BF16)
        xn_ref[...] = xn
        os_ref[0] = _dot(xn, ws_ref[...])

    om_ref[0] = _dot(xn_ref[...], wm_ref[...]).astype(BF16)


def _inproj(x, mod, nw, w_main, w_small, tm):
    B, L, _ = x.shape
    tn = 1024
    return pl.pallas_call(
        _inproj_kernel,
        grid=(B, L // tm, P_MAIN // tn),
        in_specs=[pl.BlockSpec((1, tm, D_MODEL), lambda b, i, j: (b, i, 0)),
                  pl.BlockSpec((1, 2, D_MODEL), lambda b, i, j: (b, 0, 0)),
                  pl.BlockSpec((1, D_MODEL), lambda b, i, j: (0, 0)),
                  pl.BlockSpec((D_MODEL, tn), lambda b, i, j: (0, j)),
                  pl.BlockSpec((D_MODEL, P_SMALL), lambda b, i, j: (0, 0))],
        out_specs=[pl.BlockSpec((1, tm, tn), lambda b, i, j: (b, i, j)),
                   pl.BlockSpec((1, tm, P_SMALL), lambda b, i, j: (b, i, 0))],
        out_shape=[jax.ShapeDtypeStruct((B, L, P_MAIN), BF16),
                   jax.ShapeDtypeStruct((B, L, P_SMALL), F32)],
        scratch_shapes=[pltpu.VMEM((tm, D_MODEL), BF16)],
        compiler_params=_params(("parallel", "parallel", "arbitrary")),
        name="inproj",
    )(x, mod, nw, w_main, w_small)


def _gla_consts(C, reverse):
    NL = int(np.log2(C))
    idx = np.arange(C)
    tri = (idx[None, :] <= idx[:, None]).astype(np.float32)
    mats = [tri]
    masks = [np.eye(C, dtype=np.float32)]
    for lev in range(NL):
        h = 1 << lev
        blk = idx // (2 * h)
        half = (idx // h) % 2
        ref = blk * 2 * h + h - 1
        d = tri - tri[ref]
        d[half == 0] *= -1.0
        mats.append(d)
        masks.append(((blk[:, None] == blk[None, :]) & (half[:, None] == 1)
                      & (half[None, :] == 0)).astype(np.float32))
    mats.append(1.0 - tri)
    if reverse:
        mats = [m[::-1, ::-1] for m in mats]
        masks = [m[::-1, ::-1] for m in masks]
    dmat = np.concatenate(mats, axis=0)
    return jnp.asarray(dmat, BF16), jnp.asarray(np.stack(masks), F32)


def _gla_kernel(q_ref, k_ref, v_ref, lr_ref, up_ref, bias_ref, dm_ref, mk_ref, s0_ref, *rest,
                C, TB, reverse, finalize, lr_off):
    if finalize:
        r_ref, op_ref, nw_ref, o_ref, sf_ref, st_ref, e_s, oi_s, kv_s, qe_s = rest
    else:
        o_ref, sf_ref, st_ref, e_s, oi_s, kv_s, qe_s = rest
    NL = int(np.log2(C))
    nchunk = TB // C
    blk = pl.program_id(2)

    @pl.when(blk == 0)
    def _():
        st_ref[...] = s0_ref[0, 0]

    lr = lr_ref[0][:, lr_off:lr_off + GLA_RANK]
    pre = _dot(lr.astype(BF16), up_ref[...].astype(BF16)) + bias_ref[...]
    g = -_softplus(-pre) * (1.0 / GLA_TAU)
    g_hi = g.astype(BF16)
    g_lo = (g - g_hi.astype(F32)).astype(BF16)
    gs = jnp.concatenate([g_hi, g_lo], axis=1)

    for c in range(nchunk):
        ex = _dot(dm_ref[...], gs[c * C:(c + 1) * C])
        e_s[c] = jnp.exp(ex[:, :GLA_DK] + ex[:, GLA_DK:])

    for c in range(nchunk):
        rows = slice(c * C, (c + 1) * C)
        q = q_ref[0, rows, :].astype(F32) * (GLA_DK ** -0.5)
        k = k_ref[0, rows, :].astype(F32)
        v = v_ref[0, rows, :]
        att = mk_ref[0] * _dot_nt(q.astype(BF16), k.astype(BF16))
        for lev in range(NL):
            e_l = e_s[c, (lev + 1) * C:(lev + 2) * C, :]
            att = att + mk_ref[lev + 1] * _dot_nt((q * e_l).astype(BF16), (k * e_l).astype(BF16))
        oi_s[rows, :] = _dot(att.astype(BF16), v)
        kv_s[c] = _dot_tn(v, (k * e_s[c, (NL + 1) * C:(NL + 2) * C, :]).astype(BF16))
        qe_s[rows, :] = (q * e_s[c, 0:C, :]).astype(BF16)

    st = st_ref[...]
    for c in (reversed(range(nchunk)) if reverse else range(nchunk)):
        rows = slice(c * C, (c + 1) * C)
        o = oi_s[rows, :] + _dot_nt(qe_s[rows, :], st.astype(BF16))
        dec = e_s[c, 0:1, :] if reverse else e_s[c, C - 1:C, :]
        st = dec * st + kv_s[c]
        if finalize:
            o = o + op_ref[0, rows, :]
            y = o * lax.rsqrt(jnp.mean(o * o, axis=-1, keepdims=True) + EPS) * nw_ref[...]
            r = r_ref[0, rows, :].astype(F32)
            o_ref[0, rows, :] = (y * _silu(r)).astype(o_ref.dtype)
        else:
            o_ref[0, rows, :] = o.astype(o_ref.dtype)
    st_ref[...] = st

    @pl.when(blk == pl.num_programs(2) - 1)
    def _():
        sf_ref[0, 0] = st


def _gla_pass(pm, ps, up, bias, s0, *, reverse, finalize, TB, o_prev=None, norm_w=None):
    B, L, _ = pm.shape
    C = GLA_CHUNK
    nb = L // TB
    dmat, masks = _gla_consts(C, reverse)
    tmap = (lambda i: nb - 1 - i) if reverse else (lambda i: i)
    qb, kb = P_Q // GLA_DK, P_K // GLA_DK
    vb, rb = P_V // GLA_DV, P_R // GLA_DV
    in_specs = [
        pl.BlockSpec((1, TB, GLA_DK), lambda b, h, i: (b, tmap(i), qb + h)),
        pl.BlockSpec((1, TB, GLA_DK), lambda b, h, i: (b, tmap(i), kb + h)),
        pl.BlockSpec((1, TB, GLA_DV), lambda b, h, i: (b, tmap(i), vb + h)),
        pl.BlockSpec((1, TB, P_SMALL), lambda b, h, i: (b, tmap(i), 0)),
        pl.BlockSpec((GLA_RANK, GLA_DK), lambda b, h, i: (0, h)),
        pl.BlockSpec((1, GLA_DK), lambda b, h, i: (0, h)),
        _const_spec(dmat.shape),
        _const_spec(masks.shape),
        pl.BlockSpec((1, 1, GLA_DV, GLA_DK), lambda b, h, i: (b, h, 0, 0)),
    ]
    args = [pm, pm, pm, ps, up, bias, dmat, masks, s0]
    if finalize:
        in_specs += [
            pl.BlockSpec((1, TB, GLA_DV), lambda b, h, i: (b, tmap(i), rb + h)),
            pl.BlockSpec((1, TB, GLA_DV), lambda b, h, i: (b, tmap(i), h)),
            _const_spec((1, GLA_DV)),
        ]
        args += [pm, o_prev, norm_w]
    kern = functools.partial(_gla_kernel, C=C, TB=TB, reverse=reverse, finalize=finalize,
                             lr_off=S_LRB if reverse else S_LRF)
    return pl.pallas_call(
        kern,
        grid=(B, GLA_HEADS, nb),
        in_specs=in_specs,
        out_specs=[pl.BlockSpec((1, TB, GLA_DV), lambda b, h, i: (b, tmap(i), h)),
                   pl.BlockSpec((1, 1, GLA_DV, GLA_DK), lambda b, h, i: (b, h, 0, 0))],
        out_shape=[jax.ShapeDtypeStruct((B, L, GLA_V), BF16 if finalize else F32),
                   jax.ShapeDtypeStruct((B, GLA_HEADS, GLA_DV, GLA_DK), F32)],
        scratch_shapes=[pltpu.VMEM((GLA_DV, GLA_DK), F32),
                        pltpu.VMEM((TB // C, dmat.shape[0], GLA_DK), F32),
                        pltpu.VMEM((TB, GLA_DV), F32),
                        pltpu.VMEM((TB // C, GLA_DV, GLA_DK), F32),
                        pltpu.VMEM((TB, GLA_DK), BF16)],
        compiler_params=_params(("parallel", "parallel", "arbitrary")),
        name="gla_" + ("rev" if reverse else "fwd") + ("_fin" if finalize else ""),
    )(*args)


def _conv_silu(u, w, b, pos, rowlen):
    T = u.shape[0]
    out = b + u * w[CONV_LEFT:CONV_LEFT + 1, :]
    for j in range(SSM_CONV):
        off = j - CONV_LEFT
        if off == 0:
            continue
        shifted = pltpu.roll(u, (-off) % T, 0)
        ok = (pos + off >= 0) & (pos + off < rowlen)
        out = out + jnp.where(ok, shifted, 0.0) * w[j:j + 1, :]
    return _silu(out)


def _ssd_kernel(xs_ref, bm_ref, cm_ref, ps_ref, sel_ref, wx_ref, wb_ref, wc_ref, bx_ref, bb_ref, bc_ref,
                dtb_ref, alog_ref, tri_ref, s0_ref, *rest, C, TB, rowlen, reverse, finalize):
    if finalize:
        z_ref, yp_ref, dsk_ref, nw_ref, y_ref, sf_ref, st_ref, xc_s, bc_s, cc_s, ya_s, xw_s, dec_s = rest
    else:
        y_ref, sf_ref, st_ref, xc_s, bc_s, cc_s, ya_s, xw_s, dec_s = rest
    nchunk = TB // C
    blk = pl.program_id(2)

    @pl.when(blk == 0)
    def _():
        st_ref[...] = s0_ref[0, 0]

    pos = lax.broadcasted_iota(jnp.int32, (TB, 1), 0) % rowlen
    xc_s[...] = _conv_silu(xs_ref[0].astype(F32), wx_ref[...], bx_ref[...], pos, rowlen)
    bc_s[...] = _conv_silu(bm_ref[0].astype(F32), wb_ref[...], bb_ref[...], pos, rowlen)
    cc_s[...] = _conv_silu(cm_ref[0].astype(F32), wc_ref[...], bc_ref[...], pos, rowlen)
    neg_a = -jnp.exp(alog_ref[0])
    ti = lax.broadcasted_iota(jnp.int32, (C, C), 0)
    si = lax.broadcasted_iota(jnp.int32, (C, C), 1)
    keep = (si >= ti) if reverse else (si <= ti)
    last = 0 if reverse else C - 1

    def body(i, carry):
        ci = (nchunk - 1 - i) if reverse else i
        rows = pl.ds(pl.multiple_of(ci * C, C), C)
        raw = _dot01r(ps_ref[0, rows, :], sel_ref[0])
        dt = _softplus(raw + dtb_ref[0])
        a = dt * neg_a
        cum = _dot01(tri_ref[...], a)
        cum_t = jnp.transpose(cum)
        xc = xc_s[rows, :]
        bmc = bc_s[rows, :].astype(BF16)
        cmc = cc_s[rows, :].astype(BF16)
        cb = _dot_nt(cmc, bmc)
        st = st_ref[...]
        cs = _dot(cmc, st.astype(BF16))
        for e in range(SSM_HPG):
            lanes = slice(e * SSM_HEADDIM, (e + 1) * SSM_HEADDIM)
            ccol = cum[:, e:e + 1]
            seg = ccol - cum_t[e:e + 1, :]
            lm = jnp.where(keep, jnp.exp(jnp.minimum(seg, 0.0)), 0.0)
            xdt = xc[:, lanes] * dt[:, e:e + 1]
            y = _dot((cb * lm).astype(BF16), xdt.astype(BF16))
            y = y + cs[:, lanes] * jnp.exp(ccol)
            ya_s[rows, lanes] = y
            clast = cum[last:last + 1, e:e + 1]
            xw_s[:, lanes] = xdt * jnp.exp(clast - ccol)
            dec_s[:, lanes] = jnp.broadcast_to(jnp.exp(clast), (8, SSM_HEADDIM))
        st_ref[...] = dec_s[0:1, :] * st + _dot_tn(bmc, xw_s[...].astype(BF16))
        return carry

    lax.fori_loop(0, nchunk, body, 0)

    if finalize:
        y = ya_s[...] + yp_ref[0] + dsk_ref[...] * xc_s[...]
        y = y * _silu(z_ref[0].astype(F32))
        y = y * lax.rsqrt(jnp.mean(y * y, axis=-1, keepdims=True) + EPS) * nw_ref[...]
        y_ref[0] = y.astype(y_ref.dtype)
    else:
        y_ref[0] = ya_s[...]

    @pl.when(blk == pl.num_programs(2) - 1)
    def _():
        sf_ref[0, 0] = st_ref[...]


def _ssd_pass(pm, ps, conv_w, conv_b, dt_bias, a_log, s0, *, reverse, finalize, TB, rowlen,
              y_prev=None, d_skip_x=None, norm_w=None):
    B, L, _ = pm.shape
    C = SSD_CHUNK
    nb = L // TB
    G = SSM_GROUPS
    tmap = (lambda i: nb - 1 - i) if reverse else (lambda i: i)
    idx = np.arange(C)
    tri = (idx[None, :] >= idx[:, None]) if reverse else (idx[None, :] <= idx[:, None])
    tri = jnp.asarray(tri.astype(np.float32), BF16)
    sel = np.zeros((G, P_SMALL, 128), np.float32)
    off = S_DTB if reverse else S_DTF
    for g in range(G):
        sel[g, off + g * SSM_HPG + np.arange(SSM_HPG), np.arange(SSM_HPG)] = 1.0
    sel = jnp.asarray(sel, BF16)
    xsb, bmb, cmb, zb = P_XS // SSM_GW, P_BM // SSM_STATE, P_CM // SSM_STATE, P_Z // SSM_GW
    wbb, wcb = SSM_INNER // SSM_STATE, (SSM_INNER + SSM_BC) // SSM_STATE
    in_specs = [
        pl.BlockSpec((1, TB, SSM_GW), lambda b, g, i: (b, tmap(i), xsb + g)),
        pl.BlockSpec((1, TB, SSM_STATE), lambda b, g, i: (b, tmap(i), bmb + g)),
        pl.BlockSpec((1, TB, SSM_STATE), lambda b, g, i: (b, tmap(i), cmb + g)),
        pl.BlockSpec((1, TB, P_SMALL), lambda b, g, i: (b, tmap(i), 0)),
        pl.BlockSpec((1, P_SMALL, 128), lambda b, g, i: (g, 0, 0)),
        pl.BlockSpec((SSM_CONV, SSM_GW), lambda b, g, i: (0, g)),
        pl.BlockSpec((SSM_CONV, SSM_STATE), lambda b, g, i: (0, wbb + g)),
        pl.BlockSpec((SSM_CONV, SSM_STATE), lambda b, g, i: (0, wcb + g)),
        pl.BlockSpec((1, SSM_GW), lambda b, g, i: (0, g)),
        pl.BlockSpec((1, SSM_STATE), lambda b, g, i: (0, wbb + g)),
        pl.BlockSpec((1, SSM_STATE), lambda b, g, i: (0, wcb + g)),
        pl.BlockSpec((1, 1, 128), lambda b, g, i: (g, 0, 0)),
        pl.BlockSpec((1, 1, 128), lambda b, g, i: (g, 0, 0)),
        _const_spec((C, C)),
        pl.BlockSpec((1, 1, SSM_STATE, SSM_GW), lambda b, g, i: (b, g, 0, 0)),
    ]
    args = [pm, pm, pm, ps, sel, conv_w, conv_w, conv_w, conv_b, conv_b, conv_b, dt_bias, a_log, tri, s0]
    if finalize:
        in_specs += [
            pl.BlockSpec((1, TB, SSM_GW), lambda b, g, i: (b, tmap(i), zb + g)),
            pl.BlockSpec((1, TB, SSM_GW), lambda b, g, i: (b, tmap(i), g)),
            pl.BlockSpec((1, SSM_GW), lambda b, g, i: (0, g)),
            pl.BlockSpec((1, SSM_GW), lambda b, g, i: (0, g)),
        ]
        args += [pm, y_prev, d_skip_x, norm_w]
    kern = functools.partial(_ssd_kernel, C=C, TB=TB, rowlen=rowlen, reverse=reverse, finalize=finalize)
    return pl.pallas_call(
        kern,
        grid=(B, G, nb),
        in_specs=in_specs,
        out_specs=[pl.BlockSpec((1, TB, SSM_GW), lambda b, g, i: (b, tmap(i), g)),
                   pl.BlockSpec((1, 1, SSM_STATE, SSM_GW), lambda b, g, i: (b, g, 0, 0))],
        out_shape=[jax.ShapeDtypeStruct((B, L, SSM_INNER), BF16 if finalize else F32),
                   jax.ShapeDtypeStruct((B, G, SSM_STATE, SSM_GW), F32)],
        scratch_shapes=[pltpu.VMEM((SSM_STATE, SSM_GW), F32),
                        pltpu.VMEM((TB, SSM_GW), F32),
                        pltpu.VMEM((TB, SSM_STATE), F32),
                        pltpu.VMEM((TB, SSM_STATE), F32),
                        pltpu.VMEM((TB, SSM_GW), F32),
                        pltpu.VMEM((C, SSM_GW), F32),
                        pltpu.VMEM((8, SSM_GW), F32)],
        compiler_params=_params(("parallel", "parallel", "arbitrary")),
        name="ssd_" + ("rev" if reverse else "fwd") + ("_fin" if finalize else ""),
    )(*args)


def _merge_kernel(oa_ref, ob_ref, ga_ref, gb_ref, x_ref, g1_ref, wpa_ref, wpb_ref, wout_ref, h_ref):
    ya = _dot(oa_ref[0], wpa_ref[...])
    yb = _dot(ob_ref[0], wpb_ref[...])
    m = _sigmoid(ga_ref[0].astype(F32)) * ya + _sigmoid(gb_ref[0].astype(F32)) * yb
    mix = _dot(m.astype(BF16), wout_ref[...])
    h_ref[0] = x_ref[0] + g1_ref[0] * mix


def _merge(o_a, o_b, pm, x, g1, w_pa, w_pb, w_out, tm):
    B, L, _ = x.shape
    gab, gbb = P_GA // D_MODEL, P_GB // D_MODEL
    return pl.pallas_call(
        _merge_kernel,
        grid=(B, L // tm),
        in_specs=[pl.BlockSpec((1, tm, GLA_V), lambda b, i: (b, i, 0)),
                  pl.BlockSpec((1, tm, SSM_INNER), lambda b, i: (b, i, 0)),
                  pl.BlockSpec((1, tm, D_MODEL), lambda b, i: (b, i, gab)),
                  pl.BlockSpec((1, tm, D_MODEL), lambda b, i: (b, i, gbb)),
                  pl.BlockSpec((1, tm, D_MODEL), lambda b, i: (b, i, 0)),
                  pl.BlockSpec((1, 1, D_MODEL), lambda b, i: (b, 0, 0)),
                  _const_spec(w_pa.shape), _const_spec(w_pb.shape), _const_spec(w_out.shape)],
        out_specs=pl.BlockSpec((1, tm, D_MODEL), lambda b, i: (b, i, 0)),
        out_shape=jax.ShapeDtypeStruct((B, L, D_MODEL), F32),
        compiler_params=_params(("parallel", "parallel")),
        name="merge",
    )(o_a, o_b, pm, pm, x, g1, w_pa, w_pb, w_out)


FFN_CHUNK = 256


def _ffn_kernel(h_ref, mod_ref, n2_ref, fw_ref, wg_ref, wu_ref, wd_ref, o_ref):
    h = h_ref[0]
    hn = h * lax.rsqrt(jnp.mean(h * h, axis=-1, keepdims=True) + EPS) * n2_ref[...]
    hn = (hn * (1.0 + mod_ref[0, 1:2, :]) + mod_ref[0, 0:1, :]).astype(BF16)
    acc = jnp.zeros(h.shape, F32)
    for c in range(D_FF // FFN_CHUNK):
        cols = slice(c * FFN_CHUNK, (c + 1) * FFN_CHUNK)
        gt = _dot(hn, wg_ref[:, cols])
        up = _dot(hn, wu_ref[:, cols])
        acc = acc + _dot((_silu(gt) * up).astype(BF16), wd_ref[cols, :])
    h2 = h + mod_ref[0, 2:3, :] * acc
    o_ref[0] = h2 * lax.rsqrt(jnp.mean(h2 * h2, axis=-1, keepdims=True) + EPS) * fw_ref[...]


def _ffn(h, mod, n2w, fw, w_gate, w_up, w_down, tm):
    B, L, _ = h.shape
    return pl.pallas_call(
        _ffn_kernel,
        grid=(B, L // tm),
        in_specs=[pl.BlockSpec((1, tm, D_MODEL), lambda b, i: (b, i, 0)),
                  pl.BlockSpec((1, 3, D_MODEL), lambda b, i: (b, 0, 0)),
                  _const_spec((1, D_MODEL)), _const_spec((1, D_MODEL)),
                  pl.BlockSpec(w_gate.shape, lambda b, i: (0, 0), pipeline_mode=pl.Buffered(1)),
                  pl.BlockSpec(w_up.shape, lambda b, i: (0, 0), pipeline_mode=pl.Buffered(1)),
                  pl.BlockSpec(w_down.shape, lambda b, i: (0, 0), pipeline_mode=pl.Buffered(1))],
        out_specs=pl.BlockSpec((1, tm, D_MODEL), lambda b, i: (b, i, 0)),
        out_shape=jax.ShapeDtypeStruct((B, L, D_MODEL), F32),
        compiler_params=_params(("parallel", "parallel")),
        name="ffn",
    )(h, mod, n2w, fw, w_gate, w_up, w_down)


def _pick_block(L, pref):
    tb = min(L, pref)
    assert L % tb == 0
    return tb


def kernel(x, c, ctx, c_ctx, w_ada, b_ada, norm1_w, w_in, gla_up_f, gla_bias_f, gla_up_b, gla_bias_b,
           gla_norm_w, conv_w, conv_b, dt_bias_f, dt_bias_b, a_log_f, a_log_b, d_skip, ssm_norm_w,
           w_pa, w_pb, w_out, norm2_w, w_gate, w_up, w_down, final_norm_w):
    B, L, D = x.shape
    Lc = ctx.shape[1]
    depth = w_ada.shape[0]
    assert depth == 1 and D == D_MODEL
    assert L % GRID_W == 0 and L % SSD_CHUNK == 0 and Lc % SSD_CHUNK == 0
    lay = 0

    nrow = -(-(B + 1) // 8) * 8
    cc = jnp.zeros((nrow, D), F32).at[:B].set(c).at[B].set(c_ctx)
    ada = _ada(cc, w_ada[lay], b_ada[lay][None, :])
    sh1, sc1, g1, sh2, sc2, g2 = [ada[:, i * D:(i + 1) * D] for i in range(6)]
    mod1 = jnp.stack([sh1[:B], sc1[:B]], axis=1)
    mod1_c = jnp.broadcast_to(jnp.stack([sh1[B], sc1[B]])[None], (B, 2, D))
    mod2 = jnp.stack([sh2[:B], sc2[:B], g2[:B]], axis=1)
    g1_l = g1[:B, None, :]

    wi = w_in[lay]
    o = _IN_OFF
    w_main = jnp.concatenate([wi[:, o[0]:o[4]], wi[:, o[6]:o[10]], wi[:, o[12]:o[14]]], axis=1).astype(BF16)
    w_small = jnp.concatenate([wi[:, o[4]:o[6]], wi[:, o[10]:o[12]],
                               jnp.zeros((D, P_SMALL - 2 * GLA_RANK - 2 * SSM_HEADS), F32)], axis=1).astype(BF16)
    nw1 = norm1_w[lay][None, :]

    grp = lambda p: jnp.pad(p[lay].reshape(SSM_GROUPS, 1, SSM_HPG), ((0, 0), (0, 0), (0, 128 - SSM_HPG)))
    dtb_f, dtb_b, al_f, al_b = grp(dt_bias_f), grp(dt_bias_b), grp(a_log_f), grp(a_log_b)
    cw, cb_ = conv_w[lay], conv_b[lay][None, :]
    up_f, up_b = gla_up_f[lay], gla_up_b[lay]
    bi_f, bi_b = gla_bias_f[lay][None, :], gla_bias_b[lay][None, :]

    gla_zero = jnp.zeros((B, GLA_HEADS, GLA_DV, GLA_DK), F32)
    ssd_zero = jnp.zeros((B, SSM_GROUPS, SSM_STATE, SSM_GW), F32)

    pm_c, ps_c = _inproj(ctx, mod1_c, nw1, w_main, w_small, _pick_block(Lc, 256))
    tbc = _pick_block(Lc, 256)
    _, sg_f = _gla_pass(pm_c, ps_c, up_f, bi_f, gla_zero, reverse=False, finalize=False, TB=tbc)
    _, sg_b = _gla_pass(pm_c, ps_c, up_b, bi_b, gla_zero, reverse=True, finalize=False, TB=tbc)
    _, ss_f = _ssd_pass(pm_c, ps_c, cw, cb_, dtb_f, al_f, ssd_zero, reverse=False, finalize=False,
                        TB=Lc, rowlen=Lc)
    _, ss_b = _ssd_pass(pm_c, ps_c, cw, cb_, dtb_b, al_b, ssd_zero, reverse=True, finalize=False,
                        TB=Lc, rowlen=Lc)

    pm, ps = _inproj(x, mod1, nw1, w_main, w_small, _pick_block(L, 512))
    tbg = _pick_block(L, 512)
    og_f, _ = _gla_pass(pm, ps, up_f, bi_f, sg_f, reverse=False, finalize=False, TB=tbg)
    o_a, _ = _gla_pass(pm, ps, up_b, bi_b, sg_b, reverse=True, finalize=True, TB=tbg,
                       o_prev=og_f, norm_w=gla_norm_w[lay][None, :])
    tbs = _pick_block(L, 256)
    ys_f, _ = _ssd_pass(pm, ps, cw, cb_, dtb_f, al_f, ss_f, reverse=False, finalize=False,
                        TB=tbs, rowlen=GRID_W)
    o_b, _ = _ssd_pass(pm, ps, cw, cb_, dtb_b, al_b, ss_b, reverse=True, finalize=True,
                       TB=tbs, rowlen=GRID_W, y_prev=ys_f,
                       d_skip_x=jnp.repeat(d_skip[lay], SSM_HEADDIM)[None, :],
                       norm_w=ssm_norm_w[lay][None, :])

    h = _merge(o_a, o_b, pm, x, g1_l, w_pa[lay].astype(BF16), w_pb[lay].astype(BF16),
               w_out[lay].astype(BF16), _pick_block(L, 256))
    return _ffn(h, mod2, norm2_w[lay][None, :], final_norm_w[None, :], w_gate[lay].astype(BF16),
                w_up[lay].astype(BF16), w_down[lay].astype(BF16), _pick_block(L, 256))
```

```python
import functools

import numpy as np
import jax
import jax.numpy as jnp
from jax import lax
from jax.experimental import pallas as pl
from jax.experimental.pallas import tpu as pltpu

F32 = jnp.float32
BF16 = jnp.bfloat16

D_MODEL = 1024
GRID_W = 64
EPS = 1e-6

GLA_HEADS = 4
GLA_DK = 128
GLA_DV = 256
GLA_QK = GLA_HEADS * GLA_DK
GLA_V = GLA_HEADS * GLA_DV
GLA_RANK = 16
GLA_TAU = 16.0

SSM_INNER = 2 * D_MODEL
SSM_HEADDIM = 64
SSM_HEADS = SSM_INNER // SSM_HEADDIM
SSM_GROUPS = 4
SSM_HPG = SSM_HEADS // SSM_GROUPS
SSM_STATE = 128
SSM_BC = SSM_GROUPS * SSM_STATE
SSM_CONV = 4
CONV_LEFT = 2
SSM_GW = SSM_HPG * SSM_HEADDIM

D_FF = ((8 * D_MODEL // 3 + 255) // 256) * 256

_IN_WIDTHS = (GLA_QK, GLA_QK, GLA_V, GLA_V, GLA_RANK, GLA_RANK,
              SSM_INNER, SSM_INNER, SSM_BC, SSM_BC, SSM_HEADS, SSM_HEADS, D_MODEL, D_MODEL)
_IN_OFF = np.concatenate([[0], np.cumsum(_IN_WIDTHS)]).tolist()

P_Z, P_XS = 0, 2048
P_Q, P_K, P_V, P_R = 4096, 4608, 5120, 6144
P_BM, P_CM = 7168, 7680
P_GA, P_GB = 8192, 9216
P_MAIN = 10240
S_LRF, S_LRB, S_DTF, S_DTB = 0, 16, 32, 64
P_SMALL = 128

GLA_CHUNK = 128
SSD_CHUNK = 128

VMEM_LIMIT = 56 * 1024 * 1024
NEG_BIG = -1e30
LOG2E = 1.4426950408889634


def _sigmoid(x):
    return 1.0 / (1.0 + jnp.exp(-x))


def _silu(x):
    return x * _sigmoid(x)


def _softplus(x):
    return jnp.maximum(x, 0.0) + jnp.log(1.0 + jnp.exp(-jnp.abs(x)))


def _split3(x):
    hi = x.astype(BF16)
    r1 = x - hi.astype(F32)
    mid = r1.astype(BF16)
    lo = (r1 - mid.astype(F32)).astype(BF16)
    return hi, mid, lo


def _dot(a, b):
    return jnp.dot(a, b, preferred_element_type=F32)


def _dot_nt(a, b):
    return lax.dot_general(a, b, (((1,), (1,)), ((), ())), preferred_element_type=F32)


def _dot_tn(a, b):
    return lax.dot_general(a, b, (((0,), (0,)), ((), ())), preferred_element_type=F32)


def _dot01(m01, x):
    hi, mid, lo = _split3(x)
    return _dot(m01, hi) + _dot(m01, mid) + _dot(m01, lo)


def _params(sem):
    return pltpu.CompilerParams(dimension_semantics=sem, vmem_limit_bytes=VMEM_LIMIT)


def _const_spec(shape):
    n = len(shape)
    return pl.BlockSpec(shape, lambda *_: (0,) * n)


def _ada_kernel(c_ref, w_ref, b_ref, o_ref):
    s = _silu(c_ref[...])
    o_ref[...] = jnp.dot(s, w_ref[...], preferred_element_type=F32,
                         precision=lax.Precision.HIGHEST) + b_ref[...]


def _ada(cc, w, b):
    rows = cc.shape[0]
    n = w.shape[1]
    tn = 1536
    return pl.pallas_call(
        _ada_kernel,
        grid=(n // tn,),
        in_specs=[pl.BlockSpec((rows, D_MODEL), lambda j: (0, 0)),
                  pl.BlockSpec((D_MODEL, tn), lambda j: (0, j)),
                  pl.BlockSpec((1, tn), lambda j: (0, j))],
        out_specs=pl.BlockSpec((rows, tn), lambda j: (0, j)),
        out_shape=jax.ShapeDtypeStruct((rows, n), F32),
        compiler_params=_params(("arbitrary",)),
        name="ada",
    )(cc, w, b)


def _inproj_kernel(x_ref, mod_ref, nw_ref, wm_ref, ws_ref, om_ref, os_ref, xn_ref):
    @pl.when(pl.program_id(2) == 0)
    def _():
        x = x_ref[0]
        y = x * lax.rsqrt(jnp.mean(x * x, axis=-1, keepdims=True) + EPS) * nw_ref[...]
        y = y * (1.0 + mod_ref[0, 1:2, :]) + mod_ref[0, 0:1, :]
        xn = y.astype(BF16)
        xn_ref[...] = xn
        os_ref[0] = _dot(xn, ws_ref[...])

    om_ref[0] = _dot(xn_ref[...], wm_ref[...]).astype(BF16)


def _inproj(x, mod, nw, w_main, w_small, tm):
    B, L, _ = x.shape
    tn = 2048
    return pl.pallas_call(
        _inproj_kernel,
        grid=(B, L // tm, P_MAIN // tn),
        in_specs=[pl.BlockSpec((1, tm, D_MODEL), lambda b, i, j: (b, i, 0)),
                  pl.BlockSpec((1, 2, D_MODEL), lambda b, i, j: (b, 0, 0)),
                  pl.BlockSpec((1, D_MODEL), lambda b, i, j: (0, 0)),
                  pl.BlockSpec((D_MODEL, tn), lambda b, i, j: (0, j)),
                  pl.BlockSpec((D_MODEL, P_SMALL), lambda b, i, j: (0, 0))],
        out_specs=[pl.BlockSpec((1, tm, tn), lambda b, i, j: (b, i, j)),
                   pl.BlockSpec((1, tm, P_SMALL), lambda b, i, j: (b, i, 0))],
        out_shape=[jax.ShapeDtypeStruct((B, L, P_MAIN), BF16),
                   jax.ShapeDtypeStruct((B, L, P_SMALL), F32)],
        scratch_shapes=[pltpu.VMEM((tm, D_MODEL), BF16)],
        compiler_params=_params(("parallel", "parallel", "arbitrary")),
        name="inproj",
    )(x, mod, nw, w_main, w_small)


def _gla_consts(C, reverse):
    NL = int(np.log2(C))
    idx = np.arange(C)
    tri = (idx[None, :] <= idx[:, None]).astype(np.float32)
    mats = [tri]
    masks = [np.eye(C, dtype=np.float32)]
    for lev in range(NL):
        h = 1 << lev
        blk = idx // (2 * h)
        half = (idx // h) % 2
        ref = blk * 2 * h + h - 1
        d = tri - tri[ref]
        d[half == 0] *= -1.0
        mats.append(d)
        masks.append(((blk[:, None] == blk[None, :]) & (half[:, None] == 1)
                      & (half[None, :] == 0)).astype(np.float32))
    mats.append(1.0 - tri)
    if reverse:
        mats = [m[::-1, ::-1] for m in mats]
        masks = [m[::-1, ::-1] for m in masks]
    dmat = np.concatenate(mats, axis=0)
    return jnp.asarray(dmat, BF16), jnp.asarray(np.stack(masks), F32)


def _gla_kernel(q_ref, k_ref, v_ref, lr_ref, up_ref, bias_ref, dm_ref, mk_ref, s0_ref, *rest,
                C, TB, reverse, finalize, lr_off):
    if finalize:
        r_ref, op_ref, nw_ref, o_ref, sf_ref, st_ref, e_s, oi_s, kv_s, qe_s = rest
    else:
        o_ref, sf_ref, st_ref, e_s, oi_s, kv_s, qe_s = rest
    NL = int(np.log2(C))
    nchunk = TB // C
    blk = pl.program_id(2)

    @pl.when(blk == 0)
    def _():
        st_ref[...] = s0_ref[0, 0]

    lr = lr_ref[0][:, lr_off:lr_off + GLA_RANK]
    pre = _dot(lr.astype(BF16), up_ref[...].astype(BF16)) + bias_ref[...]
    g = -_softplus(-pre) * (1.0 / GLA_TAU)
    g_hi = g.astype(BF16)
    g_lo = (g - g_hi.astype(F32)).astype(BF16)
    gs = jnp.concatenate([g_hi, g_lo], axis=1)

    for c in range(nchunk):
        ex = _dot(dm_ref[...], gs[c * C:(c + 1) * C])
        e_s[c] = jnp.exp(ex[:, :GLA_DK] + ex[:, GLA_DK:])

    for c in range(nchunk):
        rows = slice(c * C, (c + 1) * C)
        q = q_ref[0, rows, :].astype(F32) * (GLA_DK ** -0.5)
        k = k_ref[0, rows, :].astype(F32)
        v = v_ref[0, rows, :]
        att = mk_ref[0] * _dot_nt(q.astype(BF16), k.astype(BF16))
        for lev in range(NL):
            e_l = e_s[c, (lev + 1) * C:(lev + 2) * C, :]
            att = att + mk_ref[lev + 1] * _dot_nt((q * e_l).astype(BF16), (k * e_l).astype(BF16))
        oi_s[rows, :] = _dot(att.astype(BF16), v)
        kv_s[c] = _dot_tn(v, (k * e_s[c, (NL + 1) * C:(NL + 2) * C, :]).astype(BF16))
        qe_s[rows, :] = (q * e_s[c, 0:C, :]).astype(BF16)

    st = st_ref[...]
    for c in (reversed(range(nchunk)) if reverse else range(nchunk)):
        rows = slice(c * C, (c + 1) * C)
        o = oi_s[rows, :] + _dot_nt(qe_s[rows, :], st.astype(BF16))
        dec = e_s[c, 0:1, :] if reverse else e_s[c, C - 1:C, :]
        st = dec * st + kv_s[c]
        if finalize:
            o = o + op_ref[0, rows, :]
            y = o * lax.rsqrt(jnp.mean(o * o, axis=-1, keepdims=True) + EPS) * nw_ref[...]
            r = r_ref[0, rows, :].astype(F32)
            o_ref[0, rows, :] = (y * _silu(r)).astype(o_ref.dtype)
        else:
            o_ref[0, rows, :] = o.astype(o_ref.dtype)
    st_ref[...] = st

    @pl.when(blk == pl.num_programs(2) - 1)
    def _():
        sf_ref[0, 0] = st


def _gla_pass(pm, ps, up, bias, s0, *, reverse, finalize, TB, o_prev=None, norm_w=None):
    B, L, _ = pm.shape
    C = GLA_CHUNK
    nb = L // TB
    dmat, masks = _gla_consts(C, reverse)
    tmap = (lambda i: nb - 1 - i) if reverse else (lambda i: i)
    qb, kb = P_Q // GLA_DK, P_K // GLA_DK
    vb, rb = P_V // GLA_DV, P_R // GLA_DV
    in_specs = [
        pl.BlockSpec((1, TB, GLA_DK), lambda b, h, i: (b, tmap(i), qb + h)),
        pl.BlockSpec((1, TB, GLA_DK), lambda b, h, i: (b, tmap(i), kb + h)),
        pl.BlockSpec((1, TB, GLA_DV), lambda b, h, i: (b, tmap(i), vb + h)),
        pl.BlockSpec((1, TB, P_SMALL), lambda b, h, i: (b, tmap(i), 0)),
        pl.BlockSpec((GLA_RANK, GLA_DK), lambda b, h, i: (0, h)),
        pl.BlockSpec((1, GLA_DK), lambda b, h, i: (0, h)),
        _const_spec(dmat.shape),
        _const_spec(masks.shape),
        pl.BlockSpec((1, 1, GLA_DV, GLA_DK), lambda b, h, i: (b, h, 0, 0)),
    ]
    args = [pm, pm, pm, ps, up, bias, dmat, masks, s0]
    if finalize:
        in_specs += [
            pl.BlockSpec((1, TB, GLA_DV), lambda b, h, i: (b, tmap(i), rb + h)),
            pl.BlockSpec((1, TB, GLA_DV), lambda b, h, i: (b, tmap(i), h)),
            _const_spec((1, GLA_DV)),
        ]
        args += [pm, o_prev, norm_w]
    kern = functools.partial(_gla_kernel, C=C, TB=TB, reverse=reverse, finalize=finalize,
                             lr_off=S_LRB if reverse else S_LRF)
    return pl.pallas_call(
        kern,
        grid=(B, GLA_HEADS, nb),
        in_specs=in_specs,
        out_specs=[pl.BlockSpec((1, TB, GLA_DV), lambda b, h, i: (b, tmap(i), h)),
                   pl.BlockSpec((1, 1, GLA_DV, GLA_DK), lambda b, h, i: (b, h, 0, 0))],
        out_shape=[jax.ShapeDtypeStruct((B, L, GLA_V), BF16 if finalize else F32),
                   jax.ShapeDtypeStruct((B, GLA_HEADS, GLA_DV, GLA_DK), F32)],
        scratch_shapes=[pltpu.VMEM((GLA_DV, GLA_DK), F32),
                        pltpu.VMEM((TB // C, dmat.shape[0], GLA_DK), F32),
                        pltpu.VMEM((TB, GLA_DV), F32),
                        pltpu.VMEM((TB // C, GLA_DV, GLA_DK), F32),
                        pltpu.VMEM((TB, GLA_DK), BF16)],
        compiler_params=_params(("parallel", "parallel", "arbitrary")),
        name="gla_" + ("rev" if reverse else "fwd") + ("_fin" if finalize else ""),
    )(*args)


def _conv_silu(u, w, b, pos, rowlen):
    T = u.shape[0]
    out = b + u * w[CONV_LEFT:CONV_LEFT + 1, :]
    for j in range(SSM_CONV):
        off = j - CONV_LEFT
        if off == 0:
            continue
        shifted = pltpu.roll(u, (-off) % T, 0)
        ok = (pos + off >= 0) & (pos + off < rowlen)
        out = out + jnp.where(ok, shifted, 0.0) * w[j:j + 1, :]
    return _silu(out)


def _ssd_kernel(*refs, C, TB, rowlen, reverse, finalize, conv, emit, lane0):
    refs = list(refs)
    xs_ref, bm_ref, cm_ref, ps_ref = refs[:4]
    refs = refs[4:]
    if conv:
        wx_ref, wb_ref, wc_ref, bx_ref, bb_ref, bc_ref = refs[:6]
        refs = refs[6:]
    dtb_ref, alog_ref, tri_ref, s0_ref = refs[:4]
    refs = refs[4:]
    if finalize:
        z_ref, yp_ref, dsk_ref, nw_ref = refs[:4]
        refs = refs[4:]
    y_ref, sf_ref = refs[:2]
    refs = refs[2:]
    if emit:
        xo_ref, bo_ref, co_ref = refs[:3]
        refs = refs[3:]
    st_ref, xc_s, bc_s, cc_s, ya_s = refs
    nchunk = TB // C
    blk = pl.program_id(1)
    G, HPG, P, N = SSM_GROUPS, SSM_HPG, SSM_HEADDIM, SSM_STATE

    @pl.when(blk == 0)
    def _():
        st_ref[...] = s0_ref[0]

    if conv:
        pos = lax.broadcasted_iota(jnp.int32, (TB, 1), 0) % rowlen
        xc_s[...] = _conv_silu(xs_ref[0].astype(F32), wx_ref[...], bx_ref[...], pos, rowlen).astype(BF16)
        bc_s[...] = _conv_silu(bm_ref[0].astype(F32), wb_ref[...], bb_ref[...], pos, rowlen).astype(BF16)
        cc_s[...] = _conv_silu(cm_ref[0].astype(F32), wc_ref[...], bc_ref[...], pos, rowlen).astype(BF16)
        if emit:
            xo_ref[0] = xc_s[...]
            bo_ref[0] = bc_s[...]
            co_ref[0] = cc_s[...]
    else:
        xc_s[...] = xs_ref[0]
        bc_s[...] = bm_ref[0]
        cc_s[...] = cm_ref[0]

    neg_a = -jnp.exp(alog_ref[...])
    ti = lax.broadcasted_iota(jnp.int32, (C, C), 0)
    si = lax.broadcasted_iota(jnp.int32, (C, C), 1)
    keep = (si >= ti) if reverse else (si <= ti)
    lo = lax.broadcasted_iota(jnp.int32, (1, 2 * P), 1) < P
    last = 0 if reverse else C - 1
    nheads = G * HPG

    for c in (reversed(range(nchunk)) if reverse else range(nchunk)):
        rows = slice(c * C, (c + 1) * C)
        dt = _softplus(ps_ref[0, rows, :] + dtb_ref[...])
        cum = _dot01(tri_ref[...], dt * neg_a) * LOG2E
        cum_t = jnp.transpose(cum)[lane0:lane0 + nheads, :]
        dt_t = jnp.transpose(dt)[lane0:lane0 + nheads, :]
        w_t = dt_t * jnp.exp2(cum_t[:, last:last + 1] - cum_t)
        cdl_t = cum_t - jnp.log2(dt_t)
        dec_all = jnp.exp2(cum[last:last + 1, :])
        for g in range(G):
            bm_g = bc_s[rows, g * N:(g + 1) * N]
            cm_g = cc_s[rows, g * N:(g + 1) * N]
            cb = _dot_nt(cm_g, bm_g)
            bm_t = jnp.transpose(bm_g.astype(F32))
            cm_f = cm_g.astype(F32)
            for pr in range(HPG // 2):
                lanes = slice(g * SSM_GW + pr * 2 * P, g * SSM_GW + (pr + 1) * 2 * P)
                x_pair = xc_s[rows, lanes]
                s_pair = st_ref[g, :, pr * 2 * P:(pr + 1) * 2 * P]
                s_bf = s_pair.astype(BF16)
                y = None
                ds = None
                decs = []
                for half in range(2):
                    j = g * HPG + pr * 2 + half
                    lane = lane0 + j
                    sel = lo if half == 0 else jnp.logical_not(lo)
                    bc_ = jnp.broadcast_to(cum[:, lane:lane + 1], (C, C))
                    m = cb * jnp.exp2(jnp.where(keep, bc_ - cdl_t[j:j + 1, :], NEG_BIG))
                    cd = cm_f * jnp.exp2(bc_)
                    lhs = jnp.concatenate([m, cd], axis=1).astype(BF16)
                    rhs = jnp.concatenate([jnp.where(sel, x_pair, jnp.zeros_like(x_pair)),
                                           jnp.where(sel, s_bf, jnp.zeros_like(s_bf))], axis=0)
                    yh = _dot(lhs, rhs)
                    dh = _dot((bm_t * w_t[j:j + 1, :]).astype(BF16),
                              jnp.where(sel, x_pair, jnp.zeros_like(x_pair)))
                    y = yh if y is None else y + yh
                    ds = dh if ds is None else ds + dh
                    decs.append(dec_all[:, lane:lane + 1])
                ya_s[rows, lanes] = y
                dec = jnp.where(lo, decs[0], decs[1])
                st_ref[g, :, pr * 2 * P:(pr + 1) * 2 * P] = dec * s_pair + ds

    if finalize:
        for g in range(G):
            lanes = slice(g * SSM_GW, (g + 1) * SSM_GW)
            y = ya_s[:, lanes] + yp_ref[0, :, lanes] + dsk_ref[:, lanes] * xc_s[:, lanes].astype(F32)
            y = y * _silu(z_ref[0, :, lanes].astype(F32))
            y = y * lax.rsqrt(jnp.mean(y * y, axis=-1, keepdims=True) + EPS) * nw_ref[:, lanes]
            y_ref[0, :, lanes] = y.astype(y_ref.dtype)
    else:
        y_ref[0] = ya_s[...]

    @pl.when(blk == pl.num_programs(1) - 1)
    def _():
        sf_ref[0] = st_ref[...]


def _ssd_pass(src, ps, conv_w, conv_b, dt_bias, a_log, s0, *, reverse, finalize, TB, rowlen, emit=False,
              y_prev=None, z_src=None, d_skip_x=None, norm_w=None):
    (xs_a, xs_o), (bm_a, bm_o), (cm_a, cm_o) = src
    B, L, _ = ps.shape
    C = SSD_CHUNK
    nb = L // TB
    G = SSM_GROUPS
    conv = conv_w is not None
    tmap = (lambda i: nb - 1 - i) if reverse else (lambda i: i)
    idx = np.arange(C)
    tri = (idx[None, :] >= idx[:, None]) if reverse else (idx[None, :] <= idx[:, None])
    tri = jnp.asarray(tri.astype(np.float32), BF16)
    in_specs = [
        pl.BlockSpec((1, TB, SSM_INNER), lambda b, i: (b, tmap(i), xs_o)),
        pl.BlockSpec((1, TB, SSM_BC), lambda b, i: (b, tmap(i), bm_o)),
        pl.BlockSpec((1, TB, SSM_BC), lambda b, i: (b, tmap(i), cm_o)),
        pl.BlockSpec((1, TB, P_SMALL), lambda b, i: (b, tmap(i), 0)),
    ]
    args = [xs_a, bm_a, cm_a, ps]
    if conv:
        nx = SSM_INNER // SSM_BC
        in_specs += [
            pl.BlockSpec((SSM_CONV, SSM_INNER), lambda b, i: (0, 0)),
            pl.BlockSpec((SSM_CONV, SSM_BC), lambda b, i: (0, nx)),
            pl.BlockSpec((SSM_CONV, SSM_BC), lambda b, i: (0, nx + 1)),
            pl.BlockSpec((1, SSM_INNER), lambda b, i: (0, 0)),
            pl.BlockSpec((1, SSM_BC), lambda b, i: (0, nx)),
            pl.BlockSpec((1, SSM_BC), lambda b, i: (0, nx + 1)),
        ]
        args += [conv_w, conv_w, conv_w, conv_b, conv_b, conv_b]
    in_specs += [_const_spec((1, P_SMALL)), _const_spec((1, P_SMALL)), _const_spec((C, C)),
                 pl.BlockSpec((1, G, SSM_STATE, SSM_GW), lambda b, i: (b, 0, 0, 0))]
    args += [dt_bias, a_log, tri, s0]
    if finalize:
        z_a, z_o = z_src
        in_specs += [
            pl.BlockSpec((1, TB, SSM_INNER), lambda b, i: (b, tmap(i), z_o)),
            pl.BlockSpec((1, TB, SSM_INNER), lambda b, i: (b, tmap(i), 0)),
            _const_spec((1, SSM_INNER)), _const_spec((1, SSM_INNER)),
        ]
        args += [z_a, y_prev, d_skip_x, norm_w]
    out_specs = [pl.BlockSpec((1, TB, SSM_INNER), lambda b, i: (b, tmap(i), 0)),
                 pl.BlockSpec((1, G, SSM_STATE, SSM_GW), lambda b, i: (b, 0, 0, 0))]
    out_shape = [jax.ShapeDtypeStruct((B, L, SSM_INNER), BF16 if finalize else F32),
                 jax.ShapeDtypeStruct((B, G, SSM_STATE, SSM_GW), F32)]
    if emit:
        out_specs += [pl.BlockSpec((1, TB, SSM_INNER), lambda b, i: (b, tmap(i), 0)),
                      pl.BlockSpec((1, TB, SSM_BC), lambda b, i: (b, tmap(i), 0)),
                      pl.BlockSpec((1, TB, SSM_BC), lambda b, i: (b, tmap(i), 0))]
        out_shape += [jax.ShapeDtypeStruct((B, L, SSM_INNER), BF16),
                      jax.ShapeDtypeStruct((B, L, SSM_BC), BF16),
                      jax.ShapeDtypeStruct((B, L, SSM_BC), BF16)]
    kern = functools.partial(_ssd_kernel, C=C, TB=TB, rowlen=rowlen, reverse=reverse, finalize=finalize,
                             conv=conv, emit=emit, lane0=S_DTB if reverse else S_DTF)
    return pl.pallas_call(
        kern,
        grid=(B, nb),
        in_specs=in_specs,
        out_specs=out_specs,
        out_shape=out_shape,
        scratch_shapes=[pltpu.VMEM((G, SSM_STATE, SSM_GW), F32),
                        pltpu.VMEM((TB, SSM_INNER), BF16),
                        pltpu.VMEM((TB, SSM_BC), BF16),
                        pltpu.VMEM((TB, SSM_BC), BF16),
                        pltpu.VMEM((TB, SSM_INNER), F32)],
        compiler_params=_params(("parallel", "arbitrary")),
        name="ssd_" + ("rev" if reverse else "fwd") + ("_fin" if finalize else ""),
    )(*args)


def _merge_kernel(oa_ref, ob_ref, ga_ref, gb_ref, x_ref, g1_ref, wpa_ref, wpb_ref, wout_ref, h_ref):
    ya = _dot(oa_ref[0], wpa_ref[...])
    yb = _dot(ob_ref[0], wpb_ref[...])
    m = _sigmoid(ga_ref[0].astype(F32)) * ya + _sigmoid(gb_ref[0].astype(F32)) * yb
    mix = _dot(m.astype(BF16), wout_ref[...])
    h_ref[0] = x_ref[0] + g1_ref[0] * mix


def _merge(o_a, o_b, pm, x, g1, w_pa, w_pb, w_out, tm):
    B, L, _ = x.shape
    gab, gbb = P_GA // D_MODEL, P_GB // D_MODEL
    return pl.pallas_call(
        _merge_kernel,
        grid=(B, L // tm),
        in_specs=[pl.BlockSpec((1, tm, GLA_V), lambda b, i: (b, i, 0)),
                  pl.BlockSpec((1, tm, SSM_INNER), lambda b, i: (b, i, 0)),
                  pl.BlockSpec((1, tm, D_MODEL), lambda b, i: (b, i, gab)),
                  pl.BlockSpec((1, tm, D_MODEL), lambda b, i: (b, i, gbb)),
                  pl.BlockSpec((1, tm, D_MODEL), lambda b, i: (b, i, 0)),
                  pl.BlockSpec((1, 1, D_MODEL), lambda b, i: (b, 0, 0)),
                  _const_spec(w_pa.shape), _const_spec(w_pb.shape), _const_spec(w_out.shape)],
        out_specs=pl.BlockSpec((1, tm, D_MODEL), lambda b, i: (b, i, 0)),
        out_shape=jax.ShapeDtypeStruct((B, L, D_MODEL), F32),
        compiler_params=_params(("parallel", "parallel")),
        name="merge",
    )(o_a, o_b, pm, pm, x, g1, w_pa, w_pb, w_out)


FFN_CHUNK = 256


def _ffn_kernel(h_ref, mod_ref, n2_ref, fw_ref, wg_ref, wu_ref, wd_ref, o_ref):
    h = h_ref[0]
    hn = h * lax.rsqrt(jnp.mean(h * h, axis=-1, keepdims=True) + EPS) * n2_ref[...]
    hn = (hn * (1.0 + mod_ref[0, 1:2, :]) + mod_ref[0, 0:1, :]).astype(BF16)
    acc = jnp.zeros(h.shape, F32)
    for c in range(D_FF // FFN_CHUNK):
        cols = slice(c * FFN_CHUNK, (c + 1) * FFN_CHUNK)
        gt = _dot(hn, wg_ref[:, cols])
        up = _dot(hn, wu_ref[:, cols])
        acc = acc + _dot((_silu(gt) * up).astype(BF16), wd_ref[cols, :])
    h2 = h + mod_ref[0, 2:3, :] * acc
    o_ref[0] = h2 * lax.rsqrt(jnp.mean(h2 * h2, axis=-1, keepdims=True) + EPS) * fw_ref[...]


def _ffn(h, mod, n2w, fw, w_gate, w_up, w_down, tm):
    B, L, _ = h.shape
    return pl.pallas_call(
        _ffn_kernel,
        grid=(B, L // tm),
        in_specs=[pl.BlockSpec((1, tm, D_MODEL), lambda b, i: (b, i, 0)),
                  pl.BlockSpec((1, 3, D_MODEL), lambda b, i: (b, 0, 0)),
                  _const_spec((1, D_MODEL)), _const_spec((1, D_MODEL)),
                  pl.BlockSpec(w_gate.shape, lambda b, i: (0, 0), pipeline_mode=pl.Buffered(1)),
                  pl.BlockSpec(w_up.shape, lambda b, i: (0, 0), pipeline_mode=pl.Buffered(1)),
                  pl.BlockSpec(w_down.shape, lambda b, i: (0, 0), pipeline_mode=pl.Buffered(1))],
        out_specs=pl.BlockSpec((1, tm, D_MODEL), lambda b, i: (b, i, 0)),
        out_shape=jax.ShapeDtypeStruct((B, L, D_MODEL), F32),
        compiler_params=_params(("parallel", "parallel")),
        name="ffn",
    )(h, mod, n2w, fw, w_gate, w_up, w_down)


def _pick_block(L, pref):
    tb = min(L, pref)
    assert L % tb == 0
    return tb


def kernel(x, c, ctx, c_ctx, w_ada, b_ada, norm1_w, w_in, gla_up_f, gla_bias_f, gla_up_b, gla_bias_b,
           gla_norm_w, conv_w, conv_b, dt_bias_f, dt_bias_b, a_log_f, a_log_b, d_skip, ssm_norm_w,
           w_pa, w_pb, w_out, norm2_w, w_gate, w_up, w_down, final_norm_w):
    B, L, D = x.shape
    Lc = ctx.shape[1]
    depth = w_ada.shape[0]
    assert depth == 1 and D == D_MODEL
    assert L % GRID_W == 0 and L % SSD_CHUNK == 0 and Lc % SSD_CHUNK == 0
    lay = 0

    nrow = -(-(B + 1) // 8) * 8
    cc = jnp.zeros((nrow, D), F32).at[:B].set(c).at[B].set(c_ctx)
    ada = _ada(cc, w_ada[lay], b_ada[lay][None, :])
    sh1, sc1, g1, sh2, sc2, g2 = [ada[:, i * D:(i + 1) * D] for i in range(6)]
    mod1 = jnp.stack([sh1[:B], sc1[:B]], axis=1)
    mod1_c = jnp.broadcast_to(jnp.stack([sh1[B], sc1[B]])[None], (B, 2, D))
    mod2 = jnp.stack([sh2[:B], sc2[:B], g2[:B]], axis=1)
    g1_l = g1[:B, None, :]

    wi = w_in[lay]
    o = _IN_OFF
    w_main = jnp.concatenate([wi[:, o[6]:o[8]], wi[:, o[0]:o[4]], wi[:, o[8]:o[10]], wi[:, o[12]:o[14]]],
                             axis=1).astype(BF16)
    w_small = jnp.concatenate([wi[:, o[4]:o[6]], wi[:, o[10]:o[12]],
                               jnp.zeros((D, P_SMALL - 2 * GLA_RANK - 2 * SSM_HEADS), F32)], axis=1).astype(BF16)
    nw1 = norm1_w[lay][None, :]

    at_lanes = lambda p, off: jnp.zeros((1, P_SMALL), F32).at[0, off:off + SSM_HEADS].set(p[lay])
    dtb_f, al_f = at_lanes(dt_bias_f, S_DTF), at_lanes(a_log_f, S_DTF)
    dtb_b, al_b = at_lanes(dt_bias_b, S_DTB), at_lanes(a_log_b, S_DTB)
    cw, cb_ = conv_w[lay], conv_b[lay][None, :]
    up_f, up_b = gla_up_f[lay], gla_up_b[lay]
    bi_f, bi_b = gla_bias_f[lay][None, :], gla_bias_b[lay][None, :]

    gla_zero = jnp.zeros((B, GLA_HEADS, GLA_DV, GLA_DK), F32)
    ssd_zero = jnp.zeros((B, SSM_GROUPS, SSM_STATE, SSM_GW), F32)

    pm_c, ps_c = _inproj(ctx, mod1_c, nw1, w_main, w_small, _pick_block(Lc, 256))
    tbc = _pick_block(Lc, 256)
    _, sg_f = _gla_pass(pm_c, ps_c, up_f, bi_f, gla_zero, reverse=False, finalize=False, TB=tbc)
    _, sg_b = _gla_pass(pm_c, ps_c, up_b, bi_b, gla_zero, reverse=True, finalize=False, TB=tbc)
    src_c = ((pm_c, P_XS // SSM_INNER), (pm_c, P_BM // SSM_BC), (pm_c, P_CM // SSM_BC))
    _, ss_f = _ssd_pass(src_c, ps_c, cw, cb_, dtb_f, al_f, ssd_zero, reverse=False, finalize=False,
                        TB=Lc, rowlen=Lc)
    _, ss_b = _ssd_pass(src_c, ps_c, cw, cb_, dtb_b, al_b, ssd_zero, reverse=True, finalize=False,
                        TB=Lc, rowlen=Lc)

    pm, ps = _inproj(x, mod1, nw1, w_main, w_small, _pick_block(L, 1024))
    tbg = _pick_block(L, 512)
    og_f, _ = _gla_pass(pm, ps, up_f, bi_f, sg_f, reverse=False, finalize=False, TB=tbg)
    o_a, _ = _gla_pass(pm, ps, up_b, bi_b, sg_b, reverse=True, finalize=True, TB=tbg,
                       o_prev=og_f, norm_w=gla_norm_w[lay][None, :])
    tbs = _pick_block(L, 256)
    src = ((pm, P_XS // SSM_INNER), (pm, P_BM // SSM_BC), (pm, P_CM // SSM_BC))
    ys_f, _, xc, bc, cc = _ssd_pass(src, ps, cw, cb_, dtb_f, al_f, ss_f, reverse=False, finalize=False,
                                    TB=tbs, rowlen=GRID_W, emit=True)
    o_b, _ = _ssd_pass(((xc, 0), (bc, 0), (cc, 0)), ps, None, None, dtb_b, al_b, ss_b, reverse=True,
                       finalize=True, TB=tbs, rowlen=GRID_W, y_prev=ys_f, z_src=(pm, P_Z // SSM_INNER),
                       d_skip_x=jnp.repeat(d_skip[lay], SSM_HEADDIM)[None, :],
                       norm_w=ssm_norm_w[lay][None, :])

    h = _merge(o_a, o_b, pm, x, g1_l, w_pa[lay].astype(BF16), w_pb[lay].astype(BF16),
               w_out[lay].astype(BF16), _pick_block(L, 512))
    return _ffn(h, mod2, norm2_w[lay][None, :], final_norm_w[None, :], w_gate[lay].astype(BF16),
                w_up[lay].astype(BF16), w_down[lay].astype(BF16), _pick_block(L, 512))
```

```python
import functools

import numpy as np
import jax
import jax.numpy as jnp
from jax import lax
from jax.experimental import pallas as pl
from jax.experimental.pallas import tpu as pltpu

F32 = jnp.float32
BF16 = jnp.bfloat16

D_MODEL = 1024
GRID_W = 64
EPS = 1e-6

GLA_HEADS = 4
GLA_DK = 128
GLA_DV = 256
GLA_QK = GLA_HEADS * GLA_DK
GLA_V = GLA_HEADS * GLA_DV
GLA_RANK = 16
GLA_TAU = 16.0

SSM_INNER = 2 * D_MODEL
SSM_HEADDIM = 64
SSM_HEADS = SSM_INNER // SSM_HEADDIM
SSM_GROUPS = 4
SSM_HPG = SSM_HEADS // SSM_GROUPS
SSM_STATE = 128
SSM_BC = SSM_GROUPS * SSM_STATE
SSM_CONV = 4
CONV_LEFT = 2
SSM_GW = SSM_HPG * SSM_HEADDIM

D_FF = ((8 * D_MODEL // 3 + 255) // 256) * 256

_IN_WIDTHS = (GLA_QK, GLA_QK, GLA_V, GLA_V, GLA_RANK, GLA_RANK,
              SSM_INNER, SSM_INNER, SSM_BC, SSM_BC, SSM_HEADS, SSM_HEADS, D_MODEL, D_MODEL)
_IN_OFF = np.concatenate([[0], np.cumsum(_IN_WIDTHS)]).tolist()

P_Z, P_XS = 0, 2048
P_Q, P_K, P_V, P_R = 4096, 4608, 5120, 6144
P_BM, P_CM = 7168, 7680
P_GA, P_GB = 8192, 9216
P_MAIN = 10240
S_LRF, S_LRB, S_DTF, S_DTB = 0, 16, 32, 64
P_SMALL = 128

GLA_CHUNK = 128
SSD_CHUNK = 128

VMEM_LIMIT = 56 * 1024 * 1024
NEG_BIG = -1e30
LOG2E = 1.4426950408889634


def _sigmoid(x):
    return 1.0 / (1.0 + jnp.exp(-x))


def _silu(x):
    return x * _sigmoid(x)


def _softplus(x):
    return jnp.maximum(x, 0.0) + jnp.log(1.0 + jnp.exp(-jnp.abs(x)))


def _split3(x):
    hi = x.astype(BF16)
    r1 = x - hi.astype(F32)
    mid = r1.astype(BF16)
    lo = (r1 - mid.astype(F32)).astype(BF16)
    return hi, mid, lo


def _dot(a, b):
    return jnp.dot(a, b, preferred_element_type=F32)


def _dot_nt(a, b):
    return lax.dot_general(a, b, (((1,), (1,)), ((), ())), preferred_element_type=F32)


def _dot_tn(a, b):
    return lax.dot_general(a, b, (((0,), (0,)), ((), ())), preferred_element_type=F32)


def _dot01(m01, x):
    hi, mid, lo = _split3(x)
    return _dot(m01, hi) + _dot(m01, mid) + _dot(m01, lo)


def _params(sem):
    return pltpu.CompilerParams(dimension_semantics=sem, vmem_limit_bytes=VMEM_LIMIT)


def _const_spec(shape):
    n = len(shape)
    return pl.BlockSpec(shape, lambda *_: (0,) * n)


def _ada_kernel(c_ref, w_ref, b_ref, o_ref):
    s = _silu(c_ref[...])
    o_ref[...] = jnp.dot(s, w_ref[...], preferred_element_type=F32,
                         precision=lax.Precision.HIGHEST) + b_ref[...]


def _ada(cc, w, b):
    rows = cc.shape[0]
    n = w.shape[1]
    tn = 1536
    return pl.pallas_call(
        _ada_kernel,
        grid=(n // tn,),
        in_specs=[pl.BlockSpec((rows, D_MODEL), lambda j: (0, 0)),
                  pl.BlockSpec((D_MODEL, tn), lambda j: (0, j)),
                  pl.BlockSpec((1, tn), lambda j: (0, j))],
        out_specs=pl.BlockSpec((rows, tn), lambda j: (0, j)),
        out_shape=jax.ShapeDtypeStruct((rows, n), F32),
        compiler_params=_params(("arbitrary",)),
        name="ada",
    )(cc, w, b)


def _inproj_kernel(x_ref, mod_ref, nw_ref, wm_ref, ws_ref, om_ref, os_ref, xn_ref):
    @pl.when(pl.program_id(2) == 0)
    def _():
        x = x_ref[0]
        y = x * lax.rsqrt(jnp.mean(x * x, axis=-1, keepdims=True) + EPS) * nw_ref[...]
        y = y * (1.0 + mod_ref[0, 1:2, :]) + mod_ref[0, 0:1, :]
        xn = y.astype(BF16)
        xn_ref[...] = xn
        os_ref[0] = _dot(xn, ws_ref[...])

    om_ref[0] = _dot(xn_ref[...], wm_ref[...]).astype(BF16)


def _inproj(x, mod, nw, w_main, w_small, tm):
    B, L, _ = x.shape
    tn = 2048
    return pl.pallas_call(
        _inproj_kernel,
        grid=(B, L // tm, P_MAIN // tn),
        in_specs=[pl.BlockSpec((1, tm, D_MODEL), lambda b, i, j: (b, i, 0)),
                  pl.BlockSpec((1, 2, D_MODEL), lambda b, i, j: (b, 0, 0)),
                  pl.BlockSpec((1, D_MODEL), lambda b, i, j: (0, 0)),
                  pl.BlockSpec((D_MODEL, tn), lambda b, i, j: (0, j)),
                  pl.BlockSpec((D_MODEL, P_SMALL), lambda b, i, j: (0, 0))],
        out_specs=[pl.BlockSpec((1, tm, tn), lambda b, i, j: (b, i, j)),
                   pl.BlockSpec((1, tm, P_SMALL), lambda b, i, j: (b, i, 0))],
        out_shape=[jax.ShapeDtypeStruct((B, L, P_MAIN), BF16),
                   jax.ShapeDtypeStruct((B, L, P_SMALL), F32)],
        scratch_shapes=[pltpu.VMEM((tm, D_MODEL), BF16)],
        compiler_params=_params(("parallel", "parallel", "arbitrary")),
        name="inproj",
    )(x, mod, nw, w_main, w_small)


GLA_MXU_LEVELS = 3
GLA_GROUP = 4


def _gla_consts(C, reverse):
    NL = int(np.log2(C))
    idx = np.arange(C)
    tri = (idx[None, :] <= idx[:, None]).astype(np.float32)
    mats = [tri]
    masks = [np.eye(C, dtype=np.float32)]
    refs, signs = [], []
    for lev in range(NL):
        h = 1 << lev
        blk = idx // (2 * h)
        half = (idx // h) % 2
        ref = blk * 2 * h + h - 1
        if lev < GLA_MXU_LEVELS:
            d = tri - tri[ref]
            d[half == 0] *= -1.0
            mats.append(d)
        else:
            sg = np.where(half == 1, 1.0, -1.0).astype(np.float32)
            if reverse:
                ref, sg = (C - 1 - ref)[::-1], sg[::-1]
            refs.append([int(ref[m * 2 * h]) for m in range(C // (2 * h))])
            signs.append(np.broadcast_to(sg[:, None], (C, GLA_DK)))
        masks.append(((blk[:, None] == blk[None, :]) & (half[:, None] == 1)
                      & (half[None, :] == 0)).astype(np.float32))
    if reverse:
        mats = [m[::-1, ::-1] for m in mats]
        masks = [m[::-1, ::-1] for m in masks]
    dmat = np.concatenate(mats, axis=0)
    return (jnp.asarray(dmat, BF16), jnp.asarray(np.stack(masks), F32),
            jnp.asarray(np.stack(signs), F32), refs)


def _gla_kernel(q_ref, k_ref, v_ref, lr_ref, up_ref, bias_ref, dm_ref, mk_ref, sg_ref, s0_ref, *rest,
                C, TB, reverse, finalize, lr_off, refs):
    if finalize:
        r_ref, op_ref, nw_ref, o_ref, sf_ref, st_ref, e_s, b_s, oi_s, kv_s = rest
    else:
        o_ref, sf_ref, st_ref, e_s, b_s, oi_s, kv_s = rest
    NL = int(np.log2(C))
    nchunk = TB // C
    blk = pl.program_id(2)
    last = 0 if reverse else C - 1

    @pl.when(blk == 0)
    def _():
        st_ref[...] = s0_ref[0, 0]

    lr = lr_ref[0][:, lr_off:lr_off + GLA_RANK]
    pre = _dot(lr.astype(BF16), up_ref[...].astype(BF16)) + bias_ref[...]
    g = -_softplus(-pre) * (1.0 / GLA_TAU)
    g_hi = g.astype(BF16)
    g_lo = (g - g_hi.astype(F32)).astype(BF16)
    gs = jnp.concatenate([g_hi, g_lo], axis=1)

    for c in range(nchunk):
        ex = _dot(dm_ref[...], gs[c * C:(c + 1) * C])
        ex = ex[:, :GLA_DK] + ex[:, GLA_DK:]
        b = ex[0:C]
        b_s[c] = b
        e_s[c, 0:GLA_MXU_LEVELS * C, :] = jnp.exp(ex[C:(1 + GLA_MXU_LEVELS) * C])
        for li, lev in enumerate(range(GLA_MXU_LEVELS, NL)):
            h2 = 2 << lev
            bref = jnp.concatenate([jnp.broadcast_to(b[r:r + 1, :], (h2, GLA_DK)) for r in refs[li]], axis=0)
            e_s[c, lev * C:(lev + 1) * C, :] = jnp.exp(sg_ref[li] * (b - bref))
        e_s[c, NL * C:(NL + 1) * C, :] = jnp.exp(b[last:last + 1, :] - b)

    for c in range(nchunk):
        rows = slice(c * C, (c + 1) * C)
        q = q_ref[0, rows, :].astype(F32) * (GLA_DK ** -0.5)
        k = k_ref[0, rows, :].astype(F32)
        v = v_ref[0, rows, :]
        att = mk_ref[0] * _dot_nt(q.astype(BF16), k.astype(BF16))
        for lev in range(NL):
            e_l = e_s[c, lev * C:(lev + 1) * C, :]
            att = att + mk_ref[lev + 1] * _dot_nt((q * e_l).astype(BF16), (k * e_l).astype(BF16))
        oi_s[rows, :] = _dot(att.astype(BF16), v)
        kv_s[c] = _dot_tn(v, (k * e_s[c, NL * C:(NL + 1) * C, :]).astype(BF16))

    order = list(reversed(range(nchunk))) if reverse else list(range(nchunk))
    st = st_ref[...]
    for g0 in range(0, nchunk, GLA_GROUP):
        grp = order[g0:g0 + GLA_GROUP]
        wcat = jnp.concatenate([st.astype(BF16)] + [kv_s[c].astype(BF16) for c in grp[:-1]], axis=1)
        for i, c in enumerate(grp):
            rows = slice(c * C, (c + 1) * C)
            q = q_ref[0, rows, :].astype(F32) * (GLA_DK ** -0.5)
            expo = b_s[c]
            pieces = []
            for j in range(i - 1, -2, -1):
                pieces.insert(0, (q * jnp.exp(expo)).astype(BF16))
                if j >= 0:
                    expo = expo + b_s[grp[j], last:last + 1, :]
            lhs = jnp.concatenate(pieces, axis=1)
            o = oi_s[rows, :] + _dot_nt(lhs, wcat[:, :(i + 1) * GLA_DK])
            if finalize:
                o = o + op_ref[0, rows, :]
                y = o * lax.rsqrt(jnp.mean(o * o, axis=-1, keepdims=True) + EPS) * nw_ref[...]
                r = r_ref[0, rows, :].astype(F32)
                o_ref[0, rows, :] = (y * _silu(r)).astype(o_ref.dtype)
            else:
                o_ref[0, rows, :] = o.astype(o_ref.dtype)
        for c in grp:
            st = jnp.exp(b_s[c, last:last + 1, :]) * st + kv_s[c]
    st_ref[...] = st

    @pl.when(blk == pl.num_programs(2) - 1)
    def _():
        sf_ref[0, 0] = st


def _gla_pass(pm, ps, up, bias, s0, *, reverse, finalize, TB, o_prev=None, norm_w=None):
    B, L, _ = pm.shape
    C = GLA_CHUNK
    nb = L // TB
    assert C >= (2 << GLA_MXU_LEVELS)
    dmat, masks, signs, refs = _gla_consts(C, reverse)
    tmap = (lambda i: nb - 1 - i) if reverse else (lambda i: i)
    qb, kb = P_Q // GLA_DK, P_K // GLA_DK
    vb, rb = P_V // GLA_DV, P_R // GLA_DV
    in_specs = [
        pl.BlockSpec((1, TB, GLA_DK), lambda b, h, i: (b, tmap(i), qb + h)),
        pl.BlockSpec((1, TB, GLA_DK), lambda b, h, i: (b, tmap(i), kb + h)),
        pl.BlockSpec((1, TB, GLA_DV), lambda b, h, i: (b, tmap(i), vb + h)),
        pl.BlockSpec((1, TB, P_SMALL), lambda b, h, i: (b, tmap(i), 0)),
        pl.BlockSpec((GLA_RANK, GLA_DK), lambda b, h, i: (0, h)),
        pl.BlockSpec((1, GLA_DK), lambda b, h, i: (0, h)),
        _const_spec(dmat.shape),
        _const_spec(masks.shape),
        _const_spec(signs.shape),
        pl.BlockSpec((1, 1, GLA_DV, GLA_DK), lambda b, h, i: (b, h, 0, 0)),
    ]
    args = [pm, pm, pm, ps, up, bias, dmat, masks, signs, s0]
    if finalize:
        in_specs += [
            pl.BlockSpec((1, TB, GLA_DV), lambda b, h, i: (b, tmap(i), rb + h)),
            pl.BlockSpec((1, TB, GLA_DV), lambda b, h, i: (b, tmap(i), h)),
            _const_spec((1, GLA_DV)),
        ]
        args += [pm, o_prev, norm_w]
    kern = functools.partial(_gla_kernel, C=C, TB=TB, reverse=reverse, finalize=finalize,
                             lr_off=S_LRB if reverse else S_LRF, refs=refs)
    return pl.pallas_call(
        kern,
        grid=(B, GLA_HEADS, nb),
        in_specs=in_specs,
        out_specs=[pl.BlockSpec((1, TB, GLA_DV), lambda b, h, i: (b, tmap(i), h)),
                   pl.BlockSpec((1, 1, GLA_DV, GLA_DK), lambda b, h, i: (b, h, 0, 0))],
        out_shape=[jax.ShapeDtypeStruct((B, L, GLA_V), BF16 if finalize else F32),
                   jax.ShapeDtypeStruct((B, GLA_HEADS, GLA_DV, GLA_DK), F32)],
        scratch_shapes=[pltpu.VMEM((GLA_DV, GLA_DK), F32),
                        pltpu.VMEM((TB // C, (masks.shape[0]) * C, GLA_DK), F32),
                        pltpu.VMEM((TB // C, C, GLA_DK), F32),
                        pltpu.VMEM((TB, GLA_DV), F32),
                        pltpu.VMEM((TB // C, GLA_DV, GLA_DK), F32)],
        compiler_params=_params(("parallel", "parallel", "arbitrary")),
        name="gla_" + ("rev" if reverse else "fwd") + ("_fin" if finalize else ""),
    )(*args)


def _conv_shift_mats(T, rowlen):
    t = np.arange(T)
    mats = []
    for j in range(SSM_CONV):
        off = j - CONV_LEFT
        if off == 0:
            continue
        src = t + off
        ok = (src // rowlen == t // rowlen) & (src >= 0) & (src < T)
        m = np.zeros((T, T), np.float32)
        m[t[ok], src[ok]] = 1.0
        mats.append(m)
    return jnp.asarray(np.stack(mats), BF16)


def _conv_silu(u, w, b, sh_ref):
    acc = b + u.astype(F32) * w[CONV_LEFT:CONV_LEFT + 1, :]
    taps = [j for j in range(SSM_CONV) if j != CONV_LEFT]
    for i, j in enumerate(taps):
        acc = acc + _dot(sh_ref[i], u) * w[j:j + 1, :]
    return _silu(acc)


def _ssd_kernel(*refs, C, TB, reverse, finalize, conv, emit, lane0):
    refs = list(refs)
    xs_ref, bm_ref, cm_ref, ps_ref = refs[:4]
    refs = refs[4:]
    if conv:
        wx_ref, wb_ref, wc_ref, bx_ref, bb_ref, bc_ref, sh_ref = refs[:7]
        refs = refs[7:]
    dtb_ref, alog_ref, tri_ref, s0_ref = refs[:4]
    refs = refs[4:]
    if finalize:
        z_ref, yp_ref, dsk_ref, nw_ref = refs[:4]
        refs = refs[4:]
    y_ref, sf_ref = refs[:2]
    refs = refs[2:]
    if emit:
        xo_ref, bo_ref, co_ref = refs[:3]
        refs = refs[3:]
    st_ref, xc_s, bc_s, cc_s, ya_s = refs
    nchunk = TB // C
    blk = pl.program_id(1)
    G, HPG, P, N = SSM_GROUPS, SSM_HPG, SSM_HEADDIM, SSM_STATE

    @pl.when(blk == 0)
    def _():
        st_ref[...] = s0_ref[0]

    if conv:
        xc_s[...] = _conv_silu(xs_ref[0], wx_ref[...], bx_ref[...], sh_ref).astype(BF16)
        bc_s[...] = _conv_silu(bm_ref[0], wb_ref[...], bb_ref[...], sh_ref).astype(BF16)
        cc_s[...] = _conv_silu(cm_ref[0], wc_ref[...], bc_ref[...], sh_ref).astype(BF16)
        if emit:
            xo_ref[0] = xc_s[...]
            bo_ref[0] = bc_s[...]
            co_ref[0] = cc_s[...]
    else:
        xc_s[...] = xs_ref[0]
        bc_s[...] = bm_ref[0]
        cc_s[...] = cm_ref[0]

    neg_a = -jnp.exp(alog_ref[...])
    ti = lax.broadcasted_iota(jnp.int32, (C, C), 0)
    si = lax.broadcasted_iota(jnp.int32, (C, C), 1)
    keep = (si >= ti) if reverse else (si <= ti)
    lo = lax.broadcasted_iota(jnp.int32, (1, 2 * P), 1) < P
    last = 0 if reverse else C - 1
    nheads = G * HPG

    for c in (reversed(range(nchunk)) if reverse else range(nchunk)):
        rows = slice(c * C, (c + 1) * C)
        dt = _softplus(ps_ref[0, rows, :] + dtb_ref[...])
        cum = _dot01(tri_ref[...], dt * neg_a) * LOG2E
        cum_t = jnp.transpose(cum)[lane0:lane0 + nheads, :]
        dt_t = jnp.transpose(dt)[lane0:lane0 + nheads, :]
        w_t = dt_t * jnp.exp2(cum_t[:, last:last + 1] - cum_t)
        cdl_t = cum_t - jnp.log2(dt_t)
        dec_all = jnp.exp2(cum[last:last + 1, :])
        for g in range(G):
            bm_g = bc_s[rows, g * N:(g + 1) * N]
            cm_g = cc_s[rows, g * N:(g + 1) * N]
            cb = _dot_nt(cm_g, bm_g)
            bm_t = jnp.transpose(bm_g.astype(F32))
            cm_f = cm_g.astype(F32)
            for pr in range(HPG // 2):
                lanes = slice(g * SSM_GW + pr * 2 * P, g * SSM_GW + (pr + 1) * 2 * P)
                x_pair = xc_s[rows, lanes]
                s_pair = st_ref[g, :, pr * 2 * P:(pr + 1) * 2 * P]
                s_bf = s_pair.astype(BF16)
                y = None
                ds = None
                decs = []
                for half in range(2):
                    j = g * HPG + pr * 2 + half
                    lane = lane0 + j
                    sel = lo if half == 0 else jnp.logical_not(lo)
                    bc_ = jnp.broadcast_to(cum[:, lane:lane + 1], (C, C))
                    m = cb * jnp.exp2(jnp.where(keep, bc_ - cdl_t[j:j + 1, :], NEG_BIG))
                    cd = cm_f * jnp.exp2(bc_)
                    lhs = jnp.concatenate([m, cd], axis=1).astype(BF16)
                    rhs = jnp.concatenate([jnp.where(sel, x_pair, jnp.zeros_like(x_pair)),
                                           jnp.where(sel, s_bf, jnp.zeros_like(s_bf))], axis=0)
                    yh = _dot(lhs, rhs)
                    dh = _dot((bm_t * w_t[j:j + 1, :]).astype(BF16),
                              jnp.where(sel, x_pair, jnp.zeros_like(x_pair)))
                    y = yh if y is None else y + yh
                    ds = dh if ds is None else ds + dh
                    decs.append(dec_all[:, lane:lane + 1])
                ya_s[rows, lanes] = y
                dec = jnp.where(lo, decs[0], decs[1])
                st_ref[g, :, pr * 2 * P:(pr + 1) * 2 * P] = dec * s_pair + ds

    if finalize:
        for g in range(G):
            lanes = slice(g * SSM_GW, (g + 1) * SSM_GW)
            y = ya_s[:, lanes] + yp_ref[0, :, lanes] + dsk_ref[:, lanes] * xc_s[:, lanes].astype(F32)
            y = y * _silu(z_ref[0, :, lanes].astype(F32))
            y = y * lax.rsqrt(jnp.mean(y * y, axis=-1, keepdims=True) + EPS) * nw_ref[:, lanes]
            y_ref[0, :, lanes] = y.astype(y_ref.dtype)
    else:
        y_ref[0] = ya_s[...]

    @pl.when(blk == pl.num_programs(1) - 1)
    def _():
        sf_ref[0] = st_ref[...]


def _ssd_pass(src, ps, conv_w, conv_b, dt_bias, a_log, s0, *, reverse, finalize, TB, rowlen, emit=False,
              y_prev=None, z_src=None, d_skip_x=None, norm_w=None):
    (xs_a, xs_o), (bm_a, bm_o), (cm_a, cm_o) = src
    B, L, _ = ps.shape
    C = SSD_CHUNK
    nb = L // TB
    G = SSM_GROUPS
    conv = conv_w is not None
    tmap = (lambda i: nb - 1 - i) if reverse else (lambda i: i)
    idx = np.arange(C)
    tri = (idx[None, :] >= idx[:, None]) if reverse else (idx[None, :] <= idx[:, None])
    tri = jnp.asarray(tri.astype(np.float32), BF16)
    in_specs = [
        pl.BlockSpec((1, TB, SSM_INNER), lambda b, i: (b, tmap(i), xs_o)),
        pl.BlockSpec((1, TB, SSM_BC), lambda b, i: (b, tmap(i), bm_o)),
        pl.BlockSpec((1, TB, SSM_BC), lambda b, i: (b, tmap(i), cm_o)),
        pl.BlockSpec((1, TB, P_SMALL), lambda b, i: (b, tmap(i), 0)),
    ]
    args = [xs_a, bm_a, cm_a, ps]
    if conv:
        nx = SSM_INNER // SSM_BC
        in_specs += [
            pl.BlockSpec((SSM_CONV, SSM_INNER), lambda b, i: (0, 0)),
            pl.BlockSpec((SSM_CONV, SSM_BC), lambda b, i: (0, nx)),
            pl.BlockSpec((SSM_CONV, SSM_BC), lambda b, i: (0, nx + 1)),
            pl.BlockSpec((1, SSM_INNER), lambda b, i: (0, 0)),
            pl.BlockSpec((1, SSM_BC), lambda b, i: (0, nx)),
            pl.BlockSpec((1, SSM_BC), lambda b, i: (0, nx + 1)),
            _const_spec((SSM_CONV - 1, TB, TB)),
        ]
        args += [conv_w, conv_w, conv_w, conv_b, conv_b, conv_b, _conv_shift_mats(TB, rowlen)]
    in_specs += [_const_spec((1, P_SMALL)), _const_spec((1, P_SMALL)), _const_spec((C, C)),
                 pl.BlockSpec((1, G, SSM_STATE, SSM_GW), lambda b, i: (b, 0, 0, 0))]
    args += [dt_bias, a_log, tri, s0]
    if finalize:
        z_a, z_o = z_src
        in_specs += [
            pl.BlockSpec((1, TB, SSM_INNER), lambda b, i: (b, tmap(i), z_o)),
            pl.BlockSpec((1, TB, SSM_INNER), lambda b, i: (b, tmap(i), 0)),
            _const_spec((1, SSM_INNER)), _const_spec((1, SSM_INNER)),
        ]
        args += [z_a, y_prev, d_skip_x, norm_w]
    out_specs = [pl.BlockSpec((1, TB, SSM_INNER), lambda b, i: (b, tmap(i), 0)),
                 pl.BlockSpec((1, G, SSM_STATE, SSM_GW), lambda b, i: (b, 0, 0, 0))]
    out_shape = [jax.ShapeDtypeStruct((B, L, SSM_INNER), BF16 if finalize else F32),
                 jax.ShapeDtypeStruct((B, G, SSM_STATE, SSM_GW), F32)]
    if emit:
        out_specs += [pl.BlockSpec((1, TB, SSM_INNER), lambda b, i: (b, tmap(i), 0)),
                      pl.BlockSpec((1, TB, SSM_BC), lambda b, i: (b, tmap(i), 0)),
                      pl.BlockSpec((1, TB, SSM_BC), lambda b, i: (b, tmap(i), 0))]
        out_shape += [jax.ShapeDtypeStruct((B, L, SSM_INNER), BF16),
                      jax.ShapeDtypeStruct((B, L, SSM_BC), BF16),
                      jax.ShapeDtypeStruct((B, L, SSM_BC), BF16)]
    kern = functools.partial(_ssd_kernel, C=C, TB=TB, reverse=reverse, finalize=finalize,
                             conv=conv, emit=emit, lane0=S_DTB if reverse else S_DTF)
    return pl.pallas_call(
        kern,
        grid=(B, nb),
        in_specs=in_specs,
        out_specs=out_specs,
        out_shape=out_shape,
        scratch_shapes=[pltpu.VMEM((G, SSM_STATE, SSM_GW), F32),
                        pltpu.VMEM((TB, SSM_INNER), BF16),
                        pltpu.VMEM((TB, SSM_BC), BF16),
                        pltpu.VMEM((TB, SSM_BC), BF16),
                        pltpu.VMEM((TB, SSM_INNER), F32)],
        compiler_params=_params(("parallel", "arbitrary")),
        name="ssd_" + ("rev" if reverse else "fwd") + ("_fin" if finalize else ""),
    )(*args)


def _merge_kernel(oa_ref, ob_ref, ga_ref, gb_ref, x_ref, g1_ref, wpa_ref, wpb_ref, wout_ref, h_ref):
    ya = _dot(oa_ref[0], wpa_ref[...])
    yb = _dot(ob_ref[0], wpb_ref[...])
    m = _sigmoid(ga_ref[0].astype(F32)) * ya + _sigmoid(gb_ref[0].astype(F32)) * yb
    mix = _dot(m.astype(BF16), wout_ref[...])
    h_ref[0] = x_ref[0] + g1_ref[0] * mix


def _merge(o_a, o_b, pm, x, g1, w_pa, w_pb, w_out, tm):
    B, L, _ = x.shape
    gab, gbb = P_GA // D_MODEL, P_GB // D_MODEL
    return pl.pallas_call(
        _merge_kernel,
        grid=(B, L // tm),
        in_specs=[pl.BlockSpec((1, tm, GLA_V), lambda b, i: (b, i, 0)),
                  pl.BlockSpec((1, tm, SSM_INNER), lambda b, i: (b, i, 0)),
                  pl.BlockSpec((1, tm, D_MODEL), lambda b, i: (b, i, gab)),
                  pl.BlockSpec((1, tm, D_MODEL), lambda b, i: (b, i, gbb)),
                  pl.BlockSpec((1, tm, D_MODEL), lambda b, i: (b, i, 0)),
                  pl.BlockSpec((1, 1, D_MODEL), lambda b, i: (b, 0, 0)),
                  _const_spec(w_pa.shape), _const_spec(w_pb.shape), _const_spec(w_out.shape)],
        out_specs=pl.BlockSpec((1, tm, D_MODEL), lambda b, i: (b, i, 0)),
        out_shape=jax.ShapeDtypeStruct((B, L, D_MODEL), F32),
        compiler_params=_params(("parallel", "parallel")),
        name="merge",
    )(o_a, o_b, pm, pm, x, g1, w_pa, w_pb, w_out)


FFN_CHUNK = 256


def _ffn_kernel(h_ref, mod_ref, n2_ref, fw_ref, wg_ref, wu_ref, wd_ref, o_ref):
    h = h_ref[0]
    hn = h * lax.rsqrt(jnp.mean(h * h, axis=-1, keepdims=True) + EPS) * n2_ref[...]
    hn = (hn * (1.0 + mod_ref[0, 1:2, :]) + mod_ref[0, 0:1, :]).astype(BF16)
    acc = jnp.zeros(h.shape, F32)
    for c in range(D_FF // FFN_CHUNK):
        cols = slice(c * FFN_CHUNK, (c + 1) * FFN_CHUNK)
        gt = _dot(hn, wg_ref[:, cols])
        up = _dot(hn, wu_ref[:, cols])
        acc = acc + _dot((_silu(gt) * up).astype(BF16), wd_ref[cols, :])
    h2 = h + mod_ref[0, 2:3, :] * acc
    o_ref[0] = h2 * lax.rsqrt(jnp.mean(h2 * h2, axis=-1, keepdims=True) + EPS) * fw_ref[...]


def _ffn(h, mod, n2w, fw, w_gate, w_up, w_down, tm):
    B, L, _ = h.shape
    return pl.pallas_call(
        _ffn_kernel,
        grid=(B, L // tm),
        in_specs=[pl.BlockSpec((1, tm, D_MODEL), lambda b, i: (b, i, 0)),
                  pl.BlockSpec((1, 3, D_MODEL), lambda b, i: (b, 0, 0)),
                  _const_spec((1, D_MODEL)), _const_spec((1, D_MODEL)),
                  pl.BlockSpec(w_gate.shape, lambda b, i: (0, 0), pipeline_mode=pl.Buffered(1)),
                  pl.BlockSpec(w_up.shape, lambda b, i: (0, 0), pipeline_mode=pl.Buffered(1)),
                  pl.BlockSpec(w_down.shape, lambda b, i: (0, 0), pipeline_mode=pl.Buffered(1))],
        out_specs=pl.BlockSpec((1, tm, D_MODEL), lambda b, i: (b, i, 0)),
        out_shape=jax.ShapeDtypeStruct((B, L, D_MODEL), F32),
        compiler_params=_params(("parallel", "parallel")),
        name="ffn",
    )(h, mod, n2w, fw, w_gate, w_up, w_down)


def _pick_block(L, pref):
    tb = min(L, pref)
    assert L % tb == 0
    return tb


def kernel(x, c, ctx, c_ctx, w_ada, b_ada, norm1_w, w_in, gla_up_f, gla_bias_f, gla_up_b, gla_bias_b,
           gla_norm_w, conv_w, conv_b, dt_bias_f, dt_bias_b, a_log_f, a_log_b, d_skip, ssm_norm_w,
           w_pa, w_pb, w_out, norm2_w, w_gate, w_up, w_down, final_norm_w):
    B, L, D = x.shape
    Lc = ctx.shape[1]
    depth = w_ada.shape[0]
    assert depth == 1 and D == D_MODEL
    assert L % GRID_W == 0 and L % SSD_CHUNK == 0 and Lc % SSD_CHUNK == 0
    lay = 0

    nrow = -(-(B + 1) // 8) * 8
    cc = jnp.zeros((nrow, D), F32).at[:B].set(c).at[B].set(c_ctx)
    ada = _ada(cc, w_ada[lay], b_ada[lay][None, :])
    sh1, sc1, g1, sh2, sc2, g2 = [ada[:, i * D:(i + 1) * D] for i in range(6)]
    mod1 = jnp.stack([sh1[:B], sc1[:B]], axis=1)
    mod1_c = jnp.broadcast_to(jnp.stack([sh1[B], sc1[B]])[None], (B, 2, D))
    mod2 = jnp.stack([sh2[:B], sc2[:B], g2[:B]], axis=1)
    g1_l = g1[:B, None, :]

    wi = w_in[lay]
    o = _IN_OFF
    w_main = jnp.concatenate([wi[:, o[6]:o[8]], wi[:, o[0]:o[4]], wi[:, o[8]:o[10]], wi[:, o[12]:o[14]]],
                             axis=1).astype(BF16)
    w_small = jnp.concatenate([wi[:, o[4]:o[6]], wi[:, o[10]:o[12]],
                               jnp.zeros((D, P_SMALL - 2 * GLA_RANK - 2 * SSM_HEADS), F32)], axis=1).astype(BF16)
    nw1 = norm1_w[lay][None, :]

    at_lanes = lambda p, off: jnp.zeros((1, P_SMALL), F32).at[0, off:off + SSM_HEADS].set(p[lay])
    dtb_f, al_f = at_lanes(dt_bias_f, S_DTF), at_lanes(a_log_f, S_DTF)
    dtb_b, al_b = at_lanes(dt_bias_b, S_DTB), at_lanes(a_log_b, S_DTB)
    cw, cb_ = conv_w[lay], conv_b[lay][None, :]
    up_f, up_b = gla_up_f[lay], gla_up_b[lay]
    bi_f, bi_b = gla_bias_f[lay][None, :], gla_bias_b[lay][None, :]

    gla_zero = jnp.zeros((B, GLA_HEADS, GLA_DV, GLA_DK), F32)
    ssd_zero = jnp.zeros((B, SSM_GROUPS, SSM_STATE, SSM_GW), F32)

    pm_c, ps_c = _inproj(ctx, mod1_c, nw1, w_main, w_small, _pick_block(Lc, 256))
    tbc = _pick_block(Lc, 256)
    _, sg_f = _gla_pass(pm_c, ps_c, up_f, bi_f, gla_zero, reverse=False, finalize=False, TB=tbc)
    _, sg_b = _gla_pass(pm_c, ps_c, up_b, bi_b, gla_zero, reverse=True, finalize=False, TB=tbc)
    src_c = ((pm_c, P_XS // SSM_INNER), (pm_c, P_BM // SSM_BC), (pm_c, P_CM // SSM_BC))
    _, ss_f = _ssd_pass(src_c, ps_c, cw, cb_, dtb_f, al_f, ssd_zero, reverse=False, finalize=False,
                        TB=Lc, rowlen=Lc)
    _, ss_b = _ssd_pass(src_c, ps_c, cw, cb_, dtb_b, al_b, ssd_zero, reverse=True, finalize=False,
                        TB=Lc, rowlen=Lc)

    pm, ps = _inproj(x, mod1, nw1, w_main, w_small, _pick_block(L, 1024))
    tbg = _pick_block(L, 1024)
    og_f, _ = _gla_pass(pm, ps, up_f, bi_f, sg_f, reverse=False, finalize=False, TB=tbg)
    o_a, _ = _gla_pass(pm, ps, up_b, bi_b, sg_b, reverse=True, finalize=True, TB=tbg,
                       o_prev=og_f, norm_w=gla_norm_w[lay][None, :])
    tbs = _pick_block(L, 256)
    src = ((pm, P_XS // SSM_INNER), (pm, P_BM // SSM_BC), (pm, P_CM // SSM_BC))
    ys_f, _, xc, bc, cc = _ssd_pass(src, ps, cw, cb_, dtb_f, al_f, ss_f, reverse=False, finalize=False,
                                    TB=tbs, rowlen=GRID_W, emit=True)
    o_b, _ = _ssd_pass(((xc, 0), (bc, 0), (cc, 0)), ps, None, None, dtb_b, al_b, ss_b, reverse=True,
                       finalize=True, TB=tbs, rowlen=GRID_W, y_prev=ys_f, z_src=(pm, P_Z // SSM_INNER),
                       d_skip_x=jnp.repeat(d_skip[lay], SSM_HEADDIM)[None, :],
                       norm_w=ssm_norm_w[lay][None, :])

    h = _merge(o_a, o_b, pm, x, g1_l, w_pa[lay].astype(BF16), w_pb[lay].astype(BF16),
               w_out[lay].astype(BF16), _pick_block(L, 512))
    return _ffn(h, mod2, norm2_w[lay][None, :], final_norm_w[None, :], w_gate[lay].astype(BF16),
                w_up[lay].astype(BF16), w_down[lay].astype(BF16), _pick_block(L, 512))
```

```python
import functools

import numpy as np
import jax
import jax.numpy as jnp
from jax import lax
from jax.experimental import pallas as pl
from jax.experimental.pallas import tpu as pltpu

F32 = jnp.float32
BF16 = jnp.bfloat16

D_MODEL = 1024
GRID_W = 64
EPS = 1e-6

GLA_HEADS = 4
GLA_DK = 128
GLA_DV = 256
GLA_QK = GLA_HEADS * GLA_DK
GLA_V = GLA_HEADS * GLA_DV
GLA_RANK = 16
GLA_TAU = 16.0

SSM_INNER = 2 * D_MODEL
SSM_HEADDIM = 64
SSM_HEADS = SSM_INNER // SSM_HEADDIM
SSM_GROUPS = 4
SSM_HPG = SSM_HEADS // SSM_GROUPS
SSM_STATE = 128
SSM_BC = SSM_GROUPS * SSM_STATE
SSM_CONV = 4
CONV_LEFT = 2
SSM_GW = SSM_HPG * SSM_HEADDIM

D_FF = ((8 * D_MODEL // 3 + 255) // 256) * 256

_IN_WIDTHS = (GLA_QK, GLA_QK, GLA_V, GLA_V, GLA_RANK, GLA_RANK,
              SSM_INNER, SSM_INNER, SSM_BC, SSM_BC, SSM_HEADS, SSM_HEADS, D_MODEL, D_MODEL)
_IN_OFF = np.concatenate([[0], np.cumsum(_IN_WIDTHS)]).tolist()

P_Z, P_XS = 0, 2048
P_Q, P_K, P_V, P_R = 4096, 4608, 5120, 6144
P_BM, P_CM = 7168, 7680
P_GA, P_GB = 8192, 9216
P_MAIN = 10240
S_LRF, S_LRB, S_DTF, S_DTB = 0, 16, 32, 64
P_SMALL = 128

GLA_CHUNK = 128
SSD_CHUNK = 128

VMEM_LIMIT = 56 * 1024 * 1024
NEG_BIG = -1e30
LOG2E = 1.4426950408889634


def _sigmoid(x):
    return 1.0 / (1.0 + jnp.exp(-x))


def _silu(x):
    return x * _sigmoid(x)


def _softplus(x):
    return jnp.maximum(x, 0.0) + jnp.log(1.0 + jnp.exp(-jnp.abs(x)))


def _split3(x):
    hi = x.astype(BF16)
    r1 = x - hi.astype(F32)
    mid = r1.astype(BF16)
    lo = (r1 - mid.astype(F32)).astype(BF16)
    return hi, mid, lo


def _dot(a, b):
    return jnp.dot(a, b, preferred_element_type=F32)


def _dot_nt(a, b):
    return lax.dot_general(a, b, (((1,), (1,)), ((), ())), preferred_element_type=F32)


def _dot_tn(a, b):
    return lax.dot_general(a, b, (((0,), (0,)), ((), ())), preferred_element_type=F32)


def _dot01(m01, x):
    hi, mid, lo = _split3(x)
    return _dot(m01, hi) + _dot(m01, mid) + _dot(m01, lo)


def _params(sem):
    return pltpu.CompilerParams(dimension_semantics=sem, vmem_limit_bytes=VMEM_LIMIT)


def _const_spec(shape):
    n = len(shape)
    return pl.BlockSpec(shape, lambda *_: (0,) * n)


def _ada_kernel(c_ref, w_ref, b_ref, o_ref):
    s = _silu(c_ref[...])
    o_ref[...] = jnp.dot(s, w_ref[...], preferred_element_type=F32,
                         precision=lax.Precision.HIGHEST) + b_ref[...]


def _ada(cc, w, b):
    rows = cc.shape[0]
    n = w.shape[1]
    tn = 1536
    return pl.pallas_call(
        _ada_kernel,
        grid=(n // tn,),
        in_specs=[pl.BlockSpec((rows, D_MODEL), lambda j: (0, 0)),
                  pl.BlockSpec((D_MODEL, tn), lambda j: (0, j)),
                  pl.BlockSpec((1, tn), lambda j: (0, j))],
        out_specs=pl.BlockSpec((rows, tn), lambda j: (0, j)),
        out_shape=jax.ShapeDtypeStruct((rows, n), F32),
        compiler_params=_params(("arbitrary",)),
        name="ada",
    )(cc, w, b)


def _inproj_kernel(x_ref, mod_ref, nw_ref, wm_ref, ws_ref, om_ref, os_ref, xn_ref):
    @pl.when(pl.program_id(2) == 0)
    def _():
        x = x_ref[0]
        y = x * lax.rsqrt(jnp.mean(x * x, axis=-1, keepdims=True) + EPS) * nw_ref[...]
        y = y * (1.0 + mod_ref[0, 1:2, :]) + mod_ref[0, 0:1, :]
        xn = y.astype(BF16)
        xn_ref[...] = xn
        os_ref[0] = _dot(xn, ws_ref[...])

    om_ref[0] = _dot(xn_ref[...], wm_ref[...]).astype(BF16)


def _inproj(x, mod, nw, w_main, w_small, tm):
    B, L, _ = x.shape
    tn = 2048
    return pl.pallas_call(
        _inproj_kernel,
        grid=(B, L // tm, P_MAIN // tn),
        in_specs=[pl.BlockSpec((1, tm, D_MODEL), lambda b, i, j: (b, i, 0)),
                  pl.BlockSpec((1, 2, D_MODEL), lambda b, i, j: (b, 0, 0)),
                  pl.BlockSpec((1, D_MODEL), lambda b, i, j: (0, 0)),
                  pl.BlockSpec((D_MODEL, tn), lambda b, i, j: (0, j)),
                  pl.BlockSpec((D_MODEL, P_SMALL), lambda b, i, j: (0, 0))],
        out_specs=[pl.BlockSpec((1, tm, tn), lambda b, i, j: (b, i, j)),
                   pl.BlockSpec((1, tm, P_SMALL), lambda b, i, j: (b, i, 0))],
        out_shape=[jax.ShapeDtypeStruct((B, L, P_MAIN), BF16),
                   jax.ShapeDtypeStruct((B, L, P_SMALL), F32)],
        scratch_shapes=[pltpu.VMEM((tm, D_MODEL), BF16)],
        compiler_params=_params(("parallel", "parallel", "arbitrary")),
        name="inproj",
    )(x, mod, nw, w_main, w_small)


GLA_MXU_LEVELS = 3
GLA_GROUP = 4


def _gla_consts(C, reverse):
    NL = int(np.log2(C))
    idx = np.arange(C)
    tri = (idx[None, :] <= idx[:, None]).astype(np.float32)
    mats = [tri]
    masks = [np.eye(C, dtype=np.float32)]
    refs, signs = [], []
    for lev in range(NL):
        h = 1 << lev
        blk = idx // (2 * h)
        half = (idx // h) % 2
        ref = blk * 2 * h + h - 1
        if lev < GLA_MXU_LEVELS:
            d = tri - tri[ref]
            d[half == 0] *= -1.0
            mats.append(d)
        else:
            sg = np.where(half == 1, 1.0, -1.0).astype(np.float32)
            if reverse:
                ref, sg = (C - 1 - ref)[::-1], sg[::-1]
            refs.append([int(ref[m * 2 * h]) for m in range(C // (2 * h))])
            signs.append(np.broadcast_to(sg[:, None], (C, GLA_DK)))
        masks.append(((blk[:, None] == blk[None, :]) & (half[:, None] == 1)
                      & (half[None, :] == 0)).astype(np.float32))
    if reverse:
        mats = [m[::-1, ::-1] for m in mats]
        masks = [m[::-1, ::-1] for m in masks]
    dmat = np.concatenate(mats, axis=0)
    return (jnp.asarray(dmat, BF16), jnp.asarray(np.stack(masks), F32),
            jnp.asarray(np.stack(signs), F32), refs)


def _gla_stream(q_ref, k_ref, v_ref, lr_ref, up_ref, bias_ref, dm_ref, mk_ref, sg_ref, s0_ref,
                o_ref, sf_ref, st_ref, e_s, b_s, oi_s, kv_s, *, C, TB, reverse, refs, part):
    NL = int(np.log2(C))
    nchunk = TB // C
    blk = pl.program_id(2)
    last = 0 if reverse else C - 1
    lr_off = S_LRB if reverse else S_LRF

    if part == "init":
        @pl.when(blk == 0)
        def _():
            st_ref[...] = s0_ref[0, 0]
        return
    if part == "final":
        @pl.when(blk == pl.num_programs(2) - 1)
        def _():
            sf_ref[0, 0] = st_ref[...]
        return

    lr = lr_ref[0][:, lr_off:lr_off + GLA_RANK]
    pre = _dot(lr.astype(BF16), up_ref[...].astype(BF16)) + bias_ref[...]
    g = -_softplus(-pre) * (1.0 / GLA_TAU)
    g_hi = g.astype(BF16)
    g_lo = (g - g_hi.astype(F32)).astype(BF16)
    gs = jnp.concatenate([g_hi, g_lo], axis=1)

    for c in range(nchunk):
        ex = _dot(dm_ref[...], gs[c * C:(c + 1) * C])
        ex = ex[:, :GLA_DK] + ex[:, GLA_DK:]
        b = ex[0:C]
        b_s[c] = b
        e_s[c, 0:GLA_MXU_LEVELS * C, :] = jnp.exp(ex[C:(1 + GLA_MXU_LEVELS) * C])
        for li, lev in enumerate(range(GLA_MXU_LEVELS, NL)):
            h2 = 2 << lev
            bref = jnp.concatenate([jnp.broadcast_to(b[r:r + 1, :], (h2, GLA_DK)) for r in refs[li]], axis=0)
            e_s[c, lev * C:(lev + 1) * C, :] = jnp.exp(sg_ref[li] * (b - bref))
        e_s[c, NL * C:(NL + 1) * C, :] = jnp.exp(b[last:last + 1, :] - b)

    for c in range(nchunk):
        rows = slice(c * C, (c + 1) * C)
        q = q_ref[0, rows, :].astype(F32) * (GLA_DK ** -0.5)
        k = k_ref[0, rows, :].astype(F32)
        v = v_ref[0, rows, :]
        att = mk_ref[0] * _dot_nt(q.astype(BF16), k.astype(BF16))
        for lev in range(NL):
            e_l = e_s[c, lev * C:(lev + 1) * C, :]
            att = att + mk_ref[lev + 1] * _dot_nt((q * e_l).astype(BF16), (k * e_l).astype(BF16))
        oi_s[rows, :] = _dot(att.astype(BF16), v)
        kv_s[c] = _dot_tn((k * e_s[c, NL * C:(NL + 1) * C, :]).astype(BF16), v)

    order = list(reversed(range(nchunk))) if reverse else list(range(nchunk))
    st = st_ref[...]
    for g0 in range(0, nchunk, GLA_GROUP):
        grp = order[g0:g0 + GLA_GROUP]
        wcat = jnp.concatenate([st.astype(BF16)] + [kv_s[c].astype(BF16) for c in grp[:-1]], axis=0)
        for i, c in enumerate(grp):
            rows = slice(c * C, (c + 1) * C)
            q = q_ref[0, rows, :].astype(F32) * (GLA_DK ** -0.5)
            expo = b_s[c]
            pieces = []
            for j in range(i - 1, -2, -1):
                pieces.insert(0, (q * jnp.exp(expo)).astype(BF16))
                if j >= 0:
                    expo = expo + b_s[grp[j], last:last + 1, :]
            lhs = jnp.concatenate(pieces, axis=1)
            o_ref[0, rows, :] = oi_s[rows, :] + _dot(lhs, wcat[:(i + 1) * GLA_DK, :])
        for c in grp:
            dec = jnp.transpose(jnp.broadcast_to(jnp.exp(b_s[c, last:last + 1, :]), (GLA_DK, GLA_DK)))
            st = jnp.concatenate([dec] * (GLA_DV // GLA_DK), axis=1) * st + kv_s[c]
    st_ref[...] = st


GLA_STREAM_IN, GLA_STREAM_OUT, GLA_STREAM_SCRATCH = 10, 2, 5


def _gla_kernel(*refs, C, TB, lv_refs):
    n_in, n_out, n_scr = GLA_STREAM_IN, GLA_STREAM_OUT, GLA_STREAM_SCRATCH
    ins = [refs[d * n_in:(d + 1) * n_in] for d in range(2)]
    outs = [refs[2 * n_in + d * n_out:2 * n_in + (d + 1) * n_out] for d in range(2)]
    base = 2 * (n_in + n_out)
    scr = [refs[base + d * n_scr:base + (d + 1) * n_scr] for d in range(2)]
    for part in ("init", "body", "final"):
        for d, reverse in enumerate((False, True)):
            _gla_stream(*ins[d], *outs[d], *scr[d], C=C, TB=TB, reverse=reverse, refs=lv_refs[d], part=part)


def _gla_bidir(pm, ps, up, bias, s0, *, TB):
    B, L, _ = pm.shape
    C = GLA_CHUNK
    nb = L // TB
    assert C >= (2 << GLA_MXU_LEVELS)
    qb, kb, vb = P_Q // GLA_DK, P_K // GLA_DK, P_V // GLA_DV
    in_specs, args, out_specs, out_shape, scratch, lv_refs = [], [], [], [], [], []
    for d, reverse in enumerate((False, True)):
        dmat, masks, signs, refs = _gla_consts(C, reverse)
        lv_refs.append(refs)
        tmap = (lambda i: nb - 1 - i) if reverse else (lambda i: i)
        in_specs += [
            pl.BlockSpec((1, TB, GLA_DK), lambda b, h, i, tmap=tmap: (b, tmap(i), qb + h)),
            pl.BlockSpec((1, TB, GLA_DK), lambda b, h, i, tmap=tmap: (b, tmap(i), kb + h)),
            pl.BlockSpec((1, TB, GLA_DV), lambda b, h, i, tmap=tmap: (b, tmap(i), vb + h)),
            pl.BlockSpec((1, TB, P_SMALL), lambda b, h, i, tmap=tmap: (b, tmap(i), 0)),
            pl.BlockSpec((GLA_RANK, GLA_DK), lambda b, h, i: (0, h)),
            pl.BlockSpec((1, GLA_DK), lambda b, h, i: (0, h)),
            _const_spec(dmat.shape),
            _const_spec(masks.shape),
            _const_spec(signs.shape),
            pl.BlockSpec((1, 1, GLA_DK, GLA_DV), lambda b, h, i: (b, h, 0, 0)),
        ]
        args += [pm, pm, pm, ps, up[d], bias[d], dmat, masks, signs, s0[d]]
        out_specs += [pl.BlockSpec((1, TB, GLA_DV), lambda b, h, i, tmap=tmap: (b, tmap(i), h)),
                      pl.BlockSpec((1, 1, GLA_DK, GLA_DV), lambda b, h, i: (b, h, 0, 0))]
        out_shape += [jax.ShapeDtypeStruct((B, L, GLA_V), F32),
                      jax.ShapeDtypeStruct((B, GLA_HEADS, GLA_DK, GLA_DV), F32)]
        scratch += [pltpu.VMEM((GLA_DK, GLA_DV), F32),
                    pltpu.VMEM((TB // C, masks.shape[0] * C, GLA_DK), F32),
                    pltpu.VMEM((TB // C, C, GLA_DK), F32),
                    pltpu.VMEM((TB, GLA_DV), F32),
                    pltpu.VMEM((TB // C, GLA_DK, GLA_DV), F32)]
    assert len(in_specs) == 2 * GLA_STREAM_IN and len(scratch) == 2 * GLA_STREAM_SCRATCH
    return pl.pallas_call(
        functools.partial(_gla_kernel, C=C, TB=TB, lv_refs=lv_refs),
        grid=(B, GLA_HEADS, nb),
        in_specs=in_specs,
        out_specs=out_specs,
        out_shape=out_shape,
        scratch_shapes=scratch,
        compiler_params=_params(("parallel", "parallel", "arbitrary")),
        name="gla",
    )(*args)


def _conv_shift_mats(T, rowlen):
    t = np.arange(T)
    mats = []
    for j in range(SSM_CONV):
        off = j - CONV_LEFT
        if off == 0:
            continue
        src = t + off
        ok = (src // rowlen == t // rowlen) & (src >= 0) & (src < T)
        m = np.zeros((T, T), np.float32)
        m[t[ok], src[ok]] = 1.0
        mats.append(m)
    return jnp.asarray(np.stack(mats), BF16)


def _conv_silu(u, w, b, sh_ref):
    acc = b + u.astype(F32) * w[CONV_LEFT:CONV_LEFT + 1, :]
    taps = [j for j in range(SSM_CONV) if j != CONV_LEFT]
    for i, j in enumerate(taps):
        acc = acc + _dot(sh_ref[i], u) * w[j:j + 1, :]
    return _silu(acc)


def _ssd_kernel(*refs, C, TB, reverse, finalize, conv, emit, lane0):
    refs = list(refs)
    xs_ref, bm_ref, cm_ref, ps_ref = refs[:4]
    refs = refs[4:]
    if conv:
        wx_ref, wb_ref, wc_ref, bx_ref, bb_ref, bc_ref, sh_ref = refs[:7]
        refs = refs[7:]
    dtb_ref, alog_ref, tri_ref, s0_ref = refs[:4]
    refs = refs[4:]
    if finalize:
        z_ref, yp_ref, dsk_ref, nw_ref = refs[:4]
        refs = refs[4:]
    y_ref, sf_ref = refs[:2]
    refs = refs[2:]
    if emit:
        xo_ref, bo_ref, co_ref = refs[:3]
        refs = refs[3:]
    st_ref, xc_s, bc_s, cc_s, ya_s = refs
    nchunk = TB // C
    blk = pl.program_id(1)
    G, HPG, P, N = SSM_GROUPS, SSM_HPG, SSM_HEADDIM, SSM_STATE

    @pl.when(blk == 0)
    def _():
        st_ref[...] = s0_ref[0]

    if conv:
        xc_s[...] = _conv_silu(xs_ref[0], wx_ref[...], bx_ref[...], sh_ref).astype(BF16)
        bc_s[...] = _conv_silu(bm_ref[0], wb_ref[...], bb_ref[...], sh_ref).astype(BF16)
        cc_s[...] = _conv_silu(cm_ref[0], wc_ref[...], bc_ref[...], sh_ref).astype(BF16)
        if emit:
            xo_ref[0] = xc_s[...]
            bo_ref[0] = bc_s[...]
            co_ref[0] = cc_s[...]
    else:
        xc_s[...] = xs_ref[0]
        bc_s[...] = bm_ref[0]
        cc_s[...] = cm_ref[0]

    neg_a = -jnp.exp(alog_ref[...])
    ti = lax.broadcasted_iota(jnp.int32, (C, C), 0)
    si = lax.broadcasted_iota(jnp.int32, (C, C), 1)
    keep = (si >= ti) if reverse else (si <= ti)
    lo = lax.broadcasted_iota(jnp.int32, (1, 2 * P), 1) < P
    last = 0 if reverse else C - 1
    nheads = G * HPG

    for c in (reversed(range(nchunk)) if reverse else range(nchunk)):
        rows = slice(c * C, (c + 1) * C)
        dt = _softplus(ps_ref[0, rows, :] + dtb_ref[...])
        cum = _dot01(tri_ref[...], dt * neg_a) * LOG2E
        cum_t = jnp.transpose(cum)[lane0:lane0 + nheads, :]
        dt_t = jnp.transpose(dt)[lane0:lane0 + nheads, :]
        w_t = dt_t * jnp.exp2(cum_t[:, last:last + 1] - cum_t)
        cdl_t = cum_t - jnp.log2(dt_t)
        dec_all = jnp.exp2(cum[last:last + 1, :])
        for g in range(G):
            bm_g = bc_s[rows, g * N:(g + 1) * N]
            cm_g = cc_s[rows, g * N:(g + 1) * N]
            cb = _dot_nt(cm_g, bm_g)
            bm_t = jnp.transpose(bm_g.astype(F32))
            cm_f = cm_g.astype(F32)
            for pr in range(HPG // 2):
                lanes = slice(g * SSM_GW + pr * 2 * P, g * SSM_GW + (pr + 1) * 2 * P)
                x_pair = xc_s[rows, lanes]
                s_pair = st_ref[g, :, pr * 2 * P:(pr + 1) * 2 * P]
                s_bf = s_pair.astype(BF16)
                y = None
                ds = None
                decs = []
                for half in range(2):
                    j = g * HPG + pr * 2 + half
                    lane = lane0 + j
                    sel = lo if half == 0 else jnp.logical_not(lo)
                    bc_ = jnp.broadcast_to(cum[:, lane:lane + 1], (C, C))
                    m = cb * jnp.exp2(jnp.where(keep, bc_ - cdl_t[j:j + 1, :], NEG_BIG))
                    cd = cm_f * jnp.exp2(bc_)
                    lhs = jnp.concatenate([m, cd], axis=1).astype(BF16)
                    rhs = jnp.concatenate([jnp.where(sel, x_pair, jnp.zeros_like(x_pair)),
                                           jnp.where(sel, s_bf, jnp.zeros_like(s_bf))], axis=0)
                    yh = _dot(lhs, rhs)
                    dh = _dot((bm_t * w_t[j:j + 1, :]).astype(BF16),
                              jnp.where(sel, x_pair, jnp.zeros_like(x_pair)))
                    y = yh if y is None else y + yh
                    ds = dh if ds is None else ds + dh
                    decs.append(dec_all[:, lane:lane + 1])
                ya_s[rows, lanes] = y
                dec = jnp.where(lo, decs[0], decs[1])
                st_ref[g, :, pr * 2 * P:(pr + 1) * 2 * P] = dec * s_pair + ds

    if finalize:
        for g in range(G):
            lanes = slice(g * SSM_GW, (g + 1) * SSM_GW)
            y = ya_s[:, lanes] + yp_ref[0, :, lanes] + dsk_ref[:, lanes] * xc_s[:, lanes].astype(F32)
            y = y * _silu(z_ref[0, :, lanes].astype(F32))
            y = y * lax.rsqrt(jnp.mean(y * y, axis=-1, keepdims=True) + EPS) * nw_ref[:, lanes]
            y_ref[0, :, lanes] = y.astype(y_ref.dtype)
    else:
        y_ref[0] = ya_s[...]

    @pl.when(blk == pl.num_programs(1) - 1)
    def _():
        sf_ref[0] = st_ref[...]


def _ssd_pass(src, ps, conv_w, conv_b, dt_bias, a_log, s0, *, reverse, finalize, TB, rowlen, emit=False,
              y_prev=None, z_src=None, d_skip_x=None, norm_w=None):
    (xs_a, xs_o), (bm_a, bm_o), (cm_a, cm_o) = src
    B, L, _ = ps.shape
    C = SSD_CHUNK
    nb = L // TB
    G = SSM_GROUPS
    conv = conv_w is not None
    tmap = (lambda i: nb - 1 - i) if reverse else (lambda i: i)
    idx = np.arange(C)
    tri = (idx[None, :] >= idx[:, None]) if reverse else (idx[None, :] <= idx[:, None])
    tri = jnp.asarray(tri.astype(np.float32), BF16)
    in_specs = [
        pl.BlockSpec((1, TB, SSM_INNER), lambda b, i: (b, tmap(i), xs_o)),
        pl.BlockSpec((1, TB, SSM_BC), lambda b, i: (b, tmap(i), bm_o)),
        pl.BlockSpec((1, TB, SSM_BC), lambda b, i: (b, tmap(i), cm_o)),
        pl.BlockSpec((1, TB, P_SMALL), lambda b, i: (b, tmap(i), 0)),
    ]
    args = [xs_a, bm_a, cm_a, ps]
    if conv:
        nx = SSM_INNER // SSM_BC
        in_specs += [
            pl.BlockSpec((SSM_CONV, SSM_INNER), lambda b, i: (0, 0)),
            pl.BlockSpec((SSM_CONV, SSM_BC), lambda b, i: (0, nx)),
            pl.BlockSpec((SSM_CONV, SSM_BC), lambda b, i: (0, nx + 1)),
            pl.BlockSpec((1, SSM_INNER), lambda b, i: (0, 0)),
            pl.BlockSpec((1, SSM_BC), lambda b, i: (0, nx)),
            pl.BlockSpec((1, SSM_BC), lambda b, i: (0, nx + 1)),
            _const_spec((SSM_CONV - 1, TB, TB)),
        ]
        args += [conv_w, conv_w, conv_w, conv_b, conv_b, conv_b, _conv_shift_mats(TB, rowlen)]
    in_specs += [_const_spec((1, P_SMALL)), _const_spec((1, P_SMALL)), _const_spec((C, C)),
                 pl.BlockSpec((1, G, SSM_STATE, SSM_GW), lambda b, i: (b, 0, 0, 0))]
    args += [dt_bias, a_log, tri, s0]
    if finalize:
        z_a, z_o = z_src
        in_specs += [
            pl.BlockSpec((1, TB, SSM_INNER), lambda b, i: (b, tmap(i), z_o)),
            pl.BlockSpec((1, TB, SSM_INNER), lambda b, i: (b, tmap(i), 0)),
            _const_spec((1, SSM_INNER)), _const_spec((1, SSM_INNER)),
        ]
        args += [z_a, y_prev, d_skip_x, norm_w]
    out_specs = [pl.BlockSpec((1, TB, SSM_INNER), lambda b, i: (b, tmap(i), 0)),
                 pl.BlockSpec((1, G, SSM_STATE, SSM_GW), lambda b, i: (b, 0, 0, 0))]
    out_shape = [jax.ShapeDtypeStruct((B, L, SSM_INNER), BF16 if finalize else F32),
                 jax.ShapeDtypeStruct((B, G, SSM_STATE, SSM_GW), F32)]
    if emit:
        out_specs += [pl.BlockSpec((1, TB, SSM_INNER), lambda b, i: (b, tmap(i), 0)),
                      pl.BlockSpec((1, TB, SSM_BC), lambda b, i: (b, tmap(i), 0)),
                      pl.BlockSpec((1, TB, SSM_BC), lambda b, i: (b, tmap(i), 0))]
        out_shape += [jax.ShapeDtypeStruct((B, L, SSM_INNER), BF16),
                      jax.ShapeDtypeStruct((B, L, SSM_BC), BF16),
                      jax.ShapeDtypeStruct((B, L, SSM_BC), BF16)]
    kern = functools.partial(_ssd_kernel, C=C, TB=TB, reverse=reverse, finalize=finalize,
                             conv=conv, emit=emit, lane0=S_DTB if reverse else S_DTF)
    return pl.pallas_call(
        kern,
        grid=(B, nb),
        in_specs=in_specs,
        out_specs=out_specs,
        out_shape=out_shape,
        scratch_shapes=[pltpu.VMEM((G, SSM_STATE, SSM_GW), F32),
                        pltpu.VMEM((TB, SSM_INNER), BF16),
                        pltpu.VMEM((TB, SSM_BC), BF16),
                        pltpu.VMEM((TB, SSM_BC), BF16),
                        pltpu.VMEM((TB, SSM_INNER), F32)],
        compiler_params=_params(("parallel", "arbitrary")),
        name="ssd_" + ("rev" if reverse else "fwd") + ("_fin" if finalize else ""),
    )(*args)


def _merge_kernel(of_ref, ob_ref, r_ref, gnw_ref, sb_ref, ga_ref, gb_ref, x_ref, g1_ref, wpa_ref, wpb_ref,
                  wout_ref, h_ref):
    o = of_ref[0] + ob_ref[0]
    heads = []
    for h in range(GLA_HEADS):
        oh = o[:, h * GLA_DV:(h + 1) * GLA_DV]
        heads.append(oh * lax.rsqrt(jnp.mean(oh * oh, axis=-1, keepdims=True) + EPS) * gnw_ref[...])
    oa = (jnp.concatenate(heads, axis=1) * _silu(r_ref[0].astype(F32))).astype(BF16)
    ya = _dot(oa, wpa_ref[...])
    yb = _dot(sb_ref[0], wpb_ref[...])
    m = _sigmoid(ga_ref[0].astype(F32)) * ya + _sigmoid(gb_ref[0].astype(F32)) * yb
    mix = _dot(m.astype(BF16), wout_ref[...])
    h_ref[0] = x_ref[0] + g1_ref[0] * mix


def _merge(o_f, o_b, gla_norm_w, s_b, pm, x, g1, w_pa, w_pb, w_out, tm):
    B, L, _ = x.shape
    gab, gbb, rb = P_GA // D_MODEL, P_GB // D_MODEL, P_R // GLA_V
    return pl.pallas_call(
        _merge_kernel,
        grid=(B, L // tm),
        in_specs=[pl.BlockSpec((1, tm, GLA_V), lambda b, i: (b, i, 0)),
                  pl.BlockSpec((1, tm, GLA_V), lambda b, i: (b, i, 0)),
                  pl.BlockSpec((1, tm, GLA_V), lambda b, i: (b, i, rb)),
                  _const_spec((1, GLA_DV)),
                  pl.BlockSpec((1, tm, SSM_INNER), lambda b, i: (b, i, 0)),
                  pl.BlockSpec((1, tm, D_MODEL), lambda b, i: (b, i, gab)),
                  pl.BlockSpec((1, tm, D_MODEL), lambda b, i: (b, i, gbb)),
                  pl.BlockSpec((1, tm, D_MODEL), lambda b, i: (b, i, 0)),
                  pl.BlockSpec((1, 1, D_MODEL), lambda b, i: (b, 0, 0)),
                  _const_spec(w_pa.shape), _const_spec(w_pb.shape), _const_spec(w_out.shape)],
        out_specs=pl.BlockSpec((1, tm, D_MODEL), lambda b, i: (b, i, 0)),
        out_shape=jax.ShapeDtypeStruct((B, L, D_MODEL), F32),
        compiler_params=_params(("parallel", "parallel")),
        name="merge",
    )(o_f, o_b, pm, gla_norm_w, s_b, pm, pm, x, g1, w_pa, w_pb, w_out)


FFN_CHUNK = 256


def _ffn_kernel(h_ref, mod_ref, n2_ref, fw_ref, wg_ref, wu_ref, wd_ref, o_ref):
    h = h_ref[0]
    hn = h * lax.rsqrt(jnp.mean(h * h, axis=-1, keepdims=True) + EPS) * n2_ref[...]
    hn = (hn * (1.0 + mod_ref[0, 1:2, :]) + mod_ref[0, 0:1, :]).astype(BF16)
    acc = jnp.zeros(h.shape, F32)
    for c in range(D_FF // FFN_CHUNK):
        cols = slice(c * FFN_CHUNK, (c + 1) * FFN_CHUNK)
        gt = _dot(hn, wg_ref[:, cols])
        up = _dot(hn, wu_ref[:, cols])
        acc = acc + _dot((_silu(gt) * up).astype(BF16), wd_ref[cols, :])
    h2 = h + mod_ref[0, 2:3, :] * acc
    o_ref[0] = h2 * lax.rsqrt(jnp.mean(h2 * h2, axis=-1, keepdims=True) + EPS) * fw_ref[...]


def _ffn(h, mod, n2w, fw, w_gate, w_up, w_down, tm):
    B, L, _ = h.shape
    return pl.pallas_call(
        _ffn_kernel,
        grid=(B, L // tm),
        in_specs=[pl.BlockSpec((1, tm, D_MODEL), lambda b, i: (b, i, 0)),
                  pl.BlockSpec((1, 3, D_MODEL), lambda b, i: (b, 0, 0)),
                  _const_spec((1, D_MODEL)), _const_spec((1, D_MODEL)),
                  pl.BlockSpec(w_gate.shape, lambda b, i: (0, 0), pipeline_mode=pl.Buffered(1)),
                  pl.BlockSpec(w_up.shape, lambda b, i: (0, 0), pipeline_mode=pl.Buffered(1)),
                  pl.BlockSpec(w_down.shape, lambda b, i: (0, 0), pipeline_mode=pl.Buffered(1))],
        out_specs=pl.BlockSpec((1, tm, D_MODEL), lambda b, i: (b, i, 0)),
        out_shape=jax.ShapeDtypeStruct((B, L, D_MODEL), F32),
        compiler_params=_params(("parallel", "parallel")),
        name="ffn",
    )(h, mod, n2w, fw, w_gate, w_up, w_down)


def _pick_block(L, pref):
    tb = min(L, pref)
    assert L % tb == 0
    return tb


def kernel(x, c, ctx, c_ctx, w_ada, b_ada, norm1_w, w_in, gla_up_f, gla_bias_f, gla_up_b, gla_bias_b,
           gla_norm_w, conv_w, conv_b, dt_bias_f, dt_bias_b, a_log_f, a_log_b, d_skip, ssm_norm_w,
           w_pa, w_pb, w_out, norm2_w, w_gate, w_up, w_down, final_norm_w):
    B, L, D = x.shape
    Lc = ctx.shape[1]
    depth = w_ada.shape[0]
    assert depth == 1 and D == D_MODEL
    assert L % GRID_W == 0 and L % SSD_CHUNK == 0 and Lc % SSD_CHUNK == 0
    lay = 0

    nrow = -(-(B + 1) // 8) * 8
    cc = jnp.zeros((nrow, D), F32).at[:B].set(c).at[B].set(c_ctx)
    ada = _ada(cc, w_ada[lay], b_ada[lay][None, :])
    sh1, sc1, g1, sh2, sc2, g2 = [ada[:, i * D:(i + 1) * D] for i in range(6)]
    mod1 = jnp.stack([sh1[:B], sc1[:B]], axis=1)
    mod1_c = jnp.broadcast_to(jnp.stack([sh1[B], sc1[B]])[None], (B, 2, D))
    mod2 = jnp.stack([sh2[:B], sc2[:B], g2[:B]], axis=1)
    g1_l = g1[:B, None, :]

    wi = w_in[lay]
    o = _IN_OFF
    w_main = jnp.concatenate([wi[:, o[6]:o[8]], wi[:, o[0]:o[4]], wi[:, o[8]:o[10]], wi[:, o[12]:o[14]]],
                             axis=1).astype(BF16)
    w_small = jnp.concatenate([wi[:, o[4]:o[6]], wi[:, o[10]:o[12]],
                               jnp.zeros((D, P_SMALL - 2 * GLA_RANK - 2 * SSM_HEADS), F32)], axis=1).astype(BF16)
    nw1 = norm1_w[lay][None, :]

    at_lanes = lambda p, off: jnp.zeros((1, P_SMALL), F32).at[0, off:off + SSM_HEADS].set(p[lay])
    dtb_f, al_f = at_lanes(dt_bias_f, S_DTF), at_lanes(a_log_f, S_DTF)
    dtb_b, al_b = at_lanes(dt_bias_b, S_DTB), at_lanes(a_log_b, S_DTB)
    cw, cb_ = conv_w[lay], conv_b[lay][None, :]
    up_f, up_b = gla_up_f[lay], gla_up_b[lay]
    bi_f, bi_b = gla_bias_f[lay][None, :], gla_bias_b[lay][None, :]

    gla_zero = jnp.zeros((B, GLA_HEADS, GLA_DK, GLA_DV), F32)
    ssd_zero = jnp.zeros((B, SSM_GROUPS, SSM_STATE, SSM_GW), F32)

    pm_c, ps_c = _inproj(ctx, mod1_c, nw1, w_main, w_small, _pick_block(Lc, 256))
    _, sg_f, _, sg_b = _gla_bidir(pm_c, ps_c, (up_f, up_b), (bi_f, bi_b), (gla_zero, gla_zero),
                                  TB=_pick_block(Lc, 256))
    src_c = ((pm_c, P_XS // SSM_INNER), (pm_c, P_BM // SSM_BC), (pm_c, P_CM // SSM_BC))
    _, ss_f = _ssd_pass(src_c, ps_c, cw, cb_, dtb_f, al_f, ssd_zero, reverse=False, finalize=False,
                        TB=Lc, rowlen=Lc)
    _, ss_b = _ssd_pass(src_c, ps_c, cw, cb_, dtb_b, al_b, ssd_zero, reverse=True, finalize=False,
                        TB=Lc, rowlen=Lc)

    pm, ps = _inproj(x, mod1, nw1, w_main, w_small, _pick_block(L, 1024))
    tbg = _pick_block(L, 1024)
    og_f, _, og_b, _ = _gla_bidir(pm, ps, (up_f, up_b), (bi_f, bi_b), (sg_f, sg_b), TB=tbg)
    tbs = _pick_block(L, 256)
    src = ((pm, P_XS // SSM_INNER), (pm, P_BM // SSM_BC), (pm, P_CM // SSM_BC))
    ys_f, _, xc, bc, cc = _ssd_pass(src, ps, cw, cb_, dtb_f, al_f, ss_f, reverse=False, finalize=False,
                                    TB=tbs, rowlen=GRID_W, emit=True)
    o_b, _ = _ssd_pass(((xc, 0), (bc, 0), (cc, 0)), ps, None, None, dtb_b, al_b, ss_b, reverse=True,
                       finalize=True, TB=tbs, rowlen=GRID_W, y_prev=ys_f, z_src=(pm, P_Z // SSM_INNER),
                       d_skip_x=jnp.repeat(d_skip[lay], SSM_HEADDIM)[None, :],
                       norm_w=ssm_norm_w[lay][None, :])

    h = _merge(og_f, og_b, gla_norm_w[lay][None, :], o_b, pm, x, g1_l, w_pa[lay].astype(BF16),
               w_pb[lay].astype(BF16), w_out[lay].astype(BF16), _pick_block(L, 512))
    return _ffn(h, mod2, norm2_w[lay][None, :], final_norm_w[None, :], w_gate[lay].astype(BF16),
                w_up[lay].astype(BF16), w_down[lay].astype(BF16), _pick_block(L, 512))
```

```python
import functools

import numpy as np
import jax
import jax.numpy as jnp
from jax import lax
from jax.experimental import pallas as pl
from jax.experimental.pallas import tpu as pltpu

F32 = jnp.float32
BF16 = jnp.bfloat16

D_MODEL = 1024
GRID_W = 64
EPS = 1e-6

GLA_HEADS = 4
GLA_DK = 128
GLA_DV = 256
GLA_QK = GLA_HEADS * GLA_DK
GLA_V = GLA_HEADS * GLA_DV
GLA_RANK = 16
GLA_TAU = 16.0

SSM_INNER = 2 * D_MODEL
SSM_HEADDIM = 64
SSM_HEADS = SSM_INNER // SSM_HEADDIM
SSM_GROUPS = 4
SSM_HPG = SSM_HEADS // SSM_GROUPS
SSM_STATE = 128
SSM_BC = SSM_GROUPS * SSM_STATE
SSM_CONV = 4
CONV_LEFT = 2
SSM_GW = SSM_HPG * SSM_HEADDIM

D_FF = ((8 * D_MODEL // 3 + 255) // 256) * 256

_IN_WIDTHS = (GLA_QK, GLA_QK, GLA_V, GLA_V, GLA_RANK, GLA_RANK,
              SSM_INNER, SSM_INNER, SSM_BC, SSM_BC, SSM_HEADS, SSM_HEADS, D_MODEL, D_MODEL)
_IN_OFF = np.concatenate([[0], np.cumsum(_IN_WIDTHS)]).tolist()

P_Z, P_XS = 0, 2048
P_Q, P_K, P_V, P_R = 4096, 4608, 5120, 6144
P_BM, P_CM = 7168, 7680
P_GA, P_GB = 8192, 9216
P_MAIN = 10240
S_LRF, S_LRB, S_DTF, S_DTB = 0, 16, 32, 64
P_SMALL = 128

GLA_CHUNK = 128
SSD_CHUNK = 128

VMEM_LIMIT = 56 * 1024 * 1024
NEG_BIG = -1e30
LOG2E = 1.4426950408889634


def _sigmoid(x):
    return 1.0 / (1.0 + jnp.exp(-x))


def _silu(x):
    return x * _sigmoid(x)


def _softplus(x):
    return jnp.maximum(x, 0.0) + jnp.log(1.0 + jnp.exp(-jnp.abs(x)))


def _split3(x):
    hi = x.astype(BF16)
    r1 = x - hi.astype(F32)
    mid = r1.astype(BF16)
    lo = (r1 - mid.astype(F32)).astype(BF16)
    return hi, mid, lo


def _dot(a, b):
    return jnp.dot(a, b, preferred_element_type=F32)


def _dot_nt(a, b):
    return lax.dot_general(a, b, (((1,), (1,)), ((), ())), preferred_element_type=F32)


def _dot_tn(a, b):
    return lax.dot_general(a, b, (((0,), (0,)), ((), ())), preferred_element_type=F32)


def _dot01(m01, x):
    hi, mid, lo = _split3(x)
    return _dot(m01, hi) + _dot(m01, mid) + _dot(m01, lo)


def _params(sem):
    return pltpu.CompilerParams(dimension_semantics=sem, vmem_limit_bytes=VMEM_LIMIT)


def _const_spec(shape):
    n = len(shape)
    return pl.BlockSpec(shape, lambda *_: (0,) * n)


def _ada_kernel(c_ref, w_ref, b_ref, o_ref):
    s = _silu(c_ref[...])
    o_ref[...] = jnp.dot(s, w_ref[...], preferred_element_type=F32,
                         precision=lax.Precision.HIGHEST) + b_ref[...]


def _ada(cc, w, b):
    rows = cc.shape[0]
    n = w.shape[1]
    tn = 1536
    return pl.pallas_call(
        _ada_kernel,
        grid=(n // tn,),
        in_specs=[pl.BlockSpec((rows, D_MODEL), lambda j: (0, 0)),
                  pl.BlockSpec((D_MODEL, tn), lambda j: (0, j)),
                  pl.BlockSpec((1, tn), lambda j: (0, j))],
        out_specs=pl.BlockSpec((rows, tn), lambda j: (0, j)),
        out_shape=jax.ShapeDtypeStruct((rows, n), F32),
        compiler_params=_params(("arbitrary",)),
        name="ada",
    )(cc, w, b)


def _inproj_kernel(x_ref, mod_ref, nw_ref, wm_ref, ws_ref, om_ref, os_ref, xn_ref):
    @pl.when(pl.program_id(2) == 0)
    def _():
        x = x_ref[0]
        y = x * lax.rsqrt(jnp.mean(x * x, axis=-1, keepdims=True) + EPS) * nw_ref[...]
        y = y * (1.0 + mod_ref[0, 1:2, :]) + mod_ref[0, 0:1, :]
        xn = y.astype(BF16)
        xn_ref[...] = xn
        os_ref[0] = _dot(xn, ws_ref[...])

    om_ref[0] = _dot(xn_ref[...], wm_ref[...]).astype(BF16)


def _inproj(x, mod, nw, w_main, w_small, tm):
    B, L, _ = x.shape
    tn = 2048
    return pl.pallas_call(
        _inproj_kernel,
        grid=(B, L // tm, P_MAIN // tn),
        in_specs=[pl.BlockSpec((1, tm, D_MODEL), lambda b, i, j: (b, i, 0)),
                  pl.BlockSpec((1, 2, D_MODEL), lambda b, i, j: (b, 0, 0)),
                  pl.BlockSpec((1, D_MODEL), lambda b, i, j: (0, 0)),
                  pl.BlockSpec((D_MODEL, tn), lambda b, i, j: (0, j)),
                  pl.BlockSpec((D_MODEL, P_SMALL), lambda b, i, j: (0, 0))],
        out_specs=[pl.BlockSpec((1, tm, tn), lambda b, i, j: (b, i, j)),
                   pl.BlockSpec((1, tm, P_SMALL), lambda b, i, j: (b, i, 0))],
        out_shape=[jax.ShapeDtypeStruct((B, L, P_MAIN), BF16),
                   jax.ShapeDtypeStruct((B, L, P_SMALL), F32)],
        scratch_shapes=[pltpu.VMEM((tm, D_MODEL), BF16)],
        compiler_params=_params(("parallel", "parallel", "arbitrary")),
        name="inproj",
    )(x, mod, nw, w_main, w_small)


GLA_MXU_LEVELS = 3
GLA_GROUP = 4


def _gla_consts(C, reverse):
    NL = int(np.log2(C))
    idx = np.arange(C)
    tri = (idx[None, :] <= idx[:, None]).astype(np.float32)
    mats = [tri]
    masks = [np.eye(C, dtype=np.float32)]
    refs, signs = [], []
    for lev in range(NL):
        h = 1 << lev
        blk = idx // (2 * h)
        half = (idx // h) % 2
        ref = blk * 2 * h + h - 1
        if lev < GLA_MXU_LEVELS:
            d = tri - tri[ref]
            d[half == 0] *= -1.0
            mats.append(d)
        else:
            sg = np.where(half == 1, 1.0, -1.0).astype(np.float32)
            if reverse:
                ref, sg = (C - 1 - ref)[::-1], sg[::-1]
            refs.append([int(ref[m * 2 * h]) for m in range(C // (2 * h))])
            signs.append(np.broadcast_to(sg[:, None], (C, GLA_DK)))
        masks.append(((blk[:, None] == blk[None, :]) & (half[:, None] == 1)
                      & (half[None, :] == 0)).astype(np.float32))
    if reverse:
        mats = [m[::-1, ::-1] for m in mats]
        masks = [m[::-1, ::-1] for m in masks]
    dmat = np.concatenate(mats, axis=0)
    return (jnp.asarray(dmat, BF16), jnp.asarray(np.stack(masks), F32),
            jnp.asarray(np.stack(signs), F32), refs)


def _gla_stream(q_ref, k_ref, v_ref, lr_ref, up_ref, bias_ref, dm_ref, mk_ref, sg_ref, s0_ref,
                o_ref, sf_ref, st_ref, e_s, b_s, oi_s, kv_s, *, C, TB, reverse, refs, part):
    NL = int(np.log2(C))
    nchunk = TB // C
    blk = pl.program_id(2)
    last = 0 if reverse else C - 1
    lr_off = S_LRB if reverse else S_LRF

    if part == "init":
        @pl.when(blk == 0)
        def _():
            st_ref[...] = s0_ref[0, 0]
        return
    if part == "final":
        @pl.when(blk == pl.num_programs(2) - 1)
        def _():
            sf_ref[0, 0] = st_ref[...]
        return

    lr = lr_ref[0][:, lr_off:lr_off + GLA_RANK]
    pre = _dot(lr.astype(BF16), up_ref[...].astype(BF16)) + bias_ref[...]
    g = -_softplus(-pre) * (1.0 / GLA_TAU)
    g_hi = g.astype(BF16)
    g_lo = (g - g_hi.astype(F32)).astype(BF16)
    gs = jnp.concatenate([g_hi, g_lo], axis=1)

    for c in range(nchunk):
        ex = _dot(dm_ref[...], gs[c * C:(c + 1) * C])
        ex = ex[:, :GLA_DK] + ex[:, GLA_DK:]
        b = ex[0:C]
        b_s[c] = b
        e_s[c, 0:GLA_MXU_LEVELS * C, :] = jnp.exp(ex[C:(1 + GLA_MXU_LEVELS) * C])
        for li, lev in enumerate(range(GLA_MXU_LEVELS, NL)):
            h2 = 2 << lev
            bref = jnp.concatenate([jnp.broadcast_to(b[r:r + 1, :], (h2, GLA_DK)) for r in refs[li]], axis=0)
            e_s[c, lev * C:(lev + 1) * C, :] = jnp.exp(sg_ref[li] * (b - bref))
        e_s[c, NL * C:(NL + 1) * C, :] = jnp.exp(b[last:last + 1, :] - b)

    for c in range(nchunk):
        rows = slice(c * C, (c + 1) * C)
        q = q_ref[0, rows, :].astype(F32) * (GLA_DK ** -0.5)
        k = k_ref[0, rows, :].astype(F32)
        v = v_ref[0, rows, :]
        att = mk_ref[0] * _dot_nt(q.astype(BF16), k.astype(BF16))
        for lev in range(NL):
            e_l = e_s[c, lev * C:(lev + 1) * C, :]
            att = att + mk_ref[lev + 1] * _dot_nt((q * e_l).astype(BF16), (k * e_l).astype(BF16))
        oi_s[rows, :] = _dot(att.astype(BF16), v)
        kv_s[c] = _dot_tn((k * e_s[c, NL * C:(NL + 1) * C, :]).astype(BF16), v)

    order = list(reversed(range(nchunk))) if reverse else list(range(nchunk))
    st = st_ref[...]
    for g0 in range(0, nchunk, GLA_GROUP):
        grp = order[g0:g0 + GLA_GROUP]
        wcat = jnp.concatenate([st.astype(BF16)] + [kv_s[c].astype(BF16) for c in grp[:-1]], axis=0)
        for i, c in enumerate(grp):
            rows = slice(c * C, (c + 1) * C)
            q = q_ref[0, rows, :].astype(F32) * (GLA_DK ** -0.5)
            expo = b_s[c]
            pieces = []
            for j in range(i - 1, -2, -1):
                pieces.insert(0, (q * jnp.exp(expo)).astype(BF16))
                if j >= 0:
                    expo = expo + b_s[grp[j], last:last + 1, :]
            lhs = jnp.concatenate(pieces, axis=1)
            o_ref[0, rows, :] = oi_s[rows, :] + _dot(lhs, wcat[:(i + 1) * GLA_DK, :])
        for c in grp:
            dec = jnp.transpose(jnp.broadcast_to(jnp.exp(b_s[c, last:last + 1, :]), (GLA_DK, GLA_DK)))
            st = jnp.concatenate([dec] * (GLA_DV // GLA_DK), axis=1) * st + kv_s[c]
    st_ref[...] = st


GLA_STREAM_IN, GLA_STREAM_OUT, GLA_STREAM_SCRATCH = 10, 2, 5


def _gla_kernel(*refs, C, TB, lv_refs):
    n_in, n_out, n_scr = GLA_STREAM_IN, GLA_STREAM_OUT, GLA_STREAM_SCRATCH
    ins = [refs[d * n_in:(d + 1) * n_in] for d in range(2)]
    outs = [refs[2 * n_in + d * n_out:2 * n_in + (d + 1) * n_out] for d in range(2)]
    base = 2 * (n_in + n_out)
    scr = [refs[base + d * n_scr:base + (d + 1) * n_scr] for d in range(2)]
    for part in ("init", "body", "final"):
        for d, reverse in enumerate((False, True)):
            _gla_stream(*ins[d], *outs[d], *scr[d], C=C, TB=TB, reverse=reverse, refs=lv_refs[d], part=part)


def _gla_bidir(pm, ps, up, bias, s0, *, TB):
    B, L, _ = pm.shape
    C = GLA_CHUNK
    nb = L // TB
    assert C >= (2 << GLA_MXU_LEVELS)
    qb, kb, vb = P_Q // GLA_DK, P_K // GLA_DK, P_V // GLA_DV
    in_specs, args, out_specs, out_shape, scratch, lv_refs = [], [], [], [], [], []
    for d, reverse in enumerate((False, True)):
        dmat, masks, signs, refs = _gla_consts(C, reverse)
        lv_refs.append(refs)
        tmap = (lambda i: nb - 1 - i) if reverse else (lambda i: i)
        in_specs += [
            pl.BlockSpec((1, TB, GLA_DK), lambda b, h, i, tmap=tmap: (b, tmap(i), qb + h)),
            pl.BlockSpec((1, TB, GLA_DK), lambda b, h, i, tmap=tmap: (b, tmap(i), kb + h)),
            pl.BlockSpec((1, TB, GLA_DV), lambda b, h, i, tmap=tmap: (b, tmap(i), vb + h)),
            pl.BlockSpec((1, TB, P_SMALL), lambda b, h, i, tmap=tmap: (b, tmap(i), 0)),
            pl.BlockSpec((GLA_RANK, GLA_DK), lambda b, h, i: (0, h)),
            pl.BlockSpec((1, GLA_DK), lambda b, h, i: (0, h)),
            _const_spec(dmat.shape),
            _const_spec(masks.shape),
            _const_spec(signs.shape),
            pl.BlockSpec((1, 1, GLA_DK, GLA_DV), lambda b, h, i: (b, h, 0, 0)),
        ]
        args += [pm, pm, pm, ps, up[d], bias[d], dmat, masks, signs, s0[d]]
        out_specs += [pl.BlockSpec((1, TB, GLA_DV), lambda b, h, i, tmap=tmap: (b, tmap(i), h)),
                      pl.BlockSpec((1, 1, GLA_DK, GLA_DV), lambda b, h, i: (b, h, 0, 0))]
        out_shape += [jax.ShapeDtypeStruct((B, L, GLA_V), F32),
                      jax.ShapeDtypeStruct((B, GLA_HEADS, GLA_DK, GLA_DV), F32)]
        scratch += [pltpu.VMEM((GLA_DK, GLA_DV), F32),
                    pltpu.VMEM((TB // C, masks.shape[0] * C, GLA_DK), F32),
                    pltpu.VMEM((TB // C, C, GLA_DK), F32),
                    pltpu.VMEM((TB, GLA_DV), F32),
                    pltpu.VMEM((TB // C, GLA_DK, GLA_DV), F32)]
    assert len(in_specs) == 2 * GLA_STREAM_IN and len(scratch) == 2 * GLA_STREAM_SCRATCH
    return pl.pallas_call(
        functools.partial(_gla_kernel, C=C, TB=TB, lv_refs=lv_refs),
        grid=(B, GLA_HEADS, nb),
        in_specs=in_specs,
        out_specs=out_specs,
        out_shape=out_shape,
        scratch_shapes=scratch,
        compiler_params=_params(("parallel", "parallel", "arbitrary")),
        name="gla",
    )(*args)


def _conv_shift_mats(T, rowlen):
    t = np.arange(T)
    mats = []
    for j in range(SSM_CONV):
        off = j - CONV_LEFT
        if off == 0:
            continue
        src = t + off
        ok = (src // rowlen == t // rowlen) & (src >= 0) & (src < T)
        m = np.zeros((T, T), np.float32)
        m[t[ok], src[ok]] = 1.0
        mats.append(m)
    return jnp.asarray(np.stack(mats), BF16)


def _conv_silu(u, w, b, sh_ref):
    acc = b + u.astype(F32) * w[CONV_LEFT:CONV_LEFT + 1, :]
    taps = [j for j in range(SSM_CONV) if j != CONV_LEFT]
    for i, j in enumerate(taps):
        acc = acc + _dot(sh_ref[i], u) * w[j:j + 1, :]
    return _silu(acc)


def _ssd_kernel(*refs, C, TB, reverse, finalize, conv, emit, lane0):
    refs = list(refs)
    xs_ref, bm_ref, cm_ref, ps_ref = refs[:4]
    refs = refs[4:]
    if conv:
        wx_ref, wb_ref, wc_ref, bx_ref, bb_ref, bc_ref, sh_ref = refs[:7]
        refs = refs[7:]
    dtb_ref, alog_ref, tri_ref, s0_ref = refs[:4]
    refs = refs[4:]
    if finalize:
        z_ref, yp_ref, dsk_ref, nw_ref = refs[:4]
        refs = refs[4:]
    y_ref, sf_ref = refs[:2]
    refs = refs[2:]
    if emit:
        xo_ref, bo_ref, co_ref = refs[:3]
        refs = refs[3:]
    st_ref, xc_s, bc_s, cc_s, ya_s = refs
    nchunk = TB // C
    blk = pl.program_id(1)
    G, HPG, P, N = SSM_GROUPS, SSM_HPG, SSM_HEADDIM, SSM_STATE

    @pl.when(blk == 0)
    def _():
        st_ref[...] = s0_ref[0]

    if conv:
        xc_s[...] = _conv_silu(xs_ref[0], wx_ref[...], bx_ref[...], sh_ref).astype(BF16)
        bc_s[...] = _conv_silu(bm_ref[0], wb_ref[...], bb_ref[...], sh_ref).astype(BF16)
        cc_s[...] = _conv_silu(cm_ref[0], wc_ref[...], bc_ref[...], sh_ref).astype(BF16)
        if emit:
            xo_ref[0] = xc_s[...]
            bo_ref[0] = bc_s[...]
            co_ref[0] = cc_s[...]
    else:
        xc_s[...] = xs_ref[0]
        bc_s[...] = bm_ref[0]
        cc_s[...] = cm_ref[0]

    neg_a = -jnp.exp(alog_ref[...])
    ti = lax.broadcasted_iota(jnp.int32, (C, C), 0)
    si = lax.broadcasted_iota(jnp.int32, (C, C), 1)
    keep = (si >= ti) if reverse else (si <= ti)
    lo = lax.broadcasted_iota(jnp.int32, (1, 2 * P), 1) < P
    last = 0 if reverse else C - 1
    nheads = G * HPG

    for c in (reversed(range(nchunk)) if reverse else range(nchunk)):
        rows = slice(c * C, (c + 1) * C)
        dt = _softplus(ps_ref[0, rows, :] + dtb_ref[...])
        cum = _dot01(tri_ref[...], dt * neg_a) * LOG2E
        cum_t = jnp.transpose(cum)[lane0:lane0 + nheads, :]
        dt_t = jnp.transpose(dt)[lane0:lane0 + nheads, :]
        w_t = (dt_t * jnp.exp2(cum_t[:, last:last + 1] - cum_t)).astype(BF16)
        cdl_t = cum_t - jnp.log2(dt_t)
        dec_all = jnp.exp2(cum[last:last + 1, :])
        for g in range(G):
            bm_g = bc_s[rows, g * N:(g + 1) * N]
            cm_g = cc_s[rows, g * N:(g + 1) * N]
            cb = _dot_nt(cm_g, bm_g).astype(BF16)
            bm_t = jnp.transpose(bm_g.astype(F32)).astype(BF16)
            for pr in range(HPG // 2):
                lanes = slice(g * SSM_GW + pr * 2 * P, g * SSM_GW + (pr + 1) * 2 * P)
                x_pair = xc_s[rows, lanes]
                s_pair = st_ref[g, :, pr * 2 * P:(pr + 1) * 2 * P]
                s_bf = s_pair.astype(BF16)
                y = None
                ds = None
                decs = []
                for half in range(2):
                    j = g * HPG + pr * 2 + half
                    lane = lane0 + j
                    sel = lo if half == 0 else jnp.logical_not(lo)
                    bc_ = jnp.broadcast_to(cum[:, lane:lane + 1], (C, C))
                    m = cb * jnp.exp2(jnp.where(keep, bc_ - cdl_t[j:j + 1, :], NEG_BIG)).astype(BF16)
                    cd = cm_g * jnp.exp2(bc_).astype(BF16)
                    lhs = jnp.concatenate([m, cd], axis=1)
                    rhs = jnp.concatenate([jnp.where(sel, x_pair, jnp.zeros_like(x_pair)),
                                           jnp.where(sel, s_bf, jnp.zeros_like(s_bf))], axis=0)
                    yh = _dot(lhs, rhs)
                    dh = _dot(bm_t * w_t[j:j + 1, :],
                              jnp.where(sel, x_pair, jnp.zeros_like(x_pair)))
                    y = yh if y is None else y + yh
                    ds = dh if ds is None else ds + dh
                    decs.append(dec_all[:, lane:lane + 1])
                ya_s[rows, lanes] = y
                dec = jnp.where(lo, decs[0], decs[1])
                st_ref[g, :, pr * 2 * P:(pr + 1) * 2 * P] = dec * s_pair + ds

    if finalize:
        for g in range(G):
            lanes = slice(g * SSM_GW, (g + 1) * SSM_GW)
            y = ya_s[:, lanes] + yp_ref[0, :, lanes] + dsk_ref[:, lanes] * xc_s[:, lanes].astype(F32)
            y = y * _silu(z_ref[0, :, lanes].astype(F32))
            y = y * lax.rsqrt(jnp.mean(y * y, axis=-1, keepdims=True) + EPS) * nw_ref[:, lanes]
            y_ref[0, :, lanes] = y.astype(y_ref.dtype)
    else:
        y_ref[0] = ya_s[...]

    @pl.when(blk == pl.num_programs(1) - 1)
    def _():
        sf_ref[0] = st_ref[...]


def _ssd_pass(src, ps, conv_w, conv_b, dt_bias, a_log, s0, *, reverse, finalize, TB, rowlen, emit=False,
              y_prev=None, z_src=None, d_skip_x=None, norm_w=None):
    (xs_a, xs_o), (bm_a, bm_o), (cm_a, cm_o) = src
    B, L, _ = ps.shape
    C = SSD_CHUNK
    nb = L // TB
    G = SSM_GROUPS
    conv = conv_w is not None
    tmap = (lambda i: nb - 1 - i) if reverse else (lambda i: i)
    idx = np.arange(C)
    tri = (idx[None, :] >= idx[:, None]) if reverse else (idx[None, :] <= idx[:, None])
    tri = jnp.asarray(tri.astype(np.float32), BF16)
    in_specs = [
        pl.BlockSpec((1, TB, SSM_INNER), lambda b, i: (b, tmap(i), xs_o)),
        pl.BlockSpec((1, TB, SSM_BC), lambda b, i: (b, tmap(i), bm_o)),
        pl.BlockSpec((1, TB, SSM_BC), lambda b, i: (b, tmap(i), cm_o)),
        pl.BlockSpec((1, TB, P_SMALL), lambda b, i: (b, tmap(i), 0)),
    ]
    args = [xs_a, bm_a, cm_a, ps]
    if conv:
        nx = SSM_INNER // SSM_BC
        in_specs += [
            pl.BlockSpec((SSM_CONV, SSM_INNER), lambda b, i: (0, 0)),
            pl.BlockSpec((SSM_CONV, SSM_BC), lambda b, i: (0, nx)),
            pl.BlockSpec((SSM_CONV, SSM_BC), lambda b, i: (0, nx + 1)),
            pl.BlockSpec((1, SSM_INNER), lambda b, i: (0, 0)),
            pl.BlockSpec((1, SSM_BC), lambda b, i: (0, nx)),
            pl.BlockSpec((1, SSM_BC), lambda b, i: (0, nx + 1)),
            _const_spec((SSM_CONV - 1, TB, TB)),
        ]
        args += [conv_w, conv_w, conv_w, conv_b, conv_b, conv_b, _conv_shift_mats(TB, rowlen)]
    in_specs += [_const_spec((1, P_SMALL)), _const_spec((1, P_SMALL)), _const_spec((C, C)),
                 pl.BlockSpec((1, G, SSM_STATE, SSM_GW), lambda b, i: (b, 0, 0, 0))]
    args += [dt_bias, a_log, tri, s0]
    if finalize:
        z_a, z_o = z_src
        in_specs += [
            pl.BlockSpec((1, TB, SSM_INNER), lambda b, i: (b, tmap(i), z_o)),
            pl.BlockSpec((1, TB, SSM_INNER), lambda b, i: (b, tmap(i), 0)),
            _const_spec((1, SSM_INNER)), _const_spec((1, SSM_INNER)),
        ]
        args += [z_a, y_prev, d_skip_x, norm_w]
    out_specs = [pl.BlockSpec((1, TB, SSM_INNER), lambda b, i: (b, tmap(i), 0)),
                 pl.BlockSpec((1, G, SSM_STATE, SSM_GW), lambda b, i: (b, 0, 0, 0))]
    out_shape = [jax.ShapeDtypeStruct((B, L, SSM_INNER), BF16 if finalize else F32),
                 jax.ShapeDtypeStruct((B, G, SSM_STATE, SSM_GW), F32)]
    if emit:
        out_specs += [pl.BlockSpec((1, TB, SSM_INNER), lambda b, i: (b, tmap(i), 0)),
                      pl.BlockSpec((1, TB, SSM_BC), lambda b, i: (b, tmap(i), 0)),
                      pl.BlockSpec((1, TB, SSM_BC), lambda b, i: (b, tmap(i), 0))]
        out_shape += [jax.ShapeDtypeStruct((B, L, SSM_INNER), BF16),
                      jax.ShapeDtypeStruct((B, L, SSM_BC), BF16),
                      jax.ShapeDtypeStruct((B, L, SSM_BC), BF16)]
    kern = functools.partial(_ssd_kernel, C=C, TB=TB, reverse=reverse, finalize=finalize,
                             conv=conv, emit=emit, lane0=S_DTB if reverse else S_DTF)
    return pl.pallas_call(
        kern,
        grid=(B, nb),
        in_specs=in_specs,
        out_specs=out_specs,
        out_shape=out_shape,
        scratch_shapes=[pltpu.VMEM((G, SSM_STATE, SSM_GW), F32),
                        pltpu.VMEM((TB, SSM_INNER), BF16),
                        pltpu.VMEM((TB, SSM_BC), BF16),
                        pltpu.VMEM((TB, SSM_BC), BF16),
                        pltpu.VMEM((TB, SSM_INNER), F32)],
        compiler_params=_params(("parallel", "arbitrary")),
        name="ssd_" + ("rev" if reverse else "fwd") + ("_fin" if finalize else ""),
    )(*args)


def _merge_kernel(of_ref, ob_ref, r_ref, gnw_ref, sb_ref, ga_ref, gb_ref, x_ref, g1_ref, wpa_ref, wpb_ref,
                  wout_ref, h_ref):
    o = of_ref[0] + ob_ref[0]
    heads = []
    for h in range(GLA_HEADS):
        oh = o[:, h * GLA_DV:(h + 1) * GLA_DV]
        heads.append(oh * lax.rsqrt(jnp.mean(oh * oh, axis=-1, keepdims=True) + EPS) * gnw_ref[...])
    oa = (jnp.concatenate(heads, axis=1) * _silu(r_ref[0].astype(F32))).astype(BF16)
    ya = _dot(oa, wpa_ref[...])
    yb = _dot(sb_ref[0], wpb_ref[...])
    m = _sigmoid(ga_ref[0].astype(F32)) * ya + _sigmoid(gb_ref[0].astype(F32)) * yb
    mix = _dot(m.astype(BF16), wout_ref[...])
    h_ref[0] = x_ref[0] + g1_ref[0] * mix


def _merge(o_f, o_b, gla_norm_w, s_b, pm, x, g1, w_pa, w_pb, w_out, tm):
    B, L, _ = x.shape
    gab, gbb, rb = P_GA // D_MODEL, P_GB // D_MODEL, P_R // GLA_V
    return pl.pallas_call(
        _merge_kernel,
        grid=(B, L // tm),
        in_specs=[pl.BlockSpec((1, tm, GLA_V), lambda b, i: (b, i, 0)),
                  pl.BlockSpec((1, tm, GLA_V), lambda b, i: (b, i, 0)),
                  pl.BlockSpec((1, tm, GLA_V), lambda b, i: (b, i, rb)),
                  _const_spec((1, GLA_DV)),
                  pl.BlockSpec((1, tm, SSM_INNER), lambda b, i: (b, i, 0)),
                  pl.BlockSpec((1, tm, D_MODEL), lambda b, i: (b, i, gab)),
                  pl.BlockSpec((1, tm, D_MODEL), lambda b, i: (b, i, gbb)),
                  pl.BlockSpec((1, tm, D_MODEL), lambda b, i: (b, i, 0)),
                  pl.BlockSpec((1, 1, D_MODEL), lambda b, i: (b, 0, 0)),
                  _const_spec(w_pa.shape), _const_spec(w_pb.shape), _const_spec(w_out.shape)],
        out_specs=pl.BlockSpec((1, tm, D_MODEL), lambda b, i: (b, i, 0)),
        out_shape=jax.ShapeDtypeStruct((B, L, D_MODEL), F32),
        compiler_params=_params(("parallel", "parallel")),
        name="merge",
    )(o_f, o_b, pm, gla_norm_w, s_b, pm, pm, x, g1, w_pa, w_pb, w_out)


FFN_CHUNK = 256


def _ffn_kernel(h_ref, mod_ref, n2_ref, fw_ref, wg_ref, wu_ref, wd_ref, o_ref):
    h = h_ref[0]
    hn = h * lax.rsqrt(jnp.mean(h * h, axis=-1, keepdims=True) + EPS) * n2_ref[...]
    hn = (hn * (1.0 + mod_ref[0, 1:2, :]) + mod_ref[0, 0:1, :]).astype(BF16)
    acc = jnp.zeros(h.shape, F32)
    for c in range(D_FF // FFN_CHUNK):
        cols = slice(c * FFN_CHUNK, (c + 1) * FFN_CHUNK)
        gt = _dot(hn, wg_ref[:, cols])
        up = _dot(hn, wu_ref[:, cols])
        acc = acc + _dot((_silu(gt) * up).astype(BF16), wd_ref[cols, :])
    h2 = h + mod_ref[0, 2:3, :] * acc
    o_ref[0] = h2 * lax.rsqrt(jnp.mean(h2 * h2, axis=-1, keepdims=True) + EPS) * fw_ref[...]


def _ffn(h, mod, n2w, fw, w_gate, w_up, w_down, tm):
    B, L, _ = h.shape
    return pl.pallas_call(
        _ffn_kernel,
        grid=(B, L // tm),
        in_specs=[pl.BlockSpec((1, tm, D_MODEL), lambda b, i: (b, i, 0)),
                  pl.BlockSpec((1, 3, D_MODEL), lambda b, i: (b, 0, 0)),
                  _const_spec((1, D_MODEL)), _const_spec((1, D_MODEL)),
                  pl.BlockSpec(w_gate.shape, lambda b, i: (0, 0), pipeline_mode=pl.Buffered(1)),
                  pl.BlockSpec(w_up.shape, lambda b, i: (0, 0), pipeline_mode=pl.Buffered(1)),
                  pl.BlockSpec(w_down.shape, lambda b, i: (0, 0), pipeline_mode=pl.Buffered(1))],
        out_specs=pl.BlockSpec((1, tm, D_MODEL), lambda b, i: (b, i, 0)),
        out_shape=jax.ShapeDtypeStruct((B, L, D_MODEL), F32),
        compiler_params=_params(("parallel", "parallel")),
        name="ffn",
    )(h, mod, n2w, fw, w_gate, w_up, w_down)


def _pick_block(L, pref):
    tb = min(L, pref)
    assert L % tb == 0
    return tb


def kernel(x, c, ctx, c_ctx, w_ada, b_ada, norm1_w, w_in, gla_up_f, gla_bias_f, gla_up_b, gla_bias_b,
           gla_norm_w, conv_w, conv_b, dt_bias_f, dt_bias_b, a_log_f, a_log_b, d_skip, ssm_norm_w,
           w_pa, w_pb, w_out, norm2_w, w_gate, w_up, w_down, final_norm_w):
    B, L, D = x.shape
    Lc = ctx.shape[1]
    depth = w_ada.shape[0]
    assert depth == 1 and D == D_MODEL
    assert L % GRID_W == 0 and L % SSD_CHUNK == 0 and Lc % SSD_CHUNK == 0
    lay = 0

    nrow = -(-(B + 1) // 8) * 8
    cc = jnp.zeros((nrow, D), F32).at[:B].set(c).at[B].set(c_ctx)
    ada = _ada(cc, w_ada[lay], b_ada[lay][None, :])
    sh1, sc1, g1, sh2, sc2, g2 = [ada[:, i * D:(i + 1) * D] for i in range(6)]
    mod1 = jnp.stack([sh1[:B], sc1[:B]], axis=1)
    mod1_c = jnp.broadcast_to(jnp.stack([sh1[B], sc1[B]])[None], (B, 2, D))
    mod2 = jnp.stack([sh2[:B], sc2[:B], g2[:B]], axis=1)
    g1_l = g1[:B, None, :]

    wi = w_in[lay].astype(BF16)
    o = _IN_OFF
    w_main = jnp.concatenate([wi[:, o[6]:o[8]], wi[:, o[0]:o[4]], wi[:, o[8]:o[10]], wi[:, o[12]:o[14]]], axis=1)
    w_small = jnp.concatenate([wi[:, o[4]:o[6]], wi[:, o[10]:o[12]],
                               jnp.zeros((D, P_SMALL - 2 * GLA_RANK - 2 * SSM_HEADS), BF16)], axis=1)
    nw1 = norm1_w[lay][None, :]

    at_lanes = lambda p, off: jnp.zeros((1, P_SMALL), F32).at[0, off:off + SSM_HEADS].set(p[lay])
    dtb_f, al_f = at_lanes(dt_bias_f, S_DTF), at_lanes(a_log_f, S_DTF)
    dtb_b, al_b = at_lanes(dt_bias_b, S_DTB), at_lanes(a_log_b, S_DTB)
    cw, cb_ = conv_w[lay], conv_b[lay][None, :]
    up_f, up_b = gla_up_f[lay], gla_up_b[lay]
    bi_f, bi_b = gla_bias_f[lay][None, :], gla_bias_b[lay][None, :]

    gla_zero = jnp.zeros((B, GLA_HEADS, GLA_DK, GLA_DV), F32)
    ssd_zero = jnp.zeros((B, SSM_GROUPS, SSM_STATE, SSM_GW), F32)

    pm_c, ps_c = _inproj(ctx, mod1_c, nw1, w_main, w_small, _pick_block(Lc, 256))
    _, sg_f, _, sg_b = _gla_bidir(pm_c, ps_c, (up_f, up_b), (bi_f, bi_b), (gla_zero, gla_zero),
                                  TB=_pick_block(Lc, 256))
    src_c = ((pm_c, P_XS // SSM_INNER), (pm_c, P_BM // SSM_BC), (pm_c, P_CM // SSM_BC))
    _, ss_f = _ssd_pass(src_c, ps_c, cw, cb_, dtb_f, al_f, ssd_zero, reverse=False, finalize=False,
                        TB=Lc, rowlen=Lc)
    _, ss_b = _ssd_pass(src_c, ps_c, cw, cb_, dtb_b, al_b, ssd_zero, reverse=True, finalize=False,
                        TB=Lc, rowlen=Lc)

    pm, ps = _inproj(x, mod1, nw1, w_main, w_small, _pick_block(L, 1024))
    tbg = _pick_block(L, 1024)
    og_f, _, og_b, _ = _gla_bidir(pm, ps, (up_f, up_b), (bi_f, bi_b), (sg_f, sg_b), TB=tbg)
    tbs = _pick_block(L, 256)
    src = ((pm, P_XS // SSM_INNER), (pm, P_BM // SSM_BC), (pm, P_CM // SSM_BC))
    ys_f, _, xc, bc, cc = _ssd_pass(src, ps, cw, cb_, dtb_f, al_f, ss_f, reverse=False, finalize=False,
                                    TB=tbs, rowlen=GRID_W, emit=True)
    o_b, _ = _ssd_pass(((xc, 0), (bc, 0), (cc, 0)), ps, None, None, dtb_b, al_b, ss_b, reverse=True,
                       finalize=True, TB=tbs, rowlen=GRID_W, y_prev=ys_f, z_src=(pm, P_Z // SSM_INNER),
                       d_skip_x=jnp.repeat(d_skip[lay], SSM_HEADDIM)[None, :],
                       norm_w=ssm_norm_w[lay][None, :])

    h = _merge(og_f, og_b, gla_norm_w[lay][None, :], o_b, pm, x, g1_l, w_pa[lay].astype(BF16),
               w_pb[lay].astype(BF16), w_out[lay].astype(BF16), _pick_block(L, 512))
    return _ffn(h, mod2, norm2_w[lay][None, :], final_norm_w[None, :], w_gate[lay].astype(BF16),
                w_up[lay].astype(BF16), w_down[lay].astype(BF16), _pick_block(L, 512))
```

```python
import functools

import numpy as np
import jax
import jax.numpy as jnp
from jax import lax
from jax.experimental import pallas as pl
from jax.experimental.pallas import tpu as pltpu

F32 = jnp.float32
BF16 = jnp.bfloat16

D_MODEL = 1024
GRID_W = 64
EPS = 1e-6

GLA_HEADS = 4
GLA_DK = 128
GLA_DV = 256
GLA_QK = GLA_HEADS * GLA_DK
GLA_V = GLA_HEADS * GLA_DV
GLA_RANK = 16
GLA_TAU = 16.0

SSM_INNER = 2 * D_MODEL
SSM_HEADDIM = 64
SSM_HEADS = SSM_INNER // SSM_HEADDIM
SSM_GROUPS = 4
SSM_HPG = SSM_HEADS // SSM_GROUPS
SSM_STATE = 128
SSM_BC = SSM_GROUPS * SSM_STATE
SSM_CONV = 4
CONV_LEFT = 2
SSM_GW = SSM_HPG * SSM_HEADDIM

D_FF = ((8 * D_MODEL // 3 + 255) // 256) * 256

_IN_WIDTHS = (GLA_QK, GLA_QK, GLA_V, GLA_V, GLA_RANK, GLA_RANK,
              SSM_INNER, SSM_INNER, SSM_BC, SSM_BC, SSM_HEADS, SSM_HEADS, D_MODEL, D_MODEL)
_IN_OFF = np.concatenate([[0], np.cumsum(_IN_WIDTHS)]).tolist()

P_Z, P_XS = 0, 2048
P_Q, P_K, P_V, P_R = 4096, 4608, 5120, 6144
P_BM, P_CM = 7168, 7680
P_GA, P_GB = 8192, 9216
P_MAIN = 10240
S_LRF, S_LRB, S_DTF, S_DTB = 0, 16, 32, 64
P_SMALL = 128

GLA_CHUNK = 128
SSD_CHUNK = 128

VMEM_LIMIT = 56 * 1024 * 1024
NEG_BIG = -1e30
LOG2E = 1.4426950408889634


def _sigmoid(x):
    return 1.0 / (1.0 + jnp.exp(-x))


def _silu(x):
    return x * _sigmoid(x)


def _softplus(x):
    return jnp.maximum(x, 0.0) + jnp.log(1.0 + jnp.exp(-jnp.abs(x)))


def _split3(x):
    hi = x.astype(BF16)
    r1 = x - hi.astype(F32)
    mid = r1.astype(BF16)
    lo = (r1 - mid.astype(F32)).astype(BF16)
    return hi, mid, lo


def _dot(a, b):
    return jnp.dot(a, b, preferred_element_type=F32)


def _dot_nt(a, b):
    return lax.dot_general(a, b, (((1,), (1,)), ((), ())), preferred_element_type=F32)


def _dot_tn(a, b):
    return lax.dot_general(a, b, (((0,), (0,)), ((), ())), preferred_element_type=F32)


def _dot01(m01, x):
    hi, mid, lo = _split3(x)
    return _dot(m01, hi) + _dot(m01, mid) + _dot(m01, lo)


def _params(sem):
    return pltpu.CompilerParams(dimension_semantics=sem, vmem_limit_bytes=VMEM_LIMIT)


def _const_spec(shape):
    n = len(shape)
    return pl.BlockSpec(shape, lambda *_: (0,) * n)


def _ada_kernel(c_ref, w_ref, b_ref, o_ref):
    s = _silu(c_ref[...])
    w = w_ref[...]
    s_hi = s.astype(BF16)
    s_lo = (s - s_hi.astype(F32)).astype(BF16)
    w_hi = w.astype(BF16)
    w_lo = (w - w_hi.astype(F32)).astype(BF16)
    o_ref[...] = _dot(s_hi, w_hi) + _dot(s_hi, w_lo) + _dot(s_lo, w_hi) + b_ref[...]


def _ada(cc, w, b):
    rows = cc.shape[0]
    n = w.shape[1]
    tn = 1536
    return pl.pallas_call(
        _ada_kernel,
        grid=(n // tn,),
        in_specs=[pl.BlockSpec((rows, D_MODEL), lambda j: (0, 0)),
                  pl.BlockSpec((D_MODEL, tn), lambda j: (0, j)),
                  pl.BlockSpec((1, tn), lambda j: (0, j))],
        out_specs=pl.BlockSpec((rows, tn), lambda j: (0, j)),
        out_shape=jax.ShapeDtypeStruct((rows, n), F32),
        compiler_params=_params(("arbitrary",)),
        name="ada",
    )(cc, w, b)


def _regroup_kernel(w_ref, wm_ref, ws_ref):
    o = _IN_OFF
    w = w_ref[...]
    parts = [w[:, o[6]:o[8]], w[:, o[0]:o[4]], w[:, o[8]:o[10]], w[:, o[12]:o[14]]]
    wm_ref[...] = jnp.concatenate(parts, axis=1).astype(BF16)
    small = [w[:, o[4]:o[6]], w[:, o[10]:o[12]],
             jnp.zeros((w.shape[0], P_SMALL - 2 * GLA_RANK - 2 * SSM_HEADS), F32)]
    ws_ref[...] = jnp.concatenate(small, axis=1).astype(BF16)


def _regroup(w_in):
    D, n = w_in.shape
    tr = 128
    return pl.pallas_call(
        _regroup_kernel,
        grid=(D // tr,),
        in_specs=[pl.BlockSpec((tr, n), lambda i: (i, 0))],
        out_specs=[pl.BlockSpec((tr, P_MAIN), lambda i: (i, 0)),
                   pl.BlockSpec((tr, P_SMALL), lambda i: (i, 0))],
        out_shape=[jax.ShapeDtypeStruct((D, P_MAIN), BF16), jax.ShapeDtypeStruct((D, P_SMALL), BF16)],
        compiler_params=_params(("arbitrary",)),
        name="regroup",
    )(w_in)


def _inproj_kernel(x_ref, mod_ref, nw_ref, wm_ref, ws_ref, om_ref, os_ref, xn_ref):
    @pl.when(pl.program_id(2) == 0)
    def _():
        x = x_ref[0]
        y = x * lax.rsqrt(jnp.mean(x * x, axis=-1, keepdims=True) + EPS) * nw_ref[...]
        y = y * (1.0 + mod_ref[0, 1:2, :]) + mod_ref[0, 0:1, :]
        xn = y.astype(BF16)
        xn_ref[...] = xn
        os_ref[0] = _dot(xn, ws_ref[...])

    om_ref[0] = _dot(xn_ref[...], wm_ref[...]).astype(BF16)


def _inproj(x, mod, nw, w_main, w_small, tm):
    B, L, _ = x.shape
    tn = 2048
    return pl.pallas_call(
        _inproj_kernel,
        grid=(B, L // tm, P_MAIN // tn),
        in_specs=[pl.BlockSpec((1, tm, D_MODEL), lambda b, i, j: (b, i, 0)),
                  pl.BlockSpec((1, 2, D_MODEL), lambda b, i, j: (b, 0, 0)),
                  pl.BlockSpec((1, D_MODEL), lambda b, i, j: (0, 0)),
                  pl.BlockSpec((D_MODEL, tn), lambda b, i, j: (0, j)),
                  pl.BlockSpec((D_MODEL, P_SMALL), lambda b, i, j: (0, 0))],
        out_specs=[pl.BlockSpec((1, tm, tn), lambda b, i, j: (b, i, j)),
                   pl.BlockSpec((1, tm, P_SMALL), lambda b, i, j: (b, i, 0))],
        out_shape=[jax.ShapeDtypeStruct((B, L, P_MAIN), BF16),
                   jax.ShapeDtypeStruct((B, L, P_SMALL), F32)],
        scratch_shapes=[pltpu.VMEM((tm, D_MODEL), BF16)],
        compiler_params=_params(("parallel", "parallel", "arbitrary")),
        name="inproj",
    )(x, mod, nw, w_main, w_small)


GLA_MXU_LEVELS = 3
GLA_GROUP = 4


def _gla_consts(C, reverse):
    NL = int(np.log2(C))
    idx = np.arange(C)
    tri = (idx[None, :] <= idx[:, None]).astype(np.float32)
    mats = [tri]
    masks = [np.eye(C, dtype=np.float32)]
    refs, signs = [], []
    for lev in range(NL):
        h = 1 << lev
        blk = idx // (2 * h)
        half = (idx // h) % 2
        ref = blk * 2 * h + h - 1
        if lev < GLA_MXU_LEVELS:
            d = tri - tri[ref]
            d[half == 0] *= -1.0
            mats.append(d)
        else:
            sg = np.where(half == 1, 1.0, -1.0).astype(np.float32)
            if reverse:
                ref, sg = (C - 1 - ref)[::-1], sg[::-1]
            refs.append([int(ref[m * 2 * h]) for m in range(C // (2 * h))])
            signs.append(np.broadcast_to(sg[:, None], (C, GLA_DK)))
        masks.append(((blk[:, None] == blk[None, :]) & (half[:, None] == 1)
                      & (half[None, :] == 0)).astype(np.float32))
    if reverse:
        mats = [m[::-1, ::-1] for m in mats]
        masks = [m[::-1, ::-1] for m in masks]
    dmat = np.concatenate(mats, axis=0)
    return (jnp.asarray(dmat, BF16), jnp.asarray(np.stack(masks), F32),
            jnp.asarray(np.stack(signs), F32), refs)


def _gla_stream(q_ref, k_ref, v_ref, lr_ref, up_ref, bias_ref, dm_ref, mk_ref, sg_ref, s0_ref,
                o_ref, sf_ref, st_ref, e_s, b_s, oi_s, kv_s, *, C, TB, reverse, refs, part):
    NL = int(np.log2(C))
    nchunk = TB // C
    blk = pl.program_id(2)
    last = 0 if reverse else C - 1
    lr_off = S_LRB if reverse else S_LRF

    if part == "init":
        @pl.when(blk == 0)
        def _():
            st_ref[...] = s0_ref[0, 0]
        return
    if part == "final":
        @pl.when(blk == pl.num_programs(2) - 1)
        def _():
            sf_ref[0, 0] = st_ref[...]
        return

    lr = lr_ref[0][:, lr_off:lr_off + GLA_RANK]
    pre = _dot(lr.astype(BF16), up_ref[...].astype(BF16)) + bias_ref[...]
    g = -_softplus(-pre) * (1.0 / GLA_TAU)
    g_hi = g.astype(BF16)
    g_lo = (g - g_hi.astype(F32)).astype(BF16)
    gs = jnp.concatenate([g_hi, g_lo], axis=1)

    for c in range(nchunk):
        ex = _dot(dm_ref[...], gs[c * C:(c + 1) * C])
        ex = ex[:, :GLA_DK] + ex[:, GLA_DK:]
        b = ex[0:C]
        b_s[c] = b
        e_s[c, 0:GLA_MXU_LEVELS * C, :] = jnp.exp(ex[C:(1 + GLA_MXU_LEVELS) * C])
        for li, lev in enumerate(range(GLA_MXU_LEVELS, NL)):
            h2 = 2 << lev
            bref = jnp.concatenate([jnp.broadcast_to(b[r:r + 1, :], (h2, GLA_DK)) for r in refs[li]], axis=0)
            e_s[c, lev * C:(lev + 1) * C, :] = jnp.exp(sg_ref[li] * (b - bref))
        e_s[c, NL * C:(NL + 1) * C, :] = jnp.exp(b[last:last + 1, :] - b)

    for c in range(nchunk):
        rows = slice(c * C, (c + 1) * C)
        q = q_ref[0, rows, :].astype(F32) * (GLA_DK ** -0.5)
        k = k_ref[0, rows, :].astype(F32)
        v = v_ref[0, rows, :]
        att = mk_ref[0] * _dot_nt(q.astype(BF16), k.astype(BF16))
        for lev in range(NL):
            e_l = e_s[c, lev * C:(lev + 1) * C, :]
            att = att + mk_ref[lev + 1] * _dot_nt((q * e_l).astype(BF16), (k * e_l).astype(BF16))
        oi_s[rows, :] = _dot(att.astype(BF16), v)
        kv_s[c] = _dot_tn((k * e_s[c, NL * C:(NL + 1) * C, :]).astype(BF16), v)

    order = list(reversed(range(nchunk))) if reverse else list(range(nchunk))
    st = st_ref[...]
    for g0 in range(0, nchunk, GLA_GROUP):
        grp = order[g0:g0 + GLA_GROUP]
        wcat = jnp.concatenate([st.astype(BF16)] + [kv_s[c].astype(BF16) for c in grp[:-1]], axis=0)
        for i, c in enumerate(grp):
            rows = slice(c * C, (c + 1) * C)
            q = q_ref[0, rows, :].astype(F32) * (GLA_DK ** -0.5)
            expo = b_s[c]
            pieces = []
            for j in range(i - 1, -2, -1):
                pieces.insert(0, (q * jnp.exp(expo)).astype(BF16))
                if j >= 0:
                    expo = expo + b_s[grp[j], last:last + 1, :]
            lhs = jnp.concatenate(pieces, axis=1)
            o_ref[0, rows, :] = oi_s[rows, :] + _dot(lhs, wcat[:(i + 1) * GLA_DK, :])
        for c in grp:
            dec = jnp.transpose(jnp.broadcast_to(jnp.exp(b_s[c, last:last + 1, :]), (GLA_DK, GLA_DK)))
            st = jnp.concatenate([dec] * (GLA_DV // GLA_DK), axis=1) * st + kv_s[c]
    st_ref[...] = st


GLA_STREAM_IN, GLA_STREAM_OUT, GLA_STREAM_SCRATCH = 10, 2, 5


def _gla_kernel(*refs, C, TB, lv_refs):
    n_in, n_out, n_scr = GLA_STREAM_IN, GLA_STREAM_OUT, GLA_STREAM_SCRATCH
    ins = [refs[d * n_in:(d + 1) * n_in] for d in range(2)]
    outs = [refs[2 * n_in + d * n_out:2 * n_in + (d + 1) * n_out] for d in range(2)]
    base = 2 * (n_in + n_out)
    scr = [refs[base + d * n_scr:base + (d + 1) * n_scr] for d in range(2)]
    for part in ("init", "body", "final"):
        for d, reverse in enumerate((False, True)):
            _gla_stream(*ins[d], *outs[d], *scr[d], C=C, TB=TB, reverse=reverse, refs=lv_refs[d], part=part)


def _gla_bidir(pm, ps, up, bias, s0, *, TB):
    B, L, _ = pm.shape
    C = GLA_CHUNK
    nb = L // TB
    assert C >= (2 << GLA_MXU_LEVELS)
    qb, kb, vb = P_Q // GLA_DK, P_K // GLA_DK, P_V // GLA_DV
    in_specs, args, out_specs, out_shape, scratch, lv_refs = [], [], [], [], [], []
    for d, reverse in enumerate((False, True)):
        dmat, masks, signs, refs = _gla_consts(C, reverse)
        lv_refs.append(refs)
        tmap = (lambda i: nb - 1 - i) if reverse else (lambda i: i)
        in_specs += [
            pl.BlockSpec((1, TB, GLA_DK), lambda b, h, i, tmap=tmap: (b, tmap(i), qb + h)),
            pl.BlockSpec((1, TB, GLA_DK), lambda b, h, i, tmap=tmap: (b, tmap(i), kb + h)),
            pl.BlockSpec((1, TB, GLA_DV), lambda b, h, i, tmap=tmap: (b, tmap(i), vb + h)),
            pl.BlockSpec((1, TB, P_SMALL), lambda b, h, i, tmap=tmap: (b, tmap(i), 0)),
            pl.BlockSpec((GLA_RANK, GLA_DK), lambda b, h, i: (0, h)),
            pl.BlockSpec((1, GLA_DK), lambda b, h, i: (0, h)),
            _const_spec(dmat.shape),
            _const_spec(masks.shape),
            _const_spec(signs.shape),
            pl.BlockSpec((1, 1, GLA_DK, GLA_DV), lambda b, h, i: (b, h, 0, 0)),
        ]
        args += [pm, pm, pm, ps, up[d], bias[d], dmat, masks, signs, s0[d]]
        out_specs += [pl.BlockSpec((1, TB, GLA_DV), lambda b, h, i, tmap=tmap: (b, tmap(i), h)),
                      pl.BlockSpec((1, 1, GLA_DK, GLA_DV), lambda b, h, i: (b, h, 0, 0))]
        out_shape += [jax.ShapeDtypeStruct((B, L, GLA_V), F32),
                      jax.ShapeDtypeStruct((B, GLA_HEADS, GLA_DK, GLA_DV), F32)]
        scratch += [pltpu.VMEM((GLA_DK, GLA_DV), F32),
                    pltpu.VMEM((TB // C, masks.shape[0] * C, GLA_DK), F32),
                    pltpu.VMEM((TB // C, C, GLA_DK), F32),
                    pltpu.VMEM((TB, GLA_DV), F32),
                    pltpu.VMEM((TB // C, GLA_DK, GLA_DV), F32)]
    assert len(in_specs) == 2 * GLA_STREAM_IN and len(scratch) == 2 * GLA_STREAM_SCRATCH
    return pl.pallas_call(
        functools.partial(_gla_kernel, C=C, TB=TB, lv_refs=lv_refs),
        grid=(B, GLA_HEADS, nb),
        in_specs=in_specs,
        out_specs=out_specs,
        out_shape=out_shape,
        scratch_shapes=scratch,
        compiler_params=_params(("parallel", "parallel", "arbitrary")),
        name="gla",
    )(*args)


def _conv_shift_mats(T, rowlen):
    t = np.arange(T)
    mats = []
    for j in range(SSM_CONV):
        off = j - CONV_LEFT
        if off == 0:
            continue
        src = t + off
        ok = (src // rowlen == t // rowlen) & (src >= 0) & (src < T)
        m = np.zeros((T, T), np.float32)
        m[t[ok], src[ok]] = 1.0
        mats.append(m)
    return jnp.asarray(np.stack(mats), BF16)


def _conv_silu(u, w, b, sh_ref):
    acc = b + u.astype(F32) * w[CONV_LEFT:CONV_LEFT + 1, :]
    taps = [j for j in range(SSM_CONV) if j != CONV_LEFT]
    for i, j in enumerate(taps):
        acc = acc + _dot(sh_ref[i], u) * w[j:j + 1, :]
    return _silu(acc)


def _ssd_kernel(*refs, C, TB, reverse, finalize, conv, emit, lane0):
    refs = list(refs)
    xs_ref, bm_ref, cm_ref, ps_ref = refs[:4]
    refs = refs[4:]
    if conv:
        wx_ref, wb_ref, wc_ref, bx_ref, bb_ref, bc_ref, sh_ref = refs[:7]
        refs = refs[7:]
    dtb_ref, alog_ref, tri_ref, s0_ref = refs[:4]
    refs = refs[4:]
    if finalize:
        z_ref, yp_ref, dsk_ref, nw_ref = refs[:4]
        refs = refs[4:]
    y_ref, sf_ref = refs[:2]
    refs = refs[2:]
    if emit:
        xo_ref, bo_ref, co_ref = refs[:3]
        refs = refs[3:]
    st_ref, xc_s, bc_s, cc_s, ya_s = refs
    nchunk = TB // C
    blk = pl.program_id(1)
    G, HPG, P, N = SSM_GROUPS, SSM_HPG, SSM_HEADDIM, SSM_STATE

    @pl.when(blk == 0)
    def _():
        st_ref[...] = s0_ref[0]

    if conv:
        xc_s[...] = _conv_silu(xs_ref[0], wx_ref[...], bx_ref[...], sh_ref).astype(BF16)
        bc_s[...] = _conv_silu(bm_ref[0], wb_ref[...], bb_ref[...], sh_ref).astype(BF16)
        cc_s[...] = _conv_silu(cm_ref[0], wc_ref[...], bc_ref[...], sh_ref).astype(BF16)
        if emit:
            xo_ref[0] = xc_s[...]
            bo_ref[0] = bc_s[...]
            co_ref[0] = cc_s[...]
    else:
        xc_s[...] = xs_ref[0]
        bc_s[...] = bm_ref[0]
        cc_s[...] = cm_ref[0]

    neg_a = -jnp.exp(alog_ref[...])
    ti = lax.broadcasted_iota(jnp.int32, (C, C), 0)
    si = lax.broadcasted_iota(jnp.int32, (C, C), 1)
    keep = (si >= ti) if reverse else (si <= ti)
    lo = lax.broadcasted_iota(jnp.int32, (1, 2 * P), 1) < P
    last = 0 if reverse else C - 1
    nheads = G * HPG

    for c in (reversed(range(nchunk)) if reverse else range(nchunk)):
        rows = slice(c * C, (c + 1) * C)
        dt = _softplus(ps_ref[0, rows, :] + dtb_ref[...])
        cum = _dot01(tri_ref[...], dt * neg_a) * LOG2E
        cum_t = jnp.transpose(cum)[lane0:lane0 + nheads, :]
        dt_t = jnp.transpose(dt)[lane0:lane0 + nheads, :]
        w_t = (dt_t * jnp.exp2(cum_t[:, last:last + 1] - cum_t)).astype(BF16)
        cdl_t = cum_t - jnp.log2(dt_t)
        dec_all = jnp.exp2(cum[last:last + 1, :])
        for g in range(G):
            bm_g = bc_s[rows, g * N:(g + 1) * N]
            cm_g = cc_s[rows, g * N:(g + 1) * N]
            cb = _dot_nt(cm_g, bm_g).astype(BF16)
            bm_t = jnp.transpose(bm_g.astype(F32)).astype(BF16)
            for pr in range(HPG // 2):
                lanes = slice(g * SSM_GW + pr * 2 * P, g * SSM_GW + (pr + 1) * 2 * P)
                x_pair = xc_s[rows, lanes]
                s_pair = st_ref[g, :, pr * 2 * P:(pr + 1) * 2 * P]
                s_bf = s_pair.astype(BF16)
                y = None
                ds = None
                decs = []
                for half in range(2):
                    j = g * HPG + pr * 2 + half
                    lane = lane0 + j
                    sel = lo if half == 0 else jnp.logical_not(lo)
                    bc_ = jnp.broadcast_to(cum[:, lane:lane + 1], (C, C))
                    m = cb * jnp.exp2(jnp.where(keep, bc_ - cdl_t[j:j + 1, :], NEG_BIG)).astype(BF16)
                    cd = cm_g * jnp.exp2(bc_).astype(BF16)
                    lhs = jnp.concatenate([m, cd], axis=1)
                    rhs = jnp.concatenate([jnp.where(sel, x_pair, jnp.zeros_like(x_pair)),
                                           jnp.where(sel, s_bf, jnp.zeros_like(s_bf))], axis=0)
                    yh = _dot(lhs, rhs)
                    dh = _dot(bm_t * w_t[j:j + 1, :],
                              jnp.where(sel, x_pair, jnp.zeros_like(x_pair)))
                    y = yh if y is None else y + yh
                    ds = dh if ds is None else ds + dh
                    decs.append(dec_all[:, lane:lane + 1])
                ya_s[rows, lanes] = y
                dec = jnp.where(lo, decs[0], decs[1])
                st_ref[g, :, pr * 2 * P:(pr + 1) * 2 * P] = dec * s_pair + ds

    if finalize:
        for g in range(G):
            lanes = slice(g * SSM_GW, (g + 1) * SSM_GW)
            y = ya_s[:, lanes] + yp_ref[0, :, lanes] + dsk_ref[:, lanes] * xc_s[:, lanes].astype(F32)
            y = y * _silu(z_ref[0, :, lanes].astype(F32))
            y = y * lax.rsqrt(jnp.mean(y * y, axis=-1, keepdims=True) + EPS) * nw_ref[:, lanes]
            y_ref[0, :, lanes] = y.astype(y_ref.dtype)
    else:
        y_ref[0] = ya_s[...]

    @pl.when(blk == pl.num_programs(1) - 1)
    def _():
        sf_ref[0] = st_ref[...]


def _ssd_pass(src, ps, conv_w, conv_b, dt_bias, a_log, s0, *, reverse, finalize, TB, rowlen, emit=False,
              y_prev=None, z_src=None, d_skip_x=None, norm_w=None):
    (xs_a, xs_o), (bm_a, bm_o), (cm_a, cm_o) = src
    B, L, _ = ps.shape
    C = SSD_CHUNK
    nb = L // TB
    G = SSM_GROUPS
    conv = conv_w is not None
    tmap = (lambda i: nb - 1 - i) if reverse else (lambda i: i)
    idx = np.arange(C)
    tri = (idx[None, :] >= idx[:, None]) if reverse else (idx[None, :] <= idx[:, None])
    tri = jnp.asarray(tri.astype(np.float32), BF16)
    in_specs = [
        pl.BlockSpec((1, TB, SSM_INNER), lambda b, i: (b, tmap(i), xs_o)),
        pl.BlockSpec((1, TB, SSM_BC), lambda b, i: (b, tmap(i), bm_o)),
        pl.BlockSpec((1, TB, SSM_BC), lambda b, i: (b, tmap(i), cm_o)),
        pl.BlockSpec((1, TB, P_SMALL), lambda b, i: (b, tmap(i), 0)),
    ]
    args = [xs_a, bm_a, cm_a, ps]
    if conv:
        nx = SSM_INNER // SSM_BC
        in_specs += [
            pl.BlockSpec((SSM_CONV, SSM_INNER), lambda b, i: (0, 0)),
            pl.BlockSpec((SSM_CONV, SSM_BC), lambda b, i: (0, nx)),
            pl.BlockSpec((SSM_CONV, SSM_BC), lambda b, i: (0, nx + 1)),
            pl.BlockSpec((1, SSM_INNER), lambda b, i: (0, 0)),
            pl.BlockSpec((1, SSM_BC), lambda b, i: (0, nx)),
            pl.BlockSpec((1, SSM_BC), lambda b, i: (0, nx + 1)),
            _const_spec((SSM_CONV - 1, TB, TB)),
        ]
        args += [conv_w, conv_w, conv_w, conv_b, conv_b, conv_b, _conv_shift_mats(TB, rowlen)]
    in_specs += [_const_spec((1, P_SMALL)), _const_spec((1, P_SMALL)), _const_spec((C, C)),
                 pl.BlockSpec((1, G, SSM_STATE, SSM_GW), lambda b, i: (b, 0, 0, 0))]
    args += [dt_bias, a_log, tri, s0]
    if finalize:
        z_a, z_o = z_src
        in_specs += [
            pl.BlockSpec((1, TB, SSM_INNER), lambda b, i: (b, tmap(i), z_o)),
            pl.BlockSpec((1, TB, SSM_INNER), lambda b, i: (b, tmap(i), 0)),
            _const_spec((1, SSM_INNER)), _const_spec((1, SSM_INNER)),
        ]
        args += [z_a, y_prev, d_skip_x, norm_w]
    out_specs = [pl.BlockSpec((1, TB, SSM_INNER), lambda b, i: (b, tmap(i), 0)),
                 pl.BlockSpec((1, G, SSM_STATE, SSM_GW), lambda b, i: (b, 0, 0, 0))]
    out_shape = [jax.ShapeDtypeStruct((B, L, SSM_INNER), BF16 if finalize else F32),
                 jax.ShapeDtypeStruct((B, G, SSM_STATE, SSM_GW), F32)]
    if emit:
        out_specs += [pl.BlockSpec((1, TB, SSM_INNER), lambda b, i: (b, tmap(i), 0)),
                      pl.BlockSpec((1, TB, SSM_BC), lambda b, i: (b, tmap(i), 0)),
                      pl.BlockSpec((1, TB, SSM_BC), lambda b, i: (b, tmap(i), 0))]
        out_shape += [jax.ShapeDtypeStruct((B, L, SSM_INNER), BF16),
                      jax.ShapeDtypeStruct((B, L, SSM_BC), BF16),
                      jax.ShapeDtypeStruct((B, L, SSM_BC), BF16)]
    kern = functools.partial(_ssd_kernel, C=C, TB=TB, reverse=reverse, finalize=finalize,
                             conv=conv, emit=emit, lane0=S_DTB if reverse else S_DTF)
    return pl.pallas_call(
        kern,
        grid=(B, nb),
        in_specs=in_specs,
        out_specs=out_specs,
        out_shape=out_shape,
        scratch_shapes=[pltpu.VMEM((G, SSM_STATE, SSM_GW), F32),
                        pltpu.VMEM((TB, SSM_INNER), BF16),
                        pltpu.VMEM((TB, SSM_BC), BF16),
                        pltpu.VMEM((TB, SSM_BC), BF16),
                        pltpu.VMEM((TB, SSM_INNER), F32)],
        compiler_params=_params(("parallel", "arbitrary")),
        name="ssd_" + ("rev" if reverse else "fwd") + ("_fin" if finalize else ""),
    )(*args)


def _merge_kernel(of_ref, ob_ref, r_ref, gnw_ref, sb_ref, ga_ref, gb_ref, x_ref, g1_ref, wpa_ref, wpb_ref,
                  wout_ref, h_ref):
    o = of_ref[0] + ob_ref[0]
    heads = []
    for h in range(GLA_HEADS):
        oh = o[:, h * GLA_DV:(h + 1) * GLA_DV]
        heads.append(oh * lax.rsqrt(jnp.mean(oh * oh, axis=-1, keepdims=True) + EPS) * gnw_ref[...])
    oa = (jnp.concatenate(heads, axis=1) * _silu(r_ref[0].astype(F32))).astype(BF16)
    ya = _dot(oa, wpa_ref[...])
    yb = _dot(sb_ref[0], wpb_ref[...])
    m = _sigmoid(ga_ref[0].astype(F32)) * ya + _sigmoid(gb_ref[0].astype(F32)) * yb
    mix = _dot(m.astype(BF16), wout_ref[...])
    h_ref[0] = x_ref[0] + g1_ref[0] * mix


def _merge(o_f, o_b, gla_norm_w, s_b, pm, x, g1, w_pa, w_pb, w_out, tm):
    B, L, _ = x.shape
    gab, gbb, rb = P_GA // D_MODEL, P_GB // D_MODEL, P_R // GLA_V
    return pl.pallas_call(
        _merge_kernel,
        grid=(B, L // tm),
        in_specs=[pl.BlockSpec((1, tm, GLA_V), lambda b, i: (b, i, 0)),
                  pl.BlockSpec((1, tm, GLA_V), lambda b, i: (b, i, 0)),
                  pl.BlockSpec((1, tm, GLA_V), lambda b, i: (b, i, rb)),
                  _const_spec((1, GLA_DV)),
                  pl.BlockSpec((1, tm, SSM_INNER), lambda b, i: (b, i, 0)),
                  pl.BlockSpec((1, tm, D_MODEL), lambda b, i: (b, i, gab)),
                  pl.BlockSpec((1, tm, D_MODEL), lambda b, i: (b, i, gbb)),
                  pl.BlockSpec((1, tm, D_MODEL), lambda b, i: (b, i, 0)),
                  pl.BlockSpec((1, 1, D_MODEL), lambda b, i: (b, 0, 0)),
                  _const_spec(w_pa.shape), _const_spec(w_pb.shape), _const_spec(w_out.shape)],
        out_specs=pl.BlockSpec((1, tm, D_MODEL), lambda b, i: (b, i, 0)),
        out_shape=jax.ShapeDtypeStruct((B, L, D_MODEL), F32),
        compiler_params=_params(("parallel", "parallel")),
        name="merge",
    )(o_f, o_b, pm, gla_norm_w, s_b, pm, pm, x, g1, w_pa, w_pb, w_out)


FFN_CHUNK = 256


def _ffn_kernel(h_ref, mod_ref, n2_ref, fw_ref, wg_ref, wu_ref, wd_ref, o_ref):
    h = h_ref[0]
    hn = h * lax.rsqrt(jnp.mean(h * h, axis=-1, keepdims=True) + EPS) * n2_ref[...]
    hn = (hn * (1.0 + mod_ref[0, 1:2, :]) + mod_ref[0, 0:1, :]).astype(BF16)
    acc = jnp.zeros(h.shape, F32)
    for c in range(D_FF // FFN_CHUNK):
        cols = slice(c * FFN_CHUNK, (c + 1) * FFN_CHUNK)
        gt = _dot(hn, wg_ref[:, cols])
        up = _dot(hn, wu_ref[:, cols])
        acc = acc + _dot((_silu(gt) * up).astype(BF16), wd_ref[cols, :])
    h2 = h + mod_ref[0, 2:3, :] * acc
    o_ref[0] = h2 * lax.rsqrt(jnp.mean(h2 * h2, axis=-1, keepdims=True) + EPS) * fw_ref[...]


def _ffn(h, mod, n2w, fw, w_gate, w_up, w_down, tm):
    B, L, _ = h.shape
    return pl.pallas_call(
        _ffn_kernel,
        grid=(B, L // tm),
        in_specs=[pl.BlockSpec((1, tm, D_MODEL), lambda b, i: (b, i, 0)),
                  pl.BlockSpec((1, 3, D_MODEL), lambda b, i: (b, 0, 0)),
                  _const_spec((1, D_MODEL)), _const_spec((1, D_MODEL)),
                  pl.BlockSpec(w_gate.shape, lambda b, i: (0, 0), pipeline_mode=pl.Buffered(1)),
                  pl.BlockSpec(w_up.shape, lambda b, i: (0, 0), pipeline_mode=pl.Buffered(1)),
                  pl.BlockSpec(w_down.shape, lambda b, i: (0, 0), pipeline_mode=pl.Buffered(1))],
        out_specs=pl.BlockSpec((1, tm, D_MODEL), lambda b, i: (b, i, 0)),
        out_shape=jax.ShapeDtypeStruct((B, L, D_MODEL), F32),
        compiler_params=_params(("parallel", "parallel")),
        name="ffn",
    )(h, mod, n2w, fw, w_gate, w_up, w_down)


def _pick_block(L, pref):
    tb = min(L, pref)
    assert L % tb == 0
    return tb


def kernel(x, c, ctx, c_ctx, w_ada, b_ada, norm1_w, w_in, gla_up_f, gla_bias_f, gla_up_b, gla_bias_b,
           gla_norm_w, conv_w, conv_b, dt_bias_f, dt_bias_b, a_log_f, a_log_b, d_skip, ssm_norm_w,
           w_pa, w_pb, w_out, norm2_w, w_gate, w_up, w_down, final_norm_w):
    B, L, D = x.shape
    Lc = ctx.shape[1]
    depth = w_ada.shape[0]
    assert depth == 1 and D == D_MODEL
    assert L % GRID_W == 0 and L % SSD_CHUNK == 0 and Lc % SSD_CHUNK == 0
    lay = 0

    nrow = -(-(B + 1) // 8) * 8
    cc = jnp.zeros((nrow, D), F32).at[:B].set(c).at[B].set(c_ctx)
    ada = _ada(cc, w_ada[lay], b_ada[lay][None, :])
    sh1, sc1, g1, sh2, sc2, g2 = [ada[:, i * D:(i + 1) * D] for i in range(6)]
    mod1 = jnp.stack([sh1[:B], sc1[:B]], axis=1)
    mod1_c = jnp.broadcast_to(jnp.stack([sh1[B], sc1[B]])[None], (B, 2, D))
    mod2 = jnp.stack([sh2[:B], sc2[:B], g2[:B]], axis=1)
    g1_l = g1[:B, None, :]

    w_main, w_small = _regroup(w_in[lay])
    nw1 = norm1_w[lay][None, :]

    at_lanes = lambda p, off: jnp.zeros((1, P_SMALL), F32).at[0, off:off + SSM_HEADS].set(p[lay])
    dtb_f, al_f = at_lanes(dt_bias_f, S_DTF), at_lanes(a_log_f, S_DTF)
    dtb_b, al_b = at_lanes(dt_bias_b, S_DTB), at_lanes(a_log_b, S_DTB)
    cw, cb_ = conv_w[lay], conv_b[lay][None, :]
    up_f, up_b = gla_up_f[lay], gla_up_b[lay]
    bi_f, bi_b = gla_bias_f[lay][None, :], gla_bias_b[lay][None, :]

    gla_zero = jnp.zeros((B, GLA_HEADS, GLA_DK, GLA_DV), F32)
    ssd_zero = jnp.zeros((B, SSM_GROUPS, SSM_STATE, SSM_GW), F32)

    pm_c, ps_c = _inproj(ctx, mod1_c, nw1, w_main, w_small, _pick_block(Lc, 256))
    _, sg_f, _, sg_b = _gla_bidir(pm_c, ps_c, (up_f, up_b), (bi_f, bi_b), (gla_zero, gla_zero),
                                  TB=_pick_block(Lc, 256))
    src_c = ((pm_c, P_XS // SSM_INNER), (pm_c, P_BM // SSM_BC), (pm_c, P_CM // SSM_BC))
    _, ss_f = _ssd_pass(src_c, ps_c, cw, cb_, dtb_f, al_f, ssd_zero, reverse=False, finalize=False,
                        TB=Lc, rowlen=Lc)
    _, ss_b = _ssd_pass(src_c, ps_c, cw, cb_, dtb_b, al_b, ssd_zero, reverse=True, finalize=False,
                        TB=Lc, rowlen=Lc)

    pm, ps = _inproj(x, mod1, nw1, w_main, w_small, _pick_block(L, 1024))
    tbg = _pick_block(L, 1024)
    og_f, _, og_b, _ = _gla_bidir(pm, ps, (up_f, up_b), (bi_f, bi_b), (sg_f, sg_b), TB=tbg)
    tbs = _pick_block(L, 256)
    src = ((pm, P_XS // SSM_INNER), (pm, P_BM // SSM_BC), (pm, P_CM // SSM_BC))
    ys_f, _, xc, bc, cc = _ssd_pass(src, ps, cw, cb_, dtb_f, al_f, ss_f, reverse=False, finalize=False,
                                    TB=tbs, rowlen=GRID_W, emit=True)
    o_b, _ = _ssd_pass(((xc, 0), (bc, 0), (cc, 0)), ps, None, None, dtb_b, al_b, ss_b, reverse=True,
                       finalize=True, TB=tbs, rowlen=GRID_W, y_prev=ys_f, z_src=(pm, P_Z // SSM_INNER),
                       d_skip_x=jnp.repeat(d_skip[lay], SSM_HEADDIM)[None, :],
                       norm_w=ssm_norm_w[lay][None, :])

    h = _merge(og_f, og_b, gla_norm_w[lay][None, :], o_b, pm, x, g1_l, w_pa[lay].astype(BF16),
               w_pb[lay].astype(BF16), w_out[lay].astype(BF16), _pick_block(L, 512))
    return _ffn(h, mod2, norm2_w[lay][None, :], final_norm_w[None, :], w_gate[lay].astype(BF16),
                w_up[lay].astype(BF16), w_down[lay].astype(BF16), _pick_block(L, 512))
```

```python
import functools

import numpy as np
import jax
import jax.numpy as jnp
from jax import lax
from jax.experimental import pallas as pl
from jax.experimental.pallas import tpu as pltpu

F32 = jnp.float32
BF16 = jnp.bfloat16

D_MODEL = 1024
GRID_W = 64
EPS = 1e-6

GLA_HEADS = 4
GLA_DK = 128
GLA_DV = 256
GLA_QK = GLA_HEADS * GLA_DK
GLA_V = GLA_HEADS * GLA_DV
GLA_RANK = 16
GLA_TAU = 16.0

SSM_INNER = 2 * D_MODEL
SSM_HEADDIM = 64
SSM_HEADS = SSM_INNER // SSM_HEADDIM
SSM_GROUPS = 4
SSM_HPG = SSM_HEADS // SSM_GROUPS
SSM_STATE = 128
SSM_BC = SSM_GROUPS * SSM_STATE
SSM_CONV = 4
CONV_LEFT = 2
SSM_GW = SSM_HPG * SSM_HEADDIM

D_FF = ((8 * D_MODEL // 3 + 255) // 256) * 256

_IN_WIDTHS = (GLA_QK, GLA_QK, GLA_V, GLA_V, GLA_RANK, GLA_RANK,
              SSM_INNER, SSM_INNER, SSM_BC, SSM_BC, SSM_HEADS, SSM_HEADS, D_MODEL, D_MODEL)
_IN_OFF = np.concatenate([[0], np.cumsum(_IN_WIDTHS)]).tolist()

P_Z, P_XS = 0, 2048
P_Q, P_K, P_V, P_R = 4096, 4608, 5120, 6144
P_BM, P_CM = 7168, 7680
P_GA, P_GB = 8192, 9216
P_MAIN = 10240
S_LRF, S_LRB, S_DTF, S_DTB = 0, 16, 32, 64
P_SMALL = 128

GLA_CHUNK = 128
SSD_CHUNK = 128

VMEM_LIMIT = 56 * 1024 * 1024
NEG_BIG = -1e30
LOG2E = 1.4426950408889634


def _sigmoid(x):
    return 1.0 / (1.0 + jnp.exp(-x))


def _silu(x):
    return x * _sigmoid(x)


def _softplus(x):
    return jnp.maximum(x, 0.0) + jnp.log(1.0 + jnp.exp(-jnp.abs(x)))


def _split3(x):
    hi = x.astype(BF16)
    r1 = x - hi.astype(F32)
    mid = r1.astype(BF16)
    lo = (r1 - mid.astype(F32)).astype(BF16)
    return hi, mid, lo


def _dot(a, b):
    return jnp.dot(a, b, preferred_element_type=F32)


def _dot_nt(a, b):
    return lax.dot_general(a, b, (((1,), (1,)), ((), ())), preferred_element_type=F32)


def _dot_tn(a, b):
    return lax.dot_general(a, b, (((0,), (0,)), ((), ())), preferred_element_type=F32)


def _dot01(m01, x):
    hi, mid, lo = _split3(x)
    return _dot(m01, hi) + _dot(m01, mid) + _dot(m01, lo)


def _params(sem):
    return pltpu.CompilerParams(dimension_semantics=sem, vmem_limit_bytes=VMEM_LIMIT)


def _const_spec(shape):
    n = len(shape)
    return pl.BlockSpec(shape, lambda *_: (0,) * n)


def _ada_kernel(c_ref, w_ref, b_ref, o_ref):
    s = _silu(c_ref[...])
    w = w_ref[0]
    s_hi = s.astype(BF16)
    s_lo = (s - s_hi.astype(F32)).astype(BF16)
    w_hi = w.astype(BF16)
    w_lo = (w - w_hi.astype(F32)).astype(BF16)
    o_ref[...] = _dot(s_hi, w_hi) + _dot(s_hi, w_lo) + _dot(s_lo, w_hi) + b_ref[...]


def _ada(cc, w, b, lay):
    rows = cc.shape[0]
    n = w.shape[2]
    tn = 1536
    return pl.pallas_call(
        _ada_kernel,
        grid=(n // tn,),
        in_specs=[pl.BlockSpec((rows, D_MODEL), lambda j: (0, 0)),
                  pl.BlockSpec((1, D_MODEL, tn), lambda j: (lay, 0, j)),
                  pl.BlockSpec((1, tn), lambda j: (0, j))],
        out_specs=pl.BlockSpec((rows, tn), lambda j: (0, j)),
        out_shape=jax.ShapeDtypeStruct((rows, n), F32),
        compiler_params=_params(("arbitrary",)),
        name="ada",
    )(cc, w, b)


def _regroup_kernel(w_ref, wm_ref, ws_ref):
    o = _IN_OFF
    w = w_ref[0]
    parts = [w[:, o[6]:o[8]], w[:, o[0]:o[4]], w[:, o[8]:o[10]], w[:, o[12]:o[14]]]
    wm_ref[...] = jnp.concatenate(parts, axis=1).astype(BF16)
    small = [w[:, o[4]:o[6]], w[:, o[10]:o[12]],
             jnp.zeros((w.shape[0], P_SMALL - 2 * GLA_RANK - 2 * SSM_HEADS), F32)]
    ws_ref[...] = jnp.concatenate(small, axis=1).astype(BF16)


def _regroup(w_in, lay):
    _, D, n = w_in.shape
    tr = 128
    return pl.pallas_call(
        _regroup_kernel,
        grid=(D // tr,),
        in_specs=[pl.BlockSpec((1, tr, n), lambda i: (lay, i, 0))],
        out_specs=[pl.BlockSpec((tr, P_MAIN), lambda i: (i, 0)),
                   pl.BlockSpec((tr, P_SMALL), lambda i: (i, 0))],
        out_shape=[jax.ShapeDtypeStruct((D, P_MAIN), BF16), jax.ShapeDtypeStruct((D, P_SMALL), BF16)],
        compiler_params=_params(("arbitrary",)),
        name="regroup",
    )(w_in)


def _inproj_kernel(x_ref, mod_ref, nw_ref, wm_ref, ws_ref, om_ref, os_ref, xn_ref):
    @pl.when(pl.program_id(2) == 0)
    def _():
        x = x_ref[0]
        y = x * lax.rsqrt(jnp.mean(x * x, axis=-1, keepdims=True) + EPS) * nw_ref[...]
        y = y * (1.0 + mod_ref[0, 1:2, :]) + mod_ref[0, 0:1, :]
        xn = y.astype(BF16)
        xn_ref[...] = xn
        os_ref[0] = _dot(xn, ws_ref[...])

    om_ref[0] = _dot(xn_ref[...], wm_ref[...]).astype(BF16)


def _inproj(x, mod, nw, w_main, w_small, tm):
    B, L, _ = x.shape
    tn = 2048
    return pl.pallas_call(
        _inproj_kernel,
        grid=(B, L // tm, P_MAIN // tn),
        in_specs=[pl.BlockSpec((1, tm, D_MODEL), lambda b, i, j: (b, i, 0)),
                  pl.BlockSpec((1, 2, D_MODEL), lambda b, i, j: (b, 0, 0)),
                  pl.BlockSpec((1, D_MODEL), lambda b, i, j: (0, 0)),
                  pl.BlockSpec((D_MODEL, tn), lambda b, i, j: (0, j)),
                  pl.BlockSpec((D_MODEL, P_SMALL), lambda b, i, j: (0, 0))],
        out_specs=[pl.BlockSpec((1, tm, tn), lambda b, i, j: (b, i, j)),
                   pl.BlockSpec((1, tm, P_SMALL), lambda b, i, j: (b, i, 0))],
        out_shape=[jax.ShapeDtypeStruct((B, L, P_MAIN), BF16),
                   jax.ShapeDtypeStruct((B, L, P_SMALL), F32)],
        scratch_shapes=[pltpu.VMEM((tm, D_MODEL), BF16)],
        compiler_params=_params(("parallel", "parallel", "arbitrary")),
        name="inproj",
    )(x, mod, nw, w_main, w_small)


GLA_MXU_LEVELS = 3
GLA_GROUP = 4


def _gla_consts(C, reverse):
    NL = int(np.log2(C))
    idx = np.arange(C)
    tri = (idx[None, :] <= idx[:, None]).astype(np.float32)
    mats = [tri]
    masks = [np.eye(C, dtype=np.float32)]
    refs, signs = [], []
    for lev in range(NL):
        h = 1 << lev
        blk = idx // (2 * h)
        half = (idx // h) % 2
        ref = blk * 2 * h + h - 1
        if lev < GLA_MXU_LEVELS:
            d = tri - tri[ref]
            d[half == 0] *= -1.0
            mats.append(d)
        else:
            sg = np.where(half == 1, 1.0, -1.0).astype(np.float32)
            if reverse:
                ref, sg = (C - 1 - ref)[::-1], sg[::-1]
            refs.append([int(ref[m * 2 * h]) for m in range(C // (2 * h))])
            signs.append(np.broadcast_to(sg[:, None], (C, GLA_DK)))
        masks.append(((blk[:, None] == blk[None, :]) & (half[:, None] == 1)
                      & (half[None, :] == 0)).astype(np.float32))
    if reverse:
        mats = [m[::-1, ::-1] for m in mats]
        masks = [m[::-1, ::-1] for m in masks]
    dmat = np.concatenate(mats, axis=0)
    return (jnp.asarray(dmat, BF16), jnp.asarray(np.stack(masks), F32),
            jnp.asarray(np.stack(signs), F32), refs)


def _gla_stream(q_ref, k_ref, v_ref, lr_ref, up_ref, bias_ref, dm_ref, mk_ref, sg_ref, s0_ref,
                o_ref, sf_ref, st_ref, e_s, b_s, oi_s, kv_s, *, C, TB, reverse, refs, part):
    NL = int(np.log2(C))
    nchunk = TB // C
    blk = pl.program_id(2)
    last = 0 if reverse else C - 1
    lr_off = S_LRB if reverse else S_LRF

    if part == "init":
        @pl.when(blk == 0)
        def _():
            st_ref[...] = s0_ref[0, 0]
        return
    if part == "final":
        @pl.when(blk == pl.num_programs(2) - 1)
        def _():
            sf_ref[0, 0] = st_ref[...]
        return

    lr = lr_ref[0][:, lr_off:lr_off + GLA_RANK]
    pre = _dot(lr.astype(BF16), up_ref[...].astype(BF16)) + bias_ref[...]
    g = -_softplus(-pre) * (1.0 / GLA_TAU)
    g_hi = g.astype(BF16)
    g_lo = (g - g_hi.astype(F32)).astype(BF16)
    gs = jnp.concatenate([g_hi, g_lo], axis=1)

    for c in range(nchunk):
        ex = _dot(dm_ref[...], gs[c * C:(c + 1) * C])
        ex = ex[:, :GLA_DK] + ex[:, GLA_DK:]
        b = ex[0:C]
        b_s[c] = b
        e_s[c, 0:GLA_MXU_LEVELS * C, :] = jnp.exp(ex[C:(1 + GLA_MXU_LEVELS) * C])
        for li, lev in enumerate(range(GLA_MXU_LEVELS, NL)):
            h2 = 2 << lev
            bref = jnp.concatenate([jnp.broadcast_to(b[r:r + 1, :], (h2, GLA_DK)) for r in refs[li]], axis=0)
            e_s[c, lev * C:(lev + 1) * C, :] = jnp.exp(sg_ref[li] * (b - bref))
        e_s[c, NL * C:(NL + 1) * C, :] = jnp.exp(b[last:last + 1, :] - b)

    for c in range(nchunk):
        rows = slice(c * C, (c + 1) * C)
        q = q_ref[0, rows, :].astype(F32) * (GLA_DK ** -0.5)
        k = k_ref[0, rows, :].astype(F32)
        v = v_ref[0, rows, :]
        att = mk_ref[0] * _dot_nt(q.astype(BF16), k.astype(BF16))
        for lev in range(NL):
            e_l = e_s[c, lev * C:(lev + 1) * C, :]
            att = att + mk_ref[lev + 1] * _dot_nt((q * e_l).astype(BF16), (k * e_l).astype(BF16))
        oi_s[rows, :] = _dot(att.astype(BF16), v)
        kv_s[c] = _dot_tn((k * e_s[c, NL * C:(NL + 1) * C, :]).astype(BF16), v)

    order = list(reversed(range(nchunk))) if reverse else list(range(nchunk))
    st = st_ref[...]
    for g0 in range(0, nchunk, GLA_GROUP):
        grp = order[g0:g0 + GLA_GROUP]
        wcat = jnp.concatenate([st.astype(BF16)] + [kv_s[c].astype(BF16) for c in grp[:-1]], axis=0)
        for i, c in enumerate(grp):
            rows = slice(c * C, (c + 1) * C)
            q = q_ref[0, rows, :].astype(F32) * (GLA_DK ** -0.5)
            expo = b_s[c]
            pieces = []
            for j in range(i - 1, -2, -1):
                pieces.insert(0, (q * jnp.exp(expo)).astype(BF16))
                if j >= 0:
                    expo = expo + b_s[grp[j], last:last + 1, :]
            lhs = jnp.concatenate(pieces, axis=1)
            o_ref[0, rows, :] = oi_s[rows, :] + _dot(lhs, wcat[:(i + 1) * GLA_DK, :])
        for c in grp:
            dec = jnp.transpose(jnp.broadcast_to(jnp.exp(b_s[c, last:last + 1, :]), (GLA_DK, GLA_DK)))
            st = jnp.concatenate([dec] * (GLA_DV // GLA_DK), axis=1) * st + kv_s[c]
    st_ref[...] = st


GLA_STREAM_IN, GLA_STREAM_OUT, GLA_STREAM_SCRATCH = 10, 2, 5


def _gla_kernel(*refs, C, TB, lv_refs):
    n_in, n_out, n_scr = GLA_STREAM_IN, GLA_STREAM_OUT, GLA_STREAM_SCRATCH
    ins = [refs[d * n_in:(d + 1) * n_in] for d in range(2)]
    outs = [refs[2 * n_in + d * n_out:2 * n_in + (d + 1) * n_out] for d in range(2)]
    base = 2 * (n_in + n_out)
    scr = [refs[base + d * n_scr:base + (d + 1) * n_scr] for d in range(2)]
    for part in ("init", "body", "final"):
        for d, reverse in enumerate((False, True)):
            _gla_stream(*ins[d], *outs[d], *scr[d], C=C, TB=TB, reverse=reverse, refs=lv_refs[d], part=part)


def _gla_bidir(pm, ps, up, bias, s0, *, TB):
    B, L, _ = pm.shape
    C = GLA_CHUNK
    nb = L // TB
    assert C >= (2 << GLA_MXU_LEVELS)
    qb, kb, vb = P_Q // GLA_DK, P_K // GLA_DK, P_V // GLA_DV
    in_specs, args, out_specs, out_shape, scratch, lv_refs = [], [], [], [], [], []
    for d, reverse in enumerate((False, True)):
        dmat, masks, signs, refs = _gla_consts(C, reverse)
        lv_refs.append(refs)
        tmap = (lambda i: nb - 1 - i) if reverse else (lambda i: i)
        in_specs += [
            pl.BlockSpec((1, TB, GLA_DK), lambda b, h, i, tmap=tmap: (b, tmap(i), qb + h)),
            pl.BlockSpec((1, TB, GLA_DK), lambda b, h, i, tmap=tmap: (b, tmap(i), kb + h)),
            pl.BlockSpec((1, TB, GLA_DV), lambda b, h, i, tmap=tmap: (b, tmap(i), vb + h)),
            pl.BlockSpec((1, TB, P_SMALL), lambda b, h, i, tmap=tmap: (b, tmap(i), 0)),
            pl.BlockSpec((GLA_RANK, GLA_DK), lambda b, h, i: (0, h)),
            pl.BlockSpec((1, GLA_DK), lambda b, h, i: (0, h)),
            _const_spec(dmat.shape),
            _const_spec(masks.shape),
            _const_spec(signs.shape),
            pl.BlockSpec((1, 1, GLA_DK, GLA_DV), lambda b, h, i: (b, h, 0, 0)),
        ]
        args += [pm, pm, pm, ps, up[d], bias[d], dmat, masks, signs, s0[d]]
        out_specs += [pl.BlockSpec((1, TB, GLA_DV), lambda b, h, i, tmap=tmap: (b, tmap(i), h)),
                      pl.BlockSpec((1, 1, GLA_DK, GLA_DV), lambda b, h, i: (b, h, 0, 0))]
        out_shape += [jax.ShapeDtypeStruct((B, L, GLA_V), F32),
                      jax.ShapeDtypeStruct((B, GLA_HEADS, GLA_DK, GLA_DV), F32)]
        scratch += [pltpu.VMEM((GLA_DK, GLA_DV), F32),
                    pltpu.VMEM((TB // C, masks.shape[0] * C, GLA_DK), F32),
                    pltpu.VMEM((TB // C, C, GLA_DK), F32),
                    pltpu.VMEM((TB, GLA_DV), F32),
                    pltpu.VMEM((TB // C, GLA_DK, GLA_DV), F32)]
    assert len(in_specs) == 2 * GLA_STREAM_IN and len(scratch) == 2 * GLA_STREAM_SCRATCH
    return pl.pallas_call(
        functools.partial(_gla_kernel, C=C, TB=TB, lv_refs=lv_refs),
        grid=(B, GLA_HEADS, nb),
        in_specs=in_specs,
        out_specs=out_specs,
        out_shape=out_shape,
        scratch_shapes=scratch,
        compiler_params=_params(("parallel", "parallel", "arbitrary")),
        name="gla",
    )(*args)


def _conv_shift_mats(T, rowlen):
    t = np.arange(T)
    mats = []
    for j in range(SSM_CONV):
        off = j - CONV_LEFT
        if off == 0:
            continue
        src = t + off
        ok = (src // rowlen == t // rowlen) & (src >= 0) & (src < T)
        m = np.zeros((T, T), np.float32)
        m[t[ok], src[ok]] = 1.0
        mats.append(m)
    return jnp.asarray(np.stack(mats), BF16)


def _conv_silu(u, w, b, sh_ref):
    acc = b + u.astype(F32) * w[CONV_LEFT:CONV_LEFT + 1, :]
    taps = [j for j in range(SSM_CONV) if j != CONV_LEFT]
    for i, j in enumerate(taps):
        acc = acc + _dot(sh_ref[i], u) * w[j:j + 1, :]
    return _silu(acc)


def _ssd_kernel(*refs, C, TB, reverse, finalize, conv, emit, lane0):
    refs = list(refs)
    xs_ref, bm_ref, cm_ref, ps_ref = refs[:4]
    refs = refs[4:]
    if conv:
        wx_ref, wb_ref, wc_ref, bx_ref, bb_ref, bc_ref, sh_ref = refs[:7]
        refs = refs[7:]
    dtb_ref, alog_ref, tri_ref, s0_ref = refs[:4]
    refs = refs[4:]
    if finalize:
        z_ref, yp_ref, dsk_ref, nw_ref = refs[:4]
        refs = refs[4:]
    y_ref, sf_ref = refs[:2]
    refs = refs[2:]
    if emit:
        xo_ref, bo_ref, co_ref = refs[:3]
        refs = refs[3:]
    st_ref, xc_s, bc_s, cc_s, ya_s = refs
    nchunk = TB // C
    blk = pl.program_id(1)
    G, HPG, P, N = SSM_GROUPS, SSM_HPG, SSM_HEADDIM, SSM_STATE

    @pl.when(blk == 0)
    def _():
        st_ref[...] = s0_ref[0]

    if conv:
        xc_s[...] = _conv_silu(xs_ref[0], wx_ref[...], bx_ref[...], sh_ref).astype(BF16)
        bc_s[...] = _conv_silu(bm_ref[0], wb_ref[...], bb_ref[...], sh_ref).astype(BF16)
        cc_s[...] = _conv_silu(cm_ref[0], wc_ref[...], bc_ref[...], sh_ref).astype(BF16)
        if emit:
            xo_ref[0] = xc_s[...]
            bo_ref[0] = bc_s[...]
            co_ref[0] = cc_s[...]
    else:
        xc_s[...] = xs_ref[0]
        bc_s[...] = bm_ref[0]
        cc_s[...] = cm_ref[0]

    neg_a = -jnp.exp(alog_ref[...])
    ti = lax.broadcasted_iota(jnp.int32, (C, C), 0)
    si = lax.broadcasted_iota(jnp.int32, (C, C), 1)
    keep = (si >= ti) if reverse else (si <= ti)
    lo = lax.broadcasted_iota(jnp.int32, (1, 2 * P), 1) < P
    last = 0 if reverse else C - 1
    nheads = G * HPG

    for c in (reversed(range(nchunk)) if reverse else range(nchunk)):
        rows = slice(c * C, (c + 1) * C)
        dt = _softplus(ps_ref[0, rows, :] + dtb_ref[...])
        cum = _dot01(tri_ref[...], dt * neg_a) * LOG2E
        cum_t = jnp.transpose(cum)[lane0:lane0 + nheads, :]
        dt_t = jnp.transpose(dt)[lane0:lane0 + nheads, :]
        w_t = (dt_t * jnp.exp2(cum_t[:, last:last + 1] - cum_t)).astype(BF16)
        cdl_t = cum_t - jnp.log2(dt_t)
        dec_all = jnp.exp2(cum[last:last + 1, :])
        for g in range(G):
            bm_g = bc_s[rows, g * N:(g + 1) * N]
            cm_g = cc_s[rows, g * N:(g + 1) * N]
            cb = _dot_nt(cm_g, bm_g).astype(BF16)
            bm_t = jnp.transpose(bm_g.astype(F32)).astype(BF16)
            for pr in range(HPG // 2):
                lanes = slice(g * SSM_GW + pr * 2 * P, g * SSM_GW + (pr + 1) * 2 * P)
                x_pair = xc_s[rows, lanes]
                s_pair = st_ref[g, :, pr * 2 * P:(pr + 1) * 2 * P]
                s_bf = s_pair.astype(BF16)
                y = None
                ds = None
                decs = []
                for half in range(2):
                    j = g * HPG + pr * 2 + half
                    lane = lane0 + j
                    sel = lo if half == 0 else jnp.logical_not(lo)
                    bc_ = jnp.broadcast_to(cum[:, lane:lane + 1], (C, C))
                    m = cb * jnp.exp2(jnp.where(keep, bc_ - cdl_t[j:j + 1, :], NEG_BIG)).astype(BF16)
                    cd = cm_g * jnp.exp2(bc_).astype(BF16)
                    lhs = jnp.concatenate([m, cd], axis=1)
                    rhs = jnp.concatenate([jnp.where(sel, x_pair, jnp.zeros_like(x_pair)),
                                           jnp.where(sel, s_bf, jnp.zeros_like(s_bf))], axis=0)
                    yh = _dot(lhs, rhs)
                    dh = _dot(bm_t * w_t[j:j + 1, :],
                              jnp.where(sel, x_pair, jnp.zeros_like(x_pair)))
                    y = yh if y is None else y + yh
                    ds = dh if ds is None else ds + dh
                    decs.append(dec_all[:, lane:lane + 1])
                ya_s[rows, lanes] = y
                dec = jnp.where(lo, decs[0], decs[1])
                st_ref[g, :, pr * 2 * P:(pr + 1) * 2 * P] = dec * s_pair + ds

    if finalize:
        for g in range(G):
            lanes = slice(g * SSM_GW, (g + 1) * SSM_GW)
            y = ya_s[:, lanes] + yp_ref[0, :, lanes] + dsk_ref[:, lanes] * xc_s[:, lanes].astype(F32)
            y = y * _silu(z_ref[0, :, lanes].astype(F32))
            y = y * lax.rsqrt(jnp.mean(y * y, axis=-1, keepdims=True) + EPS) * nw_ref[:, lanes]
            y_ref[0, :, lanes] = y.astype(y_ref.dtype)
    else:
        y_ref[0] = ya_s[...]

    @pl.when(blk == pl.num_programs(1) - 1)
    def _():
        sf_ref[0] = st_ref[...]


def _ssd_pass(src, ps, conv_w, conv_b, dt_bias, a_log, s0, *, reverse, finalize, TB, rowlen, emit=False,
              y_prev=None, z_src=None, d_skip_x=None, norm_w=None):
    (xs_a, xs_o), (bm_a, bm_o), (cm_a, cm_o) = src
    B, L, _ = ps.shape
    C = SSD_CHUNK
    nb = L // TB
    G = SSM_GROUPS
    conv = conv_w is not None
    tmap = (lambda i: nb - 1 - i) if reverse else (lambda i: i)
    idx = np.arange(C)
    tri = (idx[None, :] >= idx[:, None]) if reverse else (idx[None, :] <= idx[:, None])
    tri = jnp.asarray(tri.astype(np.float32), BF16)
    in_specs = [
        pl.BlockSpec((1, TB, SSM_INNER), lambda b, i: (b, tmap(i), xs_o)),
        pl.BlockSpec((1, TB, SSM_BC), lambda b, i: (b, tmap(i), bm_o)),
        pl.BlockSpec((1, TB, SSM_BC), lambda b, i: (b, tmap(i), cm_o)),
        pl.BlockSpec((1, TB, P_SMALL), lambda b, i: (b, tmap(i), 0)),
    ]
    args = [xs_a, bm_a, cm_a, ps]
    if conv:
        nx = SSM_INNER // SSM_BC
        in_specs += [
            pl.BlockSpec((SSM_CONV, SSM_INNER), lambda b, i: (0, 0)),
            pl.BlockSpec((SSM_CONV, SSM_BC), lambda b, i: (0, nx)),
            pl.BlockSpec((SSM_CONV, SSM_BC), lambda b, i: (0, nx + 1)),
            pl.BlockSpec((1, SSM_INNER), lambda b, i: (0, 0)),
            pl.BlockSpec((1, SSM_BC), lambda b, i: (0, nx)),
            pl.BlockSpec((1, SSM_BC), lambda b, i: (0, nx + 1)),
            _const_spec((SSM_CONV - 1, TB, TB)),
        ]
        args += [conv_w, conv_w, conv_w, conv_b, conv_b, conv_b, _conv_shift_mats(TB, rowlen)]
    in_specs += [_const_spec((1, P_SMALL)), _const_spec((1, P_SMALL)), _const_spec((C, C)),
                 pl.BlockSpec((1, G, SSM_STATE, SSM_GW), lambda b, i: (b, 0, 0, 0))]
    args += [dt_bias, a_log, tri, s0]
    if finalize:
        z_a, z_o = z_src
        in_specs += [
            pl.BlockSpec((1, TB, SSM_INNER), lambda b, i: (b, tmap(i), z_o)),
            pl.BlockSpec((1, TB, SSM_INNER), lambda b, i: (b, tmap(i), 0)),
            _const_spec((1, SSM_INNER)), _const_spec((1, SSM_INNER)),
        ]
        args += [z_a, y_prev, d_skip_x, norm_w]
    out_specs = [pl.BlockSpec((1, TB, SSM_INNER), lambda b, i: (b, tmap(i), 0)),
                 pl.BlockSpec((1, G, SSM_STATE, SSM_GW), lambda b, i: (b, 0, 0, 0))]
    out_shape = [jax.ShapeDtypeStruct((B, L, SSM_INNER), BF16 if finalize else F32),
                 jax.ShapeDtypeStruct((B, G, SSM_STATE, SSM_GW), F32)]
    if emit:
        out_specs += [pl.BlockSpec((1, TB, SSM_INNER), lambda b, i: (b, tmap(i), 0)),
                      pl.BlockSpec((1, TB, SSM_BC), lambda b, i: (b, tmap(i), 0)),
                      pl.BlockSpec((1, TB, SSM_BC), lambda b, i: (b, tmap(i), 0))]
        out_shape += [jax.ShapeDtypeStruct((B, L, SSM_INNER), BF16),
                      jax.ShapeDtypeStruct((B, L, SSM_BC), BF16),
                      jax.ShapeDtypeStruct((B, L, SSM_BC), BF16)]
    kern = functools.partial(_ssd_kernel, C=C, TB=TB, reverse=reverse, finalize=finalize,
                             conv=conv, emit=emit, lane0=S_DTB if reverse else S_DTF)
    return pl.pallas_call(
        kern,
        grid=(B, nb),
        in_specs=in_specs,
        out_specs=out_specs,
        out_shape=out_shape,
        scratch_shapes=[pltpu.VMEM((G, SSM_STATE, SSM_GW), F32),
                        pltpu.VMEM((TB, SSM_INNER), BF16),
                        pltpu.VMEM((TB, SSM_BC), BF16),
                        pltpu.VMEM((TB, SSM_BC), BF16),
                        pltpu.VMEM((TB, SSM_INNER), F32)],
        compiler_params=_params(("parallel", "arbitrary")),
        name="ssd_" + ("rev" if reverse else "fwd") + ("_fin" if finalize else ""),
    )(*args)


def _merge_kernel(of_ref, ob_ref, r_ref, gnw_ref, sb_ref, ga_ref, gb_ref, x_ref, g1_ref, wpa_ref, wpb_ref,
                  wout_ref, h_ref):
    o = of_ref[0] + ob_ref[0]
    heads = []
    for h in range(GLA_HEADS):
        oh = o[:, h * GLA_DV:(h + 1) * GLA_DV]
        heads.append(oh * lax.rsqrt(jnp.mean(oh * oh, axis=-1, keepdims=True) + EPS) * gnw_ref[...])
    oa = (jnp.concatenate(heads, axis=1) * _silu(r_ref[0].astype(F32))).astype(BF16)
    ya = _dot(oa, wpa_ref[...])
    yb = _dot(sb_ref[0], wpb_ref[...])
    m = _sigmoid(ga_ref[0].astype(F32)) * ya + _sigmoid(gb_ref[0].astype(F32)) * yb
    mix = _dot(m.astype(BF16), wout_ref[...])
    h_ref[0] = x_ref[0] + g1_ref[0] * mix


def _merge(o_f, o_b, gla_norm_w, s_b, pm, x, g1, w_pa, w_pb, w_out, tm):
    B, L, _ = x.shape
    gab, gbb, rb = P_GA // D_MODEL, P_GB // D_MODEL, P_R // GLA_V
    return pl.pallas_call(
        _merge_kernel,
        grid=(B, L // tm),
        in_specs=[pl.BlockSpec((1, tm, GLA_V), lambda b, i: (b, i, 0)),
                  pl.BlockSpec((1, tm, GLA_V), lambda b, i: (b, i, 0)),
                  pl.BlockSpec((1, tm, GLA_V), lambda b, i: (b, i, rb)),
                  _const_spec((1, GLA_DV)),
                  pl.BlockSpec((1, tm, SSM_INNER), lambda b, i: (b, i, 0)),
                  pl.BlockSpec((1, tm, D_MODEL), lambda b, i: (b, i, gab)),
                  pl.BlockSpec((1, tm, D_MODEL), lambda b, i: (b, i, gbb)),
                  pl.BlockSpec((1, tm, D_MODEL), lambda b, i: (b, i, 0)),
                  pl.BlockSpec((1, 1, D_MODEL), lambda b, i: (b, 0, 0)),
                  _const_spec(w_pa.shape), _const_spec(w_pb.shape), _const_spec(w_out.shape)],
        out_specs=pl.BlockSpec((1, tm, D_MODEL), lambda b, i: (b, i, 0)),
        out_shape=jax.ShapeDtypeStruct((B, L, D_MODEL), F32),
        compiler_params=_params(("parallel", "parallel")),
        name="merge",
    )(o_f, o_b, pm, gla_norm_w, s_b, pm, pm, x, g1, w_pa, w_pb, w_out)


FFN_CHUNK = 256


def _ffn_kernel(h_ref, mod_ref, n2_ref, fw_ref, wg_ref, wu_ref, wd_ref, o_ref):
    h = h_ref[0]
    hn = h * lax.rsqrt(jnp.mean(h * h, axis=-1, keepdims=True) + EPS) * n2_ref[...]
    hn = (hn * (1.0 + mod_ref[0, 1:2, :]) + mod_ref[0, 0:1, :]).astype(BF16)
    acc = jnp.zeros(h.shape, F32)
    for c in range(D_FF // FFN_CHUNK):
        cols = slice(c * FFN_CHUNK, (c + 1) * FFN_CHUNK)
        gt = _dot(hn, wg_ref[:, cols])
        up = _dot(hn, wu_ref[:, cols])
        acc = acc + _dot((_silu(gt) * up).astype(BF16), wd_ref[cols, :])
    h2 = h + mod_ref[0, 2:3, :] * acc
    o_ref[0] = h2 * lax.rsqrt(jnp.mean(h2 * h2, axis=-1, keepdims=True) + EPS) * fw_ref[...]


def _ffn(h, mod, n2w, fw, w_gate, w_up, w_down, tm):
    B, L, _ = h.shape
    return pl.pallas_call(
        _ffn_kernel,
        grid=(B, L // tm),
        in_specs=[pl.BlockSpec((1, tm, D_MODEL), lambda b, i: (b, i, 0)),
                  pl.BlockSpec((1, 3, D_MODEL), lambda b, i: (b, 0, 0)),
                  _const_spec((1, D_MODEL)), _const_spec((1, D_MODEL)),
                  pl.BlockSpec(w_gate.shape, lambda b, i: (0, 0), pipeline_mode=pl.Buffered(1)),
                  pl.BlockSpec(w_up.shape, lambda b, i: (0, 0), pipeline_mode=pl.Buffered(1)),
                  pl.BlockSpec(w_down.shape, lambda b, i: (0, 0), pipeline_mode=pl.Buffered(1))],
        out_specs=pl.BlockSpec((1, tm, D_MODEL), lambda b, i: (b, i, 0)),
        out_shape=jax.ShapeDtypeStruct((B, L, D_MODEL), F32),
        compiler_params=_params(("parallel", "parallel")),
        name="ffn",
    )(h, mod, n2w, fw, w_gate, w_up, w_down)


def _pick_block(L, pref):
    tb = min(L, pref)
    assert L % tb == 0
    return tb


def kernel(x, c, ctx, c_ctx, w_ada, b_ada, norm1_w, w_in, gla_up_f, gla_bias_f, gla_up_b, gla_bias_b,
           gla_norm_w, conv_w, conv_b, dt_bias_f, dt_bias_b, a_log_f, a_log_b, d_skip, ssm_norm_w,
           w_pa, w_pb, w_out, norm2_w, w_gate, w_up, w_down, final_norm_w):
    B, L, D = x.shape
    Lc = ctx.shape[1]
    depth = w_ada.shape[0]
    assert depth == 1 and D == D_MODEL
    assert L % GRID_W == 0 and L % SSD_CHUNK == 0 and Lc % SSD_CHUNK == 0
    lay = 0

    nrow = -(-(B + 1) // 8) * 8
    cc = jnp.zeros((nrow, D), F32).at[:B].set(c).at[B].set(c_ctx)
    ada = _ada(cc, w_ada, b_ada[lay][None, :], lay)
    sh1, sc1, g1, sh2, sc2, g2 = [ada[:, i * D:(i + 1) * D] for i in range(6)]
    mod1 = jnp.stack([sh1[:B], sc1[:B]], axis=1)
    mod1_c = jnp.broadcast_to(jnp.stack([sh1[B], sc1[B]])[None], (B, 2, D))
    mod2 = jnp.stack([sh2[:B], sc2[:B], g2[:B]], axis=1)
    g1_l = g1[:B, None, :]

    w_main, w_small = _regroup(w_in, lay)
    nw1 = norm1_w[lay][None, :]

    at_lanes = lambda p, off: jnp.zeros((1, P_SMALL), F32).at[0, off:off + SSM_HEADS].set(p[lay])
    dtb_f, al_f = at_lanes(dt_bias_f, S_DTF), at_lanes(a_log_f, S_DTF)
    dtb_b, al_b = at_lanes(dt_bias_b, S_DTB), at_lanes(a_log_b, S_DTB)
    cw, cb_ = conv_w[lay], conv_b[lay][None, :]
    up_f, up_b = gla_up_f[lay], gla_up_b[lay]
    bi_f, bi_b = gla_bias_f[lay][None, :], gla_bias_b[lay][None, :]

    gla_zero = jnp.zeros((B, GLA_HEADS, GLA_DK, GLA_DV), F32)
    ssd_zero = jnp.zeros((B, SSM_GROUPS, SSM_STATE, SSM_GW), F32)

    pm_c, ps_c = _inproj(ctx, mod1_c, nw1, w_main, w_small, _pick_block(Lc, 256))
    _, sg_f, _, sg_b = _gla_bidir(pm_c, ps_c, (up_f, up_b), (bi_f, bi_b), (gla_zero, gla_zero),
                                  TB=_pick_block(Lc, 256))
    src_c = ((pm_c, P_XS // SSM_INNER), (pm_c, P_BM // SSM_BC), (pm_c, P_CM // SSM_BC))
    _, ss_f = _ssd_pass(src_c, ps_c, cw, cb_, dtb_f, al_f, ssd_zero, reverse=False, finalize=False,
                        TB=Lc, rowlen=Lc)
    _, ss_b = _ssd_pass(src_c, ps_c, cw, cb_, dtb_b, al_b, ssd_zero, reverse=True, finalize=False,
                        TB=Lc, rowlen=Lc)

    pm, ps = _inproj(x, mod1, nw1, w_main, w_small, _pick_block(L, 1024))
    tbg = _pick_block(L, 1024)
    og_f, _, og_b, _ = _gla_bidir(pm, ps, (up_f, up_b), (bi_f, bi_b), (sg_f, sg_b), TB=tbg)
    tbs = _pick_block(L, 256)
    src = ((pm, P_XS // SSM_INNER), (pm, P_BM // SSM_BC), (pm, P_CM // SSM_BC))
    ys_f, _, xc, bc, cc = _ssd_pass(src, ps, cw, cb_, dtb_f, al_f, ss_f, reverse=False, finalize=False,
                                    TB=tbs, rowlen=GRID_W, emit=True)
    o_b, _ = _ssd_pass(((xc, 0), (bc, 0), (cc, 0)), ps, None, None, dtb_b, al_b, ss_b, reverse=True,
                       finalize=True, TB=tbs, rowlen=GRID_W, y_prev=ys_f, z_src=(pm, P_Z // SSM_INNER),
                       d_skip_x=jnp.repeat(d_skip[lay], SSM_HEADDIM)[None, :],
                       norm_w=ssm_norm_w[lay][None, :])

    h = _merge(og_f, og_b, gla_norm_w[lay][None, :], o_b, pm, x, g1_l, w_pa[lay].astype(BF16),
               w_pb[lay].astype(BF16), w_out[lay].astype(BF16), _pick_block(L, 512))
    return _ffn(h, mod2, norm2_w[lay][None, :], final_norm_w[None, :], w_gate[lay].astype(BF16),
                w_up[lay].astype(BF16), w_down[lay].astype(BF16), _pick_block(L, 512))
```

```python
import functools

import numpy as np
import jax
import jax.numpy as jnp
from jax import lax
from jax.experimental import pallas as pl
from jax.experimental.pallas import tpu as pltpu

F32 = jnp.float32
BF16 = jnp.bfloat16

D_MODEL = 1024
GRID_W = 64
EPS = 1e-6

GLA_HEADS = 4
GLA_DK = 128
GLA_DV = 256
GLA_QK = GLA_HEADS * GLA_DK
GLA_V = GLA_HEADS * GLA_DV
GLA_RANK = 16
GLA_TAU = 16.0

SSM_INNER = 2 * D_MODEL
SSM_HEADDIM = 64
SSM_HEADS = SSM_INNER // SSM_HEADDIM
SSM_GROUPS = 4
SSM_HPG = SSM_HEADS // SSM_GROUPS
SSM_STATE = 128
SSM_BC = SSM_GROUPS * SSM_STATE
SSM_CONV = 4
CONV_LEFT = 2
SSM_GW = SSM_HPG * SSM_HEADDIM

D_FF = ((8 * D_MODEL // 3 + 255) // 256) * 256

_IN_WIDTHS = (GLA_QK, GLA_QK, GLA_V, GLA_V, GLA_RANK, GLA_RANK,
              SSM_INNER, SSM_INNER, SSM_BC, SSM_BC, SSM_HEADS, SSM_HEADS, D_MODEL, D_MODEL)
_IN_OFF = np.concatenate([[0], np.cumsum(_IN_WIDTHS)]).tolist()

P_Z, P_XS = 0, 2048
P_Q, P_K, P_V, P_R = 4096, 4608, 5120, 6144
P_BM, P_CM = 7168, 7680
P_GA, P_GB = 8192, 9216
P_MAIN = 10240
S_LRF, S_LRB, S_DTF, S_DTB = 0, 16, 32, 64
P_SMALL = 128

GLA_CHUNK = 128
SSD_CHUNK = 128

VMEM_LIMIT = 56 * 1024 * 1024
NEG_BIG = -1e30
LOG2E = 1.4426950408889634


def _sigmoid(x):
    return 1.0 / (1.0 + jnp.exp(-x))


def _silu(x):
    return x * _sigmoid(x)


def _softplus(x):
    return jnp.maximum(x, 0.0) + jnp.log(1.0 + jnp.exp(-jnp.abs(x)))


def _split3(x):
    hi = x.astype(BF16)
    r1 = x - hi.astype(F32)
    mid = r1.astype(BF16)
    lo = (r1 - mid.astype(F32)).astype(BF16)
    return hi, mid, lo


def _dot(a, b):
    return jnp.dot(a, b, preferred_element_type=F32)


def _dot_nt(a, b):
    return lax.dot_general(a, b, (((1,), (1,)), ((), ())), preferred_element_type=F32)


def _dot_tn(a, b):
    return lax.dot_general(a, b, (((0,), (0,)), ((), ())), preferred_element_type=F32)


def _dot01(m01, x):
    hi, mid, lo = _split3(x)
    return _dot(m01, hi) + _dot(m01, mid) + _dot(m01, lo)


def _params(sem):
    return pltpu.CompilerParams(dimension_semantics=sem, vmem_limit_bytes=VMEM_LIMIT)


def _const_spec(shape):
    n = len(shape)
    return pl.BlockSpec(shape, lambda *_: (0,) * n)


def _ada_kernel(c_ref, w_ref, b_ref, o_ref):
    s = _silu(c_ref[...])
    w = w_ref[0]
    s_hi = s.astype(BF16)
    s_lo = (s - s_hi.astype(F32)).astype(BF16)
    w_hi = w.astype(BF16)
    w_lo = (w - w_hi.astype(F32)).astype(BF16)
    o_ref[...] = _dot(s_hi, w_hi) + _dot(s_hi, w_lo) + _dot(s_lo, w_hi) + b_ref[...]


def _ada(cc, w, b, lay):
    rows = cc.shape[0]
    n = w.shape[2]
    tn = 1536
    return pl.pallas_call(
        _ada_kernel,
        grid=(n // tn,),
        in_specs=[pl.BlockSpec((rows, D_MODEL), lambda j: (0, 0)),
                  pl.BlockSpec((1, D_MODEL, tn), lambda j: (lay, 0, j)),
                  pl.BlockSpec((1, tn), lambda j: (0, j))],
        out_specs=pl.BlockSpec((rows, tn), lambda j: (0, j)),
        out_shape=jax.ShapeDtypeStruct((rows, n), F32),
        compiler_params=_params(("arbitrary",)),
        name="ada",
    )(cc, w, b)


def _inproj_kernel(x_ref, mod_ref, nw_ref, wm_ref, ws_ref, om_ref, os_ref, xn_ref):
    @pl.when(pl.program_id(2) == 0)
    def _():
        x = x_ref[0]
        y = x * lax.rsqrt(jnp.mean(x * x, axis=-1, keepdims=True) + EPS) * nw_ref[...]
        y = y * (1.0 + mod_ref[0, 1:2, :]) + mod_ref[0, 0:1, :]
        xn = y.astype(BF16)
        xn_ref[...] = xn
        os_ref[0] = _dot_nt(xn, ws_ref[...])

    om_ref[0] = _dot_nt(xn_ref[...], wm_ref[...]).astype(BF16)


def _inproj(x, mod, nw, w_main, w_small, tm):
    B, L, _ = x.shape
    tn = 2048
    return pl.pallas_call(
        _inproj_kernel,
        grid=(B, L // tm, P_MAIN // tn),
        in_specs=[pl.BlockSpec((1, tm, D_MODEL), lambda b, i, j: (b, i, 0)),
                  pl.BlockSpec((1, 2, D_MODEL), lambda b, i, j: (b, 0, 0)),
                  pl.BlockSpec((1, D_MODEL), lambda b, i, j: (0, 0)),
                  pl.BlockSpec((tn, D_MODEL), lambda b, i, j: (j, 0)),
                  pl.BlockSpec((P_SMALL, D_MODEL), lambda b, i, j: (0, 0))],
        out_specs=[pl.BlockSpec((1, tm, tn), lambda b, i, j: (b, i, j)),
                   pl.BlockSpec((1, tm, P_SMALL), lambda b, i, j: (b, i, 0))],
        out_shape=[jax.ShapeDtypeStruct((B, L, P_MAIN), BF16),
                   jax.ShapeDtypeStruct((B, L, P_SMALL), F32)],
        scratch_shapes=[pltpu.VMEM((tm, D_MODEL), BF16)],
        compiler_params=_params(("parallel", "parallel", "arbitrary")),
        name="inproj",
    )(x, mod, nw, w_main, w_small)


GLA_MXU_LEVELS = 3
GLA_GROUP = 4


def _gla_consts(C, reverse):
    NL = int(np.log2(C))
    idx = np.arange(C)
    tri = (idx[None, :] <= idx[:, None]).astype(np.float32)
    mats = [tri]
    masks = [np.eye(C, dtype=np.float32)]
    refs, signs = [], []
    for lev in range(NL):
        h = 1 << lev
        blk = idx // (2 * h)
        half = (idx // h) % 2
        ref = blk * 2 * h + h - 1
        if lev < GLA_MXU_LEVELS:
            d = tri - tri[ref]
            d[half == 0] *= -1.0
            mats.append(d)
        else:
            sg = np.where(half == 1, 1.0, -1.0).astype(np.float32)
            if reverse:
                ref, sg = (C - 1 - ref)[::-1], sg[::-1]
            refs.append([int(ref[m * 2 * h]) for m in range(C // (2 * h))])
            signs.append(np.broadcast_to(sg[:, None], (C, GLA_DK)))
        masks.append(((blk[:, None] == blk[None, :]) & (half[:, None] == 1)
                      & (half[None, :] == 0)).astype(np.float32))
    if reverse:
        mats = [m[::-1, ::-1] for m in mats]
        masks = [m[::-1, ::-1] for m in masks]
    dmat = np.concatenate(mats, axis=0)
    return (jnp.asarray(dmat, BF16), jnp.asarray(np.stack(masks), F32),
            jnp.asarray(np.stack(signs), F32), refs)


def _gla_stream(q_ref, k_ref, v_ref, lr_ref, up_ref, bias_ref, dm_ref, mk_ref, sg_ref, s0_ref,
                o_ref, sf_ref, st_ref, e_s, b_s, oi_s, kv_s, *, C, TB, reverse, refs, part):
    NL = int(np.log2(C))
    nchunk = TB // C
    blk = pl.program_id(2)
    last = 0 if reverse else C - 1
    lr_off = S_LRB if reverse else S_LRF

    if part == "init":
        @pl.when(blk == 0)
        def _():
            st_ref[...] = s0_ref[0, 0]
        return
    if part == "final":
        @pl.when(blk == pl.num_programs(2) - 1)
        def _():
            sf_ref[0, 0] = st_ref[...]
        return

    lr = lr_ref[0][:, lr_off:lr_off + GLA_RANK]
    pre = _dot(lr.astype(BF16), up_ref[...].astype(BF16)) + bias_ref[...]
    g = -_softplus(-pre) * (1.0 / GLA_TAU)
    g_hi = g.astype(BF16)
    g_lo = (g - g_hi.astype(F32)).astype(BF16)
    gs = jnp.concatenate([g_hi, g_lo], axis=1)

    for c in range(nchunk):
        ex = _dot(dm_ref[...], gs[c * C:(c + 1) * C])
        ex = ex[:, :GLA_DK] + ex[:, GLA_DK:]
        b = ex[0:C]
        b_s[c] = b
        e_s[c, 0:GLA_MXU_LEVELS * C, :] = jnp.exp(ex[C:(1 + GLA_MXU_LEVELS) * C])
        for li, lev in enumerate(range(GLA_MXU_LEVELS, NL)):
            h2 = 2 << lev
            bref = jnp.concatenate([jnp.broadcast_to(b[r:r + 1, :], (h2, GLA_DK)) for r in refs[li]], axis=0)
            e_s[c, lev * C:(lev + 1) * C, :] = jnp.exp(sg_ref[li] * (b - bref))
        e_s[c, NL * C:(NL + 1) * C, :] = jnp.exp(b[last:last + 1, :] - b)

    for c in range(nchunk):
        rows = slice(c * C, (c + 1) * C)
        q = q_ref[0, rows, :].astype(F32) * (GLA_DK ** -0.5)
        k = k_ref[0, rows, :].astype(F32)
        v = v_ref[0, rows, :]
        att = mk_ref[0] * _dot_nt(q.astype(BF16), k.astype(BF16))
        for lev in range(NL):
            e_l = e_s[c, lev * C:(lev + 1) * C, :]
            att = att + mk_ref[lev + 1] * _dot_nt((q * e_l).astype(BF16), (k * e_l).astype(BF16))
        oi_s[rows, :] = _dot(att.astype(BF16), v)
        kv_s[c] = _dot_tn((k * e_s[c, NL * C:(NL + 1) * C, :]).astype(BF16), v)

    order = list(reversed(range(nchunk))) if reverse else list(range(nchunk))
    st = st_ref[...]
    for g0 in range(0, nchunk, GLA_GROUP):
        grp = order[g0:g0 + GLA_GROUP]
        wcat = jnp.concatenate([st.astype(BF16)] + [kv_s[c].astype(BF16) for c in grp[:-1]], axis=0)
        for i, c in enumerate(grp):
            rows = slice(c * C, (c + 1) * C)
            q = q_ref[0, rows, :].astype(F32) * (GLA_DK ** -0.5)
            expo = b_s[c]
            pieces = []
            for j in range(i - 1, -2, -1):
                pieces.insert(0, (q * jnp.exp(expo)).astype(BF16))
                if j >= 0:
                    expo = expo + b_s[grp[j], last:last + 1, :]
            lhs = jnp.concatenate(pieces, axis=1)
            o_ref[0, rows, :] = oi_s[rows, :] + _dot(lhs, wcat[:(i + 1) * GLA_DK, :])
        for c in grp:
            dec = jnp.transpose(jnp.broadcast_to(jnp.exp(b_s[c, last:last + 1, :]), (GLA_DK, GLA_DK)))
            st = jnp.concatenate([dec] * (GLA_DV // GLA_DK), axis=1) * st + kv_s[c]
    st_ref[...] = st


GLA_STREAM_IN, GLA_STREAM_OUT, GLA_STREAM_SCRATCH = 10, 2, 5


def _gla_kernel(*refs, C, TB, lv_refs):
    n_in, n_out, n_scr = GLA_STREAM_IN, GLA_STREAM_OUT, GLA_STREAM_SCRATCH
    ins = [refs[d * n_in:(d + 1) * n_in] for d in range(2)]
    outs = [refs[2 * n_in + d * n_out:2 * n_in + (d + 1) * n_out] for d in range(2)]
    base = 2 * (n_in + n_out)
    scr = [refs[base + d * n_scr:base + (d + 1) * n_scr] for d in range(2)]
    for part in ("init", "body", "final"):
        for d, reverse in enumerate((False, True)):
            _gla_stream(*ins[d], *outs[d], *scr[d], C=C, TB=TB, reverse=reverse, refs=lv_refs[d], part=part)


def _gla_bidir(pm, ps, up, bias, s0, *, TB):
    B, L, _ = pm.shape
    C = GLA_CHUNK
    nb = L // TB
    assert C >= (2 << GLA_MXU_LEVELS)
    qb, kb, vb = P_Q // GLA_DK, P_K // GLA_DK, P_V // GLA_DV
    in_specs, args, out_specs, out_shape, scratch, lv_refs = [], [], [], [], [], []
    for d, reverse in enumerate((False, True)):
        dmat, masks, signs, refs = _gla_consts(C, reverse)
        lv_refs.append(refs)
        tmap = (lambda i: nb - 1 - i) if reverse else (lambda i: i)
        in_specs += [
            pl.BlockSpec((1, TB, GLA_DK), lambda b, h, i, tmap=tmap: (b, tmap(i), qb + h)),
            pl.BlockSpec((1, TB, GLA_DK), lambda b, h, i, tmap=tmap: (b, tmap(i), kb + h)),
            pl.BlockSpec((1, TB, GLA_DV), lambda b, h, i, tmap=tmap: (b, tmap(i), vb + h)),
            pl.BlockSpec((1, TB, P_SMALL), lambda b, h, i, tmap=tmap: (b, tmap(i), 0)),
            pl.BlockSpec((GLA_RANK, GLA_DK), lambda b, h, i: (0, h)),
            pl.BlockSpec((1, GLA_DK), lambda b, h, i: (0, h)),
            _const_spec(dmat.shape),
            _const_spec(masks.shape),
            _const_spec(signs.shape),
            pl.BlockSpec((1, 1, GLA_DK, GLA_DV), lambda b, h, i: (b, h, 0, 0)),
        ]
        args += [pm, pm, pm, ps, up[d], bias[d], dmat, masks, signs, s0[d]]
        out_specs += [pl.BlockSpec((1, TB, GLA_DV), lambda b, h, i, tmap=tmap: (b, tmap(i), h)),
                      pl.BlockSpec((1, 1, GLA_DK, GLA_DV), lambda b, h, i: (b, h, 0, 0))]
        out_shape += [jax.ShapeDtypeStruct((B, L, GLA_V), F32),
                      jax.ShapeDtypeStruct((B, GLA_HEADS, GLA_DK, GLA_DV), F32)]
        scratch += [pltpu.VMEM((GLA_DK, GLA_DV), F32),
                    pltpu.VMEM((TB // C, masks.shape[0] * C, GLA_DK), F32),
                    pltpu.VMEM((TB // C, C, GLA_DK), F32),
                    pltpu.VMEM((TB, GLA_DV), F32),
                    pltpu.VMEM((TB // C, GLA_DK, GLA_DV), F32)]
    assert len(in_specs) == 2 * GLA_STREAM_IN and len(scratch) == 2 * GLA_STREAM_SCRATCH
    return pl.pallas_call(
        functools.partial(_gla_kernel, C=C, TB=TB, lv_refs=lv_refs),
        grid=(B, GLA_HEADS, nb),
        in_specs=in_specs,
        out_specs=out_specs,
        out_shape=out_shape,
        scratch_shapes=scratch,
        compiler_params=_params(("parallel", "parallel", "arbitrary")),
        name="gla",
    )(*args)


def _conv_shift_mats(T, rowlen):
    t = np.arange(T)
    mats = []
    for j in range(SSM_CONV):
        off = j - CONV_LEFT
        if off == 0:
            continue
        src = t + off
        ok = (src // rowlen == t // rowlen) & (src >= 0) & (src < T)
        m = np.zeros((T, T), np.float32)
        m[t[ok], src[ok]] = 1.0
        mats.append(m)
    return jnp.asarray(np.stack(mats), BF16)


def _conv_silu(u, w, b, sh_ref):
    acc = b + u.astype(F32) * w[CONV_LEFT:CONV_LEFT + 1, :]
    taps = [j for j in range(SSM_CONV) if j != CONV_LEFT]
    for i, j in enumerate(taps):
        acc = acc + _dot(sh_ref[i], u) * w[j:j + 1, :]
    return _silu(acc)


def _ssd_kernel(*refs, C, TB, reverse, finalize, conv, emit, lane0):
    refs = list(refs)
    xs_ref, bm_ref, cm_ref, ps_ref = refs[:4]
    refs = refs[4:]
    if conv:
        wx_ref, wb_ref, wc_ref, bx_ref, bb_ref, bc_ref, sh_ref = refs[:7]
        refs = refs[7:]
    dtb_ref, alog_ref, tri_ref, s0_ref = refs[:4]
    refs = refs[4:]
    if finalize:
        z_ref, yp_ref, dsk_ref, nw_ref = refs[:4]
        refs = refs[4:]
    y_ref, sf_ref = refs[:2]
    refs = refs[2:]
    if emit:
        xo_ref, bo_ref, co_ref = refs[:3]
        refs = refs[3:]
    st_ref, xc_s, bc_s, cc_s, ya_s = refs
    nchunk = TB // C
    blk = pl.program_id(1)
    G, HPG, P, N = SSM_GROUPS, SSM_HPG, SSM_HEADDIM, SSM_STATE

    @pl.when(blk == 0)
    def _():
        st_ref[...] = s0_ref[0]

    if conv:
        xc_s[...] = _conv_silu(xs_ref[0], wx_ref[...], bx_ref[...], sh_ref).astype(BF16)
        bc_s[...] = _conv_silu(bm_ref[0], wb_ref[...], bb_ref[...], sh_ref).astype(BF16)
        cc_s[...] = _conv_silu(cm_ref[0], wc_ref[...], bc_ref[...], sh_ref).astype(BF16)
        if emit:
            xo_ref[0] = xc_s[...]
            bo_ref[0] = bc_s[...]
            co_ref[0] = cc_s[...]
    else:
        xc_s[...] = xs_ref[0]
        bc_s[...] = bm_ref[0]
        cc_s[...] = cm_ref[0]

    neg_a = -jnp.exp(alog_ref[...])
    ti = lax.broadcasted_iota(jnp.int32, (C, C), 0)
    si = lax.broadcasted_iota(jnp.int32, (C, C), 1)
    keep = (si >= ti) if reverse else (si <= ti)
    lo = lax.broadcasted_iota(jnp.int32, (1, 2 * P), 1) < P
    last = 0 if reverse else C - 1
    nheads = G * HPG

    for c in (reversed(range(nchunk)) if reverse else range(nchunk)):
        rows = slice(c * C, (c + 1) * C)
        dt = _softplus(ps_ref[0, rows, :] + dtb_ref[...])
        cum = _dot01(tri_ref[...], dt * neg_a) * LOG2E
        cum_t = jnp.transpose(cum)[lane0:lane0 + nheads, :]
        dt_t = jnp.transpose(dt)[lane0:lane0 + nheads, :]
        w_t = (dt_t * jnp.exp2(cum_t[:, last:last + 1] - cum_t)).astype(BF16)
        cdl_t = cum_t - jnp.log2(dt_t)
        dec_all = jnp.exp2(cum[last:last + 1, :])
        for g in range(G):
            bm_g = bc_s[rows, g * N:(g + 1) * N]
            cm_g = cc_s[rows, g * N:(g + 1) * N]
            cb = _dot_nt(cm_g, bm_g).astype(BF16)
            bm_t = jnp.transpose(bm_g.astype(F32)).astype(BF16)
            for pr in range(HPG // 2):
                lanes = slice(g * SSM_GW + pr * 2 * P, g * SSM_GW + (pr + 1) * 2 * P)
                x_pair = xc_s[rows, lanes]
                s_pair = st_ref[g, :, pr * 2 * P:(pr + 1) * 2 * P]
                s_bf = s_pair.astype(BF16)
                y = None
                ds = None
                decs = []
                for half in range(2):
                    j = g * HPG + pr * 2 + half
                    lane = lane0 + j
                    sel = lo if half == 0 else jnp.logical_not(lo)
                    bc_ = jnp.broadcast_to(cum[:, lane:lane + 1], (C, C))
                    m = cb * jnp.exp2(jnp.where(keep, bc_ - cdl_t[j:j + 1, :], NEG_BIG)).astype(BF16)
                    cd = cm_g * jnp.exp2(bc_).astype(BF16)
                    lhs = jnp.concatenate([m, cd], axis=1)
                    rhs = jnp.concatenate([jnp.where(sel, x_pair, jnp.zeros_like(x_pair)),
                                           jnp.where(sel, s_bf, jnp.zeros_like(s_bf))], axis=0)
                    yh = _dot(lhs, rhs)
                    dh = _dot(bm_t * w_t[j:j + 1, :],
                              jnp.where(sel, x_pair, jnp.zeros_like(x_pair)))
                    y = yh if y is None else y + yh
                    ds = dh if ds is None else ds + dh
                    decs.append(dec_all[:, lane:lane + 1])
                ya_s[rows, lanes] = y
                dec = jnp.where(lo, decs[0], decs[1])
                st_ref[g, :, pr * 2 * P:(pr + 1) * 2 * P] = dec * s_pair + ds

    if finalize:
        for g in range(G):
            lanes = slice(g * SSM_GW, (g + 1) * SSM_GW)
            y = ya_s[:, lanes] + yp_ref[0, :, lanes] + dsk_ref[:, lanes] * xc_s[:, lanes].astype(F32)
            y = y * _silu(z_ref[0, :, lanes].astype(F32))
            y = y * lax.rsqrt(jnp.mean(y * y, axis=-1, keepdims=True) + EPS) * nw_ref[:, lanes]
            y_ref[0, :, lanes] = y.astype(y_ref.dtype)
    else:
        y_ref[0] = ya_s[...]

    @pl.when(blk == pl.num_programs(1) - 1)
    def _():
        sf_ref[0] = st_ref[...]


def _ssd_pass(src, ps, conv_w, conv_b, dt_bias, a_log, s0, *, reverse, finalize, TB, rowlen, emit=False,
              y_prev=None, z_src=None, d_skip_x=None, norm_w=None):
    (xs_a, xs_o), (bm_a, bm_o), (cm_a, cm_o) = src
    B, L, _ = ps.shape
    C = SSD_CHUNK
    nb = L // TB
    G = SSM_GROUPS
    conv = conv_w is not None
    tmap = (lambda i: nb - 1 - i) if reverse else (lambda i: i)
    idx = np.arange(C)
    tri = (idx[None, :] >= idx[:, None]) if reverse else (idx[None, :] <= idx[:, None])
    tri = jnp.asarray(tri.astype(np.float32), BF16)
    in_specs = [
        pl.BlockSpec((1, TB, SSM_INNER), lambda b, i: (b, tmap(i), xs_o)),
        pl.BlockSpec((1, TB, SSM_BC), lambda b, i: (b, tmap(i), bm_o)),
        pl.BlockSpec((1, TB, SSM_BC), lambda b, i: (b, tmap(i), cm_o)),
        pl.BlockSpec((1, TB, P_SMALL), lambda b, i: (b, tmap(i), 0)),
    ]
    args = [xs_a, bm_a, cm_a, ps]
    if conv:
        nx = SSM_INNER // SSM_BC
        in_specs += [
            pl.BlockSpec((SSM_CONV, SSM_INNER), lambda b, i: (0, 0)),
            pl.BlockSpec((SSM_CONV, SSM_BC), lambda b, i: (0, nx)),
            pl.BlockSpec((SSM_CONV, SSM_BC), lambda b, i: (0, nx + 1)),
            pl.BlockSpec((1, SSM_INNER), lambda b, i: (0, 0)),
            pl.BlockSpec((1, SSM_BC), lambda b, i: (0, nx)),
            pl.BlockSpec((1, SSM_BC), lambda b, i: (0, nx + 1)),
            _const_spec((SSM_CONV - 1, TB, TB)),
        ]
        args += [conv_w, conv_w, conv_w, conv_b, conv_b, conv_b, _conv_shift_mats(TB, rowlen)]
    in_specs += [_const_spec((1, P_SMALL)), _const_spec((1, P_SMALL)), _const_spec((C, C)),
                 pl.BlockSpec((1, G, SSM_STATE, SSM_GW), lambda b, i: (b, 0, 0, 0))]
    args += [dt_bias, a_log, tri, s0]
    if finalize:
        z_a, z_o = z_src
        in_specs += [
            pl.BlockSpec((1, TB, SSM_INNER), lambda b, i: (b, tmap(i), z_o)),
            pl.BlockSpec((1, TB, SSM_INNER), lambda b, i: (b, tmap(i), 0)),
            _const_spec((1, SSM_INNER)), _const_spec((1, SSM_INNER)),
        ]
        args += [z_a, y_prev, d_skip_x, norm_w]
    out_specs = [pl.BlockSpec((1, TB, SSM_INNER), lambda b, i: (b, tmap(i), 0)),
                 pl.BlockSpec((1, G, SSM_STATE, SSM_GW), lambda b, i: (b, 0, 0, 0))]
    out_shape = [jax.ShapeDtypeStruct((B, L, SSM_INNER), BF16 if finalize else F32),
                 jax.ShapeDtypeStruct((B, G, SSM_STATE, SSM_GW), F32)]
    if emit:
        out_specs += [pl.BlockSpec((1, TB, SSM_INNER), lambda b, i: (b, tmap(i), 0)),
                      pl.BlockSpec((1, TB, SSM_BC), lambda b, i: (b, tmap(i), 0)),
                      pl.BlockSpec((1, TB, SSM_BC), lambda b, i: (b, tmap(i), 0))]
        out_shape += [jax.ShapeDtypeStruct((B, L, SSM_INNER), BF16),
                      jax.ShapeDtypeStruct((B, L, SSM_BC), BF16),
                      jax.ShapeDtypeStruct((B, L, SSM_BC), BF16)]
    kern = functools.partial(_ssd_kernel, C=C, TB=TB, reverse=reverse, finalize=finalize,
                             conv=conv, emit=emit, lane0=S_DTB if reverse else S_DTF)
    return pl.pallas_call(
        kern,
        grid=(B, nb),
        in_specs=in_specs,
        out_specs=out_specs,
        out_shape=out_shape,
        scratch_shapes=[pltpu.VMEM((G, SSM_STATE, SSM_GW), F32),
                        pltpu.VMEM((TB, SSM_INNER), BF16),
                        pltpu.VMEM((TB, SSM_BC), BF16),
                        pltpu.VMEM((TB, SSM_BC), BF16),
                        pltpu.VMEM((TB, SSM_INNER), F32)],
        compiler_params=_params(("parallel", "arbitrary")),
        name="ssd_" + ("rev" if reverse else "fwd") + ("_fin" if finalize else ""),
    )(*args)


def _merge_kernel(of_ref, ob_ref, r_ref, gnw_ref, sb_ref, ga_ref, gb_ref, x_ref, g1_ref, wpa_ref, wpb_ref,
                  wout_ref, h_ref):
    o = of_ref[0] + ob_ref[0]
    heads = []
    for h in range(GLA_HEADS):
        oh = o[:, h * GLA_DV:(h + 1) * GLA_DV]
        heads.append(oh * lax.rsqrt(jnp.mean(oh * oh, axis=-1, keepdims=True) + EPS) * gnw_ref[...])
    oa = (jnp.concatenate(heads, axis=1) * _silu(r_ref[0].astype(F32))).astype(BF16)
    ya = _dot(oa, wpa_ref[...])
    yb = _dot(sb_ref[0], wpb_ref[...])
    m = _sigmoid(ga_ref[0].astype(F32)) * ya + _sigmoid(gb_ref[0].astype(F32)) * yb
    mix = _dot(m.astype(BF16), wout_ref[...])
    h_ref[0] = x_ref[0] + g1_ref[0] * mix


def _merge(o_f, o_b, gla_norm_w, s_b, pm, x, g1, w_pa, w_pb, w_out, tm):
    B, L, _ = x.shape
    gab, gbb, rb = P_GA // D_MODEL, P_GB // D_MODEL, P_R // GLA_V
    return pl.pallas_call(
        _merge_kernel,
        grid=(B, L // tm),
        in_specs=[pl.BlockSpec((1, tm, GLA_V), lambda b, i: (b, i, 0)),
                  pl.BlockSpec((1, tm, GLA_V), lambda b, i: (b, i, 0)),
                  pl.BlockSpec((1, tm, GLA_V), lambda b, i: (b, i, rb)),
                  _const_spec((1, GLA_DV)),
                  pl.BlockSpec((1, tm, SSM_INNER), lambda b, i: (b, i, 0)),
                  pl.BlockSpec((1, tm, D_MODEL), lambda b, i: (b, i, gab)),
                  pl.BlockSpec((1, tm, D_MODEL), lambda b, i: (b, i, gbb)),
                  pl.BlockSpec((1, tm, D_MODEL), lambda b, i: (b, i, 0)),
                  pl.BlockSpec((1, 1, D_MODEL), lambda b, i: (b, 0, 0)),
                  _const_spec(w_pa.shape), _const_spec(w_pb.shape), _const_spec(w_out.shape)],
        out_specs=pl.BlockSpec((1, tm, D_MODEL), lambda b, i: (b, i, 0)),
        out_shape=jax.ShapeDtypeStruct((B, L, D_MODEL), F32),
        compiler_params=_params(("parallel", "parallel")),
        name="merge",
    )(o_f, o_b, pm, gla_norm_w, s_b, pm, pm, x, g1, w_pa, w_pb, w_out)


FFN_CHUNK = 256


def _ffn_kernel(h_ref, mod_ref, n2_ref, fw_ref, wg_ref, wu_ref, wd_ref, o_ref):
    h = h_ref[0]
    hn = h * lax.rsqrt(jnp.mean(h * h, axis=-1, keepdims=True) + EPS) * n2_ref[...]
    hn = (hn * (1.0 + mod_ref[0, 1:2, :]) + mod_ref[0, 0:1, :]).astype(BF16)
    acc = jnp.zeros(h.shape, F32)
    for c in range(D_FF // FFN_CHUNK):
        cols = slice(c * FFN_CHUNK, (c + 1) * FFN_CHUNK)
        gt = _dot(hn, wg_ref[:, cols])
        up = _dot(hn, wu_ref[:, cols])
        acc = acc + _dot((_silu(gt) * up).astype(BF16), wd_ref[cols, :])
    h2 = h + mod_ref[0, 2:3, :] * acc
    o_ref[0] = h2 * lax.rsqrt(jnp.mean(h2 * h2, axis=-1, keepdims=True) + EPS) * fw_ref[...]


def _ffn(h, mod, n2w, fw, w_gate, w_up, w_down, tm):
    B, L, _ = h.shape
    return pl.pallas_call(
        _ffn_kernel,
        grid=(B, L // tm),
        in_specs=[pl.BlockSpec((1, tm, D_MODEL), lambda b, i: (b, i, 0)),
                  pl.BlockSpec((1, 3, D_MODEL), lambda b, i: (b, 0, 0)),
                  _const_spec((1, D_MODEL)), _const_spec((1, D_MODEL)),
                  pl.BlockSpec(w_gate.shape, lambda b, i: (0, 0), pipeline_mode=pl.Buffered(1)),
                  pl.BlockSpec(w_up.shape, lambda b, i: (0, 0), pipeline_mode=pl.Buffered(1)),
                  pl.BlockSpec(w_down.shape, lambda b, i: (0, 0), pipeline_mode=pl.Buffered(1))],
        out_specs=pl.BlockSpec((1, tm, D_MODEL), lambda b, i: (b, i, 0)),
        out_shape=jax.ShapeDtypeStruct((B, L, D_MODEL), F32),
        compiler_params=_params(("parallel", "parallel")),
        name="ffn",
    )(h, mod, n2w, fw, w_gate, w_up, w_down)


def _pick_block(L, pref):
    tb = min(L, pref)
    assert L % tb == 0
    return tb


def kernel(x, c, ctx, c_ctx, w_ada, b_ada, norm1_w, w_in, gla_up_f, gla_bias_f, gla_up_b, gla_bias_b,
           gla_norm_w, conv_w, conv_b, dt_bias_f, dt_bias_b, a_log_f, a_log_b, d_skip, ssm_norm_w,
           w_pa, w_pb, w_out, norm2_w, w_gate, w_up, w_down, final_norm_w):
    B, L, D = x.shape
    Lc = ctx.shape[1]
    depth = w_ada.shape[0]
    assert depth == 1 and D == D_MODEL
    assert L % GRID_W == 0 and L % SSD_CHUNK == 0 and Lc % SSD_CHUNK == 0
    lay = 0

    nrow = -(-(B + 1) // 8) * 8
    cc = jnp.zeros((nrow, D), F32).at[:B].set(c).at[B].set(c_ctx)
    ada = _ada(cc, w_ada, b_ada[lay][None, :], lay)
    sh1, sc1, g1, sh2, sc2, g2 = [ada[:, i * D:(i + 1) * D] for i in range(6)]
    mod1 = jnp.stack([sh1[:B], sc1[:B]], axis=1)
    mod1_c = jnp.broadcast_to(jnp.stack([sh1[B], sc1[B]])[None], (B, 2, D))
    mod2 = jnp.stack([sh2[:B], sc2[:B], g2[:B]], axis=1)
    g1_l = g1[:B, None, :]

    wt = jnp.swapaxes(w_in[lay], 0, 1)
    o = _IN_OFF
    w_main = jnp.concatenate([wt[o[6]:o[8]], wt[o[0]:o[4]], wt[o[8]:o[10]], wt[o[12]:o[14]]], axis=0).astype(BF16)
    w_small = jnp.concatenate([wt[o[4]:o[6]], wt[o[10]:o[12]],
                               jnp.zeros((P_SMALL - 2 * GLA_RANK - 2 * SSM_HEADS, D), F32)], axis=0).astype(BF16)
    nw1 = norm1_w[lay][None, :]

    at_lanes = lambda p, off: jnp.zeros((1, P_SMALL), F32).at[0, off:off + SSM_HEADS].set(p[lay])
    dtb_f, al_f = at_lanes(dt_bias_f, S_DTF), at_lanes(a_log_f, S_DTF)
    dtb_b, al_b = at_lanes(dt_bias_b, S_DTB), at_lanes(a_log_b, S_DTB)
    cw, cb_ = conv_w[lay], conv_b[lay][None, :]
    up_f, up_b = gla_up_f[lay], gla_up_b[lay]
    bi_f, bi_b = gla_bias_f[lay][None, :], gla_bias_b[lay][None, :]

    gla_zero = jnp.zeros((B, GLA_HEADS, GLA_DK, GLA_DV), F32)
    ssd_zero = jnp.zeros((B, SSM_GROUPS, SSM_STATE, SSM_GW), F32)

    pm_c, ps_c = _inproj(ctx, mod1_c, nw1, w_main, w_small, _pick_block(Lc, 256))
    _, sg_f, _, sg_b = _gla_bidir(pm_c, ps_c, (up_f, up_b), (bi_f, bi_b), (gla_zero, gla_zero),
                                  TB=_pick_block(Lc, 256))
    src_c = ((pm_c, P_XS // SSM_INNER), (pm_c, P_BM // SSM_BC), (pm_c, P_CM // SSM_BC))
    _, ss_f = _ssd_pass(src_c, ps_c, cw, cb_, dtb_f, al_f, ssd_zero, reverse=False, finalize=False,
                        TB=Lc, rowlen=Lc)
    _, ss_b = _ssd_pass(src_c, ps_c, cw, cb_, dtb_b, al_b, ssd_zero, reverse=True, finalize=False,
                        TB=Lc, rowlen=Lc)

    pm, ps = _inproj(x, mod1, nw1, w_main, w_small, _pick_block(L, 1024))
    tbg = _pick_block(L, 1024)
    og_f, _, og_b, _ = _gla_bidir(pm, ps, (up_f, up_b), (bi_f, bi_b), (sg_f, sg_b), TB=tbg)
    tbs = _pick_block(L, 256)
    src = ((pm, P_XS // SSM_INNER), (pm, P_BM // SSM_BC), (pm, P_CM // SSM_BC))
    ys_f, _, xc, bc, cc = _ssd_pass(src, ps, cw, cb_, dtb_f, al_f, ss_f, reverse=False, finalize=False,
                                    TB=tbs, rowlen=GRID_W, emit=True)
    o_b, _ = _ssd_pass(((xc, 0), (bc, 0), (cc, 0)), ps, None, None, dtb_b, al_b, ss_b, reverse=True,
                       finalize=True, TB=tbs, rowlen=GRID_W, y_prev=ys_f, z_src=(pm, P_Z // SSM_INNER),
                       d_skip_x=jnp.repeat(d_skip[lay], SSM_HEADDIM)[None, :],
                       norm_w=ssm_norm_w[lay][None, :])

    h = _merge(og_f, og_b, gla_norm_w[lay][None, :], o_b, pm, x, g1_l, w_pa[lay].astype(BF16),
               w_pb[lay].astype(BF16), w_out[lay].astype(BF16), _pick_block(L, 512))
    return _ffn(h, mod2, norm2_w[lay][None, :], final_norm_w[None, :], w_gate[lay].astype(BF16),
                w_up[lay].astype(BF16), w_down[lay].astype(BF16), _pick_block(L, 512))
```

```python
import functools

import numpy as np
import jax
import jax.numpy as jnp
from jax import lax
from jax.experimental import pallas as pl
from jax.experimental.pallas import tpu as pltpu

F32 = jnp.float32
BF16 = jnp.bfloat16

D_MODEL = 1024
GRID_W = 64
EPS = 1e-6

GLA_HEADS = 4
GLA_DK = 128
GLA_DV = 256
GLA_QK = GLA_HEADS * GLA_DK
GLA_V = GLA_HEADS * GLA_DV
GLA_RANK = 16
GLA_TAU = 16.0

SSM_INNER = 2 * D_MODEL
SSM_HEADDIM = 64
SSM_HEADS = SSM_INNER // SSM_HEADDIM
SSM_GROUPS = 4
SSM_HPG = SSM_HEADS // SSM_GROUPS
SSM_STATE = 128
SSM_BC = SSM_GROUPS * SSM_STATE
SSM_CONV = 4
CONV_LEFT = 2
SSM_GW = SSM_HPG * SSM_HEADDIM

D_FF = ((8 * D_MODEL // 3 + 255) // 256) * 256

_IN_WIDTHS = (GLA_QK, GLA_QK, GLA_V, GLA_V, GLA_RANK, GLA_RANK,
              SSM_INNER, SSM_INNER, SSM_BC, SSM_BC, SSM_HEADS, SSM_HEADS, D_MODEL, D_MODEL)
_IN_OFF = np.concatenate([[0], np.cumsum(_IN_WIDTHS)]).tolist()

P_Z, P_XS = 0, 2048
P_Q, P_K, P_V, P_R = 4096, 4608, 5120, 6144
P_BM, P_CM = 7168, 7680
P_GA, P_GB = 8192, 9216
P_MAIN = 10240
S_LRF, S_LRB, S_DTF, S_DTB = 0, 16, 32, 64
P_SMALL = 128

GLA_CHUNK = 128
SSD_CHUNK = 128

VMEM_LIMIT = 56 * 1024 * 1024
NEG_BIG = -1e30
LOG2E = 1.4426950408889634


def _sigmoid(x):
    return 1.0 / (1.0 + jnp.exp(-x))


def _silu(x):
    return x * _sigmoid(x)


def _softplus(x):
    return jnp.maximum(x, 0.0) + jnp.log(1.0 + jnp.exp(-jnp.abs(x)))


def _split3(x):
    hi = x.astype(BF16)
    r1 = x - hi.astype(F32)
    mid = r1.astype(BF16)
    lo = (r1 - mid.astype(F32)).astype(BF16)
    return hi, mid, lo


def _dot(a, b):
    return jnp.dot(a, b, preferred_element_type=F32)


def _dot_nt(a, b):
    return lax.dot_general(a, b, (((1,), (1,)), ((), ())), preferred_element_type=F32)


def _dot_tn(a, b):
    return lax.dot_general(a, b, (((0,), (0,)), ((), ())), preferred_element_type=F32)


def _dot01(m01, x):
    hi, mid, lo = _split3(x)
    return _dot(m01, hi) + _dot(m01, mid) + _dot(m01, lo)


def _params(sem):
    return pltpu.CompilerParams(dimension_semantics=sem, vmem_limit_bytes=VMEM_LIMIT)


def _const_spec(shape):
    n = len(shape)
    return pl.BlockSpec(shape, lambda *_: (0,) * n)


def _ada_kernel(c_ref, w_ref, b_ref, o_ref):
    s = _silu(c_ref[...])
    w = w_ref[0]
    s_hi = s.astype(BF16)
    s_lo = (s - s_hi.astype(F32)).astype(BF16)
    w_hi = w.astype(BF16)
    w_lo = (w - w_hi.astype(F32)).astype(BF16)
    o_ref[...] = _dot(s_hi, w_hi) + _dot(s_hi, w_lo) + _dot(s_lo, w_hi) + b_ref[...]


def _ada(cc, w, b, lay):
    rows = cc.shape[0]
    n = w.shape[2]
    tn = 1536
    return pl.pallas_call(
        _ada_kernel,
        grid=(n // tn,),
        in_specs=[pl.BlockSpec((rows, D_MODEL), lambda j: (0, 0)),
                  pl.BlockSpec((1, D_MODEL, tn), lambda j: (lay, 0, j)),
                  pl.BlockSpec((1, tn), lambda j: (0, j))],
        out_specs=pl.BlockSpec((rows, tn), lambda j: (0, j)),
        out_shape=jax.ShapeDtypeStruct((rows, n), F32),
        compiler_params=_params(("arbitrary",)),
        name="ada",
    )(cc, w, b)


REGROUP_ROWS = 1024
REGROUP_ALIGN = 16
_MAIN_RUNS = ((P_Z, _IN_OFF[6], 2 * SSM_INNER), (P_Q, _IN_OFF[0], 2 * GLA_QK + 2 * GLA_V),
              (P_BM, _IN_OFF[8], 2 * SSM_BC), (P_GA, _IN_OFF[12], 2 * D_MODEL))


def _regroup_kernel(wt_ref, wm_ref):
    wm_ref[...] = jnp.transpose(wt_ref[...]).astype(BF16)


def _regroup(wt):
    _, D = wt.shape
    R = REGROUP_ROWS
    U = REGROUP_ALIGN
    assert all(dst % R == 0 and n % R == 0 and src % U == 0 for dst, src, n in _MAIN_RUNS)

    def src_row(i):
        units = jnp.int32(0)
        for dst, src, n in _MAIN_RUNS:
            inside = (i * R >= dst) & (i * R < dst + n)
            units = jnp.where(inside, (src - dst) // U + i * (R // U), units)
        return units * U

    return pl.pallas_call(
        _regroup_kernel,
        grid=(P_MAIN // R,),
        in_specs=[pl.BlockSpec((pl.Element(R), pl.Element(D)), lambda i: (src_row(i), 0))],
        out_specs=pl.BlockSpec((D, R), lambda i: (0, i)),
        out_shape=jax.ShapeDtypeStruct((D, P_MAIN), BF16),
        compiler_params=_params(("arbitrary",)),
        name="regroup",
    )(wt)


def _inproj_kernel(x_ref, mod_ref, nw_ref, wm_ref, ws_ref, om_ref, os_ref, xn_ref):
    @pl.when(pl.program_id(2) == 0)
    def _():
        x = x_ref[0]
        y = x * lax.rsqrt(jnp.mean(x * x, axis=-1, keepdims=True) + EPS) * nw_ref[...]
        y = y * (1.0 + mod_ref[0, 1:2, :]) + mod_ref[0, 0:1, :]
        xn = y.astype(BF16)
        xn_ref[...] = xn
        os_ref[0] = _dot(xn, ws_ref[...])

    om_ref[0] = _dot(xn_ref[...], wm_ref[...]).astype(BF16)


def _inproj(x, mod, nw, w_main, w_small, tm):
    B, L, _ = x.shape
    tn = 2048
    return pl.pallas_call(
        _inproj_kernel,
        grid=(B, L // tm, P_MAIN // tn),
        in_specs=[pl.BlockSpec((1, tm, D_MODEL), lambda b, i, j: (b, i, 0)),
                  pl.BlockSpec((1, 2, D_MODEL), lambda b, i, j: (b, 0, 0)),
                  pl.BlockSpec((1, D_MODEL), lambda b, i, j: (0, 0)),
                  pl.BlockSpec((D_MODEL, tn), lambda b, i, j: (0, j)),
                  pl.BlockSpec((D_MODEL, P_SMALL), lambda b, i, j: (0, 0))],
        out_specs=[pl.BlockSpec((1, tm, tn), lambda b, i, j: (b, i, j)),
                   pl.BlockSpec((1, tm, P_SMALL), lambda b, i, j: (b, i, 0))],
        out_shape=[jax.ShapeDtypeStruct((B, L, P_MAIN), BF16),
                   jax.ShapeDtypeStruct((B, L, P_SMALL), F32)],
        scratch_shapes=[pltpu.VMEM((tm, D_MODEL), BF16)],
        compiler_params=_params(("parallel", "parallel", "arbitrary")),
        name="inproj",
    )(x, mod, nw, w_main, w_small)


GLA_MXU_LEVELS = 3
GLA_GROUP = 4


def _gla_consts(C, reverse):
    NL = int(np.log2(C))
    idx = np.arange(C)
    tri = (idx[None, :] <= idx[:, None]).astype(np.float32)
    mats = [tri]
    masks = [np.eye(C, dtype=np.float32)]
    refs, signs = [], []
    for lev in range(NL):
        h = 1 << lev
        blk = idx // (2 * h)
        half = (idx // h) % 2
        ref = blk * 2 * h + h - 1
        if lev < GLA_MXU_LEVELS:
            d = tri - tri[ref]
            d[half == 0] *= -1.0
            mats.append(d)
        else:
            sg = np.where(half == 1, 1.0, -1.0).astype(np.float32)
            if reverse:
                ref, sg = (C - 1 - ref)[::-1], sg[::-1]
            refs.append([int(ref[m * 2 * h]) for m in range(C // (2 * h))])
            signs.append(np.broadcast_to(sg[:, None], (C, GLA_DK)))
        masks.append(((blk[:, None] == blk[None, :]) & (half[:, None] == 1)
                      & (half[None, :] == 0)).astype(np.float32))
    if reverse:
        mats = [m[::-1, ::-1] for m in mats]
        masks = [m[::-1, ::-1] for m in masks]
    dmat = np.concatenate(mats, axis=0)
    return (jnp.asarray(dmat, BF16), jnp.asarray(np.stack(masks), F32),
            jnp.asarray(np.stack(signs), F32), refs)


def _gla_stream(q_ref, k_ref, v_ref, lr_ref, up_ref, bias_ref, dm_ref, mk_ref, sg_ref, s0_ref,
                o_ref, sf_ref, st_ref, e_s, b_s, oi_s, kv_s, *, C, TB, reverse, refs, part):
    NL = int(np.log2(C))
    nchunk = TB // C
    blk = pl.program_id(2)
    last = 0 if reverse else C - 1
    lr_off = S_LRB if reverse else S_LRF

    if part == "init":
        @pl.when(blk == 0)
        def _():
            st_ref[...] = s0_ref[0, 0]
        return
    if part == "final":
        @pl.when(blk == pl.num_programs(2) - 1)
        def _():
            sf_ref[0, 0] = st_ref[...]
        return

    lr = lr_ref[0][:, lr_off:lr_off + GLA_RANK]
    pre = _dot(lr.astype(BF16), up_ref[...].astype(BF16)) + bias_ref[...]
    g = -_softplus(-pre) * (1.0 / GLA_TAU)
    g_hi = g.astype(BF16)
    g_lo = (g - g_hi.astype(F32)).astype(BF16)
    gs = jnp.concatenate([g_hi, g_lo], axis=1)

    for c in range(nchunk):
        ex = _dot(dm_ref[...], gs[c * C:(c + 1) * C])
        ex = ex[:, :GLA_DK] + ex[:, GLA_DK:]
        b = ex[0:C]
        b_s[c] = b
        e_s[c, 0:GLA_MXU_LEVELS * C, :] = jnp.exp(ex[C:(1 + GLA_MXU_LEVELS) * C])
        for li, lev in enumerate(range(GLA_MXU_LEVELS, NL)):
            h2 = 2 << lev
            bref = jnp.concatenate([jnp.broadcast_to(b[r:r + 1, :], (h2, GLA_DK)) for r in refs[li]], axis=0)
            e_s[c, lev * C:(lev + 1) * C, :] = jnp.exp(sg_ref[li] * (b - bref))
        e_s[c, NL * C:(NL + 1) * C, :] = jnp.exp(b[last:last + 1, :] - b)

    for c in range(nchunk):
        rows = slice(c * C, (c + 1) * C)
        q = q_ref[0, rows, :].astype(F32) * (GLA_DK ** -0.5)
        k = k_ref[0, rows, :].astype(F32)
        v = v_ref[0, rows, :]
        att = mk_ref[0] * _dot_nt(q.astype(BF16), k.astype(BF16))
        for lev in range(NL):
            e_l = e_s[c, lev * C:(lev + 1) * C, :]
            att = att + mk_ref[lev + 1] * _dot_nt((q * e_l).astype(BF16), (k * e_l).astype(BF16))
        oi_s[rows, :] = _dot(att.astype(BF16), v)
        kv_s[c] = _dot_tn((k * e_s[c, NL * C:(NL + 1) * C, :]).astype(BF16), v)

    order = list(reversed(range(nchunk))) if reverse else list(range(nchunk))
    st = st_ref[...]
    for g0 in range(0, nchunk, GLA_GROUP):
        grp = order[g0:g0 + GLA_GROUP]
        wcat = jnp.concatenate([st.astype(BF16)] + [kv_s[c].astype(BF16) for c in grp[:-1]], axis=0)
        for i, c in enumerate(grp):
            rows = slice(c * C, (c + 1) * C)
            q = q_ref[0, rows, :].astype(F32) * (GLA_DK ** -0.5)
            expo = b_s[c]
            pieces = []
            for j in range(i - 1, -2, -1):
                pieces.insert(0, (q * jnp.exp(expo)).astype(BF16))
                if j >= 0:
                    expo = expo + b_s[grp[j], last:last + 1, :]
            lhs = jnp.concatenate(pieces, axis=1)
            o_ref[0, rows, :] = oi_s[rows, :] + _dot(lhs, wcat[:(i + 1) * GLA_DK, :])
        for c in grp:
            dec = jnp.transpose(jnp.broadcast_to(jnp.exp(b_s[c, last:last + 1, :]), (GLA_DK, GLA_DK)))
            st = jnp.concatenate([dec] * (GLA_DV // GLA_DK), axis=1) * st + kv_s[c]
    st_ref[...] = st


GLA_STREAM_IN, GLA_STREAM_OUT, GLA_STREAM_SCRATCH = 10, 2, 5


def _gla_kernel(*refs, C, TB, lv_refs):
    n_in, n_out, n_scr = GLA_STREAM_IN, GLA_STREAM_OUT, GLA_STREAM_SCRATCH
    ins = [refs[d * n_in:(d + 1) * n_in] for d in range(2)]
    outs = [refs[2 * n_in + d * n_out:2 * n_in + (d + 1) * n_out] for d in range(2)]
    base = 2 * (n_in + n_out)
    scr = [refs[base + d * n_scr:base + (d + 1) * n_scr] for d in range(2)]
    for part in ("init", "body", "final"):
        for d, reverse in enumerate((False, True)):
            _gla_stream(*ins[d], *outs[d], *scr[d], C=C, TB=TB, reverse=reverse, refs=lv_refs[d], part=part)


def _gla_bidir(pm, ps, up, bias, s0, *, TB):
    B, L, _ = pm.shape
    C = GLA_CHUNK
    nb = L // TB
    assert C >= (2 << GLA_MXU_LEVELS)
    qb, kb, vb = P_Q // GLA_DK, P_K // GLA_DK, P_V // GLA_DV
    in_specs, args, out_specs, out_shape, scratch, lv_refs = [], [], [], [], [], []
    for d, reverse in enumerate((False, True)):
        dmat, masks, signs, refs = _gla_consts(C, reverse)
        lv_refs.append(refs)
        tmap = (lambda i: nb - 1 - i) if reverse else (lambda i: i)
        in_specs += [
            pl.BlockSpec((1, TB, GLA_DK), lambda b, h, i, tmap=tmap: (b, tmap(i), qb + h)),
            pl.BlockSpec((1, TB, GLA_DK), lambda b, h, i, tmap=tmap: (b, tmap(i), kb + h)),
            pl.BlockSpec((1, TB, GLA_DV), lambda b, h, i, tmap=tmap: (b, tmap(i), vb + h)),
            pl.BlockSpec((1, TB, P_SMALL), lambda b, h, i, tmap=tmap: (b, tmap(i), 0)),
            pl.BlockSpec((GLA_RANK, GLA_DK), lambda b, h, i: (0, h)),
            pl.BlockSpec((1, GLA_DK), lambda b, h, i: (0, h)),
            _const_spec(dmat.shape),
            _const_spec(masks.shape),
            _const_spec(signs.shape),
            pl.BlockSpec((1, 1, GLA_DK, GLA_DV), lambda b, h, i: (b, h, 0, 0)),
        ]
        args += [pm, pm, pm, ps, up[d], bias[d], dmat, masks, signs, s0[d]]
        out_specs += [pl.BlockSpec((1, TB, GLA_DV), lambda b, h, i, tmap=tmap: (b, tmap(i), h)),
                      pl.BlockSpec((1, 1, GLA_DK, GLA_DV), lambda b, h, i: (b, h, 0, 0))]
        out_shape += [jax.ShapeDtypeStruct((B, L, GLA_V), F32),
                      jax.ShapeDtypeStruct((B, GLA_HEADS, GLA_DK, GLA_DV), F32)]
        scratch += [pltpu.VMEM((GLA_DK, GLA_DV), F32),
                    pltpu.VMEM((TB // C, masks.shape[0] * C, GLA_DK), F32),
                    pltpu.VMEM((TB // C, C, GLA_DK), F32),
                    pltpu.VMEM((TB, GLA_DV), F32),
                    pltpu.VMEM((TB // C, GLA_DK, GLA_DV), F32)]
    assert len(in_specs) == 2 * GLA_STREAM_IN and len(scratch) == 2 * GLA_STREAM_SCRATCH
    return pl.pallas_call(
        functools.partial(_gla_kernel, C=C, TB=TB, lv_refs=lv_refs),
        grid=(B, GLA_HEADS, nb),
        in_specs=in_specs,
        out_specs=out_specs,
        out_shape=out_shape,
        scratch_shapes=scratch,
        compiler_params=_params(("parallel", "parallel", "arbitrary")),
        name="gla",
    )(*args)


def _conv_shift_mats(T, rowlen):
    t = np.arange(T)
    mats = []
    for j in range(SSM_CONV):
        off = j - CONV_LEFT
        if off == 0:
            continue
        src = t + off
        ok = (src // rowlen == t // rowlen) & (src >= 0) & (src < T)
        m = np.zeros((T, T), np.float32)
        m[t[ok], src[ok]] = 1.0
        mats.append(m)
    return jnp.asarray(np.stack(mats), BF16)


def _conv_silu(u, w, b, sh_ref):
    acc = b + u.astype(F32) * w[CONV_LEFT:CONV_LEFT + 1, :]
    taps = [j for j in range(SSM_CONV) if j != CONV_LEFT]
    for i, j in enumerate(taps):
        acc = acc + _dot(sh_ref[i], u) * w[j:j + 1, :]
    return _silu(acc)


def _ssd_kernel(*refs, C, TB, reverse, finalize, conv, emit, lane0):
    refs = list(refs)
    xs_ref, bm_ref, cm_ref, ps_ref = refs[:4]
    refs = refs[4:]
    if conv:
        wx_ref, wb_ref, wc_ref, bx_ref, bb_ref, bc_ref, sh_ref = refs[:7]
        refs = refs[7:]
    dtb_ref, alog_ref, tri_ref, s0_ref = refs[:4]
    refs = refs[4:]
    if finalize:
        z_ref, yp_ref, dsk_ref, nw_ref = refs[:4]
        refs = refs[4:]
    y_ref, sf_ref = refs[:2]
    refs = refs[2:]
    if emit:
        xo_ref, bo_ref, co_ref = refs[:3]
        refs = refs[3:]
    st_ref, xc_s, bc_s, cc_s, ya_s = refs
    nchunk = TB // C
    blk = pl.program_id(1)
    G, HPG, P, N = SSM_GROUPS, SSM_HPG, SSM_HEADDIM, SSM_STATE

    @pl.when(blk == 0)
    def _():
        st_ref[...] = s0_ref[0]

    if conv:
        xc_s[...] = _conv_silu(xs_ref[0], wx_ref[...], bx_ref[...], sh_ref).astype(BF16)
        bc_s[...] = _conv_silu(bm_ref[0], wb_ref[...], bb_ref[...], sh_ref).astype(BF16)
        cc_s[...] = _conv_silu(cm_ref[0], wc_ref[...], bc_ref[...], sh_ref).astype(BF16)
        if emit:
            xo_ref[0] = xc_s[...]
            bo_ref[0] = bc_s[...]
            co_ref[0] = cc_s[...]
    else:
        xc_s[...] = xs_ref[0]
        bc_s[...] = bm_ref[0]
        cc_s[...] = cm_ref[0]

    neg_a = -jnp.exp(alog_ref[...])
    ti = lax.broadcasted_iota(jnp.int32, (C, C), 0)
    si = lax.broadcasted_iota(jnp.int32, (C, C), 1)
    keep = (si >= ti) if reverse else (si <= ti)
    lo = lax.broadcasted_iota(jnp.int32, (1, 2 * P), 1) < P
    last = 0 if reverse else C - 1
    nheads = G * HPG

    for c in (reversed(range(nchunk)) if reverse else range(nchunk)):
        rows = slice(c * C, (c + 1) * C)
        dt = _softplus(ps_ref[0, rows, :] + dtb_ref[...])
        cum = _dot01(tri_ref[...], dt * neg_a) * LOG2E
        cum_t = jnp.transpose(cum)[lane0:lane0 + nheads, :]
        dt_t = jnp.transpose(dt)[lane0:lane0 + nheads, :]
        w_t = (dt_t * jnp.exp2(cum_t[:, last:last + 1] - cum_t)).astype(BF16)
        cdl_t = cum_t - jnp.log2(dt_t)
        dec_all = jnp.exp2(cum[last:last + 1, :])
        for g in range(G):
            bm_g = bc_s[rows, g * N:(g + 1) * N]
            cm_g = cc_s[rows, g * N:(g + 1) * N]
            cb = _dot_nt(cm_g, bm_g).astype(BF16)
            bm_t = jnp.transpose(bm_g.astype(F32)).astype(BF16)
            for pr in range(HPG // 2):
                lanes = slice(g * SSM_GW + pr * 2 * P, g * SSM_GW + (pr + 1) * 2 * P)
                x_pair = xc_s[rows, lanes]
                s_pair = st_ref[g, :, pr * 2 * P:(pr + 1) * 2 * P]
                s_bf = s_pair.astype(BF16)
                y = None
                ds = None
                decs = []
                for half in range(2):
                    j = g * HPG + pr * 2 + half
                    lane = lane0 + j
                    sel = lo if half == 0 else jnp.logical_not(lo)
                    bc_ = jnp.broadcast_to(cum[:, lane:lane + 1], (C, C))
                    m = cb * jnp.exp2(jnp.where(keep, bc_ - cdl_t[j:j + 1, :], NEG_BIG)).astype(BF16)
                    cd = cm_g * jnp.exp2(bc_).astype(BF16)
                    lhs = jnp.concatenate([m, cd], axis=1)
                    rhs = jnp.concatenate([jnp.where(sel, x_pair, jnp.zeros_like(x_pair)),
                                           jnp.where(sel, s_bf, jnp.zeros_like(s_bf))], axis=0)
                    yh = _dot(lhs, rhs)
                    dh = _dot(bm_t * w_t[j:j + 1, :],
                              jnp.where(sel, x_pair, jnp.zeros_like(x_pair)))
                    y = yh if y is None else y + yh
                    ds = dh if ds is None else ds + dh
                    decs.append(dec_all[:, lane:lane + 1])
                ya_s[rows, lanes] = y
                dec = jnp.where(lo, decs[0], decs[1])
                st_ref[g, :, pr * 2 * P:(pr + 1) * 2 * P] = dec * s_pair + ds

    if finalize:
        for g in range(G):
            lanes = slice(g * SSM_GW, (g + 1) * SSM_GW)
            y = ya_s[:, lanes] + yp_ref[0, :, lanes] + dsk_ref[:, lanes] * xc_s[:, lanes].astype(F32)
            y = y * _silu(z_ref[0, :, lanes].astype(F32))
            y = y * lax.rsqrt(jnp.mean(y * y, axis=-1, keepdims=True) + EPS) * nw_ref[:, lanes]
            y_ref[0, :, lanes] = y.astype(y_ref.dtype)
    else:
        y_ref[0] = ya_s[...]

    @pl.when(blk == pl.num_programs(1) - 1)
    def _():
        sf_ref[0] = st_ref[...]


def _ssd_pass(src, ps, conv_w, conv_b, dt_bias, a_log, s0, *, reverse, finalize, TB, rowlen, emit=False,
              y_prev=None, z_src=None, d_skip_x=None, norm_w=None):
    (xs_a, xs_o), (bm_a, bm_o), (cm_a, cm_o) = src
    B, L, _ = ps.shape
    C = SSD_CHUNK
    nb = L // TB
    G = SSM_GROUPS
    conv = conv_w is not None
    tmap = (lambda i: nb - 1 - i) if reverse else (lambda i: i)
    idx = np.arange(C)
    tri = (idx[None, :] >= idx[:, None]) if reverse else (idx[None, :] <= idx[:, None])
    tri = jnp.asarray(tri.astype(np.float32), BF16)
    in_specs = [
        pl.BlockSpec((1, TB, SSM_INNER), lambda b, i: (b, tmap(i), xs_o)),
        pl.BlockSpec((1, TB, SSM_BC), lambda b, i: (b, tmap(i), bm_o)),
        pl.BlockSpec((1, TB, SSM_BC), lambda b, i: (b, tmap(i), cm_o)),
        pl.BlockSpec((1, TB, P_SMALL), lambda b, i: (b, tmap(i), 0)),
    ]
    args = [xs_a, bm_a, cm_a, ps]
    if conv:
        nx = SSM_INNER // SSM_BC
        in_specs += [
            pl.BlockSpec((SSM_CONV, SSM_INNER), lambda b, i: (0, 0)),
            pl.BlockSpec((SSM_CONV, SSM_BC), lambda b, i: (0, nx)),
            pl.BlockSpec((SSM_CONV, SSM_BC), lambda b, i: (0, nx + 1)),
            pl.BlockSpec((1, SSM_INNER), lambda b, i: (0, 0)),
            pl.BlockSpec((1, SSM_BC), lambda b, i: (0, nx)),
            pl.BlockSpec((1, SSM_BC), lambda b, i: (0, nx + 1)),
            _const_spec((SSM_CONV - 1, TB, TB)),
        ]
        args += [conv_w, conv_w, conv_w, conv_b, conv_b, conv_b, _conv_shift_mats(TB, rowlen)]
    in_specs += [_const_spec((1, P_SMALL)), _const_spec((1, P_SMALL)), _const_spec((C, C)),
                 pl.BlockSpec((1, G, SSM_STATE, SSM_GW), lambda b, i: (b, 0, 0, 0))]
    args += [dt_bias, a_log, tri, s0]
    if finalize:
        z_a, z_o = z_src
        in_specs += [
            pl.BlockSpec((1, TB, SSM_INNER), lambda b, i: (b, tmap(i), z_o)),
            pl.BlockSpec((1, TB, SSM_INNER), lambda b, i: (b, tmap(i), 0)),
            _const_spec((1, SSM_INNER)), _const_spec((1, SSM_INNER)),
        ]
        args += [z_a, y_prev, d_skip_x, norm_w]
    out_specs = [pl.BlockSpec((1, TB, SSM_INNER), lambda b, i: (b, tmap(i), 0)),
                 pl.BlockSpec((1, G, SSM_STATE, SSM_GW), lambda b, i: (b, 0, 0, 0))]
    out_shape = [jax.ShapeDtypeStruct((B, L, SSM_INNER), BF16 if finalize else F32),
                 jax.ShapeDtypeStruct((B, G, SSM_STATE, SSM_GW), F32)]
    if emit:
        out_specs += [pl.BlockSpec((1, TB, SSM_INNER), lambda b, i: (b, tmap(i), 0)),
                      pl.BlockSpec((1, TB, SSM_BC), lambda b, i: (b, tmap(i), 0)),
                      pl.BlockSpec((1, TB, SSM_BC), lambda b, i: (b, tmap(i), 0))]
        out_shape += [jax.ShapeDtypeStruct((B, L, SSM_INNER), BF16),
                      jax.ShapeDtypeStruct((B, L, SSM_BC), BF16),
                      jax.ShapeDtypeStruct((B, L, SSM_BC), BF16)]
    kern = functools.partial(_ssd_kernel, C=C, TB=TB, reverse=reverse, finalize=finalize,
                             conv=conv, emit=emit, lane0=S_DTB if reverse else S_DTF)
    return pl.pallas_call(
        kern,
        grid=(B, nb),
        in_specs=in_specs,
        out_specs=out_specs,
        out_shape=out_shape,
        scratch_shapes=[pltpu.VMEM((G, SSM_STATE, SSM_GW), F32),
                        pltpu.VMEM((TB, SSM_INNER), BF16),
                        pltpu.VMEM((TB, SSM_BC), BF16),
                        pltpu.VMEM((TB, SSM_BC), BF16),
                        pltpu.VMEM((TB, SSM_INNER), F32)],
        compiler_params=_params(("parallel", "arbitrary")),
        name="ssd_" + ("rev" if reverse else "fwd") + ("_fin" if finalize else ""),
    )(*args)


def _merge_kernel(of_ref, ob_ref, r_ref, gnw_ref, sb_ref, ga_ref, gb_ref, x_ref, g1_ref, wpa_ref, wpb_ref,
                  wout_ref, h_ref):
    o = of_ref[0] + ob_ref[0]
    heads = []
    for h in range(GLA_HEADS):
        oh = o[:, h * GLA_DV:(h + 1) * GLA_DV]
        heads.append(oh * lax.rsqrt(jnp.mean(oh * oh, axis=-1, keepdims=True) + EPS) * gnw_ref[...])
    oa = (jnp.concatenate(heads, axis=1) * _silu(r_ref[0].astype(F32))).astype(BF16)
    ya = _dot(oa, wpa_ref[...])
    yb = _dot(sb_ref[0], wpb_ref[...])
    m = _sigmoid(ga_ref[0].astype(F32)) * ya + _sigmoid(gb_ref[0].astype(F32)) * yb
    mix = _dot(m.astype(BF16), wout_ref[...])
    h_ref[0] = x_ref[0] + g1_ref[0] * mix


def _merge(o_f, o_b, gla_norm_w, s_b, pm, x, g1, w_pa, w_pb, w_out, tm):
    B, L, _ = x.shape
    gab, gbb, rb = P_GA // D_MODEL, P_GB // D_MODEL, P_R // GLA_V
    return pl.pallas_call(
        _merge_kernel,
        grid=(B, L // tm),
        in_specs=[pl.BlockSpec((1, tm, GLA_V), lambda b, i: (b, i, 0)),
                  pl.BlockSpec((1, tm, GLA_V), lambda b, i: (b, i, 0)),
                  pl.BlockSpec((1, tm, GLA_V), lambda b, i: (b, i, rb)),
                  _const_spec((1, GLA_DV)),
                  pl.BlockSpec((1, tm, SSM_INNER), lambda b, i: (b, i, 0)),
                  pl.BlockSpec((1, tm, D_MODEL), lambda b, i: (b, i, gab)),
                  pl.BlockSpec((1, tm, D_MODEL), lambda b, i: (b, i, gbb)),
                  pl.BlockSpec((1, tm, D_MODEL), lambda b, i: (b, i, 0)),
                  pl.BlockSpec((1, 1, D_MODEL), lambda b, i: (b, 0, 0)),
                  _const_spec(w_pa.shape), _const_spec(w_pb.shape), _const_spec(w_out.shape)],
        out_specs=pl.BlockSpec((1, tm, D_MODEL), lambda b, i: (b, i, 0)),
        out_shape=jax.ShapeDtypeStruct((B, L, D_MODEL), F32),
        compiler_params=_params(("parallel", "parallel")),
        name="merge",
    )(o_f, o_b, pm, gla_norm_w, s_b, pm, pm, x, g1, w_pa, w_pb, w_out)


FFN_CHUNK = 256


def _ffn_kernel(h_ref, mod_ref, n2_ref, fw_ref, wg_ref, wu_ref, wd_ref, o_ref):
    h = h_ref[0]
    hn = h * lax.rsqrt(jnp.mean(h * h, axis=-1, keepdims=True) + EPS) * n2_ref[...]
    hn = (hn * (1.0 + mod_ref[0, 1:2, :]) + mod_ref[0, 0:1, :]).astype(BF16)
    acc = jnp.zeros(h.shape, F32)
    for c in range(D_FF // FFN_CHUNK):
        cols = slice(c * FFN_CHUNK, (c + 1) * FFN_CHUNK)
        gt = _dot(hn, wg_ref[:, cols])
        up = _dot(hn, wu_ref[:, cols])
        acc = acc + _dot((_silu(gt) * up).astype(BF16), wd_ref[cols, :])
    h2 = h + mod_ref[0, 2:3, :] * acc
    o_ref[0] = h2 * lax.rsqrt(jnp.mean(h2 * h2, axis=-1, keepdims=True) + EPS) * fw_ref[...]


def _ffn(h, mod, n2w, fw, w_gate, w_up, w_down, tm):
    B, L, _ = h.shape
    return pl.pallas_call(
        _ffn_kernel,
        grid=(B, L // tm),
        in_specs=[pl.BlockSpec((1, tm, D_MODEL), lambda b, i: (b, i, 0)),
                  pl.BlockSpec((1, 3, D_MODEL), lambda b, i: (b, 0, 0)),
                  _const_spec((1, D_MODEL)), _const_spec((1, D_MODEL)),
                  pl.BlockSpec(w_gate.shape, lambda b, i: (0, 0), pipeline_mode=pl.Buffered(1)),
                  pl.BlockSpec(w_up.shape, lambda b, i: (0, 0), pipeline_mode=pl.Buffered(1)),
                  pl.BlockSpec(w_down.shape, lambda b, i: (0, 0), pipeline_mode=pl.Buffered(1))],
        out_specs=pl.BlockSpec((1, tm, D_MODEL), lambda b, i: (b, i, 0)),
        out_shape=jax.ShapeDtypeStruct((B, L, D_MODEL), F32),
        compiler_params=_params(("parallel", "parallel")),
        name="ffn",
    )(h, mod, n2w, fw, w_gate, w_up, w_down)


def _pick_block(L, pref):
    tb = min(L, pref)
    assert L % tb == 0
    return tb


def kernel(x, c, ctx, c_ctx, w_ada, b_ada, norm1_w, w_in, gla_up_f, gla_bias_f, gla_up_b, gla_bias_b,
           gla_norm_w, conv_w, conv_b, dt_bias_f, dt_bias_b, a_log_f, a_log_b, d_skip, ssm_norm_w,
           w_pa, w_pb, w_out, norm2_w, w_gate, w_up, w_down, final_norm_w):
    B, L, D = x.shape
    Lc = ctx.shape[1]
    depth = w_ada.shape[0]
    assert depth == 1 and D == D_MODEL
    assert L % GRID_W == 0 and L % SSD_CHUNK == 0 and Lc % SSD_CHUNK == 0
    lay = 0

    nrow = -(-(B + 1) // 8) * 8
    cc = jnp.zeros((nrow, D), F32).at[:B].set(c).at[B].set(c_ctx)
    ada = _ada(cc, w_ada, b_ada[lay][None, :], lay)
    sh1, sc1, g1, sh2, sc2, g2 = [ada[:, i * D:(i + 1) * D] for i in range(6)]
    mod1 = jnp.stack([sh1[:B], sc1[:B]], axis=1)
    mod1_c = jnp.broadcast_to(jnp.stack([sh1[B], sc1[B]])[None], (B, 2, D))
    mod2 = jnp.stack([sh2[:B], sc2[:B], g2[:B]], axis=1)
    g1_l = g1[:B, None, :]

    wt = jnp.swapaxes(w_in[lay], 0, 1)
    w_main = _regroup(wt)
    o = _IN_OFF
    w_small = jnp.swapaxes(jnp.concatenate(
        [wt[o[4]:o[6]], wt[o[10]:o[12]], jnp.zeros((P_SMALL - 2 * GLA_RANK - 2 * SSM_HEADS, D), F32)], axis=0),
        0, 1).astype(BF16)
    nw1 = norm1_w[lay][None, :]

    at_lanes = lambda p, off: jnp.zeros((1, P_SMALL), F32).at[0, off:off + SSM_HEADS].set(p[lay])
    dtb_f, al_f = at_lanes(dt_bias_f, S_DTF), at_lanes(a_log_f, S_DTF)
    dtb_b, al_b = at_lanes(dt_bias_b, S_DTB), at_lanes(a_log_b, S_DTB)
    cw, cb_ = conv_w[lay], conv_b[lay][None, :]
    up_f, up_b = gla_up_f[lay], gla_up_b[lay]
    bi_f, bi_b = gla_bias_f[lay][None, :], gla_bias_b[lay][None, :]

    gla_zero = jnp.zeros((B, GLA_HEADS, GLA_DK, GLA_DV), F32)
    ssd_zero = jnp.zeros((B, SSM_GROUPS, SSM_STATE, SSM_GW), F32)

    pm_c, ps_c = _inproj(ctx, mod1_c, nw1, w_main, w_small, _pick_block(Lc, 256))
    _, sg_f, _, sg_b = _gla_bidir(pm_c, ps_c, (up_f, up_b), (bi_f, bi_b), (gla_zero, gla_zero),
                                  TB=_pick_block(Lc, 256))
    src_c = ((pm_c, P_XS // SSM_INNER), (pm_c, P_BM // SSM_BC), (pm_c, P_CM // SSM_BC))
    _, ss_f = _ssd_pass(src_c, ps_c, cw, cb_, dtb_f, al_f, ssd_zero, reverse=False, finalize=False,
                        TB=Lc, rowlen=Lc)
    _, ss_b = _ssd_pass(src_c, ps_c, cw, cb_, dtb_b, al_b, ssd_zero, reverse=True, finalize=False,
                        TB=Lc, rowlen=Lc)

    pm, ps = _inproj(x, mod1, nw1, w_main, w_small, _pick_block(L, 1024))
    tbg = _pick_block(L, 1024)
    og_f, _, og_b, _ = _gla_bidir(pm, ps, (up_f, up_b), (bi_f, bi_b), (sg_f, sg_b), TB=tbg)
    tbs = _pick_block(L, 256)
    src = ((pm, P_XS // SSM_INNER), (pm, P_BM // SSM_BC), (pm, P_CM // SSM_BC))
    ys_f, _, xc, bc, cc = _ssd_pass(src, ps, cw, cb_, dtb_f, al_f, ss_f, reverse=False, finalize=False,
                                    TB=tbs, rowlen=GRID_W, emit=True)
    o_b, _ = _ssd_pass(((xc, 0), (bc, 0), (cc, 0)), ps, None, None, dtb_b, al_b, ss_b, reverse=True,
                       finalize=True, TB=tbs, rowlen=GRID_W, y_prev=ys_f, z_src=(pm, P_Z // SSM_INNER),
                       d_skip_x=jnp.repeat(d_skip[lay], SSM_HEADDIM)[None, :],
                       norm_w=ssm_norm_w[lay][None, :])

    h = _merge(og_f, og_b, gla_norm_w[lay][None, :], o_b, pm, x, g1_l, w_pa[lay].astype(BF16),
               w_pb[lay].astype(BF16), w_out[lay].astype(BF16), _pick_block(L, 512))
    return _ffn(h, mod2, norm2_w[lay][None, :], final_norm_w[None, :], w_gate[lay].astype(BF16),
                w_up[lay].astype(BF16), w_down[lay].astype(BF16), _pick_block(L, 512))
```

```python
import functools

import numpy as np
import jax
import jax.numpy as jnp
from jax import lax
from jax.experimental import pallas as pl
from jax.experimental.pallas import tpu as pltpu

F32 = jnp.float32
BF16 = jnp.bfloat16

D_MODEL = 1024
GRID_W = 64
EPS = 1e-6

GLA_HEADS = 4
GLA_DK = 128
GLA_DV = 256
GLA_QK = GLA_HEADS * GLA_DK
GLA_V = GLA_HEADS * GLA_DV
GLA_RANK = 16
GLA_TAU = 16.0

SSM_INNER = 2 * D_MODEL
SSM_HEADDIM = 64
SSM_HEADS = SSM_INNER // SSM_HEADDIM
SSM_GROUPS = 4
SSM_HPG = SSM_HEADS // SSM_GROUPS
SSM_STATE = 128
SSM_BC = SSM_GROUPS * SSM_STATE
SSM_CONV = 4
CONV_LEFT = 2
SSM_GW = SSM_HPG * SSM_HEADDIM

D_FF = ((8 * D_MODEL // 3 + 255) // 256) * 256

_IN_WIDTHS = (GLA_QK, GLA_QK, GLA_V, GLA_V, GLA_RANK, GLA_RANK,
              SSM_INNER, SSM_INNER, SSM_BC, SSM_BC, SSM_HEADS, SSM_HEADS, D_MODEL, D_MODEL)
_IN_OFF = np.concatenate([[0], np.cumsum(_IN_WIDTHS)]).tolist()

P_Z, P_XS = 0, 2048
P_Q, P_K, P_V, P_R = 4096, 4608, 5120, 6144
P_BM, P_CM = 7168, 7680
P_GA, P_GB = 8192, 9216
P_MAIN = 10240
S_LRF, S_LRB, S_DTF, S_DTB = 0, 16, 32, 64
P_SMALL = 128

GLA_CHUNK = 128
SSD_CHUNK = 128

VMEM_LIMIT = 56 * 1024 * 1024
NEG_BIG = -1e30
LOG2E = 1.4426950408889634


def _sigmoid(x):
    return 1.0 / (1.0 + jnp.exp(-x))


def _silu(x):
    return x * _sigmoid(x)


def _softplus(x):
    return jnp.maximum(x, 0.0) + jnp.log(1.0 + jnp.exp(-jnp.abs(x)))


def _split3(x):
    hi = x.astype(BF16)
    r1 = x - hi.astype(F32)
    mid = r1.astype(BF16)
    lo = (r1 - mid.astype(F32)).astype(BF16)
    return hi, mid, lo


def _dot(a, b):
    return jnp.dot(a, b, preferred_element_type=F32)


def _dot_nt(a, b):
    return lax.dot_general(a, b, (((1,), (1,)), ((), ())), preferred_element_type=F32)


def _dot_tn(a, b):
    return lax.dot_general(a, b, (((0,), (0,)), ((), ())), preferred_element_type=F32)


def _dot01(m01, x):
    hi, mid, lo = _split3(x)
    return _dot(m01, hi) + _dot(m01, mid) + _dot(m01, lo)


def _params(sem):
    return pltpu.CompilerParams(dimension_semantics=sem, vmem_limit_bytes=VMEM_LIMIT)


def _const_spec(shape):
    n = len(shape)
    return pl.BlockSpec(shape, lambda *_: (0,) * n)


def _ada_kernel(c_ref, w_ref, b_ref, o_ref):
    s = _silu(c_ref[...])
    w = w_ref[0]
    s_hi = s.astype(BF16)
    s_lo = (s - s_hi.astype(F32)).astype(BF16)
    w_hi = w.astype(BF16)
    w_lo = (w - w_hi.astype(F32)).astype(BF16)
    o_ref[...] = _dot(s_hi, w_hi) + _dot(s_hi, w_lo) + _dot(s_lo, w_hi) + b_ref[...]


def _ada(cc, w, b, lay):
    rows = cc.shape[0]
    n = w.shape[2]
    tn = 1536
    return pl.pallas_call(
        _ada_kernel,
        grid=(n // tn,),
        in_specs=[pl.BlockSpec((rows, D_MODEL), lambda j: (0, 0)),
                  pl.BlockSpec((1, D_MODEL, tn), lambda j: (lay, 0, j)),
                  pl.BlockSpec((1, tn), lambda j: (0, j))],
        out_specs=pl.BlockSpec((rows, tn), lambda j: (0, j)),
        out_shape=jax.ShapeDtypeStruct((rows, n), F32),
        compiler_params=_params(("arbitrary",)),
        name="ada",
    )(cc, w, b)


REGROUP_ROWS = 1024
REGROUP_ALIGN = 16
_MAIN_RUNS = ((P_Z, _IN_OFF[6], 2 * SSM_INNER), (P_Q, _IN_OFF[0], 2 * GLA_QK + 2 * GLA_V),
              (P_BM, _IN_OFF[8], 2 * SSM_BC), (P_GA, _IN_OFF[12], 2 * D_MODEL))


def _regroup_kernel(wt_ref, lr_ref, dt_ref, wm_ref, ws_ref):
    wm_ref[...] = jnp.transpose(wt_ref[...]).astype(BF16)
    pad = jnp.zeros((P_SMALL - lr_ref.shape[0] - dt_ref.shape[0], lr_ref.shape[1]), F32)
    ws_ref[...] = jnp.transpose(jnp.concatenate([lr_ref[...], dt_ref[...], pad], axis=0)).astype(BF16)


def _regroup(wt):
    _, D = wt.shape
    R = REGROUP_ROWS
    U = REGROUP_ALIGN
    assert all(dst % R == 0 and n % R == 0 and src % U == 0 for dst, src, n in _MAIN_RUNS)

    def src_row(i):
        units = jnp.int32(0)
        for dst, src, n in _MAIN_RUNS:
            inside = (i * R >= dst) & (i * R < dst + n)
            units = jnp.where(inside, (src - dst) // U + i * (R // U), units)
        return units * U

    return pl.pallas_call(
        _regroup_kernel,
        grid=(P_MAIN // R,),
        in_specs=[pl.BlockSpec((pl.Element(R), pl.Element(D)), lambda i: (src_row(i), 0)),
                  pl.BlockSpec((pl.Element(2 * GLA_RANK), pl.Element(D)), lambda i: (_IN_OFF[4], 0)),
                  pl.BlockSpec((pl.Element(2 * SSM_HEADS), pl.Element(D)), lambda i: (_IN_OFF[10], 0))],
        out_specs=[pl.BlockSpec((D, R), lambda i: (0, i)),
                   pl.BlockSpec((D, P_SMALL), lambda i: (0, 0))],
        out_shape=[jax.ShapeDtypeStruct((D, P_MAIN), BF16), jax.ShapeDtypeStruct((D, P_SMALL), BF16)],
        compiler_params=_params(("arbitrary",)),
        name="regroup",
    )(wt, wt, wt)


def _inproj_kernel(x_ref, mod_ref, nw_ref, wm_ref, ws_ref, om_ref, os_ref, xn_ref):
    @pl.when(pl.program_id(2) == 0)
    def _():
        x = x_ref[0]
        y = x * lax.rsqrt(jnp.mean(x * x, axis=-1, keepdims=True) + EPS) * nw_ref[...]
        y = y * (1.0 + mod_ref[0, 1:2, :]) + mod_ref[0, 0:1, :]
        xn = y.astype(BF16)
        xn_ref[...] = xn
        os_ref[0] = _dot(xn, ws_ref[...])

    om_ref[0] = _dot(xn_ref[...], wm_ref[...]).astype(BF16)


def _inproj(x, mod, nw, w_main, w_small, tm):
    B, L, _ = x.shape
    tn = 2048
    return pl.pallas_call(
        _inproj_kernel,
        grid=(B, L // tm, P_MAIN // tn),
        in_specs=[pl.BlockSpec((1, tm, D_MODEL), lambda b, i, j: (b, i, 0)),
                  pl.BlockSpec((1, 2, D_MODEL), lambda b, i, j: (b, 0, 0)),
                  pl.BlockSpec((1, D_MODEL), lambda b, i, j: (0, 0)),
                  pl.BlockSpec((D_MODEL, tn), lambda b, i, j: (0, j)),
                  pl.BlockSpec((D_MODEL, P_SMALL), lambda b, i, j: (0, 0))],
        out_specs=[pl.BlockSpec((1, tm, tn), lambda b, i, j: (b, i, j)),
                   pl.BlockSpec((1, tm, P_SMALL), lambda b, i, j: (b, i, 0))],
        out_shape=[jax.ShapeDtypeStruct((B, L, P_MAIN), BF16),
                   jax.ShapeDtypeStruct((B, L, P_SMALL), F32)],
        scratch_shapes=[pltpu.VMEM((tm, D_MODEL), BF16)],
        compiler_params=_params(("parallel", "parallel", "arbitrary")),
        name="inproj",
    )(x, mod, nw, w_main, w_small)


GLA_MXU_LEVELS = 3
GLA_GROUP = 4


def _gla_consts(C, reverse):
    NL = int(np.log2(C))
    idx = np.arange(C)
    tri = (idx[None, :] <= idx[:, None]).astype(np.float32)
    mats = [tri]
    masks = [np.eye(C, dtype=np.float32)]
    refs, signs = [], []
    for lev in range(NL):
        h = 1 << lev
        blk = idx // (2 * h)
        half = (idx // h) % 2
        ref = blk * 2 * h + h - 1
        if lev < GLA_MXU_LEVELS:
            d = tri - tri[ref]
            d[half == 0] *= -1.0
            mats.append(d)
        else:
            sg = np.where(half == 1, 1.0, -1.0).astype(np.float32)
            if reverse:
                ref, sg = (C - 1 - ref)[::-1], sg[::-1]
            refs.append([int(ref[m * 2 * h]) for m in range(C // (2 * h))])
            signs.append(np.broadcast_to(sg[:, None], (C, GLA_DK)))
        masks.append(((blk[:, None] == blk[None, :]) & (half[:, None] == 1)
                      & (half[None, :] == 0)).astype(np.float32))
    if reverse:
        mats = [m[::-1, ::-1] for m in mats]
        masks = [m[::-1, ::-1] for m in masks]
    dmat = np.concatenate(mats, axis=0)
    return (jnp.asarray(dmat, BF16), jnp.asarray(np.stack(masks), F32),
            jnp.asarray(np.stack(signs), F32), refs)


def _gla_stream(q_ref, k_ref, v_ref, lr_ref, up_ref, bias_ref, dm_ref, mk_ref, sg_ref, s0_ref,
                o_ref, sf_ref, st_ref, e_s, b_s, oi_s, kv_s, *, C, TB, reverse, refs, part):
    NL = int(np.log2(C))
    nchunk = TB // C
    blk = pl.program_id(2)
    last = 0 if reverse else C - 1
    lr_off = S_LRB if reverse else S_LRF

    if part == "init":
        @pl.when(blk == 0)
        def _():
            st_ref[...] = s0_ref[0, 0]
        return
    if part == "final":
        @pl.when(blk == pl.num_programs(2) - 1)
        def _():
            sf_ref[0, 0] = st_ref[...]
        return

    lr = lr_ref[0][:, lr_off:lr_off + GLA_RANK]
    pre = _dot(lr.astype(BF16), up_ref[...].astype(BF16)) + bias_ref[...]
    g = -_softplus(-pre) * (1.0 / GLA_TAU)
    g_hi = g.astype(BF16)
    g_lo = (g - g_hi.astype(F32)).astype(BF16)
    gs = jnp.concatenate([g_hi, g_lo], axis=1)

    for c in range(nchunk):
        ex = _dot(dm_ref[...], gs[c * C:(c + 1) * C])
        ex = ex[:, :GLA_DK] + ex[:, GLA_DK:]
        b = ex[0:C]
        b_s[c] = b
        e_s[c, 0:GLA_MXU_LEVELS * C, :] = jnp.exp(ex[C:(1 + GLA_MXU_LEVELS) * C])
        for li, lev in enumerate(range(GLA_MXU_LEVELS, NL)):
            h2 = 2 << lev
            bref = jnp.concatenate([jnp.broadcast_to(b[r:r + 1, :], (h2, GLA_DK)) for r in refs[li]], axis=0)
            e_s[c, lev * C:(lev + 1) * C, :] = jnp.exp(sg_ref[li] * (b - bref))
        e_s[c, NL * C:(NL + 1) * C, :] = jnp.exp(b[last:last + 1, :] - b)

    for c in range(nchunk):
        rows = slice(c * C, (c + 1) * C)
        q = q_ref[0, rows, :].astype(F32) * (GLA_DK ** -0.5)
        k = k_ref[0, rows, :].astype(F32)
        v = v_ref[0, rows, :]
        att = mk_ref[0] * _dot_nt(q.astype(BF16), k.astype(BF16))
        for lev in range(NL):
            e_l = e_s[c, lev * C:(lev + 1) * C, :]
            att = att + mk_ref[lev + 1] * _dot_nt((q * e_l).astype(BF16), (k * e_l).astype(BF16))
        oi_s[rows, :] = _dot(att.astype(BF16), v)
        kv_s[c] = _dot_tn((k * e_s[c, NL * C:(NL + 1) * C, :]).astype(BF16), v)

    order = list(reversed(range(nchunk))) if reverse else list(range(nchunk))
    st = st_ref[...]
    for g0 in range(0, nchunk, GLA_GROUP):
        grp = order[g0:g0 + GLA_GROUP]
        wcat = jnp.concatenate([st.astype(BF16)] + [kv_s[c].astype(BF16) for c in grp[:-1]], axis=0)
        for i, c in enumerate(grp):
            rows = slice(c * C, (c + 1) * C)
            q = q_ref[0, rows, :].astype(F32) * (GLA_DK ** -0.5)
            expo = b_s[c]
            pieces = []
            for j in range(i - 1, -2, -1):
                pieces.insert(0, (q * jnp.exp(expo)).astype(BF16))
                if j >= 0:
                    expo = expo + b_s[grp[j], last:last + 1, :]
            lhs = jnp.concatenate(pieces, axis=1)
            o_ref[0, rows, :] = oi_s[rows, :] + _dot(lhs, wcat[:(i + 1) * GLA_DK, :])
        for c in grp:
            dec = jnp.transpose(jnp.broadcast_to(jnp.exp(b_s[c, last:last + 1, :]), (GLA_DK, GLA_DK)))
            st = jnp.concatenate([dec] * (GLA_DV // GLA_DK), axis=1) * st + kv_s[c]
    st_ref[...] = st


GLA_STREAM_IN, GLA_STREAM_OUT, GLA_STREAM_SCRATCH = 10, 2, 5


def _gla_kernel(*refs, C, TB, lv_refs):
    n_in, n_out, n_scr = GLA_STREAM_IN, GLA_STREAM_OUT, GLA_STREAM_SCRATCH
    ins = [refs[d * n_in:(d + 1) * n_in] for d in range(2)]
    outs = [refs[2 * n_in + d * n_out:2 * n_in + (d + 1) * n_out] for d in range(2)]
    base = 2 * (n_in + n_out)
    scr = [refs[base + d * n_scr:base + (d + 1) * n_scr] for d in range(2)]
    for part in ("init", "body", "final"):
        for d, reverse in enumerate((False, True)):
            _gla_stream(*ins[d], *outs[d], *scr[d], C=C, TB=TB, reverse=reverse, refs=lv_refs[d], part=part)


def _gla_bidir(pm, ps, up, bias, s0, *, TB):
    B, L, _ = pm.shape
    C = GLA_CHUNK
    nb = L // TB
    assert C >= (2 << GLA_MXU_LEVELS)
    qb, kb, vb = P_Q // GLA_DK, P_K // GLA_DK, P_V // GLA_DV
    in_specs, args, out_specs, out_shape, scratch, lv_refs = [], [], [], [], [], []
    for d, reverse in enumerate((False, True)):
        dmat, masks, signs, refs = _gla_consts(C, reverse)
        lv_refs.append(refs)
        tmap = (lambda i: nb - 1 - i) if reverse else (lambda i: i)
        in_specs += [
            pl.BlockSpec((1, TB, GLA_DK), lambda b, h, i, tmap=tmap: (b, tmap(i), qb + h)),
            pl.BlockSpec((1, TB, GLA_DK), lambda b, h, i, tmap=tmap: (b, tmap(i), kb + h)),
            pl.BlockSpec((1, TB, GLA_DV), lambda b, h, i, tmap=tmap: (b, tmap(i), vb + h)),
            pl.BlockSpec((1, TB, P_SMALL), lambda b, h, i, tmap=tmap: (b, tmap(i), 0)),
            pl.BlockSpec((GLA_RANK, GLA_DK), lambda b, h, i: (0, h)),
            pl.BlockSpec((1, GLA_DK), lambda b, h, i: (0, h)),
            _const_spec(dmat.shape),
            _const_spec(masks.shape),
            _const_spec(signs.shape),
            pl.BlockSpec((1, 1, GLA_DK, GLA_DV), lambda b, h, i: (b, h, 0, 0)),
        ]
        args += [pm, pm, pm, ps, up[d], bias[d], dmat, masks, signs, s0[d]]
        out_specs += [pl.BlockSpec((1, TB, GLA_DV), lambda b, h, i, tmap=tmap: (b, tmap(i), h)),
                      pl.BlockSpec((1, 1, GLA_DK, GLA_DV), lambda b, h, i: (b, h, 0, 0))]
        out_shape += [jax.ShapeDtypeStruct((B, L, GLA_V), F32),
                      jax.ShapeDtypeStruct((B, GLA_HEADS, GLA_DK, GLA_DV), F32)]
        scratch += [pltpu.VMEM((GLA_DK, GLA_DV), F32),
                    pltpu.VMEM((TB // C, masks.shape[0] * C, GLA_DK), F32),
                    pltpu.VMEM((TB // C, C, GLA_DK), F32),
                    pltpu.VMEM((TB, GLA_DV), F32),
                    pltpu.VMEM((TB // C, GLA_DK, GLA_DV), F32)]
    assert len(in_specs) == 2 * GLA_STREAM_IN and len(scratch) == 2 * GLA_STREAM_SCRATCH
    return pl.pallas_call(
        functools.partial(_gla_kernel, C=C, TB=TB, lv_refs=lv_refs),
        grid=(B, GLA_HEADS, nb),
        in_specs=in_specs,
        out_specs=out_specs,
        out_shape=out_shape,
        scratch_shapes=scratch,
        compiler_params=_params(("parallel", "parallel", "arbitrary")),
        name="gla",
    )(*args)


def _conv_shift_mats(T, rowlen):
    t = np.arange(T)
    mats = []
    for j in range(SSM_CONV):
        off = j - CONV_LEFT
        if off == 0:
            continue
        src = t + off
        ok = (src // rowlen == t // rowlen) & (src >= 0) & (src < T)
        m = np.zeros((T, T), np.float32)
        m[t[ok], src[ok]] = 1.0
        mats.append(m)
    return jnp.asarray(np.stack(mats), BF16)


def _conv_silu(u, w, b, sh_ref):
    acc = b + u.astype(F32) * w[CONV_LEFT:CONV_LEFT + 1, :]
    taps = [j for j in range(SSM_CONV) if j != CONV_LEFT]
    for i, j in enumerate(taps):
        acc = acc + _dot(sh_ref[i], u) * w[j:j + 1, :]
    return _silu(acc)


def _ssd_kernel(*refs, C, TB, reverse, finalize, conv, emit, lane0):
    refs = list(refs)
    xs_ref, bm_ref, cm_ref, ps_ref = refs[:4]
    refs = refs[4:]
    if conv:
        wx_ref, wb_ref, wc_ref, bx_ref, bb_ref, bc_ref, sh_ref = refs[:7]
        refs = refs[7:]
    dtb_ref, alog_ref, tri_ref, s0_ref = refs[:4]
    refs = refs[4:]
    if finalize:
        z_ref, yp_ref, dsk_ref, nw_ref = refs[:4]
        refs = refs[4:]
    y_ref, sf_ref = refs[:2]
    refs = refs[2:]
    if emit:
        xo_ref, bo_ref, co_ref = refs[:3]
        refs = refs[3:]
    st_ref, xc_s, bc_s, cc_s, ya_s = refs
    nchunk = TB // C
    blk = pl.program_id(1)
    G, HPG, P, N = SSM_GROUPS, SSM_HPG, SSM_HEADDIM, SSM_STATE

    @pl.when(blk == 0)
    def _():
        st_ref[...] = s0_ref[0]

    if conv:
        xc_s[...] = _conv_silu(xs_ref[0], wx_ref[...], bx_ref[...], sh_ref).astype(BF16)
        bc_s[...] = _conv_silu(bm_ref[0], wb_ref[...], bb_ref[...], sh_ref).astype(BF16)
        cc_s[...] = _conv_silu(cm_ref[0], wc_ref[...], bc_ref[...], sh_ref).astype(BF16)
        if emit:
            xo_ref[0] = xc_s[...]
            bo_ref[0] = bc_s[...]
            co_ref[0] = cc_s[...]
    else:
        xc_s[...] = xs_ref[0]
        bc_s[...] = bm_ref[0]
        cc_s[...] = cm_ref[0]

    neg_a = -jnp.exp(alog_ref[...])
    ti = lax.broadcasted_iota(jnp.int32, (C, C), 0)
    si = lax.broadcasted_iota(jnp.int32, (C, C), 1)
    keep = (si >= ti) if reverse else (si <= ti)
    lo = lax.broadcasted_iota(jnp.int32, (1, 2 * P), 1) < P
    last = 0 if reverse else C - 1
    nheads = G * HPG

    for c in (reversed(range(nchunk)) if reverse else range(nchunk)):
        rows = slice(c * C, (c + 1) * C)
        dt = _softplus(ps_ref[0, rows, :] + dtb_ref[...])
        cum = _dot01(tri_ref[...], dt * neg_a) * LOG2E
        cum_t = jnp.transpose(cum)[lane0:lane0 + nheads, :]
        dt_t = jnp.transpose(dt)[lane0:lane0 + nheads, :]
        w_t = (dt_t * jnp.exp2(cum_t[:, last:last + 1] - cum_t)).astype(BF16)
        cdl_t = cum_t - jnp.log2(dt_t)
        dec_all = jnp.exp2(cum[last:last + 1, :])
        for g in range(G):
            bm_g = bc_s[rows, g * N:(g + 1) * N]
            cm_g = cc_s[rows, g * N:(g + 1) * N]
            cb = _dot_nt(cm_g, bm_g).astype(BF16)
            bm_t = jnp.transpose(bm_g.astype(F32)).astype(BF16)
            for pr in range(HPG // 2):
                lanes = slice(g * SSM_GW + pr * 2 * P, g * SSM_GW + (pr + 1) * 2 * P)
                x_pair = xc_s[rows, lanes]
                s_pair = st_ref[g, :, pr * 2 * P:(pr + 1) * 2 * P]
                s_bf = s_pair.astype(BF16)
                y = None
                ds = None
                decs = []
                for half in range(2):
                    j = g * HPG + pr * 2 + half
                    lane = lane0 + j
                    sel = lo if half == 0 else jnp.logical_not(lo)
                    bc_ = jnp.broadcast_to(cum[:, lane:lane + 1], (C, C))
                    m = cb * jnp.exp2(jnp.where(keep, bc_ - cdl_t[j:j + 1, :], NEG_BIG)).astype(BF16)
                    cd = cm_g * jnp.exp2(bc_).astype(BF16)
                    lhs = jnp.concatenate([m, cd], axis=1)
                    rhs = jnp.concatenate([jnp.where(sel, x_pair, jnp.zeros_like(x_pair)),
                                           jnp.where(sel, s_bf, jnp.zeros_like(s_bf))], axis=0)
                    yh = _dot(lhs, rhs)
                    dh = _dot(bm_t * w_t[j:j + 1, :],
                              jnp.where(sel, x_pair, jnp.zeros_like(x_pair)))
                    y = yh if y is None else y + yh
                    ds = dh if ds is None else ds + dh
                    decs.append(dec_all[:, lane:lane + 1])
                ya_s[rows, lanes] = y
                dec = jnp.where(lo, decs[0], decs[1])
                st_ref[g, :, pr * 2 * P:(pr + 1) * 2 * P] = dec * s_pair + ds

    if finalize:
        for g in range(G):
            lanes = slice(g * SSM_GW, (g + 1) * SSM_GW)
            y = ya_s[:, lanes] + yp_ref[0, :, lanes] + dsk_ref[:, lanes] * xc_s[:, lanes].astype(F32)
            y = y * _silu(z_ref[0, :, lanes].astype(F32))
            y = y * lax.rsqrt(jnp.mean(y * y, axis=-1, keepdims=True) + EPS) * nw_ref[:, lanes]
            y_ref[0, :, lanes] = y.astype(y_ref.dtype)
    else:
        y_ref[0] = ya_s[...]

    @pl.when(blk == pl.num_programs(1) - 1)
    def _():
        sf_ref[0] = st_ref[...]


def _ssd_pass(src, ps, conv_w, conv_b, dt_bias, a_log, s0, *, reverse, finalize, TB, rowlen, emit=False,
              y_prev=None, z_src=None, d_skip_x=None, norm_w=None):
    (xs_a, xs_o), (bm_a, bm_o), (cm_a, cm_o) = src
    B, L, _ = ps.shape
    C = SSD_CHUNK
    nb = L // TB
    G = SSM_GROUPS
    conv = conv_w is not None
    tmap = (lambda i: nb - 1 - i) if reverse else (lambda i: i)
    idx = np.arange(C)
    tri = (idx[None, :] >= idx[:, None]) if reverse else (idx[None, :] <= idx[:, None])
    tri = jnp.asarray(tri.astype(np.float32), BF16)
    in_specs = [
        pl.BlockSpec((1, TB, SSM_INNER), lambda b, i: (b, tmap(i), xs_o)),
        pl.BlockSpec((1, TB, SSM_BC), lambda b, i: (b, tmap(i), bm_o)),
        pl.BlockSpec((1, TB, SSM_BC), lambda b, i: (b, tmap(i), cm_o)),
        pl.BlockSpec((1, TB, P_SMALL), lambda b, i: (b, tmap(i), 0)),
    ]
    args = [xs_a, bm_a, cm_a, ps]
    if conv:
        nx = SSM_INNER // SSM_BC
        in_specs += [
            pl.BlockSpec((SSM_CONV, SSM_INNER), lambda b, i: (0, 0)),
            pl.BlockSpec((SSM_CONV, SSM_BC), lambda b, i: (0, nx)),
            pl.BlockSpec((SSM_CONV, SSM_BC), lambda b, i: (0, nx + 1)),
            pl.BlockSpec((1, SSM_INNER), lambda b, i: (0, 0)),
            pl.BlockSpec((1, SSM_BC), lambda b, i: (0, nx)),
            pl.BlockSpec((1, SSM_BC), lambda b, i: (0, nx + 1)),
            _const_spec((SSM_CONV - 1, TB, TB)),
        ]
        args += [conv_w, conv_w, conv_w, conv_b, conv_b, conv_b, _conv_shift_mats(TB, rowlen)]
    in_specs += [_const_spec((1, P_SMALL)), _const_spec((1, P_SMALL)), _const_spec((C, C)),
                 pl.BlockSpec((1, G, SSM_STATE, SSM_GW), lambda b, i: (b, 0, 0, 0))]
    args += [dt_bias, a_log, tri, s0]
    if finalize:
        z_a, z_o = z_src
        in_specs += [
            pl.BlockSpec((1, TB, SSM_INNER), lambda b, i: (b, tmap(i), z_o)),
            pl.BlockSpec((1, TB, SSM_INNER), lambda b, i: (b, tmap(i), 0)),
            _const_spec((1, SSM_INNER)), _const_spec((1, SSM_INNER)),
        ]
        args += [z_a, y_prev, d_skip_x, norm_w]
    out_specs = [pl.BlockSpec((1, TB, SSM_INNER), lambda b, i: (b, tmap(i), 0)),
                 pl.BlockSpec((1, G, SSM_STATE, SSM_GW), lambda b, i: (b, 0, 0, 0))]
    out_shape = [jax.ShapeDtypeStruct((B, L, SSM_INNER), BF16 if finalize else F32),
                 jax.ShapeDtypeStruct((B, G, SSM_STATE, SSM_GW), F32)]
    if emit:
        out_specs += [pl.BlockSpec((1, TB, SSM_INNER), lambda b, i: (b, tmap(i), 0)),
                      pl.BlockSpec((1, TB, SSM_BC), lambda b, i: (b, tmap(i), 0)),
                      pl.BlockSpec((1, TB, SSM_BC), lambda b, i: (b, tmap(i), 0))]
        out_shape += [jax.ShapeDtypeStruct((B, L, SSM_INNER), BF16),
                      jax.ShapeDtypeStruct((B, L, SSM_BC), BF16),
                      jax.ShapeDtypeStruct((B, L, SSM_BC), BF16)]
    kern = functools.partial(_ssd_kernel, C=C, TB=TB, reverse=reverse, finalize=finalize,
                             conv=conv, emit=emit, lane0=S_DTB if reverse else S_DTF)
    return pl.pallas_call(
        kern,
        grid=(B, nb),
        in_specs=in_specs,
        out_specs=out_specs,
        out_shape=out_shape,
        scratch_shapes=[pltpu.VMEM((G, SSM_STATE, SSM_GW), F32),
                        pltpu.VMEM((TB, SSM_INNER), BF16),
                        pltpu.VMEM((TB, SSM_BC), BF16),
                        pltpu.VMEM((TB, SSM_BC), BF16),
                        pltpu.VMEM((TB, SSM_INNER), F32)],
        compiler_params=_params(("parallel", "arbitrary")),
        name="ssd_" + ("rev" if reverse else "fwd") + ("_fin" if finalize else ""),
    )(*args)


def _merge_kernel(of_ref, ob_ref, r_ref, gnw_ref, sb_ref, ga_ref, gb_ref, x_ref, g1_ref, wpa_ref, wpb_ref,
                  wout_ref, h_ref):
    o = of_ref[0] + ob_ref[0]
    heads = []
    for h in range(GLA_HEADS):
        oh = o[:, h * GLA_DV:(h + 1) * GLA_DV]
        heads.append(oh * lax.rsqrt(jnp.mean(oh * oh, axis=-1, keepdims=True) + EPS) * gnw_ref[...])
    oa = (jnp.concatenate(heads, axis=1) * _silu(r_ref[0].astype(F32))).astype(BF16)
    ya = _dot(oa, wpa_ref[...])
    yb = _dot(sb_ref[0], wpb_ref[...])
    m = _sigmoid(ga_ref[0].astype(F32)) * ya + _sigmoid(gb_ref[0].astype(F32)) * yb
    mix = _dot(m.astype(BF16), wout_ref[...])
    h_ref[0] = x_ref[0] + g1_ref[0] * mix


def _merge(o_f, o_b, gla_norm_w, s_b, pm, x, g1, w_pa, w_pb, w_out, tm):
    B, L, _ = x.shape
    gab, gbb, rb = P_GA // D_MODEL, P_GB // D_MODEL, P_R // GLA_V
    return pl.pallas_call(
        _merge_kernel,
        grid=(B, L // tm),
        in_specs=[pl.BlockSpec((1, tm, GLA_V), lambda b, i: (b, i, 0)),
                  pl.BlockSpec((1, tm, GLA_V), lambda b, i: (b, i, 0)),
                  pl.BlockSpec((1, tm, GLA_V), lambda b, i: (b, i, rb)),
                  _const_spec((1, GLA_DV)),
                  pl.BlockSpec((1, tm, SSM_INNER), lambda b, i: (b, i, 0)),
                  pl.BlockSpec((1, tm, D_MODEL), lambda b, i: (b, i, gab)),
                  pl.BlockSpec((1, tm, D_MODEL), lambda b, i: (b, i, gbb)),
                  pl.BlockSpec((1, tm, D_MODEL), lambda b, i: (b, i, 0)),
                  pl.BlockSpec((1, 1, D_MODEL), lambda b, i: (b, 0, 0)),
                  _const_spec(w_pa.shape), _const_spec(w_pb.shape), _const_spec(w_out.shape)],
        out_specs=pl.BlockSpec((1, tm, D_MODEL), lambda b, i: (b, i, 0)),
        out_shape=jax.ShapeDtypeStruct((B, L, D_MODEL), F32),
        compiler_params=_params(("parallel", "parallel")),
        name="merge",
    )(o_f, o_b, pm, gla_norm_w, s_b, pm, pm, x, g1, w_pa, w_pb, w_out)


FFN_CHUNK = 256


def _ffn_kernel(h_ref, mod_ref, n2_ref, fw_ref, wg_ref, wu_ref, wd_ref, o_ref):
    h = h_ref[0]
    hn = h * lax.rsqrt(jnp.mean(h * h, axis=-1, keepdims=True) + EPS) * n2_ref[...]
    hn = (hn * (1.0 + mod_ref[0, 1:2, :]) + mod_ref[0, 0:1, :]).astype(BF16)
    acc = jnp.zeros(h.shape, F32)
    for c in range(D_FF // FFN_CHUNK):
        cols = slice(c * FFN_CHUNK, (c + 1) * FFN_CHUNK)
        gt = _dot(hn, wg_ref[:, cols])
        up = _dot(hn, wu_ref[:, cols])
        acc = acc + _dot((_silu(gt) * up).astype(BF16), wd_ref[cols, :])
    h2 = h + mod_ref[0, 2:3, :] * acc
    o_ref[0] = h2 * lax.rsqrt(jnp.mean(h2 * h2, axis=-1, keepdims=True) + EPS) * fw_ref[...]


def _ffn(h, mod, n2w, fw, w_gate, w_up, w_down, tm):
    B, L, _ = h.shape
    return pl.pallas_call(
        _ffn_kernel,
        grid=(B, L // tm),
        in_specs=[pl.BlockSpec((1, tm, D_MODEL), lambda b, i: (b, i, 0)),
                  pl.BlockSpec((1, 3, D_MODEL), lambda b, i: (b, 0, 0)),
                  _const_spec((1, D_MODEL)), _const_spec((1, D_MODEL)),
                  pl.BlockSpec(w_gate.shape, lambda b, i: (0, 0), pipeline_mode=pl.Buffered(1)),
                  pl.BlockSpec(w_up.shape, lambda b, i: (0, 0), pipeline_mode=pl.Buffered(1)),
                  pl.BlockSpec(w_down.shape, lambda b, i: (0, 0), pipeline_mode=pl.Buffered(1))],
        out_specs=pl.BlockSpec((1, tm, D_MODEL), lambda b, i: (b, i, 0)),
        out_shape=jax.ShapeDtypeStruct((B, L, D_MODEL), F32),
        compiler_params=_params(("parallel", "parallel")),
        name="ffn",
    )(h, mod, n2w, fw, w_gate, w_up, w_down)


def _pick_block(L, pref):
    tb = min(L, pref)
    assert L % tb == 0
    return tb


def kernel(x, c, ctx, c_ctx, w_ada, b_ada, norm1_w, w_in, gla_up_f, gla_bias_f, gla_up_b, gla_bias_b,
           gla_norm_w, conv_w, conv_b, dt_bias_f, dt_bias_b, a_log_f, a_log_b, d_skip, ssm_norm_w,
           w_pa, w_pb, w_out, norm2_w, w_gate, w_up, w_down, final_norm_w):
    B, L, D = x.shape
    Lc = ctx.shape[1]
    depth = w_ada.shape[0]
    assert depth == 1 and D == D_MODEL
    assert L % GRID_W == 0 and L % SSD_CHUNK == 0 and Lc % SSD_CHUNK == 0
    lay = 0

    nrow = -(-(B + 1) // 8) * 8
    cc = jnp.zeros((nrow, D), F32).at[:B].set(c).at[B].set(c_ctx)
    ada = _ada(cc, w_ada, b_ada[lay][None, :], lay)
    sh1, sc1, g1, sh2, sc2, g2 = [ada[:, i * D:(i + 1) * D] for i in range(6)]
    mod1 = jnp.stack([sh1[:B], sc1[:B]], axis=1)
    mod1_c = jnp.broadcast_to(jnp.stack([sh1[B], sc1[B]])[None], (B, 2, D))
    mod2 = jnp.stack([sh2[:B], sc2[:B], g2[:B]], axis=1)
    g1_l = g1[:B, None, :]

    wt = jnp.swapaxes(w_in[lay], 0, 1)
    w_main, w_small = _regroup(wt)
    nw1 = norm1_w[lay][None, :]

    at_lanes = lambda p, off: jnp.zeros((1, P_SMALL), F32).at[0, off:off + SSM_HEADS].set(p[lay])
    dtb_f, al_f = at_lanes(dt_bias_f, S_DTF), at_lanes(a_log_f, S_DTF)
    dtb_b, al_b = at_lanes(dt_bias_b, S_DTB), at_lanes(a_log_b, S_DTB)
    cw, cb_ = conv_w[lay], conv_b[lay][None, :]
    up_f, up_b = gla_up_f[lay], gla_up_b[lay]
    bi_f, bi_b = gla_bias_f[lay][None, :], gla_bias_b[lay][None, :]

    gla_zero = jnp.zeros((B, GLA_HEADS, GLA_DK, GLA_DV), F32)
    ssd_zero = jnp.zeros((B, SSM_GROUPS, SSM_STATE, SSM_GW), F32)

    pm_c, ps_c = _inproj(ctx, mod1_c, nw1, w_main, w_small, _pick_block(Lc, 256))
    _, sg_f, _, sg_b = _gla_bidir(pm_c, ps_c, (up_f, up_b), (bi_f, bi_b), (gla_zero, gla_zero),
                                  TB=_pick_block(Lc, 256))
    src_c = ((pm_c, P_XS // SSM_INNER), (pm_c, P_BM // SSM_BC), (pm_c, P_CM // SSM_BC))
    _, ss_f = _ssd_pass(src_c, ps_c, cw, cb_, dtb_f, al_f, ssd_zero, reverse=False, finalize=False,
                        TB=Lc, rowlen=Lc)
    _, ss_b = _ssd_pass(src_c, ps_c, cw, cb_, dtb_b, al_b, ssd_zero, reverse=True, finalize=False,
                        TB=Lc, rowlen=Lc)

    pm, ps = _inproj(x, mod1, nw1, w_main, w_small, _pick_block(L, 1024))
    tbg = _pick_block(L, 1024)
    og_f, _, og_b, _ = _gla_bidir(pm, ps, (up_f, up_b), (bi_f, bi_b), (sg_f, sg_b), TB=tbg)
    tbs = _pick_block(L, 256)
    src = ((pm, P_XS // SSM_INNER), (pm, P_BM // SSM_BC), (pm, P_CM // SSM_BC))
    ys_f, _, xc, bc, cc = _ssd_pass(src, ps, cw, cb_, dtb_f, al_f, ss_f, reverse=False, finalize=False,
                                    TB=tbs, rowlen=GRID_W, emit=True)
    o_b, _ = _ssd_pass(((xc, 0), (bc, 0), (cc, 0)), ps, None, None, dtb_b, al_b, ss_b, reverse=True,
                       finalize=True, TB=tbs, rowlen=GRID_W, y_prev=ys_f, z_src=(pm, P_Z // SSM_INNER),
                       d_skip_x=jnp.repeat(d_skip[lay], SSM_HEADDIM)[None, :],
                       norm_w=ssm_norm_w[lay][None, :])

    h = _merge(og_f, og_b, gla_norm_w[lay][None, :], o_b, pm, x, g1_l, w_pa[lay].astype(BF16),
               w_pb[lay].astype(BF16), w_out[lay].astype(BF16), _pick_block(L, 512))
    return _ffn(h, mod2, norm2_w[lay][None, :], final_norm_w[None, :], w_gate[lay].astype(BF16),
                w_up[lay].astype(BF16), w_down[lay].astype(BF16), _pick_block(L, 512))
```

```python
import functools

import numpy as np
import jax
import jax.numpy as jnp
from jax import lax
from jax.experimental import pallas as pl
from jax.experimental.pallas import tpu as pltpu

F32 = jnp.float32
BF16 = jnp.bfloat16

D_MODEL = 1024
GRID_W = 64
EPS = 1e-6

GLA_HEADS = 4
GLA_DK = 128
GLA_DV = 256
GLA_QK = GLA_HEADS * GLA_DK
GLA_V = GLA_HEADS * GLA_DV
GLA_RANK = 16
GLA_TAU = 16.0

SSM_INNER = 2 * D_MODEL
SSM_HEADDIM = 64
SSM_HEADS = SSM_INNER // SSM_HEADDIM
SSM_GROUPS = 4
SSM_HPG = SSM_HEADS // SSM_GROUPS
SSM_STATE = 128
SSM_BC = SSM_GROUPS * SSM_STATE
SSM_CONV = 4
CONV_LEFT = 2
SSM_GW = SSM_HPG * SSM_HEADDIM

D_FF = ((8 * D_MODEL // 3 + 255) // 256) * 256

_IN_WIDTHS = (GLA_QK, GLA_QK, GLA_V, GLA_V, GLA_RANK, GLA_RANK,
              SSM_INNER, SSM_INNER, SSM_BC, SSM_BC, SSM_HEADS, SSM_HEADS, D_MODEL, D_MODEL)
_IN_OFF = np.concatenate([[0], np.cumsum(_IN_WIDTHS)]).tolist()

P_Z, P_XS = 0, 2048
P_Q, P_K, P_V, P_R = 4096, 4608, 5120, 6144
P_BM, P_CM = 7168, 7680
P_GA, P_GB = 8192, 9216
P_MAIN = 10240
S_LRF, S_LRB, S_DTF, S_DTB = 0, 16, 32, 64
P_SMALL = 128

GLA_CHUNK = 128
SSD_CHUNK = 128

ADA_TN = 1536
INPROJ_TM, INPROJ_TN = 1024, 2048
GLA_TB = 2048
SSD_TB = 512
MERGE_TM = 512
FFN_TM = 512

VMEM_LIMIT = 56 * 1024 * 1024
NEG_BIG = -1e30
LOG2E = 1.4426950408889634


def _sigmoid(x):
    return 1.0 / (1.0 + jnp.exp(-x))


def _silu(x):
    return x * _sigmoid(x)


def _softplus(x):
    return jnp.maximum(x, 0.0) + jnp.log(1.0 + jnp.exp(-jnp.abs(x)))


def _split3(x):
    hi = x.astype(BF16)
    r1 = x - hi.astype(F32)
    mid = r1.astype(BF16)
    lo = (r1 - mid.astype(F32)).astype(BF16)
    return hi, mid, lo


def _dot(a, b):
    return jnp.dot(a, b, preferred_element_type=F32)


def _dot_nt(a, b):
    return lax.dot_general(a, b, (((1,), (1,)), ((), ())), preferred_element_type=F32)


def _dot_tn(a, b):
    return lax.dot_general(a, b, (((0,), (0,)), ((), ())), preferred_element_type=F32)


def _dot01(m01, x):
    hi, mid, lo = _split3(x)
    return _dot(m01, hi) + _dot(m01, mid) + _dot(m01, lo)


def _params(sem):
    return pltpu.CompilerParams(dimension_semantics=sem, vmem_limit_bytes=VMEM_LIMIT)


def _const_spec(shape):
    n = len(shape)
    return pl.BlockSpec(shape, lambda *_: (0,) * n)


def _ada_kernel(c_ref, w_ref, b_ref, o_ref):
    s = _silu(c_ref[...])
    w = w_ref[0]
    s_hi = s.astype(BF16)
    s_lo = (s - s_hi.astype(F32)).astype(BF16)
    w_hi = w.astype(BF16)
    w_lo = (w - w_hi.astype(F32)).astype(BF16)
    o_ref[...] = _dot(s_hi, w_hi) + _dot(s_hi, w_lo) + _dot(s_lo, w_hi) + b_ref[...]


def _ada(cc, w, b, lay):
    rows = cc.shape[0]
    n = w.shape[2]
    tn = ADA_TN
    return pl.pallas_call(
        _ada_kernel,
        grid=(n // tn,),
        in_specs=[pl.BlockSpec((rows, D_MODEL), lambda j: (0, 0)),
                  pl.BlockSpec((1, D_MODEL, tn), lambda j: (lay, 0, j)),
                  pl.BlockSpec((1, tn), lambda j: (0, j))],
        out_specs=pl.BlockSpec((rows, tn), lambda j: (0, j)),
        out_shape=jax.ShapeDtypeStruct((rows, n), F32),
        compiler_params=_params(("arbitrary",)),
        name="ada",
    )(cc, w, b)


REGROUP_ROWS = 1024
REGROUP_ALIGN = 16
_MAIN_RUNS = ((P_Z, _IN_OFF[6], 2 * SSM_INNER), (P_Q, _IN_OFF[0], 2 * GLA_QK + 2 * GLA_V),
              (P_BM, _IN_OFF[8], 2 * SSM_BC), (P_GA, _IN_OFF[12], 2 * D_MODEL))


def _regroup_kernel(wt_ref, lr_ref, dt_ref, wm_ref, ws_ref):
    wm_ref[...] = jnp.transpose(wt_ref[...]).astype(BF16)
    pad = jnp.zeros((P_SMALL - lr_ref.shape[0] - dt_ref.shape[0], lr_ref.shape[1]), F32)
    ws_ref[...] = jnp.transpose(jnp.concatenate([lr_ref[...], dt_ref[...], pad], axis=0)).astype(BF16)


def _regroup(wt):
    _, D = wt.shape
    R = REGROUP_ROWS
    U = REGROUP_ALIGN
    assert all(dst % R == 0 and n % R == 0 and src % U == 0 for dst, src, n in _MAIN_RUNS)

    def src_row(i):
        units = jnp.int32(0)
        for dst, src, n in _MAIN_RUNS:
            inside = (i * R >= dst) & (i * R < dst + n)
            units = jnp.where(inside, (src - dst) // U + i * (R // U), units)
        return units * U

    return pl.pallas_call(
        _regroup_kernel,
        grid=(P_MAIN // R,),
        in_specs=[pl.BlockSpec((pl.Element(R), pl.Element(D)), lambda i: (src_row(i), 0)),
                  pl.BlockSpec((pl.Element(2 * GLA_RANK), pl.Element(D)), lambda i: (_IN_OFF[4], 0)),
                  pl.BlockSpec((pl.Element(2 * SSM_HEADS), pl.Element(D)), lambda i: (_IN_OFF[10], 0))],
        out_specs=[pl.BlockSpec((D, R), lambda i: (0, i)),
                   pl.BlockSpec((D, P_SMALL), lambda i: (0, 0))],
        out_shape=[jax.ShapeDtypeStruct((D, P_MAIN), BF16), jax.ShapeDtypeStruct((D, P_SMALL), BF16)],
        compiler_params=_params(("arbitrary",)),
        name="regroup",
    )(wt, wt, wt)


def _inproj_kernel(x_ref, mod_ref, nw_ref, wm_ref, ws_ref, om_ref, os_ref, xn_ref):
    @pl.when(pl.program_id(2) == 0)
    def _():
        x = x_ref[0]
        y = x * lax.rsqrt(jnp.mean(x * x, axis=-1, keepdims=True) + EPS) * nw_ref[...]
        y = y * (1.0 + mod_ref[0, 1:2, :]) + mod_ref[0, 0:1, :]
        xn = y.astype(BF16)
        xn_ref[...] = xn
        os_ref[0] = _dot(xn, ws_ref[...])

    om_ref[0] = _dot(xn_ref[...], wm_ref[...]).astype(BF16)


def _inproj(x, mod, nw, w_main, w_small, tm):
    B, L, _ = x.shape
    tn = INPROJ_TN
    return pl.pallas_call(
        _inproj_kernel,
        grid=(B, L // tm, P_MAIN // tn),
        in_specs=[pl.BlockSpec((1, tm, D_MODEL), lambda b, i, j: (b, i, 0)),
                  pl.BlockSpec((1, 2, D_MODEL), lambda b, i, j: (b, 0, 0)),
                  pl.BlockSpec((1, D_MODEL), lambda b, i, j: (0, 0)),
                  pl.BlockSpec((D_MODEL, tn), lambda b, i, j: (0, j)),
                  pl.BlockSpec((D_MODEL, P_SMALL), lambda b, i, j: (0, 0))],
        out_specs=[pl.BlockSpec((1, tm, tn), lambda b, i, j: (b, i, j)),
                   pl.BlockSpec((1, tm, P_SMALL), lambda b, i, j: (b, i, 0))],
        out_shape=[jax.ShapeDtypeStruct((B, L, P_MAIN), BF16),
                   jax.ShapeDtypeStruct((B, L, P_SMALL), F32)],
        scratch_shapes=[pltpu.VMEM((tm, D_MODEL), BF16)],
        compiler_params=_params(("parallel", "parallel", "arbitrary")),
        name="inproj",
    )(x, mod, nw, w_main, w_small)


GLA_MXU_LEVELS = 3
GLA_GROUP = 4


def _gla_consts(C, reverse):
    NL = int(np.log2(C))
    idx = np.arange(C)
    tri = (idx[None, :] <= idx[:, None]).astype(np.float32)
    mats = [tri]
    masks = [np.eye(C, dtype=np.float32)]
    refs, signs = [], []
    for lev in range(NL):
        h = 1 << lev
        blk = idx // (2 * h)
        half = (idx // h) % 2
        ref = blk * 2 * h + h - 1
        if lev < GLA_MXU_LEVELS:
            d = tri - tri[ref]
            d[half == 0] *= -1.0
            mats.append(d)
        else:
            sg = np.where(half == 1, 1.0, -1.0).astype(np.float32)
            if reverse:
                ref, sg = (C - 1 - ref)[::-1], sg[::-1]
            refs.append([int(ref[m * 2 * h]) for m in range(C // (2 * h))])
            signs.append(np.broadcast_to(sg[:, None], (C, GLA_DK)))
        masks.append(((blk[:, None] == blk[None, :]) & (half[:, None] == 1)
                      & (half[None, :] == 0)).astype(np.float32))
    if reverse:
        mats = [m[::-1, ::-1] for m in mats]
        masks = [m[::-1, ::-1] for m in masks]
    dmat = np.concatenate(mats, axis=0)
    return (jnp.asarray(dmat, BF16), jnp.asarray(np.stack(masks), F32),
            jnp.asarray(np.stack(signs), F32), refs)


def _gla_stream(q_ref, k_ref, v_ref, lr_ref, up_ref, bias_ref, dm_ref, mk_ref, sg_ref, s0_ref,
                o_ref, sf_ref, st_ref, e_s, b_s, oi_s, kv_s, *, C, TB, reverse, refs, part):
    NL = int(np.log2(C))
    nchunk = TB // C
    blk = pl.program_id(2)
    last = 0 if reverse else C - 1
    lr_off = S_LRB if reverse else S_LRF

    if part == "init":
        @pl.when(blk == 0)
        def _():
            st_ref[...] = s0_ref[0, 0]
        return
    if part == "final":
        @pl.when(blk == pl.num_programs(2) - 1)
        def _():
            sf_ref[0, 0] = st_ref[...]
        return

    lr = lr_ref[0][:, lr_off:lr_off + GLA_RANK]
    pre = _dot(lr.astype(BF16), up_ref[...].astype(BF16)) + bias_ref[...]
    g = -_softplus(-pre) * (1.0 / GLA_TAU)
    g_hi = g.astype(BF16)
    g_lo = (g - g_hi.astype(F32)).astype(BF16)
    gs = jnp.concatenate([g_hi, g_lo], axis=1)

    for c in range(nchunk):
        ex = _dot(dm_ref[...], gs[c * C:(c + 1) * C])
        ex = ex[:, :GLA_DK] + ex[:, GLA_DK:]
        b = ex[0:C]
        b_s[c] = b
        e_s[c, 0:GLA_MXU_LEVELS * C, :] = jnp.exp(ex[C:(1 + GLA_MXU_LEVELS) * C])
        for li, lev in enumerate(range(GLA_MXU_LEVELS, NL)):
            h2 = 2 << lev
            bref = jnp.concatenate([jnp.broadcast_to(b[r:r + 1, :], (h2, GLA_DK)) for r in refs[li]], axis=0)
            e_s[c, lev * C:(lev + 1) * C, :] = jnp.exp(sg_ref[li] * (b - bref))
        e_s[c, NL * C:(NL + 1) * C, :] = jnp.exp(b[last:last + 1, :] - b)

    for c in range(nchunk):
        rows = slice(c * C, (c + 1) * C)
        q = q_ref[0, rows, :].astype(F32) * (GLA_DK ** -0.5)
        k = k_ref[0, rows, :].astype(F32)
        v = v_ref[0, rows, :]
        att = mk_ref[0] * _dot_nt(q.astype(BF16), k.astype(BF16))
        for lev in range(NL):
            e_l = e_s[c, lev * C:(lev + 1) * C, :]
            att = att + mk_ref[lev + 1] * _dot_nt((q * e_l).astype(BF16), (k * e_l).astype(BF16))
        oi_s[rows, :] = _dot(att.astype(BF16), v)
        kv_s[c] = _dot_tn((k * e_s[c, NL * C:(NL + 1) * C, :]).astype(BF16), v)

    order = list(reversed(range(nchunk))) if reverse else list(range(nchunk))
    st = st_ref[...]
    for g0 in range(0, nchunk, GLA_GROUP):
        grp = order[g0:g0 + GLA_GROUP]
        wcat = jnp.concatenate([st.astype(BF16)] + [kv_s[c].astype(BF16) for c in grp[:-1]], axis=0)
        for i, c in enumerate(grp):
            rows = slice(c * C, (c + 1) * C)
            q = q_ref[0, rows, :].astype(F32) * (GLA_DK ** -0.5)
            expo = b_s[c]
            pieces = []
            for j in range(i - 1, -2, -1):
                pieces.insert(0, (q * jnp.exp(expo)).astype(BF16))
                if j >= 0:
                    expo = expo + b_s[grp[j], last:last + 1, :]
            lhs = jnp.concatenate(pieces, axis=1)
            o_ref[0, rows, :] = oi_s[rows, :] + _dot(lhs, wcat[:(i + 1) * GLA_DK, :])
        for c in grp:
            dec = jnp.transpose(jnp.broadcast_to(jnp.exp(b_s[c, last:last + 1, :]), (GLA_DK, GLA_DK)))
            st = jnp.concatenate([dec] * (GLA_DV // GLA_DK), axis=1) * st + kv_s[c]
    st_ref[...] = st


GLA_STREAM_IN, GLA_STREAM_OUT, GLA_STREAM_SCRATCH = 10, 2, 5


def _gla_kernel(*refs, C, TB, lv_refs):
    n_in, n_out, n_scr = GLA_STREAM_IN, GLA_STREAM_OUT, GLA_STREAM_SCRATCH
    ins = [refs[d * n_in:(d + 1) * n_in] for d in range(2)]
    outs = [refs[2 * n_in + d * n_out:2 * n_in + (d + 1) * n_out] for d in range(2)]
    base = 2 * (n_in + n_out)
    scr = [refs[base + d * n_scr:base + (d + 1) * n_scr] for d in range(2)]
    for part in ("init", "body", "final"):
        for d, reverse in enumerate((False, True)):
            _gla_stream(*ins[d], *outs[d], *scr[d], C=C, TB=TB, reverse=reverse, refs=lv_refs[d], part=part)


def _gla_bidir(pm, ps, up, bias, s0, *, TB):
    B, L, _ = pm.shape
    C = GLA_CHUNK
    nb = L // TB
    assert C >= (2 << GLA_MXU_LEVELS)
    qb, kb, vb = P_Q // GLA_DK, P_K // GLA_DK, P_V // GLA_DV
    in_specs, args, out_specs, out_shape, scratch, lv_refs = [], [], [], [], [], []
    for d, reverse in enumerate((False, True)):
        dmat, masks, signs, refs = _gla_consts(C, reverse)
        lv_refs.append(refs)
        tmap = (lambda i: nb - 1 - i) if reverse else (lambda i: i)
        in_specs += [
            pl.BlockSpec((1, TB, GLA_DK), lambda b, h, i, tmap=tmap: (b, tmap(i), qb + h)),
            pl.BlockSpec((1, TB, GLA_DK), lambda b, h, i, tmap=tmap: (b, tmap(i), kb + h)),
            pl.BlockSpec((1, TB, GLA_DV), lambda b, h, i, tmap=tmap: (b, tmap(i), vb + h)),
            pl.BlockSpec((1, TB, P_SMALL), lambda b, h, i, tmap=tmap: (b, tmap(i), 0)),
            pl.BlockSpec((GLA_RANK, GLA_DK), lambda b, h, i: (0, h)),
            pl.BlockSpec((1, GLA_DK), lambda b, h, i: (0, h)),
            _const_spec(dmat.shape),
            _const_spec(masks.shape),
            _const_spec(signs.shape),
            pl.BlockSpec((1, 1, GLA_DK, GLA_DV), lambda b, h, i: (b, h, 0, 0)),
        ]
        args += [pm, pm, pm, ps, up[d], bias[d], dmat, masks, signs, s0[d]]
        out_specs += [pl.BlockSpec((1, TB, GLA_DV), lambda b, h, i, tmap=tmap: (b, tmap(i), h)),
                      pl.BlockSpec((1, 1, GLA_DK, GLA_DV), lambda b, h, i: (b, h, 0, 0))]
        out_shape += [jax.ShapeDtypeStruct((B, L, GLA_V), F32),
                      jax.ShapeDtypeStruct((B, GLA_HEADS, GLA_DK, GLA_DV), F32)]
        scratch += [pltpu.VMEM((GLA_DK, GLA_DV), F32),
                    pltpu.VMEM((TB // C, masks.shape[0] * C, GLA_DK), F32),
                    pltpu.VMEM((TB // C, C, GLA_DK), F32),
                    pltpu.VMEM((TB, GLA_DV), F32),
                    pltpu.VMEM((TB // C, GLA_DK, GLA_DV), F32)]
    assert len(in_specs) == 2 * GLA_STREAM_IN and len(scratch) == 2 * GLA_STREAM_SCRATCH
    return pl.pallas_call(
        functools.partial(_gla_kernel, C=C, TB=TB, lv_refs=lv_refs),
        grid=(B, GLA_HEADS, nb),
        in_specs=in_specs,
        out_specs=out_specs,
        out_shape=out_shape,
        scratch_shapes=scratch,
        compiler_params=_params(("parallel", "parallel", "arbitrary")),
        name="gla",
    )(*args)


CONV_ROWS = 256


def _conv_shift_mats(T, rowlen):
    t = np.arange(T)
    mats = []
    for j in range(SSM_CONV):
        off = j - CONV_LEFT
        if off == 0:
            continue
        src = t + off
        ok = (src // rowlen == t // rowlen) & (src >= 0) & (src < T)
        m = np.zeros((T, T), np.float32)
        m[t[ok], src[ok]] = 1.0
        mats.append(m)
    return jnp.asarray(np.stack(mats), BF16)


def _conv_silu(u, w, b, sh_ref):
    acc = b + u.astype(F32) * w[CONV_LEFT:CONV_LEFT + 1, :]
    taps = [j for j in range(SSM_CONV) if j != CONV_LEFT]
    for i, j in enumerate(taps):
        acc = acc + _dot(sh_ref[i], u) * w[j:j + 1, :]
    return _silu(acc)


def _ssd_kernel(*refs, C, TB, reverse, finalize, conv, emit, lane0):
    refs = list(refs)
    xs_ref, bm_ref, cm_ref, ps_ref = refs[:4]
    refs = refs[4:]
    if conv:
        wx_ref, wb_ref, wc_ref, bx_ref, bb_ref, bc_ref, sh_ref = refs[:7]
        refs = refs[7:]
    dtb_ref, alog_ref, tri_ref, s0_ref = refs[:4]
    refs = refs[4:]
    if finalize:
        z_ref, yp_ref, dsk_ref, nw_ref = refs[:4]
        refs = refs[4:]
    y_ref, sf_ref = refs[:2]
    refs = refs[2:]
    if emit:
        xo_ref, bo_ref, co_ref = refs[:3]
        refs = refs[3:]
    st_ref, xc_s, bc_s, cc_s, ya_s = refs
    nchunk = TB // C
    blk = pl.program_id(1)
    G, HPG, P, N = SSM_GROUPS, SSM_HPG, SSM_HEADDIM, SSM_STATE

    @pl.when(blk == 0)
    def _():
        st_ref[...] = s0_ref[0]

    if conv:
        tc = sh_ref.shape[1]
        for r0 in range(0, TB, tc):
            rs = slice(r0, r0 + tc)
            xc_s[rs, :] = _conv_silu(xs_ref[0, rs, :], wx_ref[...], bx_ref[...], sh_ref).astype(BF16)
            bc_s[rs, :] = _conv_silu(bm_ref[0, rs, :], wb_ref[...], bb_ref[...], sh_ref).astype(BF16)
            cc_s[rs, :] = _conv_silu(cm_ref[0, rs, :], wc_ref[...], bc_ref[...], sh_ref).astype(BF16)
        if emit:
            xo_ref[0] = xc_s[...]
            bo_ref[0] = bc_s[...]
            co_ref[0] = cc_s[...]
    else:
        xc_s[...] = xs_ref[0]
        bc_s[...] = bm_ref[0]
        cc_s[...] = cm_ref[0]

    neg_a = -jnp.exp(alog_ref[...])
    ti = lax.broadcasted_iota(jnp.int32, (C, C), 0)
    si = lax.broadcasted_iota(jnp.int32, (C, C), 1)
    keep = (si >= ti) if reverse else (si <= ti)
    lo = lax.broadcasted_iota(jnp.int32, (1, 2 * P), 1) < P
    last = 0 if reverse else C - 1
    nheads = G * HPG

    for c in (reversed(range(nchunk)) if reverse else range(nchunk)):
        rows = slice(c * C, (c + 1) * C)
        dt = _softplus(ps_ref[0, rows, :] + dtb_ref[...])
        cum = _dot01(tri_ref[...], dt * neg_a) * LOG2E
        cum_t = jnp.transpose(cum)[lane0:lane0 + nheads, :]
        dt_t = jnp.transpose(dt)[lane0:lane0 + nheads, :]
        w_t = (dt_t * jnp.exp2(cum_t[:, last:last + 1] - cum_t)).astype(BF16)
        cdl_t = cum_t - jnp.log2(dt_t)
        dec_all = jnp.exp2(cum[last:last + 1, :])
        for g in range(G):
            bm_g = bc_s[rows, g * N:(g + 1) * N]
            cm_g = cc_s[rows, g * N:(g + 1) * N]
            cb = _dot_nt(cm_g, bm_g).astype(BF16)
            bm_t = jnp.transpose(bm_g.astype(F32)).astype(BF16)
            for pr in range(HPG // 2):
                lanes = slice(g * SSM_GW + pr * 2 * P, g * SSM_GW + (pr + 1) * 2 * P)
                x_pair = xc_s[rows, lanes]
                s_pair = st_ref[g, :, pr * 2 * P:(pr + 1) * 2 * P]
                s_bf = s_pair.astype(BF16)
                y = None
                ds = None
                decs = []
                for half in range(2):
                    j = g * HPG + pr * 2 + half
                    lane = lane0 + j
                    sel = lo if half == 0 else jnp.logical_not(lo)
                    bc_ = jnp.broadcast_to(cum[:, lane:lane + 1], (C, C))
                    m = cb * jnp.exp2(jnp.where(keep, bc_ - cdl_t[j:j + 1, :], NEG_BIG)).astype(BF16)
                    cd = cm_g * jnp.exp2(bc_).astype(BF16)
                    lhs = jnp.concatenate([m, cd], axis=1)
                    rhs = jnp.concatenate([jnp.where(sel, x_pair, jnp.zeros_like(x_pair)),
                                           jnp.where(sel, s_bf, jnp.zeros_like(s_bf))], axis=0)
                    yh = _dot(lhs, rhs)
                    dh = _dot(bm_t * w_t[j:j + 1, :],
                              jnp.where(sel, x_pair, jnp.zeros_like(x_pair)))
                    y = yh if y is None else y + yh
                    ds = dh if ds is None else ds + dh
                    decs.append(dec_all[:, lane:lane + 1])
                ya_s[rows, lanes] = y
                dec = jnp.where(lo, decs[0], decs[1])
                st_ref[g, :, pr * 2 * P:(pr + 1) * 2 * P] = dec * s_pair + ds

    if finalize:
        for g in range(G):
            lanes = slice(g * SSM_GW, (g + 1) * SSM_GW)
            y = ya_s[:, lanes] + yp_ref[0, :, lanes] + dsk_ref[:, lanes] * xc_s[:, lanes].astype(F32)
            y = y * _silu(z_ref[0, :, lanes].astype(F32))
            y = y * lax.rsqrt(jnp.mean(y * y, axis=-1, keepdims=True) + EPS) * nw_ref[:, lanes]
            y_ref[0, :, lanes] = y.astype(y_ref.dtype)
    else:
        y_ref[0] = ya_s[...]

    @pl.when(blk == pl.num_programs(1) - 1)
    def _():
        sf_ref[0] = st_ref[...]


def _ssd_pass(src, ps, conv_w, conv_b, dt_bias, a_log, s0, *, reverse, finalize, TB, rowlen, emit=False,
              y_prev=None, z_src=None, d_skip_x=None, norm_w=None):
    (xs_a, xs_o), (bm_a, bm_o), (cm_a, cm_o) = src
    B, L, _ = ps.shape
    C = SSD_CHUNK
    nb = L // TB
    G = SSM_GROUPS
    conv = conv_w is not None
    tmap = (lambda i: nb - 1 - i) if reverse else (lambda i: i)
    idx = np.arange(C)
    tri = (idx[None, :] >= idx[:, None]) if reverse else (idx[None, :] <= idx[:, None])
    tri = jnp.asarray(tri.astype(np.float32), BF16)
    in_specs = [
        pl.BlockSpec((1, TB, SSM_INNER), lambda b, i: (b, tmap(i), xs_o)),
        pl.BlockSpec((1, TB, SSM_BC), lambda b, i: (b, tmap(i), bm_o)),
        pl.BlockSpec((1, TB, SSM_BC), lambda b, i: (b, tmap(i), cm_o)),
        pl.BlockSpec((1, TB, P_SMALL), lambda b, i: (b, tmap(i), 0)),
    ]
    args = [xs_a, bm_a, cm_a, ps]
    if conv:
        tc = min(TB, max(rowlen, CONV_ROWS))
        assert TB % tc == 0 and tc % rowlen == 0
        nx = SSM_INNER // SSM_BC
        in_specs += [
            pl.BlockSpec((SSM_CONV, SSM_INNER), lambda b, i: (0, 0)),
            pl.BlockSpec((SSM_CONV, SSM_BC), lambda b, i: (0, nx)),
            pl.BlockSpec((SSM_CONV, SSM_BC), lambda b, i: (0, nx + 1)),
            pl.BlockSpec((1, SSM_INNER), lambda b, i: (0, 0)),
            pl.BlockSpec((1, SSM_BC), lambda b, i: (0, nx)),
            pl.BlockSpec((1, SSM_BC), lambda b, i: (0, nx + 1)),
            _const_spec((SSM_CONV - 1, tc, tc)),
        ]
        args += [conv_w, conv_w, conv_w, conv_b, conv_b, conv_b, _conv_shift_mats(tc, rowlen)]
    in_specs += [_const_spec((1, P_SMALL)), _const_spec((1, P_SMALL)), _const_spec((C, C)),
                 pl.BlockSpec((1, G, SSM_STATE, SSM_GW), lambda b, i: (b, 0, 0, 0))]
    args += [dt_bias, a_log, tri, s0]
    if finalize:
        z_a, z_o = z_src
        in_specs += [
            pl.BlockSpec((1, TB, SSM_INNER), lambda b, i: (b, tmap(i), z_o)),
            pl.BlockSpec((1, TB, SSM_INNER), lambda b, i: (b, tmap(i), 0)),
            _const_spec((1, SSM_INNER)), _const_spec((1, SSM_INNER)),
        ]
        args += [z_a, y_prev, d_skip_x, norm_w]
    out_specs = [pl.BlockSpec((1, TB, SSM_INNER), lambda b, i: (b, tmap(i), 0)),
                 pl.BlockSpec((1, G, SSM_STATE, SSM_GW), lambda b, i: (b, 0, 0, 0))]
    out_shape = [jax.ShapeDtypeStruct((B, L, SSM_INNER), BF16 if finalize else F32),
                 jax.ShapeDtypeStruct((B, G, SSM_STATE, SSM_GW), F32)]
    if emit:
        out_specs += [pl.BlockSpec((1, TB, SSM_INNER), lambda b, i: (b, tmap(i), 0)),
                      pl.BlockSpec((1, TB, SSM_BC), lambda b, i: (b, tmap(i), 0)),
                      pl.BlockSpec((1, TB, SSM_BC), lambda b, i: (b, tmap(i), 0))]
        out_shape += [jax.ShapeDtypeStruct((B, L, SSM_INNER), BF16),
                      jax.ShapeDtypeStruct((B, L, SSM_BC), BF16),
                      jax.ShapeDtypeStruct((B, L, SSM_BC), BF16)]
    kern = functools.partial(_ssd_kernel, C=C, TB=TB, reverse=reverse, finalize=finalize,
                             conv=conv, emit=emit, lane0=S_DTB if reverse else S_DTF)
    return pl.pallas_call(
        kern,
        grid=(B, nb),
        in_specs=in_specs,
        out_specs=out_specs,
        out_shape=out_shape,
        scratch_shapes=[pltpu.VMEM((G, SSM_STATE, SSM_GW), F32),
                        pltpu.VMEM((TB, SSM_INNER), BF16),
                        pltpu.VMEM((TB, SSM_BC), BF16),
                        pltpu.VMEM((TB, SSM_BC), BF16),
                        pltpu.VMEM((TB, SSM_INNER), F32)],
        compiler_params=_params(("parallel", "arbitrary")),
        name="ssd_" + ("rev" if reverse else "fwd") + ("_fin" if finalize else ""),
    )(*args)


def _merge_kernel(of_ref, ob_ref, r_ref, gnw_ref, sb_ref, ga_ref, gb_ref, x_ref, g1_ref, wpa_ref, wpb_ref,
                  wout_ref, h_ref):
    o = of_ref[0] + ob_ref[0]
    heads = []
    for h in range(GLA_HEADS):
        oh = o[:, h * GLA_DV:(h + 1) * GLA_DV]
        heads.append(oh * lax.rsqrt(jnp.mean(oh * oh, axis=-1, keepdims=True) + EPS) * gnw_ref[...])
    oa = (jnp.concatenate(heads, axis=1) * _silu(r_ref[0].astype(F32))).astype(BF16)
    ya = _dot(oa, wpa_ref[...])
    yb = _dot(sb_ref[0], wpb_ref[...])
    m = _sigmoid(ga_ref[0].astype(F32)) * ya + _sigmoid(gb_ref[0].astype(F32)) * yb
    mix = _dot(m.astype(BF16), wout_ref[...])
    h_ref[0] = x_ref[0] + g1_ref[0] * mix


def _merge(o_f, o_b, gla_norm_w, s_b, pm, x, g1, w_pa, w_pb, w_out, tm):
    B, L, _ = x.shape
    gab, gbb, rb = P_GA // D_MODEL, P_GB // D_MODEL, P_R // GLA_V
    return pl.pallas_call(
        _merge_kernel,
        grid=(B, L // tm),
        in_specs=[pl.BlockSpec((1, tm, GLA_V), lambda b, i: (b, i, 0)),
                  pl.BlockSpec((1, tm, GLA_V), lambda b, i: (b, i, 0)),
                  pl.BlockSpec((1, tm, GLA_V), lambda b, i: (b, i, rb)),
                  _const_spec((1, GLA_DV)),
                  pl.BlockSpec((1, tm, SSM_INNER), lambda b, i: (b, i, 0)),
                  pl.BlockSpec((1, tm, D_MODEL), lambda b, i: (b, i, gab)),
                  pl.BlockSpec((1, tm, D_MODEL), lambda b, i: (b, i, gbb)),
                  pl.BlockSpec((1, tm, D_MODEL), lambda b, i: (b, i, 0)),
                  pl.BlockSpec((1, 1, D_MODEL), lambda b, i: (b, 0, 0)),
                  _const_spec(w_pa.shape), _const_spec(w_pb.shape), _const_spec(w_out.shape)],
        out_specs=pl.BlockSpec((1, tm, D_MODEL), lambda b, i: (b, i, 0)),
        out_shape=jax.ShapeDtypeStruct((B, L, D_MODEL), F32),
        compiler_params=_params(("parallel", "parallel")),
        name="merge",
    )(o_f, o_b, pm, gla_norm_w, s_b, pm, pm, x, g1, w_pa, w_pb, w_out)


FFN_CHUNK = 256


def _ffn_kernel(h_ref, mod_ref, n2_ref, fw_ref, wg_ref, wu_ref, wd_ref, o_ref):
    h = h_ref[0]
    hn = h * lax.rsqrt(jnp.mean(h * h, axis=-1, keepdims=True) + EPS) * n2_ref[...]
    hn = (hn * (1.0 + mod_ref[0, 1:2, :]) + mod_ref[0, 0:1, :]).astype(BF16)
    acc = jnp.zeros(h.shape, F32)
    for c in range(D_FF // FFN_CHUNK):
        cols = slice(c * FFN_CHUNK, (c + 1) * FFN_CHUNK)
        gt = _dot(hn, wg_ref[:, cols])
        up = _dot(hn, wu_ref[:, cols])
        acc = acc + _dot((_silu(gt) * up).astype(BF16), wd_ref[cols, :])
    h2 = h + mod_ref[0, 2:3, :] * acc
    o_ref[0] = h2 * lax.rsqrt(jnp.mean(h2 * h2, axis=-1, keepdims=True) + EPS) * fw_ref[...]


def _ffn(h, mod, n2w, fw, w_gate, w_up, w_down, tm):
    B, L, _ = h.shape
    return pl.pallas_call(
        _ffn_kernel,
        grid=(B, L // tm),
        in_specs=[pl.BlockSpec((1, tm, D_MODEL), lambda b, i: (b, i, 0)),
                  pl.BlockSpec((1, 3, D_MODEL), lambda b, i: (b, 0, 0)),
                  _const_spec((1, D_MODEL)), _const_spec((1, D_MODEL)),
                  pl.BlockSpec(w_gate.shape, lambda b, i: (0, 0), pipeline_mode=pl.Buffered(1)),
                  pl.BlockSpec(w_up.shape, lambda b, i: (0, 0), pipeline_mode=pl.Buffered(1)),
                  pl.BlockSpec(w_down.shape, lambda b, i: (0, 0), pipeline_mode=pl.Buffered(1))],
        out_specs=pl.BlockSpec((1, tm, D_MODEL), lambda b, i: (b, i, 0)),
        out_shape=jax.ShapeDtypeStruct((B, L, D_MODEL), F32),
        compiler_params=_params(("parallel", "parallel")),
        name="ffn",
    )(h, mod, n2w, fw, w_gate, w_up, w_down)


def _pick_block(L, pref):
    tb = min(L, pref)
    assert L % tb == 0
    return tb


def kernel(x, c, ctx, c_ctx, w_ada, b_ada, norm1_w, w_in, gla_up_f, gla_bias_f, gla_up_b, gla_bias_b,
           gla_norm_w, conv_w, conv_b, dt_bias_f, dt_bias_b, a_log_f, a_log_b, d_skip, ssm_norm_w,
           w_pa, w_pb, w_out, norm2_w, w_gate, w_up, w_down, final_norm_w):
    B, L, D = x.shape
    Lc = ctx.shape[1]
    depth = w_ada.shape[0]
    assert depth == 1 and D == D_MODEL
    assert L % GRID_W == 0 and L % SSD_CHUNK == 0 and Lc % SSD_CHUNK == 0
    lay = 0

    nrow = -(-(B + 1) // 8) * 8
    cc = jnp.zeros((nrow, D), F32).at[:B].set(c).at[B].set(c_ctx)
    ada = _ada(cc, w_ada, b_ada[lay][None, :], lay)
    sh1, sc1, g1, sh2, sc2, g2 = [ada[:, i * D:(i + 1) * D] for i in range(6)]
    mod1 = jnp.stack([sh1[:B], sc1[:B]], axis=1)
    mod1_c = jnp.broadcast_to(jnp.stack([sh1[B], sc1[B]])[None], (B, 2, D))
    mod2 = jnp.stack([sh2[:B], sc2[:B], g2[:B]], axis=1)
    g1_l = g1[:B, None, :]

    wt = jnp.swapaxes(w_in[lay], 0, 1)
    w_main, w_small = _regroup(wt)
    nw1 = norm1_w[lay][None, :]

    at_lanes = lambda p, off: jnp.zeros((1, P_SMALL), F32).at[0, off:off + SSM_HEADS].set(p[lay])
    dtb_f, al_f = at_lanes(dt_bias_f, S_DTF), at_lanes(a_log_f, S_DTF)
    dtb_b, al_b = at_lanes(dt_bias_b, S_DTB), at_lanes(a_log_b, S_DTB)
    cw, cb_ = conv_w[lay], conv_b[lay][None, :]
    up_f, up_b = gla_up_f[lay], gla_up_b[lay]
    bi_f, bi_b = gla_bias_f[lay][None, :], gla_bias_b[lay][None, :]

    gla_zero = jnp.zeros((B, GLA_HEADS, GLA_DK, GLA_DV), F32)
    ssd_zero = jnp.zeros((B, SSM_GROUPS, SSM_STATE, SSM_GW), F32)

    pm_c, ps_c = _inproj(ctx, mod1_c, nw1, w_main, w_small, _pick_block(Lc, INPROJ_TM))
    _, sg_f, _, sg_b = _gla_bidir(pm_c, ps_c, (up_f, up_b), (bi_f, bi_b), (gla_zero, gla_zero),
                                  TB=_pick_block(Lc, GLA_TB))
    src_c = ((pm_c, P_XS // SSM_INNER), (pm_c, P_BM // SSM_BC), (pm_c, P_CM // SSM_BC))
    _, ss_f = _ssd_pass(src_c, ps_c, cw, cb_, dtb_f, al_f, ssd_zero, reverse=False, finalize=False,
                        TB=Lc, rowlen=Lc)
    _, ss_b = _ssd_pass(src_c, ps_c, cw, cb_, dtb_b, al_b, ssd_zero, reverse=True, finalize=False,
                        TB=Lc, rowlen=Lc)

    pm, ps = _inproj(x, mod1, nw1, w_main, w_small, _pick_block(L, INPROJ_TM))
    tbg = _pick_block(L, GLA_TB)
    og_f, _, og_b, _ = _gla_bidir(pm, ps, (up_f, up_b), (bi_f, bi_b), (sg_f, sg_b), TB=tbg)
    tbs = _pick_block(L, SSD_TB)
    src = ((pm, P_XS // SSM_INNER), (pm, P_BM // SSM_BC), (pm, P_CM // SSM_BC))
    ys_f, _, xc, bc, cc = _ssd_pass(src, ps, cw, cb_, dtb_f, al_f, ss_f, reverse=False, finalize=False,
                                    TB=tbs, rowlen=GRID_W, emit=True)
    o_b, _ = _ssd_pass(((xc, 0), (bc, 0), (cc, 0)), ps, None, None, dtb_b, al_b, ss_b, reverse=True,
                       finalize=True, TB=tbs, rowlen=GRID_W, y_prev=ys_f, z_src=(pm, P_Z // SSM_INNER),
                       d_skip_x=jnp.repeat(d_skip[lay], SSM_HEADDIM)[None, :],
                       norm_w=ssm_norm_w[lay][None, :])

    h = _merge(og_f, og_b, gla_norm_w[lay][None, :], o_b, pm, x, g1_l, w_pa[lay].astype(BF16),
               w_pb[lay].astype(BF16), w_out[lay].astype(BF16), _pick_block(L, MERGE_TM))
    return _ffn(h, mod2, norm2_w[lay][None, :], final_norm_w[None, :], w_gate[lay].astype(BF16),
                w_up[lay].astype(BF16), w_down[lay].astype(BF16), _pick_block(L, FFN_TM))
```

```python
import functools

import numpy as np
import jax
import jax.numpy as jnp
from jax import lax
from jax.experimental import pallas as pl
from jax.experimental.pallas import tpu as pltpu

F32 = jnp.float32
BF16 = jnp.bfloat16

D_MODEL = 1024
GRID_W = 64
EPS = 1e-6

GLA_HEADS = 4
GLA_DK = 128
GLA_DV = 256
GLA_QK = GLA_HEADS * GLA_DK
GLA_V = GLA_HEADS * GLA_DV
GLA_RANK = 16
GLA_TAU = 16.0

SSM_INNER = 2 * D_MODEL
SSM_HEADDIM = 64
SSM_HEADS = SSM_INNER // SSM_HEADDIM
SSM_GROUPS = 4
SSM_HPG = SSM_HEADS // SSM_GROUPS
SSM_STATE = 128
SSM_BC = SSM_GROUPS * SSM_STATE
SSM_CONV = 4
CONV_LEFT = 2
SSM_GW = SSM_HPG * SSM_HEADDIM

D_FF = ((8 * D_MODEL // 3 + 255) // 256) * 256

_IN_WIDTHS = (GLA_QK, GLA_QK, GLA_V, GLA_V, GLA_RANK, GLA_RANK,
              SSM_INNER, SSM_INNER, SSM_BC, SSM_BC, SSM_HEADS, SSM_HEADS, D_MODEL, D_MODEL)
_IN_OFF = np.concatenate([[0], np.cumsum(_IN_WIDTHS)]).tolist()

P_Z, P_XS = 0, 2048
P_Q, P_K, P_V, P_R = 4096, 4608, 5120, 6144
P_BM, P_CM = 7168, 7680
P_GA, P_GB = 8192, 9216
P_MAIN = 10240
S_LRF, S_LRB, S_DTF, S_DTB = 0, 16, 32, 64
P_SMALL = 128

GLA_CHUNK = 128
SSD_CHUNK = 128

ADA_TN = 1536
INPROJ_TM, INPROJ_TN = 1024, 2560
GLA_TB = 2048
SSD_TB = 512
MERGE_TM = 512
FFN_TM = 512

VMEM_LIMIT = 56 * 1024 * 1024
NEG_BIG = -1e30
LOG2E = 1.4426950408889634


def _sigmoid(x):
    return 1.0 / (1.0 + jnp.exp(-x))


def _silu(x):
    return x * _sigmoid(x)


def _softplus(x):
    return jnp.maximum(x, 0.0) + jnp.log(1.0 + jnp.exp(-jnp.abs(x)))


def _split3(x):
    hi = x.astype(BF16)
    r1 = x - hi.astype(F32)
    mid = r1.astype(BF16)
    lo = (r1 - mid.astype(F32)).astype(BF16)
    return hi, mid, lo


def _dot(a, b):
    return jnp.dot(a, b, preferred_element_type=F32)


def _dot_nt(a, b):
    return lax.dot_general(a, b, (((1,), (1,)), ((), ())), preferred_element_type=F32)


def _dot_tn(a, b):
    return lax.dot_general(a, b, (((0,), (0,)), ((), ())), preferred_element_type=F32)


def _dot01(m01, x):
    hi, mid, lo = _split3(x)
    return _dot(m01, hi) + _dot(m01, mid) + _dot(m01, lo)


def _params(sem):
    return pltpu.CompilerParams(dimension_semantics=sem, vmem_limit_bytes=VMEM_LIMIT)


def _const_spec(shape):
    n = len(shape)
    return pl.BlockSpec(shape, lambda *_: (0,) * n)


def _ada_kernel(c_ref, w_ref, b_ref, o_ref):
    s = _silu(c_ref[...])
    w = w_ref[0]
    s_hi = s.astype(BF16)
    s_lo = (s - s_hi.astype(F32)).astype(BF16)
    w_hi = w.astype(BF16)
    w_lo = (w - w_hi.astype(F32)).astype(BF16)
    o_ref[...] = _dot(s_hi, w_hi) + _dot(s_hi, w_lo) + _dot(s_lo, w_hi) + b_ref[...]


def _ada(cc, w, b, lay):
    rows = cc.shape[0]
    n = w.shape[2]
    tn = ADA_TN
    return pl.pallas_call(
        _ada_kernel,
        grid=(n // tn,),
        in_specs=[pl.BlockSpec((rows, D_MODEL), lambda j: (0, 0)),
                  pl.BlockSpec((1, D_MODEL, tn), lambda j: (lay, 0, j)),
                  pl.BlockSpec((1, tn), lambda j: (0, j))],
        out_specs=pl.BlockSpec((rows, tn), lambda j: (0, j)),
        out_shape=jax.ShapeDtypeStruct((rows, n), F32),
        compiler_params=_params(("arbitrary",)),
        name="ada",
    )(cc, w, b)


REGROUP_ROWS = 1024
REGROUP_ALIGN = 16
_MAIN_RUNS = ((P_Z, _IN_OFF[6], 2 * SSM_INNER), (P_Q, _IN_OFF[0], 2 * GLA_QK + 2 * GLA_V),
              (P_BM, _IN_OFF[8], 2 * SSM_BC), (P_GA, _IN_OFF[12], 2 * D_MODEL))


def _regroup_kernel(wt_ref, lr_ref, dt_ref, wm_ref, ws_ref):
    wm_ref[...] = jnp.transpose(wt_ref[...]).astype(BF16)
    pad = jnp.zeros((P_SMALL - lr_ref.shape[0] - dt_ref.shape[0], lr_ref.shape[1]), F32)
    ws_ref[...] = jnp.transpose(jnp.concatenate([lr_ref[...], dt_ref[...], pad], axis=0)).astype(BF16)


def _regroup(wt):
    _, D = wt.shape
    R = REGROUP_ROWS
    U = REGROUP_ALIGN
    assert all(dst % R == 0 and n % R == 0 and src % U == 0 for dst, src, n in _MAIN_RUNS)

    def src_row(i):
        units = jnp.int32(0)
        for dst, src, n in _MAIN_RUNS:
            inside = (i * R >= dst) & (i * R < dst + n)
            units = jnp.where(inside, (src - dst) // U + i * (R // U), units)
        return units * U

    return pl.pallas_call(
        _regroup_kernel,
        grid=(P_MAIN // R,),
        in_specs=[pl.BlockSpec((pl.Element(R), pl.Element(D)), lambda i: (src_row(i), 0)),
                  pl.BlockSpec((pl.Element(2 * GLA_RANK), pl.Element(D)), lambda i: (_IN_OFF[4], 0)),
                  pl.BlockSpec((pl.Element(2 * SSM_HEADS), pl.Element(D)), lambda i: (_IN_OFF[10], 0))],
        out_specs=[pl.BlockSpec((D, R), lambda i: (0, i)),
                   pl.BlockSpec((D, P_SMALL), lambda i: (0, 0))],
        out_shape=[jax.ShapeDtypeStruct((D, P_MAIN), BF16), jax.ShapeDtypeStruct((D, P_SMALL), BF16)],
        compiler_params=_params(("arbitrary",)),
        name="regroup",
    )(wt, wt, wt)


def _inproj_kernel(x_ref, mod_ref, nw_ref, wm_ref, ws_ref, om_ref, os_ref, xn_ref):
    @pl.when(pl.program_id(2) == 0)
    def _():
        x = x_ref[0]
        y = x * lax.rsqrt(jnp.mean(x * x, axis=-1, keepdims=True) + EPS) * nw_ref[...]
        y = y * (1.0 + mod_ref[0, 1:2, :]) + mod_ref[0, 0:1, :]
        xn = y.astype(BF16)
        xn_ref[...] = xn
        os_ref[0] = _dot(xn, ws_ref[...])

    om_ref[0] = _dot(xn_ref[...], wm_ref[...]).astype(BF16)


def _inproj(x, mod, nw, w_main, w_small, tm):
    B, L, _ = x.shape
    tn = INPROJ_TN
    return pl.pallas_call(
        _inproj_kernel,
        grid=(B, L // tm, P_MAIN // tn),
        in_specs=[pl.BlockSpec((1, tm, D_MODEL), lambda b, i, j: (b, i, 0)),
                  pl.BlockSpec((1, 2, D_MODEL), lambda b, i, j: (b, 0, 0)),
                  pl.BlockSpec((1, D_MODEL), lambda b, i, j: (0, 0)),
                  pl.BlockSpec((D_MODEL, tn), lambda b, i, j: (0, j)),
                  pl.BlockSpec((D_MODEL, P_SMALL), lambda b, i, j: (0, 0))],
        out_specs=[pl.BlockSpec((1, tm, tn), lambda b, i, j: (b, i, j)),
                   pl.BlockSpec((1, tm, P_SMALL), lambda b, i, j: (b, i, 0))],
        out_shape=[jax.ShapeDtypeStruct((B, L, P_MAIN), BF16),
                   jax.ShapeDtypeStruct((B, L, P_SMALL), F32)],
        scratch_shapes=[pltpu.VMEM((tm, D_MODEL), BF16)],
        compiler_params=_params(("parallel", "parallel", "arbitrary")),
        name="inproj",
    )(x, mod, nw, w_main, w_small)


GLA_MXU_LEVELS = 3
GLA_GROUP = 4


def _gla_consts(C, reverse):
    NL = int(np.log2(C))
    idx = np.arange(C)
    tri = (idx[None, :] <= idx[:, None]).astype(np.float32)
    mats = [tri]
    masks = [np.eye(C, dtype=np.float32)]
    refs, signs = [], []
    for lev in range(NL):
        h = 1 << lev
        blk = idx // (2 * h)
        half = (idx // h) % 2
        ref = blk * 2 * h + h - 1
        if lev < GLA_MXU_LEVELS:
            d = tri - tri[ref]
            d[half == 0] *= -1.0
            mats.append(d)
        else:
            sg = np.where(half == 1, 1.0, -1.0).astype(np.float32)
            if reverse:
                ref, sg = (C - 1 - ref)[::-1], sg[::-1]
            refs.append([int(ref[m * 2 * h]) for m in range(C // (2 * h))])
            signs.append(np.broadcast_to(sg[:, None], (C, GLA_DK)))
        masks.append(((blk[:, None] == blk[None, :]) & (half[:, None] == 1)
                      & (half[None, :] == 0)).astype(np.float32))
    if reverse:
        mats = [m[::-1, ::-1] for m in mats]
        masks = [m[::-1, ::-1] for m in masks]
    dmat = np.concatenate(mats, axis=0)
    return (jnp.asarray(dmat, BF16), jnp.asarray(np.stack(masks), F32),
            jnp.asarray(np.stack(signs), F32), refs)


def _gla_stream(q_ref, k_ref, v_ref, lr_ref, up_ref, bias_ref, dm_ref, mk_ref, sg_ref, s0_ref,
                o_ref, sf_ref, st_ref, e_s, b_s, oi_s, kv_s, *, C, TB, reverse, refs, part):
    NL = int(np.log2(C))
    nchunk = TB // C
    blk = pl.program_id(2)
    last = 0 if reverse else C - 1
    lr_off = S_LRB if reverse else S_LRF

    if part == "init":
        @pl.when(blk == 0)
        def _():
            st_ref[...] = s0_ref[0, 0]
        return
    if part == "final":
        @pl.when(blk == pl.num_programs(2) - 1)
        def _():
            sf_ref[0, 0] = st_ref[...]
        return

    lr = lr_ref[0][:, lr_off:lr_off + GLA_RANK]
    pre = _dot(lr.astype(BF16), up_ref[...].astype(BF16)) + bias_ref[...]
    g = -_softplus(-pre) * (1.0 / GLA_TAU)
    g_hi = g.astype(BF16)
    g_lo = (g - g_hi.astype(F32)).astype(BF16)
    gs = jnp.concatenate([g_hi, g_lo], axis=1)

    for c in range(nchunk):
        ex = _dot(dm_ref[...], gs[c * C:(c + 1) * C])
        ex = ex[:, :GLA_DK] + ex[:, GLA_DK:]
        b = ex[0:C]
        b_s[c] = b
        e_s[c, 0:GLA_MXU_LEVELS * C, :] = jnp.exp(ex[C:(1 + GLA_MXU_LEVELS) * C])
        for li, lev in enumerate(range(GLA_MXU_LEVELS, NL)):
            h2 = 2 << lev
            bref = jnp.concatenate([jnp.broadcast_to(b[r:r + 1, :], (h2, GLA_DK)) for r in refs[li]], axis=0)
            e_s[c, lev * C:(lev + 1) * C, :] = jnp.exp(sg_ref[li] * (b - bref))
        e_s[c, NL * C:(NL + 1) * C, :] = jnp.exp(b[last:last + 1, :] - b)

    for c in range(nchunk):
        rows = slice(c * C, (c + 1) * C)
        q = q_ref[0, rows, :].astype(F32) * (GLA_DK ** -0.5)
        k = k_ref[0, rows, :].astype(F32)
        v = v_ref[0, rows, :]
        att = mk_ref[0] * _dot_nt(q.astype(BF16), k.astype(BF16))
        for lev in range(NL):
            e_l = e_s[c, lev * C:(lev + 1) * C, :]
            att = att + mk_ref[lev + 1] * _dot_nt((q * e_l).astype(BF16), (k * e_l).astype(BF16))
        oi_s[rows, :] = _dot(att.astype(BF16), v)
        kv_s[c] = _dot_tn((k * e_s[c, NL * C:(NL + 1) * C, :]).astype(BF16), v)

    order = list(reversed(range(nchunk))) if reverse else list(range(nchunk))
    st = st_ref[...]
    for g0 in range(0, nchunk, GLA_GROUP):
        grp = order[g0:g0 + GLA_GROUP]
        wcat = jnp.concatenate([st.astype(BF16)] + [kv_s[c].astype(BF16) for c in grp[:-1]], axis=0)
        for i, c in enumerate(grp):
            rows = slice(c * C, (c + 1) * C)
            q = q_ref[0, rows, :].astype(F32) * (GLA_DK ** -0.5)
            expo = b_s[c]
            pieces = []
            for j in range(i - 1, -2, -1):
                pieces.insert(0, (q * jnp.exp(expo)).astype(BF16))
                if j >= 0:
                    expo = expo + b_s[grp[j], last:last + 1, :]
            lhs = jnp.concatenate(pieces, axis=1)
            o_ref[0, rows, :] = oi_s[rows, :] + _dot(lhs, wcat[:(i + 1) * GLA_DK, :])
        for c in grp:
            dec = jnp.transpose(jnp.broadcast_to(jnp.exp(b_s[c, last:last + 1, :]), (GLA_DK, GLA_DK)))
            st = jnp.concatenate([dec] * (GLA_DV // GLA_DK), axis=1) * st + kv_s[c]
    st_ref[...] = st


GLA_STREAM_IN, GLA_STREAM_OUT, GLA_STREAM_SCRATCH = 10, 2, 5


def _gla_kernel(*refs, C, TB, lv_refs):
    n_in, n_out, n_scr = GLA_STREAM_IN, GLA_STREAM_OUT, GLA_STREAM_SCRATCH
    ins = [refs[d * n_in:(d + 1) * n_in] for d in range(2)]
    outs = [refs[2 * n_in + d * n_out:2 * n_in + (d + 1) * n_out] for d in range(2)]
    base = 2 * (n_in + n_out)
    scr = [refs[base + d * n_scr:base + (d + 1) * n_scr] for d in range(2)]
    for part in ("init", "body", "final"):
        for d, reverse in enumerate((False, True)):
            _gla_stream(*ins[d], *outs[d], *scr[d], C=C, TB=TB, reverse=reverse, refs=lv_refs[d], part=part)


def _gla_bidir(pm, ps, up, bias, s0, *, TB):
    B, L, _ = pm.shape
    C = GLA_CHUNK
    nb = L // TB
    assert C >= (2 << GLA_MXU_LEVELS)
    qb, kb, vb = P_Q // GLA_DK, P_K // GLA_DK, P_V // GLA_DV
    in_specs, args, out_specs, out_shape, scratch, lv_refs = [], [], [], [], [], []
    for d, reverse in enumerate((False, True)):
        dmat, masks, signs, refs = _gla_consts(C, reverse)
        lv_refs.append(refs)
        tmap = (lambda i: nb - 1 - i) if reverse else (lambda i: i)
        in_specs += [
            pl.BlockSpec((1, TB, GLA_DK), lambda b, h, i, tmap=tmap: (b, tmap(i), qb + h)),
            pl.BlockSpec((1, TB, GLA_DK), lambda b, h, i, tmap=tmap: (b, tmap(i), kb + h)),
            pl.BlockSpec((1, TB, GLA_DV), lambda b, h, i, tmap=tmap: (b, tmap(i), vb + h)),
            pl.BlockSpec((1, TB, P_SMALL), lambda b, h, i, tmap=tmap: (b, tmap(i), 0)),
            pl.BlockSpec((GLA_RANK, GLA_DK), lambda b, h, i: (0, h)),
            pl.BlockSpec((1, GLA_DK), lambda b, h, i: (0, h)),
            _const_spec(dmat.shape),
            _const_spec(masks.shape),
            _const_spec(signs.shape),
            pl.BlockSpec((1, 1, GLA_DK, GLA_DV), lambda b, h, i: (b, h, 0, 0)),
        ]
        args += [pm, pm, pm, ps, up[d], bias[d], dmat, masks, signs, s0[d]]
        out_specs += [pl.BlockSpec((1, TB, GLA_DV), lambda b, h, i, tmap=tmap: (b, tmap(i), h)),
                      pl.BlockSpec((1, 1, GLA_DK, GLA_DV), lambda b, h, i: (b, h, 0, 0))]
        out_shape += [jax.ShapeDtypeStruct((B, L, GLA_V), F32),
                      jax.ShapeDtypeStruct((B, GLA_HEADS, GLA_DK, GLA_DV), F32)]
        scratch += [pltpu.VMEM((GLA_DK, GLA_DV), F32),
                    pltpu.VMEM((TB // C, masks.shape[0] * C, GLA_DK), F32),
                    pltpu.VMEM((TB // C, C, GLA_DK), F32),
                    pltpu.VMEM((TB, GLA_DV), F32),
                    pltpu.VMEM((TB // C, GLA_DK, GLA_DV), F32)]
    assert len(in_specs) == 2 * GLA_STREAM_IN and len(scratch) == 2 * GLA_STREAM_SCRATCH
    return pl.pallas_call(
        functools.partial(_gla_kernel, C=C, TB=TB, lv_refs=lv_refs),
        grid=(B, GLA_HEADS, nb),
        in_specs=in_specs,
        out_specs=out_specs,
        out_shape=out_shape,
        scratch_shapes=scratch,
        compiler_params=_params(("parallel", "parallel", "arbitrary")),
        name="gla",
    )(*args)


CONV_ROWS = 256


def _conv_shift_mats(T, rowlen):
    t = np.arange(T)
    mats = []
    for j in range(SSM_CONV):
        off = j - CONV_LEFT
        if off == 0:
            continue
        src = t + off
        ok = (src // rowlen == t // rowlen) & (src >= 0) & (src < T)
        m = np.zeros((T, T), np.float32)
        m[t[ok], src[ok]] = 1.0
        mats.append(m)
    return jnp.asarray(np.stack(mats), BF16)


def _conv_silu(u, w, b, sh_ref):
    acc = b + u.astype(F32) * w[CONV_LEFT:CONV_LEFT + 1, :]
    taps = [j for j in range(SSM_CONV) if j != CONV_LEFT]
    for i, j in enumerate(taps):
        acc = acc + _dot(sh_ref[i], u) * w[j:j + 1, :]
    return _silu(acc)


def _ssd_kernel(*refs, C, TB, reverse, finalize, conv, emit, lane0):
    refs = list(refs)
    xs_ref, bm_ref, cm_ref, ps_ref = refs[:4]
    refs = refs[4:]
    if conv:
        wx_ref, wb_ref, wc_ref, bx_ref, bb_ref, bc_ref, sh_ref = refs[:7]
        refs = refs[7:]
    dtb_ref, alog_ref, tri_ref, s0_ref = refs[:4]
    refs = refs[4:]
    if finalize:
        z_ref, yp_ref, dsk_ref, nw_ref = refs[:4]
        refs = refs[4:]
    y_ref, sf_ref = refs[:2]
    refs = refs[2:]
    if emit:
        xo_ref, bo_ref, co_ref = refs[:3]
        refs = refs[3:]
    st_ref, xc_s, bc_s, cc_s, ya_s = refs
    nchunk = TB // C
    blk = pl.program_id(1)
    G, HPG, P, N = SSM_GROUPS, SSM_HPG, SSM_HEADDIM, SSM_STATE

    @pl.when(blk == 0)
    def _():
        st_ref[...] = s0_ref[0]

    if conv:
        tc = sh_ref.shape[1]
        for r0 in range(0, TB, tc):
            rs = slice(r0, r0 + tc)
            xc_s[rs, :] = _conv_silu(xs_ref[0, rs, :], wx_ref[...], bx_ref[...], sh_ref).astype(BF16)
            bc_s[rs, :] = _conv_silu(bm_ref[0, rs, :], wb_ref[...], bb_ref[...], sh_ref).astype(BF16)
            cc_s[rs, :] = _conv_silu(cm_ref[0, rs, :], wc_ref[...], bc_ref[...], sh_ref).astype(BF16)
        if emit:
            xo_ref[0] = xc_s[...]
            bo_ref[0] = bc_s[...]
            co_ref[0] = cc_s[...]
    else:
        xc_s[...] = xs_ref[0]
        bc_s[...] = bm_ref[0]
        cc_s[...] = cm_ref[0]

    neg_a = -jnp.exp(alog_ref[...])
    ti = lax.broadcasted_iota(jnp.int32, (C, C), 0)
    si = lax.broadcasted_iota(jnp.int32, (C, C), 1)
    keep = (si >= ti) if reverse else (si <= ti)
    lo = lax.broadcasted_iota(jnp.int32, (1, 2 * P), 1) < P
    last = 0 if reverse else C - 1
    nheads = G * HPG

    for c in (reversed(range(nchunk)) if reverse else range(nchunk)):
        rows = slice(c * C, (c + 1) * C)
        dt = _softplus(ps_ref[0, rows, :] + dtb_ref[...])
        cum = _dot01(tri_ref[...], dt * neg_a) * LOG2E
        cum_t = jnp.transpose(cum)[lane0:lane0 + nheads, :]
        dt_t = jnp.transpose(dt)[lane0:lane0 + nheads, :]
        w_t = (dt_t * jnp.exp2(cum_t[:, last:last + 1] - cum_t)).astype(BF16)
        cdl_t = cum_t - jnp.log2(dt_t)
        dec_all = jnp.exp2(cum[last:last + 1, :])
        for g in range(G):
            bm_g = bc_s[rows, g * N:(g + 1) * N]
            cm_g = cc_s[rows, g * N:(g + 1) * N]
            cb = _dot_nt(cm_g, bm_g).astype(BF16)
            bm_t = jnp.transpose(bm_g.astype(F32)).astype(BF16)
            for pr in range(HPG // 2):
                lanes = slice(g * SSM_GW + pr * 2 * P, g * SSM_GW + (pr + 1) * 2 * P)
                x_pair = xc_s[rows, lanes]
                s_pair = st_ref[g, :, pr * 2 * P:(pr + 1) * 2 * P]
                s_bf = s_pair.astype(BF16)
                y = None
                ds = None
                decs = []
                for half in range(2):
                    j = g * HPG + pr * 2 + half
                    lane = lane0 + j
                    sel = lo if half == 0 else jnp.logical_not(lo)
                    bc_ = jnp.broadcast_to(cum[:, lane:lane + 1], (C, C))
                    m = cb * jnp.exp2(jnp.where(keep, bc_ - cdl_t[j:j + 1, :], NEG_BIG)).astype(BF16)
                    cd = cm_g * jnp.exp2(bc_).astype(BF16)
                    lhs = jnp.concatenate([m, cd], axis=1)
                    rhs = jnp.concatenate([jnp.where(sel, x_pair, jnp.zeros_like(x_pair)),
                                           jnp.where(sel, s_bf, jnp.zeros_like(s_bf))], axis=0)
                    yh = _dot(lhs, rhs)
                    dh = _dot(bm_t * w_t[j:j + 1, :],
                              jnp.where(sel, x_pair, jnp.zeros_like(x_pair)))
                    y = yh if y is None else y + yh
                    ds = dh if ds is None else ds + dh
                    decs.append(dec_all[:, lane:lane + 1])
                ya_s[rows, lanes] = y
                dec = jnp.where(lo, decs[0], decs[1])
                st_ref[g, :, pr * 2 * P:(pr + 1) * 2 * P] = dec * s_pair + ds

    if finalize:
        for g in range(G):
            lanes = slice(g * SSM_GW, (g + 1) * SSM_GW)
            y = ya_s[:, lanes] + yp_ref[0, :, lanes] + dsk_ref[:, lanes] * xc_s[:, lanes].astype(F32)
            y = y * _silu(z_ref[0, :, lanes].astype(F32))
            y = y * lax.rsqrt(jnp.mean(y * y, axis=-1, keepdims=True) + EPS) * nw_ref[:, lanes]
            y_ref[0, :, lanes] = y.astype(y_ref.dtype)
    else:
        y_ref[0] = ya_s[...]

    @pl.when(blk == pl.num_programs(1) - 1)
    def _():
        sf_ref[0] = st_ref[...]


def _ssd_pass(src, ps, conv_w, conv_b, dt_bias, a_log, s0, *, reverse, finalize, TB, rowlen, emit=False,
              y_prev=None, z_src=None, d_skip_x=None, norm_w=None):
    (xs_a, xs_o), (bm_a, bm_o), (cm_a, cm_o) = src
    B, L, _ = ps.shape
    C = SSD_CHUNK
    nb = L // TB
    G = SSM_GROUPS
    conv = conv_w is not None
    tmap = (lambda i: nb - 1 - i) if reverse else (lambda i: i)
    idx = np.arange(C)
    tri = (idx[None, :] >= idx[:, None]) if reverse else (idx[None, :] <= idx[:, None])
    tri = jnp.asarray(tri.astype(np.float32), BF16)
    in_specs = [
        pl.BlockSpec((1, TB, SSM_INNER), lambda b, i: (b, tmap(i), xs_o)),
        pl.BlockSpec((1, TB, SSM_BC), lambda b, i: (b, tmap(i), bm_o)),
        pl.BlockSpec((1, TB, SSM_BC), lambda b, i: (b, tmap(i), cm_o)),
        pl.BlockSpec((1, TB, P_SMALL), lambda b, i: (b, tmap(i), 0)),
    ]
    args = [xs_a, bm_a, cm_a, ps]
    if conv:
        tc = min(TB, max(rowlen, CONV_ROWS))
        assert TB % tc == 0 and tc % rowlen == 0
        nx = SSM_INNER // SSM_BC
        in_specs += [
            pl.BlockSpec((SSM_CONV, SSM_INNER), lambda b, i: (0, 0)),
            pl.BlockSpec((SSM_CONV, SSM_BC), lambda b, i: (0, nx)),
            pl.BlockSpec((SSM_CONV, SSM_BC), lambda b, i: (0, nx + 1)),
            pl.BlockSpec((1, SSM_INNER), lambda b, i: (0, 0)),
            pl.BlockSpec((1, SSM_BC), lambda b, i: (0, nx)),
            pl.BlockSpec((1, SSM_BC), lambda b, i: (0, nx + 1)),
            _const_spec((SSM_CONV - 1, tc, tc)),
        ]
        args += [conv_w, conv_w, conv_w, conv_b, conv_b, conv_b, _conv_shift_mats(tc, rowlen)]
    in_specs += [_const_spec((1, P_SMALL)), _const_spec((1, P_SMALL)), _const_spec((C, C)),
                 pl.BlockSpec((1, G, SSM_STATE, SSM_GW), lambda b, i: (b, 0, 0, 0))]
    args += [dt_bias, a_log, tri, s0]
    if finalize:
        z_a, z_o = z_src
        in_specs += [
            pl.BlockSpec((1, TB, SSM_INNER), lambda b, i: (b, tmap(i), z_o)),
            pl.BlockSpec((1, TB, SSM_INNER), lambda b, i: (b, tmap(i), 0)),
            _const_spec((1, SSM_INNER)), _const_spec((1, SSM_INNER)),
        ]
        args += [z_a, y_prev, d_skip_x, norm_w]
    out_specs = [pl.BlockSpec((1, TB, SSM_INNER), lambda b, i: (b, tmap(i), 0)),
                 pl.BlockSpec((1, G, SSM_STATE, SSM_GW), lambda b, i: (b, 0, 0, 0))]
    out_shape = [jax.ShapeDtypeStruct((B, L, SSM_INNER), BF16 if finalize else F32),
                 jax.ShapeDtypeStruct((B, G, SSM_STATE, SSM_GW), F32)]
    if emit:
        out_specs += [pl.BlockSpec((1, TB, SSM_INNER), lambda b, i: (b, tmap(i), 0)),
                      pl.BlockSpec((1, TB, SSM_BC), lambda b, i: (b, tmap(i), 0)),
                      pl.BlockSpec((1, TB, SSM_BC), lambda b, i: (b, tmap(i), 0))]
        out_shape += [jax.ShapeDtypeStruct((B, L, SSM_INNER), BF16),
                      jax.ShapeDtypeStruct((B, L, SSM_BC), BF16),
                      jax.ShapeDtypeStruct((B, L, SSM_BC), BF16)]
    kern = functools.partial(_ssd_kernel, C=C, TB=TB, reverse=reverse, finalize=finalize,
                             conv=conv, emit=emit, lane0=S_DTB if reverse else S_DTF)
    return pl.pallas_call(
        kern,
        grid=(B, nb),
        in_specs=in_specs,
        out_specs=out_specs,
        out_shape=out_shape,
        scratch_shapes=[pltpu.VMEM((G, SSM_STATE, SSM_GW), F32),
                        pltpu.VMEM((TB, SSM_INNER), BF16),
                        pltpu.VMEM((TB, SSM_BC), BF16),
                        pltpu.VMEM((TB, SSM_BC), BF16),
                        pltpu.VMEM((TB, SSM_INNER), F32)],
        compiler_params=_params(("parallel", "arbitrary")),
        name="ssd_" + ("rev" if reverse else "fwd") + ("_fin" if finalize else ""),
    )(*args)


def _merge_kernel(of_ref, ob_ref, r_ref, gnw_ref, sb_ref, ga_ref, gb_ref, x_ref, g1_ref, wpa_ref, wpb_ref,
                  wout_ref, h_ref):
    o = of_ref[0] + ob_ref[0]
    heads = []
    for h in range(GLA_HEADS):
        oh = o[:, h * GLA_DV:(h + 1) * GLA_DV]
        heads.append(oh * lax.rsqrt(jnp.mean(oh * oh, axis=-1, keepdims=True) + EPS) * gnw_ref[...])
    oa = (jnp.concatenate(heads, axis=1) * _silu(r_ref[0].astype(F32))).astype(BF16)
    ya = _dot(oa, wpa_ref[...])
    yb = _dot(sb_ref[0], wpb_ref[...])
    m = _sigmoid(ga_ref[0].astype(F32)) * ya + _sigmoid(gb_ref[0].astype(F32)) * yb
    mix = _dot(m.astype(BF16), wout_ref[...])
    h_ref[0] = x_ref[0] + g1_ref[0] * mix


def _merge(o_f, o_b, gla_norm_w, s_b, pm, x, g1, w_pa, w_pb, w_out, tm):
    B, L, _ = x.shape
    gab, gbb, rb = P_GA // D_MODEL, P_GB // D_MODEL, P_R // GLA_V
    return pl.pallas_call(
        _merge_kernel,
        grid=(B, L // tm),
        in_specs=[pl.BlockSpec((1, tm, GLA_V), lambda b, i: (b, i, 0)),
                  pl.BlockSpec((1, tm, GLA_V), lambda b, i: (b, i, 0)),
                  pl.BlockSpec((1, tm, GLA_V), lambda b, i: (b, i, rb)),
                  _const_spec((1, GLA_DV)),
                  pl.BlockSpec((1, tm, SSM_INNER), lambda b, i: (b, i, 0)),
                  pl.BlockSpec((1, tm, D_MODEL), lambda b, i: (b, i, gab)),
                  pl.BlockSpec((1, tm, D_MODEL), lambda b, i: (b, i, gbb)),
                  pl.BlockSpec((1, tm, D_MODEL), lambda b, i: (b, i, 0)),
                  pl.BlockSpec((1, 1, D_MODEL), lambda b, i: (b, 0, 0)),
                  _const_spec(w_pa.shape), _const_spec(w_pb.shape), _const_spec(w_out.shape)],
        out_specs=pl.BlockSpec((1, tm, D_MODEL), lambda b, i: (b, i, 0)),
        out_shape=jax.ShapeDtypeStruct((B, L, D_MODEL), F32),
        compiler_params=_params(("parallel", "parallel")),
        name="merge",
    )(o_f, o_b, pm, gla_norm_w, s_b, pm, pm, x, g1, w_pa, w_pb, w_out)


FFN_CHUNK = 256


def _ffn_kernel(h_ref, mod_ref, n2_ref, fw_ref, wg_ref, wu_ref, wd_ref, o_ref):
    h = h_ref[0]
    hn = h * lax.rsqrt(jnp.mean(h * h, axis=-1, keepdims=True) + EPS) * n2_ref[...]
    hn = (hn * (1.0 + mod_ref[0, 1:2, :]) + mod_ref[0, 0:1, :]).astype(BF16)
    acc = jnp.zeros(h.shape, F32)
    for c in range(D_FF // FFN_CHUNK):
        cols = slice(c * FFN_CHUNK, (c + 1) * FFN_CHUNK)
        gt = _dot(hn, wg_ref[:, cols])
        up = _dot(hn, wu_ref[:, cols])
        acc = acc + _dot((_silu(gt) * up).astype(BF16), wd_ref[cols, :])
    h2 = h + mod_ref[0, 2:3, :] * acc
    o_ref[0] = h2 * lax.rsqrt(jnp.mean(h2 * h2, axis=-1, keepdims=True) + EPS) * fw_ref[...]


def _ffn(h, mod, n2w, fw, w_gate, w_up, w_down, tm):
    B, L, _ = h.shape
    return pl.pallas_call(
        _ffn_kernel,
        grid=(B, L // tm),
        in_specs=[pl.BlockSpec((1, tm, D_MODEL), lambda b, i: (b, i, 0)),
                  pl.BlockSpec((1, 3, D_MODEL), lambda b, i: (b, 0, 0)),
                  _const_spec((1, D_MODEL)), _const_spec((1, D_MODEL)),
                  pl.BlockSpec(w_gate.shape, lambda b, i: (0, 0), pipeline_mode=pl.Buffered(1)),
                  pl.BlockSpec(w_up.shape, lambda b, i: (0, 0), pipeline_mode=pl.Buffered(1)),
                  pl.BlockSpec(w_down.shape, lambda b, i: (0, 0), pipeline_mode=pl.Buffered(1))],
        out_specs=pl.BlockSpec((1, tm, D_MODEL), lambda b, i: (b, i, 0)),
        out_shape=jax.ShapeDtypeStruct((B, L, D_MODEL), F32),
        compiler_params=_params(("parallel", "parallel")),
        name="ffn",
    )(h, mod, n2w, fw, w_gate, w_up, w_down)


def _pick_block(L, pref):
    tb = min(L, pref)
    assert L % tb == 0
    return tb


def kernel(x, c, ctx, c_ctx, w_ada, b_ada, norm1_w, w_in, gla_up_f, gla_bias_f, gla_up_b, gla_bias_b,
           gla_norm_w, conv_w, conv_b, dt_bias_f, dt_bias_b, a_log_f, a_log_b, d_skip, ssm_norm_w,
           w_pa, w_pb, w_out, norm2_w, w_gate, w_up, w_down, final_norm_w):
    B, L, D = x.shape
    Lc = ctx.shape[1]
    depth = w_ada.shape[0]
    assert depth == 1 and D == D_MODEL
    assert L % GRID_W == 0 and L % SSD_CHUNK == 0 and Lc % SSD_CHUNK == 0
    lay = 0

    nrow = -(-(B + 1) // 8) * 8
    cc = jnp.zeros((nrow, D), F32).at[:B].set(c).at[B].set(c_ctx)
    ada = _ada(cc, w_ada, b_ada[lay][None, :], lay)
    sh1, sc1, g1, sh2, sc2, g2 = [ada[:, i * D:(i + 1) * D] for i in range(6)]
    mod1 = jnp.stack([sh1[:B], sc1[:B]], axis=1)
    mod1_c = jnp.broadcast_to(jnp.stack([sh1[B], sc1[B]])[None], (B, 2, D))
    mod2 = jnp.stack([sh2[:B], sc2[:B], g2[:B]], axis=1)
    g1_l = g1[:B, None, :]

    wt = jnp.swapaxes(w_in[lay], 0, 1)
    w_main, w_small = _regroup(wt)
    nw1 = norm1_w[lay][None, :]

    at_lanes = lambda p, off: jnp.zeros((1, P_SMALL), F32).at[0, off:off + SSM_HEADS].set(p[lay])
    dtb_f, al_f = at_lanes(dt_bias_f, S_DTF), at_lanes(a_log_f, S_DTF)
    dtb_b, al_b = at_lanes(dt_bias_b, S_DTB), at_lanes(a_log_b, S_DTB)
    cw, cb_ = conv_w[lay], conv_b[lay][None, :]
    up_f, up_b = gla_up_f[lay], gla_up_b[lay]
    bi_f, bi_b = gla_bias_f[lay][None, :], gla_bias_b[lay][None, :]

    gla_zero = jnp.zeros((B, GLA_HEADS, GLA_DK, GLA_DV), F32)
    ssd_zero = jnp.zeros((B, SSM_GROUPS, SSM_STATE, SSM_GW), F32)

    pm_c, ps_c = _inproj(ctx, mod1_c, nw1, w_main, w_small, _pick_block(Lc, INPROJ_TM))
    _, sg_f, _, sg_b = _gla_bidir(pm_c, ps_c, (up_f, up_b), (bi_f, bi_b), (gla_zero, gla_zero),
                                  TB=_pick_block(Lc, GLA_TB))
    src_c = ((pm_c, P_XS // SSM_INNER), (pm_c, P_BM // SSM_BC), (pm_c, P_CM // SSM_BC))
    _, ss_f = _ssd_pass(src_c, ps_c, cw, cb_, dtb_f, al_f, ssd_zero, reverse=False, finalize=False,
                        TB=Lc, rowlen=Lc)
    _, ss_b = _ssd_pass(src_c, ps_c, cw, cb_, dtb_b, al_b, ssd_zero, reverse=True, finalize=False,
                        TB=Lc, rowlen=Lc)

    pm, ps = _inproj(x, mod1, nw1, w_main, w_small, _pick_block(L, INPROJ_TM))
    tbg = _pick_block(L, GLA_TB)
    og_f, _, og_b, _ = _gla_bidir(pm, ps, (up_f, up_b), (bi_f, bi_b), (sg_f, sg_b), TB=tbg)
    tbs = _pick_block(L, SSD_TB)
    src = ((pm, P_XS // SSM_INNER), (pm, P_BM // SSM_BC), (pm, P_CM // SSM_BC))
    ys_f, _, xc, bc, cc = _ssd_pass(src, ps, cw, cb_, dtb_f, al_f, ss_f, reverse=False, finalize=False,
                                    TB=tbs, rowlen=GRID_W, emit=True)
    o_b, _ = _ssd_pass(((xc, 0), (bc, 0), (cc, 0)), ps, None, None, dtb_b, al_b, ss_b, reverse=True,
                       finalize=True, TB=tbs, rowlen=GRID_W, y_prev=ys_f, z_src=(pm, P_Z // SSM_INNER),
                       d_skip_x=jnp.repeat(d_skip[lay], SSM_HEADDIM)[None, :],
                       norm_w=ssm_norm_w[lay][None, :])

    h = _merge(og_f, og_b, gla_norm_w[lay][None, :], o_b, pm, x, g1_l, w_pa[lay].astype(BF16),
               w_pb[lay].astype(BF16), w_out[lay].astype(BF16), _pick_block(L, MERGE_TM))
    return _ffn(h, mod2, norm2_w[lay][None, :], final_norm_w[None, :], w_gate[lay].astype(BF16),
                w_up[lay].astype(BF16), w_down[lay].astype(BF16), _pick_block(L, FFN_TM))
```

```python
import functools

import numpy as np
import jax
import jax.numpy as jnp
from jax import lax
from jax.experimental import pallas as pl
from jax.experimental.pallas import tpu as pltpu

F32 = jnp.float32
BF16 = jnp.bfloat16

D_MODEL = 1024
GRID_W = 64
EPS = 1e-6

GLA_HEADS = 4
GLA_DK = 128
GLA_DV = 256
GLA_QK = GLA_HEADS * GLA_DK
GLA_V = GLA_HEADS * GLA_DV
GLA_RANK = 16
GLA_TAU = 16.0

SSM_INNER = 2 * D_MODEL
SSM_HEADDIM = 64
SSM_HEADS = SSM_INNER // SSM_HEADDIM
SSM_GROUPS = 4
SSM_HPG = SSM_HEADS // SSM_GROUPS
SSM_STATE = 128
SSM_BC = SSM_GROUPS * SSM_STATE
SSM_CONV = 4
CONV_LEFT = 2
SSM_GW = SSM_HPG * SSM_HEADDIM

D_FF = ((8 * D_MODEL // 3 + 255) // 256) * 256

_IN_WIDTHS = (GLA_QK, GLA_QK, GLA_V, GLA_V, GLA_RANK, GLA_RANK,
              SSM_INNER, SSM_INNER, SSM_BC, SSM_BC, SSM_HEADS, SSM_HEADS, D_MODEL, D_MODEL)
_IN_OFF = np.concatenate([[0], np.cumsum(_IN_WIDTHS)]).tolist()

P_Z, P_XS = 0, 2048
P_Q, P_K, P_V, P_R = 4096, 4608, 5120, 6144
P_BM, P_CM = 7168, 7680
P_GA, P_GB = 8192, 9216
P_MAIN = 10240
S_LRF, S_LRB, S_DTF, S_DTB = 0, 16, 32, 64
P_SMALL = 128

GLA_CHUNK = 128
SSD_CHUNK = 128

ADA_TN = 1536
INPROJ_TM, INPROJ_TN = 1024, 2560
GLA_TB = 2048
SSD_TB = 512
MERGE_TM = 512
FFN_TM = 512

VMEM_LIMIT = 56 * 1024 * 1024
NEG_BIG = -1e30
LOG2E = 1.4426950408889634


def _sigmoid(x):
    return 1.0 / (1.0 + jnp.exp(-x))


def _silu(x):
    return x * _sigmoid(x)


def _softplus(x):
    return jnp.maximum(x, 0.0) + jnp.log(1.0 + jnp.exp(-jnp.abs(x)))


def _split3(x):
    hi = x.astype(BF16)
    r1 = x - hi.astype(F32)
    mid = r1.astype(BF16)
    lo = (r1 - mid.astype(F32)).astype(BF16)
    return hi, mid, lo


def _dot(a, b):
    return jnp.dot(a, b, preferred_element_type=F32)


def _dot_nt(a, b):
    return lax.dot_general(a, b, (((1,), (1,)), ((), ())), preferred_element_type=F32)


def _dot_tn(a, b):
    return lax.dot_general(a, b, (((0,), (0,)), ((), ())), preferred_element_type=F32)


def _dot01(m01, x):
    hi, mid, lo = _split3(x)
    return _dot(m01, hi) + _dot(m01, mid) + _dot(m01, lo)


def _params(sem):
    return pltpu.CompilerParams(dimension_semantics=sem, vmem_limit_bytes=VMEM_LIMIT)


def _const_spec(shape):
    n = len(shape)
    return pl.BlockSpec(shape, lambda *_: (0,) * n)


def _ada_kernel(c_ref, w_ref, b_ref, o_ref):
    s = _silu(c_ref[...])
    w = w_ref[0]
    s_hi = s.astype(BF16)
    s_lo = (s - s_hi.astype(F32)).astype(BF16)
    w_hi = w.astype(BF16)
    w_lo = (w - w_hi.astype(F32)).astype(BF16)
    o_ref[...] = _dot(s_hi, w_hi) + _dot(s_hi, w_lo) + _dot(s_lo, w_hi) + b_ref[...]


def _ada(cc, w, b, lay):
    rows = cc.shape[0]
    n = w.shape[2]
    tn = ADA_TN
    return pl.pallas_call(
        _ada_kernel,
        grid=(n // tn,),
        in_specs=[pl.BlockSpec((rows, D_MODEL), lambda j: (0, 0)),
                  pl.BlockSpec((1, D_MODEL, tn), lambda j: (lay, 0, j)),
                  pl.BlockSpec((1, tn), lambda j: (0, j))],
        out_specs=pl.BlockSpec((rows, tn), lambda j: (0, j)),
        out_shape=jax.ShapeDtypeStruct((rows, n), F32),
        compiler_params=_params(("arbitrary",)),
        name="ada",
    )(cc, w, b)


REGROUP_ROWS = 1024
REGROUP_ALIGN = 16
_MAIN_RUNS = ((P_Z, _IN_OFF[6], 2 * SSM_INNER), (P_Q, _IN_OFF[0], 2 * GLA_QK + 2 * GLA_V),
              (P_BM, _IN_OFF[8], 2 * SSM_BC), (P_GA, _IN_OFF[12], 2 * D_MODEL))


def _regroup_kernel(wt_ref, lr_ref, dt_ref, wm_ref, ws_ref):
    wm_ref[...] = jnp.transpose(wt_ref[...]).astype(BF16)
    pad = jnp.zeros((P_SMALL - lr_ref.shape[0] - dt_ref.shape[0], lr_ref.shape[1]), F32)
    ws_ref[...] = jnp.transpose(jnp.concatenate([lr_ref[...], dt_ref[...], pad], axis=0)).astype(BF16)


def _regroup(wt):
    _, D = wt.shape
    R = REGROUP_ROWS
    U = REGROUP_ALIGN
    assert all(dst % R == 0 and n % R == 0 and src % U == 0 for dst, src, n in _MAIN_RUNS)

    def src_row(i):
        units = jnp.int32(0)
        for dst, src, n in _MAIN_RUNS:
            inside = (i * R >= dst) & (i * R < dst + n)
            units = jnp.where(inside, (src - dst) // U + i * (R // U), units)
        return units * U

    return pl.pallas_call(
        _regroup_kernel,
        grid=(P_MAIN // R,),
        in_specs=[pl.BlockSpec((pl.Element(R), pl.Element(D)), lambda i: (src_row(i), 0)),
                  pl.BlockSpec((pl.Element(2 * GLA_RANK), pl.Element(D)), lambda i: (_IN_OFF[4], 0)),
                  pl.BlockSpec((pl.Element(2 * SSM_HEADS), pl.Element(D)), lambda i: (_IN_OFF[10], 0))],
        out_specs=[pl.BlockSpec((D, R), lambda i: (0, i)),
                   pl.BlockSpec((D, P_SMALL), lambda i: (0, 0))],
        out_shape=[jax.ShapeDtypeStruct((D, P_MAIN), BF16), jax.ShapeDtypeStruct((D, P_SMALL), BF16)],
        compiler_params=_params(("arbitrary",)),
        name="regroup",
    )(wt, wt, wt)


def _inproj_kernel(x_ref, mod_ref, nw_ref, wm_ref, ws_ref, om_ref, os_ref, xn_ref):
    @pl.when(pl.program_id(2) == 0)
    def _():
        x = x_ref[0]
        y = x * lax.rsqrt(jnp.mean(x * x, axis=-1, keepdims=True) + EPS) * nw_ref[...]
        y = y * (1.0 + mod_ref[0, 1:2, :]) + mod_ref[0, 0:1, :]
        xn = y.astype(BF16)
        xn_ref[...] = xn
        os_ref[0] = _dot(xn, ws_ref[...])

    om_ref[0] = _dot(xn_ref[...], wm_ref[...]).astype(BF16)


def _inproj(x, mod, nw, w_main, w_small, tm):
    B, L, _ = x.shape
    tn = INPROJ_TN
    return pl.pallas_call(
        _inproj_kernel,
        grid=(B, L // tm, P_MAIN // tn),
        in_specs=[pl.BlockSpec((1, tm, D_MODEL), lambda b, i, j: (b, i, 0)),
                  pl.BlockSpec((1, 2, D_MODEL), lambda b, i, j: (b, 0, 0)),
                  pl.BlockSpec((1, D_MODEL), lambda b, i, j: (0, 0)),
                  pl.BlockSpec((D_MODEL, tn), lambda b, i, j: (0, j)),
                  pl.BlockSpec((D_MODEL, P_SMALL), lambda b, i, j: (0, 0))],
        out_specs=[pl.BlockSpec((1, tm, tn), lambda b, i, j: (b, i, j)),
                   pl.BlockSpec((1, tm, P_SMALL), lambda b, i, j: (b, i, 0))],
        out_shape=[jax.ShapeDtypeStruct((B, L, P_MAIN), BF16),
                   jax.ShapeDtypeStruct((B, L, P_SMALL), F32)],
        scratch_shapes=[pltpu.VMEM((tm, D_MODEL), BF16)],
        compiler_params=_params(("parallel", "parallel", "arbitrary")),
        name="inproj",
    )(x, mod, nw, w_main, w_small)


GLA_MXU_LEVELS = 3
GLA_GROUP = 4


def _gla_consts(C, reverse):
    NL = int(np.log2(C))
    idx = np.arange(C)
    tri = (idx[None, :] <= idx[:, None]).astype(np.float32)
    mats = [tri]
    masks = [np.eye(C, dtype=np.float32)]
    refs, signs = [], []
    for lev in range(NL):
        h = 1 << lev
        blk = idx // (2 * h)
        half = (idx // h) % 2
        ref = blk * 2 * h + h - 1
        if lev < GLA_MXU_LEVELS:
            d = tri - tri[ref]
            d[half == 0] *= -1.0
            mats.append(d)
        else:
            sg = np.where(half == 1, 1.0, -1.0).astype(np.float32)
            if reverse:
                ref, sg = (C - 1 - ref)[::-1], sg[::-1]
            refs.append([int(ref[m * 2 * h]) for m in range(C // (2 * h))])
            signs.append(np.broadcast_to(sg[:, None], (C, GLA_DK)))
        masks.append(((blk[:, None] == blk[None, :]) & (half[:, None] == 1)
                      & (half[None, :] == 0)).astype(np.float32))
    if reverse:
        mats = [m[::-1, ::-1] for m in mats]
        masks = [m[::-1, ::-1] for m in masks]
    dmat = np.concatenate(mats, axis=0)
    return (jnp.asarray(dmat, BF16), jnp.asarray(np.stack(masks), F32),
            jnp.asarray(np.stack(signs), F32), refs)


def _gla_stream(q_ref, k_ref, v_ref, lr_ref, up_ref, bias_ref, dm_ref, mk_ref, sg_ref, s0_ref, *rest,
                C, TB, reverse, refs, part, need_out):
    if need_out:
        o_ref, sf_ref, st_ref, e_s, b_s, oi_s, kv_s = rest
    else:
        sf_ref, st_ref, e_s, b_s, kv_s = rest
    NL = int(np.log2(C))
    nchunk = TB // C
    blk = pl.program_id(2)
    last = 0 if reverse else C - 1
    lr_off = S_LRB if reverse else S_LRF

    if part == "init":
        @pl.when(blk == 0)
        def _():
            st_ref[...] = s0_ref[0, 0]
        return
    if part == "final":
        @pl.when(blk == pl.num_programs(2) - 1)
        def _():
            sf_ref[0, 0] = st_ref[...]
        return

    lr = lr_ref[0][:, lr_off:lr_off + GLA_RANK]
    pre = _dot(lr.astype(BF16), up_ref[...].astype(BF16)) + bias_ref[...]
    g = -_softplus(-pre) * (1.0 / GLA_TAU)
    g_hi = g.astype(BF16)
    g_lo = (g - g_hi.astype(F32)).astype(BF16)
    gs = jnp.concatenate([g_hi, g_lo], axis=1)

    for c in range(nchunk):
        ex = _dot(dm_ref[...], gs[c * C:(c + 1) * C])
        ex = ex[:, :GLA_DK] + ex[:, GLA_DK:]
        b = ex[0:C]
        b_s[c] = b
        e_s[c, NL * C:(NL + 1) * C, :] = jnp.exp(b[last:last + 1, :] - b)
        if not need_out:
            continue
        e_s[c, 0:GLA_MXU_LEVELS * C, :] = jnp.exp(ex[C:(1 + GLA_MXU_LEVELS) * C])
        for li, lev in enumerate(range(GLA_MXU_LEVELS, NL)):
            h2 = 2 << lev
            bref = jnp.concatenate([jnp.broadcast_to(b[r:r + 1, :], (h2, GLA_DK)) for r in refs[li]], axis=0)
            e_s[c, lev * C:(lev + 1) * C, :] = jnp.exp(sg_ref[li] * (b - bref))

    for c in range(nchunk):
        rows = slice(c * C, (c + 1) * C)
        k = k_ref[0, rows, :].astype(F32)
        v = v_ref[0, rows, :]
        kv_s[c] = _dot_tn((k * e_s[c, NL * C:(NL + 1) * C, :]).astype(BF16), v)
        if not need_out:
            continue
        q = q_ref[0, rows, :].astype(F32) * (GLA_DK ** -0.5)
        att = mk_ref[0] * _dot_nt(q.astype(BF16), k.astype(BF16))
        for lev in range(NL):
            e_l = e_s[c, lev * C:(lev + 1) * C, :]
            att = att + mk_ref[lev + 1] * _dot_nt((q * e_l).astype(BF16), (k * e_l).astype(BF16))
        oi_s[rows, :] = _dot(att.astype(BF16), v)

    order = list(reversed(range(nchunk))) if reverse else list(range(nchunk))
    st = st_ref[...]
    for g0 in range(0, nchunk, GLA_GROUP):
        grp = order[g0:g0 + GLA_GROUP]
        if need_out:
            wcat = jnp.concatenate([st.astype(BF16)] + [kv_s[c].astype(BF16) for c in grp[:-1]], axis=0)
        for i, c in enumerate(grp if need_out else ()):
            rows = slice(c * C, (c + 1) * C)
            q = q_ref[0, rows, :].astype(F32) * (GLA_DK ** -0.5)
            expo = b_s[c]
            pieces = []
            for j in range(i - 1, -2, -1):
                pieces.insert(0, (q * jnp.exp(expo)).astype(BF16))
                if j >= 0:
                    expo = expo + b_s[grp[j], last:last + 1, :]
            lhs = jnp.concatenate(pieces, axis=1)
            o_ref[0, rows, :] = oi_s[rows, :] + _dot(lhs, wcat[:(i + 1) * GLA_DK, :])
        for c in grp:
            dec = jnp.transpose(jnp.broadcast_to(jnp.exp(b_s[c, last:last + 1, :]), (GLA_DK, GLA_DK)))
            st = jnp.concatenate([dec] * (GLA_DV // GLA_DK), axis=1) * st + kv_s[c]
    st_ref[...] = st


GLA_STREAM_IN = 10


def _gla_kernel(*refs, C, TB, lv_refs, need_out):
    n_in = GLA_STREAM_IN
    n_out, n_scr = (2, 5) if need_out else (1, 4)
    ins = [refs[d * n_in:(d + 1) * n_in] for d in range(2)]
    outs = [refs[2 * n_in + d * n_out:2 * n_in + (d + 1) * n_out] for d in range(2)]
    base = 2 * (n_in + n_out)
    scr = [refs[base + d * n_scr:base + (d + 1) * n_scr] for d in range(2)]
    for part in ("init", "body", "final"):
        for d, reverse in enumerate((False, True)):
            _gla_stream(*ins[d], *outs[d], *scr[d], C=C, TB=TB, reverse=reverse, refs=lv_refs[d], part=part,
                        need_out=need_out)


def _gla_bidir(pm, ps, up, bias, s0, *, TB, need_out=True):
    B, L, _ = pm.shape
    C = GLA_CHUNK
    nb = L // TB
    assert C >= (2 << GLA_MXU_LEVELS)
    qb, kb, vb = P_Q // GLA_DK, P_K // GLA_DK, P_V // GLA_DV
    in_specs, args, out_specs, out_shape, scratch, lv_refs = [], [], [], [], [], []
    for d, reverse in enumerate((False, True)):
        dmat, masks, signs, refs = _gla_consts(C, reverse)
        lv_refs.append(refs)
        tmap = (lambda i: nb - 1 - i) if reverse else (lambda i: i)
        in_specs += [
            pl.BlockSpec((1, TB, GLA_DK), lambda b, h, i, tmap=tmap: (b, tmap(i), qb + h)),
            pl.BlockSpec((1, TB, GLA_DK), lambda b, h, i, tmap=tmap: (b, tmap(i), kb + h)),
            pl.BlockSpec((1, TB, GLA_DV), lambda b, h, i, tmap=tmap: (b, tmap(i), vb + h)),
            pl.BlockSpec((1, TB, P_SMALL), lambda b, h, i, tmap=tmap: (b, tmap(i), 0)),
            pl.BlockSpec((GLA_RANK, GLA_DK), lambda b, h, i: (0, h)),
            pl.BlockSpec((1, GLA_DK), lambda b, h, i: (0, h)),
            _const_spec(dmat.shape),
            _const_spec(masks.shape),
            _const_spec(signs.shape),
            pl.BlockSpec((1, 1, GLA_DK, GLA_DV), lambda b, h, i: (b, h, 0, 0)),
        ]
        args += [pm, pm, pm, ps, up[d], bias[d], dmat, masks, signs, s0[d]]
        if need_out:
            out_specs += [pl.BlockSpec((1, TB, GLA_DV), lambda b, h, i, tmap=tmap: (b, tmap(i), h))]
            out_shape += [jax.ShapeDtypeStruct((B, L, GLA_V), F32)]
        out_specs += [pl.BlockSpec((1, 1, GLA_DK, GLA_DV), lambda b, h, i: (b, h, 0, 0))]
        out_shape += [jax.ShapeDtypeStruct((B, GLA_HEADS, GLA_DK, GLA_DV), F32)]
        scratch += [pltpu.VMEM((GLA_DK, GLA_DV), F32),
                    pltpu.VMEM((TB // C, masks.shape[0] * C, GLA_DK), F32),
                    pltpu.VMEM((TB // C, C, GLA_DK), F32)]
        if need_out:
            scratch += [pltpu.VMEM((TB, GLA_DV), F32)]
        scratch += [pltpu.VMEM((TB // C, GLA_DK, GLA_DV), F32)]
    assert len(in_specs) == 2 * GLA_STREAM_IN
    return pl.pallas_call(
        functools.partial(_gla_kernel, C=C, TB=TB, lv_refs=lv_refs, need_out=need_out),
        grid=(B, GLA_HEADS, nb),
        in_specs=in_specs,
        out_specs=out_specs,
        out_shape=out_shape,
        scratch_shapes=scratch,
        compiler_params=_params(("parallel", "parallel", "arbitrary")),
        name="gla",
    )(*args)


CONV_ROWS = 256


def _conv_shift_mats(T, rowlen):
    t = np.arange(T)
    mats = []
    for j in range(SSM_CONV):
        off = j - CONV_LEFT
        if off == 0:
            continue
        src = t + off
        ok = (src // rowlen == t // rowlen) & (src >= 0) & (src < T)
        m = np.zeros((T, T), np.float32)
        m[t[ok], src[ok]] = 1.0
        mats.append(m)
    return jnp.asarray(np.stack(mats), BF16)


def _conv_silu(u, w, b, sh_ref):
    acc = b + u.astype(F32) * w[CONV_LEFT:CONV_LEFT + 1, :]
    taps = [j for j in range(SSM_CONV) if j != CONV_LEFT]
    for i, j in enumerate(taps):
        acc = acc + _dot(sh_ref[i], u) * w[j:j + 1, :]
    return _silu(acc)


def _ssd_kernel(*refs, C, TB, reverse, finalize, conv, emit, lane0, need_out):
    refs = list(refs)
    xs_ref, bm_ref, cm_ref, ps_ref = refs[:4]
    refs = refs[4:]
    if conv:
        wx_ref, wb_ref, wc_ref, bx_ref, bb_ref, bc_ref, sh_ref = refs[:7]
        refs = refs[7:]
    dtb_ref, alog_ref, tri_ref, s0_ref = refs[:4]
    refs = refs[4:]
    if finalize:
        z_ref, yp_ref, dsk_ref, nw_ref = refs[:4]
        refs = refs[4:]
    if need_out:
        y_ref = refs[0]
        refs = refs[1:]
    sf_ref = refs[0]
    refs = refs[1:]
    if emit:
        xo_ref, bo_ref, co_ref = refs[:3]
        refs = refs[3:]
    if need_out:
        st_ref, xc_s, bc_s, cc_s, ya_s = refs
    else:
        st_ref, xc_s, bc_s, cc_s = refs
    nchunk = TB // C
    blk = pl.program_id(1)
    G, HPG, P, N = SSM_GROUPS, SSM_HPG, SSM_HEADDIM, SSM_STATE

    @pl.when(blk == 0)
    def _():
        st_ref[...] = s0_ref[0]

    if conv:
        tc = sh_ref.shape[1]
        for r0 in range(0, TB, tc):
            rs = slice(r0, r0 + tc)
            xc_s[rs, :] = _conv_silu(xs_ref[0, rs, :], wx_ref[...], bx_ref[...], sh_ref).astype(BF16)
            bc_s[rs, :] = _conv_silu(bm_ref[0, rs, :], wb_ref[...], bb_ref[...], sh_ref).astype(BF16)
            cc_s[rs, :] = _conv_silu(cm_ref[0, rs, :], wc_ref[...], bc_ref[...], sh_ref).astype(BF16)
        if emit:
            xo_ref[0] = xc_s[...]
            bo_ref[0] = bc_s[...]
            co_ref[0] = cc_s[...]
    else:
        xc_s[...] = xs_ref[0]
        bc_s[...] = bm_ref[0]
        cc_s[...] = cm_ref[0]

    neg_a = -jnp.exp(alog_ref[...])
    ti = lax.broadcasted_iota(jnp.int32, (C, C), 0)
    si = lax.broadcasted_iota(jnp.int32, (C, C), 1)
    keep = (si >= ti) if reverse else (si <= ti)
    lo = lax.broadcasted_iota(jnp.int32, (1, 2 * P), 1) < P
    last = 0 if reverse else C - 1
    nheads = G * HPG

    for c in (reversed(range(nchunk)) if reverse else range(nchunk)):
        rows = slice(c * C, (c + 1) * C)
        dt = _softplus(ps_ref[0, rows, :] + dtb_ref[...])
        cum = _dot01(tri_ref[...], dt * neg_a) * LOG2E
        cum_t = jnp.transpose(cum)[lane0:lane0 + nheads, :]
        dt_t = jnp.transpose(dt)[lane0:lane0 + nheads, :]
        w_t = (dt_t * jnp.exp2(cum_t[:, last:last + 1] - cum_t)).astype(BF16)
        cdl_t = cum_t - jnp.log2(dt_t)
        dec_all = jnp.exp2(cum[last:last + 1, :])
        for g in range(G):
            bm_g = bc_s[rows, g * N:(g + 1) * N]
            cm_g = cc_s[rows, g * N:(g + 1) * N]
            if need_out:
                cb = _dot_nt(cm_g, bm_g).astype(BF16)
            bm_t = jnp.transpose(bm_g.astype(F32)).astype(BF16)
            for pr in range(HPG // 2):
                lanes = slice(g * SSM_GW + pr * 2 * P, g * SSM_GW + (pr + 1) * 2 * P)
                x_pair = xc_s[rows, lanes]
                s_pair = st_ref[g, :, pr * 2 * P:(pr + 1) * 2 * P]
                s_bf = s_pair.astype(BF16)
                y = None
                ds = None
                decs = []
                for half in range(2):
                    j = g * HPG + pr * 2 + half
                    lane = lane0 + j
                    sel = lo if half == 0 else jnp.logical_not(lo)
                    if need_out:
                        bc_ = jnp.broadcast_to(cum[:, lane:lane + 1], (C, C))
                        m = cb * jnp.exp2(jnp.where(keep, bc_ - cdl_t[j:j + 1, :], NEG_BIG)).astype(BF16)
                        cd = cm_g * jnp.exp2(bc_).astype(BF16)
                        lhs = jnp.concatenate([m, cd], axis=1)
                        rhs = jnp.concatenate([jnp.where(sel, x_pair, jnp.zeros_like(x_pair)),
                                               jnp.where(sel, s_bf, jnp.zeros_like(s_bf))], axis=0)
                        yh = _dot(lhs, rhs)
                        y = yh if y is None else y + yh
                    dh = _dot(bm_t * w_t[j:j + 1, :],
                              jnp.where(sel, x_pair, jnp.zeros_like(x_pair)))
                    ds = dh if ds is None else ds + dh
                    decs.append(dec_all[:, lane:lane + 1])
                if need_out:
                    ya_s[rows, lanes] = y
                dec = jnp.where(lo, decs[0], decs[1])
                st_ref[g, :, pr * 2 * P:(pr + 1) * 2 * P] = dec * s_pair + ds

    if finalize:
        for g in range(G):
            lanes = slice(g * SSM_GW, (g + 1) * SSM_GW)
            y = ya_s[:, lanes] + yp_ref[0, :, lanes] + dsk_ref[:, lanes] * xc_s[:, lanes].astype(F32)
            y = y * _silu(z_ref[0, :, lanes].astype(F32))
            y = y * lax.rsqrt(jnp.mean(y * y, axis=-1, keepdims=True) + EPS) * nw_ref[:, lanes]
            y_ref[0, :, lanes] = y.astype(y_ref.dtype)
    elif need_out:
        y_ref[0] = ya_s[...]

    @pl.when(blk == pl.num_programs(1) - 1)
    def _():
        sf_ref[0] = st_ref[...]


def _ssd_pass(src, ps, conv_w, conv_b, dt_bias, a_log, s0, *, reverse, finalize, TB, rowlen, emit=False,
              need_out=True, y_prev=None, z_src=None, d_skip_x=None, norm_w=None):
    (xs_a, xs_o), (bm_a, bm_o), (cm_a, cm_o) = src
    B, L, _ = ps.shape
    C = SSD_CHUNK
    nb = L // TB
    G = SSM_GROUPS
    conv = conv_w is not None
    tmap = (lambda i: nb - 1 - i) if reverse else (lambda i: i)
    idx = np.arange(C)
    tri = (idx[None, :] >= idx[:, None]) if reverse else (idx[None, :] <= idx[:, None])
    tri = jnp.asarray(tri.astype(np.float32), BF16)
    in_specs = [
        pl.BlockSpec((1, TB, SSM_INNER), lambda b, i: (b, tmap(i), xs_o)),
        pl.BlockSpec((1, TB, SSM_BC), lambda b, i: (b, tmap(i), bm_o)),
        pl.BlockSpec((1, TB, SSM_BC), lambda b, i: (b, tmap(i), cm_o)),
        pl.BlockSpec((1, TB, P_SMALL), lambda b, i: (b, tmap(i), 0)),
    ]
    args = [xs_a, bm_a, cm_a, ps]
    if conv:
        tc = min(TB, max(rowlen, CONV_ROWS))
        assert TB % tc == 0 and tc % rowlen == 0
        nx = SSM_INNER // SSM_BC
        in_specs += [
            pl.BlockSpec((SSM_CONV, SSM_INNER), lambda b, i: (0, 0)),
            pl.BlockSpec((SSM_CONV, SSM_BC), lambda b, i: (0, nx)),
            pl.BlockSpec((SSM_CONV, SSM_BC), lambda b, i: (0, nx + 1)),
            pl.BlockSpec((1, SSM_INNER), lambda b, i: (0, 0)),
            pl.BlockSpec((1, SSM_BC), lambda b, i: (0, nx)),
            pl.BlockSpec((1, SSM_BC), lambda b, i: (0, nx + 1)),
            _const_spec((SSM_CONV - 1, tc, tc)),
        ]
        args += [conv_w, conv_w, conv_w, conv_b, conv_b, conv_b, _conv_shift_mats(tc, rowlen)]
    in_specs += [_const_spec((1, P_SMALL)), _const_spec((1, P_SMALL)), _const_spec((C, C)),
                 pl.BlockSpec((1, G, SSM_STATE, SSM_GW), lambda b, i: (b, 0, 0, 0))]
    args += [dt_bias, a_log, tri, s0]
    if finalize:
        z_a, z_o = z_src
        in_specs += [
            pl.BlockSpec((1, TB, SSM_INNER), lambda b, i: (b, tmap(i), z_o)),
            pl.BlockSpec((1, TB, SSM_INNER), lambda b, i: (b, tmap(i), 0)),
            _const_spec((1, SSM_INNER)), _const_spec((1, SSM_INNER)),
        ]
        args += [z_a, y_prev, d_skip_x, norm_w]
    out_specs, out_shape = [], []
    if need_out:
        out_specs += [pl.BlockSpec((1, TB, SSM_INNER), lambda b, i: (b, tmap(i), 0))]
        out_shape += [jax.ShapeDtypeStruct((B, L, SSM_INNER), BF16 if finalize else F32)]
    out_specs += [pl.BlockSpec((1, G, SSM_STATE, SSM_GW), lambda b, i: (b, 0, 0, 0))]
    out_shape += [jax.ShapeDtypeStruct((B, G, SSM_STATE, SSM_GW), F32)]
    if emit:
        out_specs += [pl.BlockSpec((1, TB, SSM_INNER), lambda b, i: (b, tmap(i), 0)),
                      pl.BlockSpec((1, TB, SSM_BC), lambda b, i: (b, tmap(i), 0)),
                      pl.BlockSpec((1, TB, SSM_BC), lambda b, i: (b, tmap(i), 0))]
        out_shape += [jax.ShapeDtypeStruct((B, L, SSM_INNER), BF16),
                      jax.ShapeDtypeStruct((B, L, SSM_BC), BF16),
                      jax.ShapeDtypeStruct((B, L, SSM_BC), BF16)]
    kern = functools.partial(_ssd_kernel, C=C, TB=TB, reverse=reverse, finalize=finalize,
                             conv=conv, emit=emit, lane0=S_DTB if reverse else S_DTF, need_out=need_out)
    return pl.pallas_call(
        kern,
        grid=(B, nb),
        in_specs=in_specs,
        out_specs=out_specs,
        out_shape=out_shape,
        scratch_shapes=[pltpu.VMEM((G, SSM_STATE, SSM_GW), F32),
                        pltpu.VMEM((TB, SSM_INNER), BF16),
                        pltpu.VMEM((TB, SSM_BC), BF16),
                        pltpu.VMEM((TB, SSM_BC), BF16)]
                       + ([pltpu.VMEM((TB, SSM_INNER), F32)] if need_out else []),
        compiler_params=_params(("parallel", "arbitrary")),
        name="ssd_" + ("rev" if reverse else "fwd") + ("_fin" if finalize else ""),
    )(*args)


def _merge_kernel(of_ref, ob_ref, r_ref, gnw_ref, sb_ref, ga_ref, gb_ref, x_ref, g1_ref, wpa_ref, wpb_ref,
                  wout_ref, h_ref):
    o = of_ref[0] + ob_ref[0]
    heads = []
    for h in range(GLA_HEADS):
        oh = o[:, h * GLA_DV:(h + 1) * GLA_DV]
        heads.append(oh * lax.rsqrt(jnp.mean(oh * oh, axis=-1, keepdims=True) + EPS) * gnw_ref[...])
    oa = (jnp.concatenate(heads, axis=1) * _silu(r_ref[0].astype(F32))).astype(BF16)
    ya = _dot(oa, wpa_ref[...])
    yb = _dot(sb_ref[0], wpb_ref[...])
    m = _sigmoid(ga_ref[0].astype(F32)) * ya + _sigmoid(gb_ref[0].astype(F32)) * yb
    mix = _dot(m.astype(BF16), wout_ref[...])
    h_ref[0] = x_ref[0] + g1_ref[0] * mix


def _merge(o_f, o_b, gla_norm_w, s_b, pm, x, g1, w_pa, w_pb, w_out, tm):
    B, L, _ = x.shape
    gab, gbb, rb = P_GA // D_MODEL, P_GB // D_MODEL, P_R // GLA_V
    return pl.pallas_call(
        _merge_kernel,
        grid=(B, L // tm),
        in_specs=[pl.BlockSpec((1, tm, GLA_V), lambda b, i: (b, i, 0)),
                  pl.BlockSpec((1, tm, GLA_V), lambda b, i: (b, i, 0)),
                  pl.BlockSpec((1, tm, GLA_V), lambda b, i: (b, i, rb)),
                  _const_spec((1, GLA_DV)),
                  pl.BlockSpec((1, tm, SSM_INNER), lambda b, i: (b, i, 0)),
                  pl.BlockSpec((1, tm, D_MODEL), lambda b, i: (b, i, gab)),
                  pl.BlockSpec((1, tm, D_MODEL), lambda b, i: (b, i, gbb)),
                  pl.BlockSpec((1, tm, D_MODEL), lambda b, i: (b, i, 0)),
                  pl.BlockSpec((1, 1, D_MODEL), lambda b, i: (b, 0, 0)),
                  _const_spec(w_pa.shape), _const_spec(w_pb.shape), _const_spec(w_out.shape)],
        out_specs=pl.BlockSpec((1, tm, D_MODEL), lambda b, i: (b, i, 0)),
        out_shape=jax.ShapeDtypeStruct((B, L, D_MODEL), F32),
        compiler_params=_params(("parallel", "parallel")),
        name="merge",
    )(o_f, o_b, pm, gla_norm_w, s_b, pm, pm, x, g1, w_pa, w_pb, w_out)


FFN_CHUNK = 256


def _ffn_kernel(h_ref, mod_ref, n2_ref, fw_ref, wg_ref, wu_ref, wd_ref, o_ref):
    h = h_ref[0]
    hn = h * lax.rsqrt(jnp.mean(h * h, axis=-1, keepdims=True) + EPS) * n2_ref[...]
    hn = (hn * (1.0 + mod_ref[0, 1:2, :]) + mod_ref[0, 0:1, :]).astype(BF16)
    acc = jnp.zeros(h.shape, F32)
    for c in range(D_FF // FFN_CHUNK):
        cols = slice(c * FFN_CHUNK, (c + 1) * FFN_CHUNK)
        gt = _dot(hn, wg_ref[:, cols])
        up = _dot(hn, wu_ref[:, cols])
        acc = acc + _dot((_silu(gt) * up).astype(BF16), wd_ref[cols, :])
    h2 = h + mod_ref[0, 2:3, :] * acc
    o_ref[0] = h2 * lax.rsqrt(jnp.mean(h2 * h2, axis=-1, keepdims=True) + EPS) * fw_ref[...]


def _ffn(h, mod, n2w, fw, w_gate, w_up, w_down, tm):
    B, L, _ = h.shape
    return pl.pallas_call(
        _ffn_kernel,
        grid=(B, L // tm),
        in_specs=[pl.BlockSpec((1, tm, D_MODEL), lambda b, i: (b, i, 0)),
                  pl.BlockSpec((1, 3, D_MODEL), lambda b, i: (b, 0, 0)),
                  _const_spec((1, D_MODEL)), _const_spec((1, D_MODEL)),
                  pl.BlockSpec(w_gate.shape, lambda b, i: (0, 0), pipeline_mode=pl.Buffered(1)),
                  pl.BlockSpec(w_up.shape, lambda b, i: (0, 0), pipeline_mode=pl.Buffered(1)),
                  pl.BlockSpec(w_down.shape, lambda b, i: (0, 0), pipeline_mode=pl.Buffered(1))],
        out_specs=pl.BlockSpec((1, tm, D_MODEL), lambda b, i: (b, i, 0)),
        out_shape=jax.ShapeDtypeStruct((B, L, D_MODEL), F32),
        compiler_params=_params(("parallel", "parallel")),
        name="ffn",
    )(h, mod, n2w, fw, w_gate, w_up, w_down)


def _pick_block(L, pref):
    tb = min(L, pref)
    assert L % tb == 0
    return tb


def kernel(x, c, ctx, c_ctx, w_ada, b_ada, norm1_w, w_in, gla_up_f, gla_bias_f, gla_up_b, gla_bias_b,
           gla_norm_w, conv_w, conv_b, dt_bias_f, dt_bias_b, a_log_f, a_log_b, d_skip, ssm_norm_w,
           w_pa, w_pb, w_out, norm2_w, w_gate, w_up, w_down, final_norm_w):
    B, L, D = x.shape
    Lc = ctx.shape[1]
    depth = w_ada.shape[0]
    assert depth == 1 and D == D_MODEL
    assert L % GRID_W == 0 and L % SSD_CHUNK == 0 and Lc % SSD_CHUNK == 0
    lay = 0

    nrow = -(-(B + 1) // 8) * 8
    cc = jnp.zeros((nrow, D), F32).at[:B].set(c).at[B].set(c_ctx)
    ada = _ada(cc, w_ada, b_ada[lay][None, :], lay)
    sh1, sc1, g1, sh2, sc2, g2 = [ada[:, i * D:(i + 1) * D] for i in range(6)]
    mod1 = jnp.stack([sh1[:B], sc1[:B]], axis=1)
    mod1_c = jnp.broadcast_to(jnp.stack([sh1[B], sc1[B]])[None], (B, 2, D))
    mod2 = jnp.stack([sh2[:B], sc2[:B], g2[:B]], axis=1)
    g1_l = g1[:B, None, :]

    wt = jnp.swapaxes(w_in[lay], 0, 1)
    w_main, w_small = _regroup(wt)
    nw1 = norm1_w[lay][None, :]

    at_lanes = lambda p, off: jnp.zeros((1, P_SMALL), F32).at[0, off:off + SSM_HEADS].set(p[lay])
    dtb_f, al_f = at_lanes(dt_bias_f, S_DTF), at_lanes(a_log_f, S_DTF)
    dtb_b, al_b = at_lanes(dt_bias_b, S_DTB), at_lanes(a_log_b, S_DTB)
    cw, cb_ = conv_w[lay], conv_b[lay][None, :]
    up_f, up_b = gla_up_f[lay], gla_up_b[lay]
    bi_f, bi_b = gla_bias_f[lay][None, :], gla_bias_b[lay][None, :]

    gla_zero = jnp.zeros((B, GLA_HEADS, GLA_DK, GLA_DV), F32)
    ssd_zero = jnp.zeros((B, SSM_GROUPS, SSM_STATE, SSM_GW), F32)

    pm_c, ps_c = _inproj(ctx, mod1_c, nw1, w_main, w_small, _pick_block(Lc, INPROJ_TM))
    sg_f, sg_b = _gla_bidir(pm_c, ps_c, (up_f, up_b), (bi_f, bi_b), (gla_zero, gla_zero),
                            TB=_pick_block(Lc, GLA_TB), need_out=False)
    src_c = ((pm_c, P_XS // SSM_INNER), (pm_c, P_BM // SSM_BC), (pm_c, P_CM // SSM_BC))
    ss_f, = _ssd_pass(src_c, ps_c, cw, cb_, dtb_f, al_f, ssd_zero, reverse=False, finalize=False,
                      TB=Lc, rowlen=Lc, need_out=False)
    ss_b, = _ssd_pass(src_c, ps_c, cw, cb_, dtb_b, al_b, ssd_zero, reverse=True, finalize=False,
                      TB=Lc, rowlen=Lc, need_out=False)

    pm, ps = _inproj(x, mod1, nw1, w_main, w_small, _pick_block(L, INPROJ_TM))
    tbg = _pick_block(L, GLA_TB)
    og_f, _, og_b, _ = _gla_bidir(pm, ps, (up_f, up_b), (bi_f, bi_b), (sg_f, sg_b), TB=tbg)
    tbs = _pick_block(L, SSD_TB)
    src = ((pm, P_XS // SSM_INNER), (pm, P_BM // SSM_BC), (pm, P_CM // SSM_BC))
    ys_f, _, xc, bc, cc = _ssd_pass(src, ps, cw, cb_, dtb_f, al_f, ss_f, reverse=False, finalize=False,
                                    TB=tbs, rowlen=GRID_W, emit=True)
    o_b, _ = _ssd_pass(((xc, 0), (bc, 0), (cc, 0)), ps, None, None, dtb_b, al_b, ss_b, reverse=True,
                       finalize=True, TB=tbs, rowlen=GRID_W, y_prev=ys_f, z_src=(pm, P_Z // SSM_INNER),
                       d_skip_x=jnp.repeat(d_skip[lay], SSM_HEADDIM)[None, :],
                       norm_w=ssm_norm_w[lay][None, :])

    h = _merge(og_f, og_b, gla_norm_w[lay][None, :], o_b, pm, x, g1_l, w_pa[lay].astype(BF16),
               w_pb[lay].astype(BF16), w_out[lay].astype(BF16), _pick_block(L, MERGE_TM))
    return _ffn(h, mod2, norm2_w[lay][None, :], final_norm_w[None, :], w_gate[lay].astype(BF16),
                w_up[lay].astype(BF16), w_down[lay].astype(BF16), _pick_block(L, FFN_TM))
```

```python
import functools

import numpy as np
import jax
import jax.numpy as jnp
from jax import lax
from jax.experimental import pallas as pl
from jax.experimental.pallas import tpu as pltpu

F32 = jnp.float32
BF16 = jnp.bfloat16

D_MODEL = 1024
GRID_W = 64
EPS = 1e-6

GLA_HEADS = 4
GLA_DK = 128
GLA_DV = 256
GLA_QK = GLA_HEADS * GLA_DK
GLA_V = GLA_HEADS * GLA_DV
GLA_RANK = 16
GLA_TAU = 16.0

SSM_INNER = 2 * D_MODEL
SSM_HEADDIM = 64
SSM_HEADS = SSM_INNER // SSM_HEADDIM
SSM_GROUPS = 4
SSM_HPG = SSM_HEADS // SSM_GROUPS
SSM_STATE = 128
SSM_BC = SSM_GROUPS * SSM_STATE
SSM_CONV = 4
CONV_LEFT = 2
SSM_GW = SSM_HPG * SSM_HEADDIM

D_FF = ((8 * D_MODEL // 3 + 255) // 256) * 256

_IN_WIDTHS = (GLA_QK, GLA_QK, GLA_V, GLA_V, GLA_RANK, GLA_RANK,
              SSM_INNER, SSM_INNER, SSM_BC, SSM_BC, SSM_HEADS, SSM_HEADS, D_MODEL, D_MODEL)
_IN_OFF = np.concatenate([[0], np.cumsum(_IN_WIDTHS)]).tolist()

P_Z, P_XS = 0, 2048
P_Q, P_K, P_V, P_R = 4096, 4608, 5120, 6144
P_BM, P_CM = 7168, 7680
P_GA, P_GB = 8192, 9216
P_MAIN = 10240
S_LRF, S_LRB, S_DTF, S_DTB = 0, 16, 32, 64
P_SMALL = 128

GLA_CHUNK = 128
SSD_CHUNK = 128

ADA_TN = 1536
INPROJ_TM, INPROJ_TN = 1024, 2560
GLA_TB = 2048
SSD_TB = 512
MERGE_TM = 512
FFN_TM = 512

VMEM_LIMIT = 56 * 1024 * 1024
NEG_BIG = -1e30
LOG2E = 1.4426950408889634


def _sigmoid(x):
    return 1.0 / (1.0 + jnp.exp(-x))


def _silu(x):
    return x * _sigmoid(x)


def _softplus(x):
    return jnp.maximum(x, 0.0) + jnp.log(1.0 + jnp.exp(-jnp.abs(x)))


def _split3(x):
    hi = x.astype(BF16)
    r1 = x - hi.astype(F32)
    mid = r1.astype(BF16)
    lo = (r1 - mid.astype(F32)).astype(BF16)
    return hi, mid, lo


def _dot(a, b):
    return jnp.dot(a, b, preferred_element_type=F32)


def _dot_nt(a, b):
    return lax.dot_general(a, b, (((1,), (1,)), ((), ())), preferred_element_type=F32)


def _dot_tn(a, b):
    return lax.dot_general(a, b, (((0,), (0,)), ((), ())), preferred_element_type=F32)


def _dot01(m01, x):
    hi, mid, lo = _split3(x)
    return _dot(m01, hi) + _dot(m01, mid) + _dot(m01, lo)


def _params(sem):
    return pltpu.CompilerParams(dimension_semantics=sem, vmem_limit_bytes=VMEM_LIMIT)


def _const_spec(shape):
    n = len(shape)
    return pl.BlockSpec(shape, lambda *_: (0,) * n)


def _ada_kernel(c_ref, w_ref, b_ref, o_ref):
    s = _silu(c_ref[...])
    w = w_ref[0]
    s_hi = s.astype(BF16)
    s_lo = (s - s_hi.astype(F32)).astype(BF16)
    w_hi = w.astype(BF16)
    w_lo = (w - w_hi.astype(F32)).astype(BF16)
    o_ref[...] = _dot(s_hi, w_hi) + _dot(s_hi, w_lo) + _dot(s_lo, w_hi) + b_ref[...]


def _ada(cc, w, b, lay):
    rows = cc.shape[0]
    n = w.shape[2]
    tn = ADA_TN
    return pl.pallas_call(
        _ada_kernel,
        grid=(n // tn,),
        in_specs=[pl.BlockSpec((rows, D_MODEL), lambda j: (0, 0)),
                  pl.BlockSpec((1, D_MODEL, tn), lambda j: (lay, 0, j)),
                  pl.BlockSpec((1, tn), lambda j: (0, j))],
        out_specs=pl.BlockSpec((rows, tn), lambda j: (0, j)),
        out_shape=jax.ShapeDtypeStruct((rows, n), F32),
        compiler_params=_params(("arbitrary",)),
        name="ada",
    )(cc, w, b)


REGROUP_ROWS = 1024
REGROUP_ALIGN = 16
_MAIN_RUNS = ((P_Z, _IN_OFF[6], 2 * SSM_INNER), (P_Q, _IN_OFF[0], 2 * GLA_QK + 2 * GLA_V),
              (P_BM, _IN_OFF[8], 2 * SSM_BC), (P_GA, _IN_OFF[12], 2 * D_MODEL))


def _regroup_kernel(wt_ref, lr_ref, dt_ref, wm_ref, ws_ref):
    wm_ref[...] = jnp.transpose(wt_ref[...]).astype(BF16)
    pad = jnp.zeros((P_SMALL - lr_ref.shape[0] - dt_ref.shape[0], lr_ref.shape[1]), F32)
    ws_ref[...] = jnp.transpose(jnp.concatenate([lr_ref[...], dt_ref[...], pad], axis=0)).astype(BF16)


def _regroup(wt):
    _, D = wt.shape
    R = REGROUP_ROWS
    U = REGROUP_ALIGN
    assert all(dst % R == 0 and n % R == 0 and src % U == 0 for dst, src, n in _MAIN_RUNS)

    def src_row(i):
        units = jnp.int32(0)
        for dst, src, n in _MAIN_RUNS:
            inside = (i * R >= dst) & (i * R < dst + n)
            units = jnp.where(inside, (src - dst) // U + i * (R // U), units)
        return units * U

    return pl.pallas_call(
        _regroup_kernel,
        grid=(P_MAIN // R,),
        in_specs=[pl.BlockSpec((pl.Element(R), pl.Element(D)), lambda i: (src_row(i), 0)),
                  pl.BlockSpec((pl.Element(2 * GLA_RANK), pl.Element(D)), lambda i: (_IN_OFF[4], 0)),
                  pl.BlockSpec((pl.Element(2 * SSM_HEADS), pl.Element(D)), lambda i: (_IN_OFF[10], 0))],
        out_specs=[pl.BlockSpec((D, R), lambda i: (0, i)),
                   pl.BlockSpec((D, P_SMALL), lambda i: (0, 0))],
        out_shape=[jax.ShapeDtypeStruct((D, P_MAIN), BF16), jax.ShapeDtypeStruct((D, P_SMALL), BF16)],
        compiler_params=_params(("arbitrary",)),
        name="regroup",
    )(wt, wt, wt)


def _inproj_kernel(x_ref, mod_ref, nw_ref, wm_ref, ws_ref, om_ref, os_ref, xn_ref):
    @pl.when(pl.program_id(2) == 0)
    def _():
        x = x_ref[0]
        y = x * lax.rsqrt(jnp.mean(x * x, axis=-1, keepdims=True) + EPS) * nw_ref[...]
        y = y * (1.0 + mod_ref[0, 1:2, :]) + mod_ref[0, 0:1, :]
        xn = y.astype(BF16)
        xn_ref[...] = xn
        os_ref[0] = _dot(xn, ws_ref[...])

    om_ref[0] = _dot(xn_ref[...], wm_ref[...]).astype(BF16)


def _inproj(x, mod, nw, w_main, w_small, tm):
    B, L, _ = x.shape
    tn = INPROJ_TN
    return pl.pallas_call(
        _inproj_kernel,
        grid=(B, L // tm, P_MAIN // tn),
        in_specs=[pl.BlockSpec((1, tm, D_MODEL), lambda b, i, j: (b, i, 0)),
                  pl.BlockSpec((1, 2, D_MODEL), lambda b, i, j: (b, 0, 0)),
                  pl.BlockSpec((1, D_MODEL), lambda b, i, j: (0, 0)),
                  pl.BlockSpec((D_MODEL, tn), lambda b, i, j: (0, j)),
                  pl.BlockSpec((D_MODEL, P_SMALL), lambda b, i, j: (0, 0))],
        out_specs=[pl.BlockSpec((1, tm, tn), lambda b, i, j: (b, i, j)),
                   pl.BlockSpec((1, tm, P_SMALL), lambda b, i, j: (b, i, 0))],
        out_shape=[jax.ShapeDtypeStruct((B, L, P_MAIN), BF16),
                   jax.ShapeDtypeStruct((B, L, P_SMALL), F32)],
        scratch_shapes=[pltpu.VMEM((tm, D_MODEL), BF16)],
        compiler_params=_params(("parallel", "parallel", "arbitrary")),
        name="inproj",
    )(x, mod, nw, w_main, w_small)


GLA_MXU_LEVELS = 3
GLA_GROUP = 4


def _gla_consts(C, reverse):
    NL = int(np.log2(C))
    idx = np.arange(C)
    tri = (idx[None, :] <= idx[:, None]).astype(np.float32)
    mats = [tri]
    masks = [np.eye(C, dtype=np.float32)]
    refs, signs = [], []
    for lev in range(NL):
        h = 1 << lev
        blk = idx // (2 * h)
        half = (idx // h) % 2
        ref = blk * 2 * h + h - 1
        if lev < GLA_MXU_LEVELS:
            d = tri - tri[ref]
            d[half == 0] *= -1.0
            mats.append(d)
        else:
            sg = np.where(half == 1, 1.0, -1.0).astype(np.float32)
            if reverse:
                ref, sg = (C - 1 - ref)[::-1], sg[::-1]
            refs.append([int(ref[m * 2 * h]) for m in range(C // (2 * h))])
            signs.append(np.broadcast_to(sg[:, None], (C, GLA_DK)))
        masks.append(((blk[:, None] == blk[None, :]) & (half[:, None] == 1)
                      & (half[None, :] == 0)).astype(np.float32))
    if reverse:
        mats = [m[::-1, ::-1] for m in mats]
        masks = [m[::-1, ::-1] for m in masks]
    dmat = np.concatenate(mats, axis=0)
    return (jnp.asarray(dmat, BF16), jnp.asarray(np.stack(masks), F32),
            jnp.asarray(np.stack(signs), F32), refs)


def _gla_stream(q_ref, k_ref, v_ref, lr_ref, up_ref, bias_ref, dm_ref, mk_ref, sg_ref, s0_ref, *rest,
                C, TB, reverse, refs, part, need_out):
    if need_out:
        o_ref, sf_ref, st_ref, e_s, b_s, oi_s, kv_s = rest
    else:
        sf_ref, st_ref, e_s, b_s, kv_s = rest
    NL = int(np.log2(C))
    nchunk = TB // C
    blk = pl.program_id(2)
    last = 0 if reverse else C - 1
    lr_off = S_LRB if reverse else S_LRF

    if part == "init":
        @pl.when(blk == 0)
        def _():
            st_ref[...] = s0_ref[0, 0]
        return
    if part == "final":
        @pl.when(blk == pl.num_programs(2) - 1)
        def _():
            sf_ref[0, 0] = st_ref[...]
        return

    lr = lr_ref[0][:, lr_off:lr_off + GLA_RANK]
    pre = _dot(lr.astype(BF16), up_ref[...].astype(BF16)) + bias_ref[...]
    g = -_softplus(-pre) * (1.0 / GLA_TAU)
    g_hi = g.astype(BF16)
    g_lo = (g - g_hi.astype(F32)).astype(BF16)
    gs = jnp.concatenate([g_hi, g_lo], axis=1)

    for c in range(nchunk):
        ex = _dot(dm_ref[...], gs[c * C:(c + 1) * C])
        ex = ex[:, :GLA_DK] + ex[:, GLA_DK:]
        b = ex[0:C]
        b_s[c] = b
        if need_out:
            e_s[c, 0:GLA_MXU_LEVELS * C, :] = jnp.exp(ex[C:(1 + GLA_MXU_LEVELS) * C])
            for li, lev in enumerate(range(GLA_MXU_LEVELS, NL)):
                h2 = 2 << lev
                bref = jnp.concatenate([jnp.broadcast_to(b[r:r + 1, :], (h2, GLA_DK)) for r in refs[li]],
                                       axis=0)
                e_s[c, lev * C:(lev + 1) * C, :] = jnp.exp(sg_ref[li] * (b - bref))
        e_s[c, NL * C:(NL + 1) * C, :] = jnp.exp(b[last:last + 1, :] - b)

    for c in range(nchunk):
        rows = slice(c * C, (c + 1) * C)
        if need_out:
            q = q_ref[0, rows, :].astype(F32) * (GLA_DK ** -0.5)
        k = k_ref[0, rows, :].astype(F32)
        v = v_ref[0, rows, :]
        if need_out:
            att = mk_ref[0] * _dot_nt(q.astype(BF16), k.astype(BF16))
            for lev in range(NL):
                e_l = e_s[c, lev * C:(lev + 1) * C, :]
                att = att + mk_ref[lev + 1] * _dot_nt((q * e_l).astype(BF16), (k * e_l).astype(BF16))
            oi_s[rows, :] = _dot(att.astype(BF16), v)
        kv_s[c] = _dot_tn((k * e_s[c, NL * C:(NL + 1) * C, :]).astype(BF16), v)

    order = list(reversed(range(nchunk))) if reverse else list(range(nchunk))
    st = st_ref[...]
    for g0 in range(0, nchunk, GLA_GROUP):
        grp = order[g0:g0 + GLA_GROUP]
        if need_out:
            wcat = jnp.concatenate([st.astype(BF16)] + [kv_s[c].astype(BF16) for c in grp[:-1]], axis=0)
        for i, c in enumerate(grp if need_out else ()):
            rows = slice(c * C, (c + 1) * C)
            q = q_ref[0, rows, :].astype(F32) * (GLA_DK ** -0.5)
            expo = b_s[c]
            pieces = []
            for j in range(i - 1, -2, -1):
                pieces.insert(0, (q * jnp.exp(expo)).astype(BF16))
                if j >= 0:
                    expo = expo + b_s[grp[j], last:last + 1, :]
            lhs = jnp.concatenate(pieces, axis=1)
            o_ref[0, rows, :] = oi_s[rows, :] + _dot(lhs, wcat[:(i + 1) * GLA_DK, :])
        for c in grp:
            dec = jnp.transpose(jnp.broadcast_to(jnp.exp(b_s[c, last:last + 1, :]), (GLA_DK, GLA_DK)))
            st = jnp.concatenate([dec] * (GLA_DV // GLA_DK), axis=1) * st + kv_s[c]
    st_ref[...] = st


GLA_STREAM_IN = 10


def _gla_kernel(*refs, C, TB, lv_refs, need_out):
    n_in = GLA_STREAM_IN
    n_out, n_scr = (2, 5) if need_out else (1, 4)
    ins = [refs[d * n_in:(d + 1) * n_in] for d in range(2)]
    outs = [refs[2 * n_in + d * n_out:2 * n_in + (d + 1) * n_out] for d in range(2)]
    base = 2 * (n_in + n_out)
    scr = [refs[base + d * n_scr:base + (d + 1) * n_scr] for d in range(2)]
    for part in ("init", "body", "final"):
        for d, reverse in enumerate((False, True)):
            _gla_stream(*ins[d], *outs[d], *scr[d], C=C, TB=TB, reverse=reverse, refs=lv_refs[d], part=part,
                        need_out=need_out)


def _gla_bidir(pm, ps, up, bias, s0, *, TB, need_out=True):
    B, L, _ = pm.shape
    C = GLA_CHUNK
    nb = L // TB
    assert C >= (2 << GLA_MXU_LEVELS)
    qb, kb, vb = P_Q // GLA_DK, P_K // GLA_DK, P_V // GLA_DV
    in_specs, args, out_specs, out_shape, scratch, lv_refs = [], [], [], [], [], []
    for d, reverse in enumerate((False, True)):
        dmat, masks, signs, refs = _gla_consts(C, reverse)
        lv_refs.append(refs)
        tmap = (lambda i: nb - 1 - i) if reverse else (lambda i: i)
        in_specs += [
            pl.BlockSpec((1, TB, GLA_DK), lambda b, h, i, tmap=tmap: (b, tmap(i), qb + h)),
            pl.BlockSpec((1, TB, GLA_DK), lambda b, h, i, tmap=tmap: (b, tmap(i), kb + h)),
            pl.BlockSpec((1, TB, GLA_DV), lambda b, h, i, tmap=tmap: (b, tmap(i), vb + h)),
            pl.BlockSpec((1, TB, P_SMALL), lambda b, h, i, tmap=tmap: (b, tmap(i), 0)),
            pl.BlockSpec((GLA_RANK, GLA_DK), lambda b, h, i: (0, h)),
            pl.BlockSpec((1, GLA_DK), lambda b, h, i: (0, h)),
            _const_spec(dmat.shape),
            _const_spec(masks.shape),
            _const_spec(signs.shape),
            pl.BlockSpec((1, 1, GLA_DK, GLA_DV), lambda b, h, i: (b, h, 0, 0)),
        ]
        args += [pm, pm, pm, ps, up[d], bias[d], dmat, masks, signs, s0[d]]
        if need_out:
            out_specs += [pl.BlockSpec((1, TB, GLA_DV), lambda b, h, i, tmap=tmap: (b, tmap(i), h))]
            out_shape += [jax.ShapeDtypeStruct((B, L, GLA_V), F32)]
        out_specs += [pl.BlockSpec((1, 1, GLA_DK, GLA_DV), lambda b, h, i: (b, h, 0, 0))]
        out_shape += [jax.ShapeDtypeStruct((B, GLA_HEADS, GLA_DK, GLA_DV), F32)]
        scratch += [pltpu.VMEM((GLA_DK, GLA_DV), F32),
                    pltpu.VMEM((TB // C, masks.shape[0] * C, GLA_DK), F32),
                    pltpu.VMEM((TB // C, C, GLA_DK), F32)]
        if need_out:
            scratch += [pltpu.VMEM((TB, GLA_DV), F32)]
        scratch += [pltpu.VMEM((TB // C, GLA_DK, GLA_DV), F32)]
    assert len(in_specs) == 2 * GLA_STREAM_IN
    return pl.pallas_call(
        functools.partial(_gla_kernel, C=C, TB=TB, lv_refs=lv_refs, need_out=need_out),
        grid=(B, GLA_HEADS, nb),
        in_specs=in_specs,
        out_specs=out_specs,
        out_shape=out_shape,
        scratch_shapes=scratch,
        compiler_params=_params(("parallel", "parallel", "arbitrary")),
        name="gla",
    )(*args)


CONV_ROWS = 256


def _conv_shift_mats(T, rowlen):
    t = np.arange(T)
    mats = []
    for j in range(SSM_CONV):
        off = j - CONV_LEFT
        if off == 0:
            continue
        src = t + off
        ok = (src // rowlen == t // rowlen) & (src >= 0) & (src < T)
        m = np.zeros((T, T), np.float32)
        m[t[ok], src[ok]] = 1.0
        mats.append(m)
    return jnp.asarray(np.stack(mats), BF16)


def _conv_silu(u, w, b, sh_ref):
    acc = b + u.astype(F32) * w[CONV_LEFT:CONV_LEFT + 1, :]
    taps = [j for j in range(SSM_CONV) if j != CONV_LEFT]
    for i, j in enumerate(taps):
        acc = acc + _dot(sh_ref[i], u) * w[j:j + 1, :]
    return _silu(acc)


def _ssd_kernel(*refs, C, TB, reverse, finalize, conv, emit, lane0, need_out):
    refs = list(refs)
    xs_ref, bm_ref, cm_ref, ps_ref = refs[:4]
    refs = refs[4:]
    if conv:
        wx_ref, wb_ref, wc_ref, bx_ref, bb_ref, bc_ref, sh_ref = refs[:7]
        refs = refs[7:]
    dtb_ref, alog_ref, tri_ref, s0_ref = refs[:4]
    refs = refs[4:]
    if finalize:
        z_ref, yp_ref, dsk_ref, nw_ref = refs[:4]
        refs = refs[4:]
    if need_out:
        y_ref = refs[0]
        refs = refs[1:]
    sf_ref = refs[0]
    refs = refs[1:]
    if emit:
        xo_ref, bo_ref, co_ref = refs[:3]
        refs = refs[3:]
    if need_out:
        st_ref, xc_s, bc_s, cc_s, ya_s = refs
    else:
        st_ref, xc_s, bc_s, cc_s = refs
    nchunk = TB // C
    blk = pl.program_id(1)
    G, HPG, P, N = SSM_GROUPS, SSM_HPG, SSM_HEADDIM, SSM_STATE

    @pl.when(blk == 0)
    def _():
        st_ref[...] = s0_ref[0]

    if conv:
        tc = sh_ref.shape[1]
        for r0 in range(0, TB, tc):
            rs = slice(r0, r0 + tc)
            xc_s[rs, :] = _conv_silu(xs_ref[0, rs, :], wx_ref[...], bx_ref[...], sh_ref).astype(BF16)
            bc_s[rs, :] = _conv_silu(bm_ref[0, rs, :], wb_ref[...], bb_ref[...], sh_ref).astype(BF16)
            cc_s[rs, :] = _conv_silu(cm_ref[0, rs, :], wc_ref[...], bc_ref[...], sh_ref).astype(BF16)
        if emit:
            xo_ref[0] = xc_s[...]
            bo_ref[0] = bc_s[...]
            co_ref[0] = cc_s[...]
    else:
        xc_s[...] = xs_ref[0]
        bc_s[...] = bm_ref[0]
        cc_s[...] = cm_ref[0]

    neg_a = -jnp.exp(alog_ref[...])
    ti = lax.broadcasted_iota(jnp.int32, (C, C), 0)
    si = lax.broadcasted_iota(jnp.int32, (C, C), 1)
    keep = (si >= ti) if reverse else (si <= ti)
    lo = lax.broadcasted_iota(jnp.int32, (1, 2 * P), 1) < P
    last = 0 if reverse else C - 1
    nheads = G * HPG

    for c in (reversed(range(nchunk)) if reverse else range(nchunk)):
        rows = slice(c * C, (c + 1) * C)
        dt = _softplus(ps_ref[0, rows, :] + dtb_ref[...])
        cum = _dot01(tri_ref[...], dt * neg_a) * LOG2E
        cum_t = jnp.transpose(cum)[lane0:lane0 + nheads, :]
        dt_t = jnp.transpose(dt)[lane0:lane0 + nheads, :]
        w_t = (dt_t * jnp.exp2(cum_t[:, last:last + 1] - cum_t)).astype(BF16)
        cdl_t = cum_t - jnp.log2(dt_t)
        dec_all = jnp.exp2(cum[last:last + 1, :])
        for g in range(G):
            bm_g = bc_s[rows, g * N:(g + 1) * N]
            cm_g = cc_s[rows, g * N:(g + 1) * N]
            if need_out:
                cb = _dot_nt(cm_g, bm_g).astype(BF16)
            bm_t = jnp.transpose(bm_g.astype(F32)).astype(BF16)
            for pr in range(HPG // 2):
                lanes = slice(g * SSM_GW + pr * 2 * P, g * SSM_GW + (pr + 1) * 2 * P)
                x_pair = xc_s[rows, lanes]
                s_pair = st_ref[g, :, pr * 2 * P:(pr + 1) * 2 * P]
                s_bf = s_pair.astype(BF16)
                y = None
                ds = None
                decs = []
                for half in range(2):
                    j = g * HPG + pr * 2 + half
                    lane = lane0 + j
                    sel = lo if half == 0 else jnp.logical_not(lo)
                    if need_out:
                        bc_ = jnp.broadcast_to(cum[:, lane:lane + 1], (C, C))
                        m = cb * jnp.exp2(jnp.where(keep, bc_ - cdl_t[j:j + 1, :], NEG_BIG)).astype(BF16)
                        cd = cm_g * jnp.exp2(bc_).astype(BF16)
                        lhs = jnp.concatenate([m, cd], axis=1)
                        rhs = jnp.concatenate([jnp.where(sel, x_pair, jnp.zeros_like(x_pair)),
                                               jnp.where(sel, s_bf, jnp.zeros_like(s_bf))], axis=0)
                        yh = _dot(lhs, rhs)
                    dh = _dot(bm_t * w_t[j:j + 1, :],
                              jnp.where(sel, x_pair, jnp.zeros_like(x_pair)))
                    if need_out:
                        y = yh if y is None else y + yh
                    ds = dh if ds is None else ds + dh
                    decs.append(dec_all[:, lane:lane + 1])
                if need_out:
                    ya_s[rows, lanes] = y
                dec = jnp.where(lo, decs[0], decs[1])
                st_ref[g, :, pr * 2 * P:(pr + 1) * 2 * P] = dec * s_pair + ds

    if finalize:
        for g in range(G):
            lanes = slice(g * SSM_GW, (g + 1) * SSM_GW)
            y = ya_s[:, lanes] + yp_ref[0, :, lanes] + dsk_ref[:, lanes] * xc_s[:, lanes].astype(F32)
            y = y * _silu(z_ref[0, :, lanes].astype(F32))
            y = y * lax.rsqrt(jnp.mean(y * y, axis=-1, keepdims=True) + EPS) * nw_ref[:, lanes]
            y_ref[0, :, lanes] = y.astype(y_ref.dtype)
    elif need_out:
        y_ref[0] = ya_s[...]

    @pl.when(blk == pl.num_programs(1) - 1)
    def _():
        sf_ref[0] = st_ref[...]


def _ssd_pass(src, ps, conv_w, conv_b, dt_bias, a_log, s0, *, reverse, finalize, TB, rowlen, emit=False,
              need_out=True, y_prev=None, z_src=None, d_skip_x=None, norm_w=None):
    (xs_a, xs_o), (bm_a, bm_o), (cm_a, cm_o) = src
    B, L, _ = ps.shape
    C = SSD_CHUNK
    nb = L // TB
    G = SSM_GROUPS
    conv = conv_w is not None
    tmap = (lambda i: nb - 1 - i) if reverse else (lambda i: i)
    idx = np.arange(C)
    tri = (idx[None, :] >= idx[:, None]) if reverse else (idx[None, :] <= idx[:, None])
    tri = jnp.asarray(tri.astype(np.float32), BF16)
    in_specs = [
        pl.BlockSpec((1, TB, SSM_INNER), lambda b, i: (b, tmap(i), xs_o)),
        pl.BlockSpec((1, TB, SSM_BC), lambda b, i: (b, tmap(i), bm_o)),
        pl.BlockSpec((1, TB, SSM_BC), lambda b, i: (b, tmap(i), cm_o)),
        pl.BlockSpec((1, TB, P_SMALL), lambda b, i: (b, tmap(i), 0)),
    ]
    args = [xs_a, bm_a, cm_a, ps]
    if conv:
        tc = min(TB, max(rowlen, CONV_ROWS))
        assert TB % tc == 0 and tc % rowlen == 0
        nx = SSM_INNER // SSM_BC
        in_specs += [
            pl.BlockSpec((SSM_CONV, SSM_INNER), lambda b, i: (0, 0)),
            pl.BlockSpec((SSM_CONV, SSM_BC), lambda b, i: (0, nx)),
            pl.BlockSpec((SSM_CONV, SSM_BC), lambda b, i: (0, nx + 1)),
            pl.BlockSpec((1, SSM_INNER), lambda b, i: (0, 0)),
            pl.BlockSpec((1, SSM_BC), lambda b, i: (0, nx)),
            pl.BlockSpec((1, SSM_BC), lambda b, i: (0, nx + 1)),
            _const_spec((SSM_CONV - 1, tc, tc)),
        ]
        args += [conv_w, conv_w, conv_w, conv_b, conv_b, conv_b, _conv_shift_mats(tc, rowlen)]
    in_specs += [_const_spec((1, P_SMALL)), _const_spec((1, P_SMALL)), _const_spec((C, C)),
                 pl.BlockSpec((1, G, SSM_STATE, SSM_GW), lambda b, i: (b, 0, 0, 0))]
    args += [dt_bias, a_log, tri, s0]
    if finalize:
        z_a, z_o = z_src
        in_specs += [
            pl.BlockSpec((1, TB, SSM_INNER), lambda b, i: (b, tmap(i), z_o)),
            pl.BlockSpec((1, TB, SSM_INNER), lambda b, i: (b, tmap(i), 0)),
            _const_spec((1, SSM_INNER)), _const_spec((1, SSM_INNER)),
        ]
        args += [z_a, y_prev, d_skip_x, norm_w]
    out_specs, out_shape = [], []
    if need_out:
        out_specs += [pl.BlockSpec((1, TB, SSM_INNER), lambda b, i: (b, tmap(i), 0))]
        out_shape += [jax.ShapeDtypeStruct((B, L, SSM_INNER), BF16 if finalize else F32)]
    out_specs += [pl.BlockSpec((1, G, SSM_STATE, SSM_GW), lambda b, i: (b, 0, 0, 0))]
    out_shape += [jax.ShapeDtypeStruct((B, G, SSM_STATE, SSM_GW), F32)]
    if emit:
        out_specs += [pl.BlockSpec((1, TB, SSM_INNER), lambda b, i: (b, tmap(i), 0)),
                      pl.BlockSpec((1, TB, SSM_BC), lambda b, i: (b, tmap(i), 0)),
                      pl.BlockSpec((1, TB, SSM_BC), lambda b, i: (b, tmap(i), 0))]
        out_shape += [jax.ShapeDtypeStruct((B, L, SSM_INNER), BF16),
                      jax.ShapeDtypeStruct((B, L, SSM_BC), BF16),
                      jax.ShapeDtypeStruct((B, L, SSM_BC), BF16)]
    kern = functools.partial(_ssd_kernel, C=C, TB=TB, reverse=reverse, finalize=finalize,
                             conv=conv, emit=emit, lane0=S_DTB if reverse else S_DTF, need_out=need_out)
    return pl.pallas_call(
        kern,
        grid=(B, nb),
        in_specs=in_specs,
        out_specs=out_specs,
        out_shape=out_shape,
        scratch_shapes=[pltpu.VMEM((G, SSM_STATE, SSM_GW), F32),
                        pltpu.VMEM((TB, SSM_INNER), BF16),
                        pltpu.VMEM((TB, SSM_BC), BF16),
                        pltpu.VMEM((TB, SSM_BC), BF16)]
                       + ([pltpu.VMEM((TB, SSM_INNER), F32)] if need_out else []),
        compiler_params=_params(("parallel", "arbitrary")),
        name="ssd_" + ("rev" if reverse else "fwd") + ("_fin" if finalize else ""),
    )(*args)


def _merge_kernel(of_ref, ob_ref, r_ref, gnw_ref, sb_ref, ga_ref, gb_ref, x_ref, g1_ref, wpa_ref, wpb_ref,
                  wout_ref, h_ref):
    o = of_ref[0] + ob_ref[0]
    heads = []
    for h in range(GLA_HEADS):
        oh = o[:, h * GLA_DV:(h + 1) * GLA_DV]
        heads.append(oh * lax.rsqrt(jnp.mean(oh * oh, axis=-1, keepdims=True) + EPS) * gnw_ref[...])
    oa = (jnp.concatenate(heads, axis=1) * _silu(r_ref[0].astype(F32))).astype(BF16)
    ya = _dot(oa, wpa_ref[...])
    yb = _dot(sb_ref[0], wpb_ref[...])
    m = _sigmoid(ga_ref[0].astype(F32)) * ya + _sigmoid(gb_ref[0].astype(F32)) * yb
    mix = _dot(m.astype(BF16), wout_ref[...])
    h_ref[0] = x_ref[0] + g1_ref[0] * mix


def _merge(o_f, o_b, gla_norm_w, s_b, pm, x, g1, w_pa, w_pb, w_out, tm):
    B, L, _ = x.shape
    gab, gbb, rb = P_GA // D_MODEL, P_GB // D_MODEL, P_R // GLA_V
    return pl.pallas_call(
        _merge_kernel,
        grid=(B, L // tm),
        in_specs=[pl.BlockSpec((1, tm, GLA_V), lambda b, i: (b, i, 0)),
                  pl.BlockSpec((1, tm, GLA_V), lambda b, i: (b, i, 0)),
                  pl.BlockSpec((1, tm, GLA_V), lambda b, i: (b, i, rb)),
                  _const_spec((1, GLA_DV)),
                  pl.BlockSpec((1, tm, SSM_INNER), lambda b, i: (b, i, 0)),
                  pl.BlockSpec((1, tm, D_MODEL), lambda b, i: (b, i, gab)),
                  pl.BlockSpec((1, tm, D_MODEL), lambda b, i: (b, i, gbb)),
                  pl.BlockSpec((1, tm, D_MODEL), lambda b, i: (b, i, 0)),
                  pl.BlockSpec((1, 1, D_MODEL), lambda b, i: (b, 0, 0)),
                  _const_spec(w_pa.shape), _const_spec(w_pb.shape), _const_spec(w_out.shape)],
        out_specs=pl.BlockSpec((1, tm, D_MODEL), lambda b, i: (b, i, 0)),
        out_shape=jax.ShapeDtypeStruct((B, L, D_MODEL), F32),
        compiler_params=_params(("parallel", "parallel")),
        name="merge",
    )(o_f, o_b, pm, gla_norm_w, s_b, pm, pm, x, g1, w_pa, w_pb, w_out)


FFN_CHUNK = 256


def _ffn_kernel(h_ref, mod_ref, n2_ref, fw_ref, wg_ref, wu_ref, wd_ref, o_ref):
    h = h_ref[0]
    hn = h * lax.rsqrt(jnp.mean(h * h, axis=-1, keepdims=True) + EPS) * n2_ref[...]
    hn = (hn * (1.0 + mod_ref[0, 1:2, :]) + mod_ref[0, 0:1, :]).astype(BF16)
    acc = jnp.zeros(h.shape, F32)
    for c in range(D_FF // FFN_CHUNK):
        cols = slice(c * FFN_CHUNK, (c + 1) * FFN_CHUNK)
        gt = _dot(hn, wg_ref[:, cols])
        up = _dot(hn, wu_ref[:, cols])
        acc = acc + _dot((_silu(gt) * up).astype(BF16), wd_ref[cols, :])
    h2 = h + mod_ref[0, 2:3, :] * acc
    o_ref[0] = h2 * lax.rsqrt(jnp.mean(h2 * h2, axis=-1, keepdims=True) + EPS) * fw_ref[...]


def _ffn(h, mod, n2w, fw, w_gate, w_up, w_down, tm):
    B, L, _ = h.shape
    return pl.pallas_call(
        _ffn_kernel,
        grid=(B, L // tm),
        in_specs=[pl.BlockSpec((1, tm, D_MODEL), lambda b, i: (b, i, 0)),
                  pl.BlockSpec((1, 3, D_MODEL), lambda b, i: (b, 0, 0)),
                  _const_spec((1, D_MODEL)), _const_spec((1, D_MODEL)),
                  pl.BlockSpec(w_gate.shape, lambda b, i: (0, 0), pipeline_mode=pl.Buffered(1)),
                  pl.BlockSpec(w_up.shape, lambda b, i: (0, 0), pipeline_mode=pl.Buffered(1)),
                  pl.BlockSpec(w_down.shape, lambda b, i: (0, 0), pipeline_mode=pl.Buffered(1))],
        out_specs=pl.BlockSpec((1, tm, D_MODEL), lambda b, i: (b, i, 0)),
        out_shape=jax.ShapeDtypeStruct((B, L, D_MODEL), F32),
        compiler_params=_params(("parallel", "parallel")),
        name="ffn",
    )(h, mod, n2w, fw, w_gate, w_up, w_down)


def _pick_block(L, pref):
    tb = min(L, pref)
    assert L % tb == 0
    return tb


def kernel(x, c, ctx, c_ctx, w_ada, b_ada, norm1_w, w_in, gla_up_f, gla_bias_f, gla_up_b, gla_bias_b,
           gla_norm_w, conv_w, conv_b, dt_bias_f, dt_bias_b, a_log_f, a_log_b, d_skip, ssm_norm_w,
           w_pa, w_pb, w_out, norm2_w, w_gate, w_up, w_down, final_norm_w):
    B, L, D = x.shape
    Lc = ctx.shape[1]
    depth = w_ada.shape[0]
    assert depth == 1 and D == D_MODEL
    assert L % GRID_W == 0 and L % SSD_CHUNK == 0 and Lc % SSD_CHUNK == 0
    lay = 0

    nrow = -(-(B + 1) // 8) * 8
    cc = jnp.zeros((nrow, D), F32).at[:B].set(c).at[B].set(c_ctx)
    ada = _ada(cc, w_ada, b_ada[lay][None, :], lay)
    sh1, sc1, g1, sh2, sc2, g2 = [ada[:, i * D:(i + 1) * D] for i in range(6)]
    mod1 = jnp.stack([sh1[:B], sc1[:B]], axis=1)
    mod1_c = jnp.broadcast_to(jnp.stack([sh1[B], sc1[B]])[None], (B, 2, D))
    mod2 = jnp.stack([sh2[:B], sc2[:B], g2[:B]], axis=1)
    g1_l = g1[:B, None, :]

    wt = jnp.swapaxes(w_in[lay], 0, 1)
    w_main, w_small = _regroup(wt)
    nw1 = norm1_w[lay][None, :]

    at_lanes = lambda p, off: jnp.zeros((1, P_SMALL), F32).at[0, off:off + SSM_HEADS].set(p[lay])
    dtb_f, al_f = at_lanes(dt_bias_f, S_DTF), at_lanes(a_log_f, S_DTF)
    dtb_b, al_b = at_lanes(dt_bias_b, S_DTB), at_lanes(a_log_b, S_DTB)
    cw, cb_ = conv_w[lay], conv_b[lay][None, :]
    up_f, up_b = gla_up_f[lay], gla_up_b[lay]
    bi_f, bi_b = gla_bias_f[lay][None, :], gla_bias_b[lay][None, :]

    gla_zero = jnp.zeros((B, GLA_HEADS, GLA_DK, GLA_DV), F32)
    ssd_zero = jnp.zeros((B, SSM_GROUPS, SSM_STATE, SSM_GW), F32)

    pm_c, ps_c = _inproj(ctx, mod1_c, nw1, w_main, w_small, _pick_block(Lc, INPROJ_TM))
    sg_f, sg_b = _gla_bidir(pm_c, ps_c, (up_f, up_b), (bi_f, bi_b), (gla_zero, gla_zero),
                            TB=_pick_block(Lc, GLA_TB), need_out=False)
    src_c = ((pm_c, P_XS // SSM_INNER), (pm_c, P_BM // SSM_BC), (pm_c, P_CM // SSM_BC))
    ss_f, = _ssd_pass(src_c, ps_c, cw, cb_, dtb_f, al_f, ssd_zero, reverse=False, finalize=False,
                      TB=Lc, rowlen=Lc, need_out=False)
    ss_b, = _ssd_pass(src_c, ps_c, cw, cb_, dtb_b, al_b, ssd_zero, reverse=True, finalize=False,
                      TB=Lc, rowlen=Lc, need_out=False)

    pm, ps = _inproj(x, mod1, nw1, w_main, w_small, _pick_block(L, INPROJ_TM))
    tbg = _pick_block(L, GLA_TB)
    og_f, _, og_b, _ = _gla_bidir(pm, ps, (up_f, up_b), (bi_f, bi_b), (sg_f, sg_b), TB=tbg)
    tbs = _pick_block(L, SSD_TB)
    src = ((pm, P_XS // SSM_INNER), (pm, P_BM // SSM_BC), (pm, P_CM // SSM_BC))
    ys_f, _, xc, bc, cc = _ssd_pass(src, ps, cw, cb_, dtb_f, al_f, ss_f, reverse=False, finalize=False,
                                    TB=tbs, rowlen=GRID_W, emit=True)
    o_b, _ = _ssd_pass(((xc, 0), (bc, 0), (cc, 0)), ps, None, None, dtb_b, al_b, ss_b, reverse=True,
                       finalize=True, TB=tbs, rowlen=GRID_W, y_prev=ys_f, z_src=(pm, P_Z // SSM_INNER),
                       d_skip_x=jnp.repeat(d_skip[lay], SSM_HEADDIM)[None, :],
                       norm_w=ssm_norm_w[lay][None, :])

    h = _merge(og_f, og_b, gla_norm_w[lay][None, :], o_b, pm, x, g1_l, w_pa[lay].astype(BF16),
               w_pb[lay].astype(BF16), w_out[lay].astype(BF16), _pick_block(L, MERGE_TM))
    return _ffn(h, mod2, norm2_w[lay][None, :], final_norm_w[None, :], w_gate[lay].astype(BF16),
                w_up[lay].astype(BF16), w_down[lay].astype(BF16), _pick_block(L, FFN_TM))
```

```python
import functools

import numpy as np
import jax
import jax.numpy as jnp
from jax import lax
from jax.experimental import pallas as pl
from jax.experimental.pallas import tpu as pltpu

F32 = jnp.float32
BF16 = jnp.bfloat16

D_MODEL = 1024
GRID_W = 64
EPS = 1e-6

GLA_HEADS = 4
GLA_DK = 128
GLA_DV = 256
GLA_QK = GLA_HEADS * GLA_DK
GLA_V = GLA_HEADS * GLA_DV
GLA_RANK = 16
GLA_TAU = 16.0

SSM_INNER = 2 * D_MODEL
SSM_HEADDIM = 64
SSM_HEADS = SSM_INNER // SSM_HEADDIM
SSM_GROUPS = 4
SSM_HPG = SSM_HEADS // SSM_GROUPS
SSM_STATE = 128
SSM_BC = SSM_GROUPS * SSM_STATE
SSM_CONV = 4
CONV_LEFT = 2
SSM_GW = SSM_HPG * SSM_HEADDIM

D_FF = ((8 * D_MODEL // 3 + 255) // 256) * 256

_IN_WIDTHS = (GLA_QK, GLA_QK, GLA_V, GLA_V, GLA_RANK, GLA_RANK,
              SSM_INNER, SSM_INNER, SSM_BC, SSM_BC, SSM_HEADS, SSM_HEADS, D_MODEL, D_MODEL)
_IN_OFF = np.concatenate([[0], np.cumsum(_IN_WIDTHS)]).tolist()

P_Z, P_XS = 0, 2048
P_Q, P_K, P_V, P_R = 4096, 4608, 5120, 6144
P_BM, P_CM = 7168, 7680
P_GA, P_GB = 8192, 9216
P_MAIN = 10240
S_LRF, S_LRB, S_DTF, S_DTB = 0, 16, 32, 64
P_SMALL = 128

GLA_CHUNK = 128
SSD_CHUNK = 128

ADA_TN = 1536
INPROJ_TM, INPROJ_TN = 1024, 2560
GLA_TB = 2048
SSD_TB = 512
MERGE_TM = 512
FFN_TM = 512

VMEM_LIMIT = 56 * 1024 * 1024
NEG_BIG = -1e30
LOG2E = 1.4426950408889634


def _sigmoid(x):
    return 1.0 / (1.0 + jnp.exp(-x))


def _silu(x):
    return x * _sigmoid(x)


def _softplus(x):
    return jnp.maximum(x, 0.0) + jnp.log(1.0 + jnp.exp(-jnp.abs(x)))


def _split3(x):
    hi = x.astype(BF16)
    r1 = x - hi.astype(F32)
    mid = r1.astype(BF16)
    lo = (r1 - mid.astype(F32)).astype(BF16)
    return hi, mid, lo


def _dot(a, b):
    return jnp.dot(a, b, preferred_element_type=F32)


def _dot_nt(a, b):
    return lax.dot_general(a, b, (((1,), (1,)), ((), ())), preferred_element_type=F32)


def _dot_tn(a, b):
    return lax.dot_general(a, b, (((0,), (0,)), ((), ())), preferred_element_type=F32)


def _dot01(m01, x):
    hi, mid, lo = _split3(x)
    return _dot(m01, hi) + _dot(m01, mid) + _dot(m01, lo)


def _params(sem):
    return pltpu.CompilerParams(dimension_semantics=sem, vmem_limit_bytes=VMEM_LIMIT)


def _const_spec(shape):
    n = len(shape)
    return pl.BlockSpec(shape, lambda *_: (0,) * n)


def _ada_kernel(c_ref, w_ref, b_ref, o_ref):
    s = _silu(c_ref[...])
    w = w_ref[0]
    s_hi = s.astype(BF16)
    s_lo = (s - s_hi.astype(F32)).astype(BF16)
    w_hi = w.astype(BF16)
    w_lo = (w - w_hi.astype(F32)).astype(BF16)
    o_ref[...] = _dot(s_hi, w_hi) + _dot(s_hi, w_lo) + _dot(s_lo, w_hi) + b_ref[...]


def _ada(cc, w, b, lay):
    rows = cc.shape[0]
    n = w.shape[2]
    tn = ADA_TN
    return pl.pallas_call(
        _ada_kernel,
        grid=(n // tn,),
        in_specs=[pl.BlockSpec((rows, D_MODEL), lambda j: (0, 0)),
                  pl.BlockSpec((1, D_MODEL, tn), lambda j: (lay, 0, j)),
                  pl.BlockSpec((1, tn), lambda j: (0, j))],
        out_specs=pl.BlockSpec((rows, tn), lambda j: (0, j)),
        out_shape=jax.ShapeDtypeStruct((rows, n), F32),
        compiler_params=_params(("arbitrary",)),
        name="ada",
    )(cc, w, b)


REGROUP_ROWS = 1024
REGROUP_ALIGN = 16
_MAIN_RUNS = ((P_Z, _IN_OFF[6], 2 * SSM_INNER), (P_Q, _IN_OFF[0], 2 * GLA_QK + 2 * GLA_V),
              (P_BM, _IN_OFF[8], 2 * SSM_BC), (P_GA, _IN_OFF[12], 2 * D_MODEL))


def _regroup_kernel(wt_ref, lr_ref, dt_ref, wm_ref, ws_ref):
    wm_ref[...] = jnp.transpose(wt_ref[...]).astype(BF16)
    pad = jnp.zeros((P_SMALL - lr_ref.shape[0] - dt_ref.shape[0], lr_ref.shape[1]), F32)
    ws_ref[...] = jnp.transpose(jnp.concatenate([lr_ref[...], dt_ref[...], pad], axis=0)).astype(BF16)


def _regroup(wt):
    _, D = wt.shape
    R = REGROUP_ROWS
    U = REGROUP_ALIGN
    assert all(dst % R == 0 and n % R == 0 and src % U == 0 for dst, src, n in _MAIN_RUNS)

    def src_row(i):
        units = jnp.int32(0)
        for dst, src, n in _MAIN_RUNS:
            inside = (i * R >= dst) & (i * R < dst + n)
            units = jnp.where(inside, (src - dst) // U + i * (R // U), units)
        return units * U

    return pl.pallas_call(
        _regroup_kernel,
        grid=(P_MAIN // R,),
        in_specs=[pl.BlockSpec((pl.Element(R), pl.Element(D)), lambda i: (src_row(i), 0)),
                  pl.BlockSpec((pl.Element(2 * GLA_RANK), pl.Element(D)), lambda i: (_IN_OFF[4], 0)),
                  pl.BlockSpec((pl.Element(2 * SSM_HEADS), pl.Element(D)), lambda i: (_IN_OFF[10], 0))],
        out_specs=[pl.BlockSpec((D, R), lambda i: (0, i)),
                   pl.BlockSpec((D, P_SMALL), lambda i: (0, 0))],
        out_shape=[jax.ShapeDtypeStruct((D, P_MAIN), BF16), jax.ShapeDtypeStruct((D, P_SMALL), BF16)],
        compiler_params=_params(("arbitrary",)),
        name="regroup",
    )(wt, wt, wt)


def _inproj_kernel(x_ref, mod_ref, nw_ref, wm_ref, ws_ref, om_ref, os_ref, xn_ref):
    @pl.when(pl.program_id(2) == 0)
    def _():
        x = x_ref[0]
        y = x * lax.rsqrt(jnp.mean(x * x, axis=-1, keepdims=True) + EPS) * nw_ref[...]
        y = y * (1.0 + mod_ref[0, 1:2, :]) + mod_ref[0, 0:1, :]
        xn = y.astype(BF16)
        xn_ref[...] = xn
        os_ref[0] = _dot(xn, ws_ref[...])

    om_ref[0] = _dot(xn_ref[...], wm_ref[...]).astype(BF16)


def _inproj(x, mod, nw, w_main, w_small, tm):
    B, L, _ = x.shape
    tn = INPROJ_TN
    return pl.pallas_call(
        _inproj_kernel,
        grid=(B, L // tm, P_MAIN // tn),
        in_specs=[pl.BlockSpec((1, tm, D_MODEL), lambda b, i, j: (b, i, 0)),
                  pl.BlockSpec((1, 2, D_MODEL), lambda b, i, j: (b, 0, 0)),
                  pl.BlockSpec((1, D_MODEL), lambda b, i, j: (0, 0)),
                  pl.BlockSpec((D_MODEL, tn), lambda b, i, j: (0, j)),
                  pl.BlockSpec((D_MODEL, P_SMALL), lambda b, i, j: (0, 0))],
        out_specs=[pl.BlockSpec((1, tm, tn), lambda b, i, j: (b, i, j)),
                   pl.BlockSpec((1, tm, P_SMALL), lambda b, i, j: (b, i, 0))],
        out_shape=[jax.ShapeDtypeStruct((B, L, P_MAIN), BF16),
                   jax.ShapeDtypeStruct((B, L, P_SMALL), F32)],
        scratch_shapes=[pltpu.VMEM((tm, D_MODEL), BF16)],
        compiler_params=_params(("parallel", "parallel", "arbitrary")),
        name="inproj",
    )(x, mod, nw, w_main, w_small)


GLA_MXU_LEVELS = 1
GLA_GROUP = 4


def _gla_consts(C, reverse):
    NL = int(np.log2(C))
    idx = np.arange(C)
    tri = (idx[None, :] <= idx[:, None]).astype(np.float32)
    mats = [tri]
    masks = [np.eye(C, dtype=np.float32)]
    refs, signs = [], []
    for lev in range(NL):
        h = 1 << lev
        blk = idx // (2 * h)
        half = (idx // h) % 2
        ref = blk * 2 * h + h - 1
        if lev < GLA_MXU_LEVELS:
            d = tri - tri[ref]
            d[half == 0] *= -1.0
            mats.append(d)
        else:
            sg = np.where(half == 1, 1.0, -1.0).astype(np.float32)
            if reverse:
                ref, sg = (C - 1 - ref)[::-1], sg[::-1]
            refs.append([int(ref[m * 2 * h]) for m in range(C // (2 * h))])
            signs.append(np.broadcast_to(sg[:, None], (C, GLA_DK)))
        masks.append(((blk[:, None] == blk[None, :]) & (half[:, None] == 1)
                      & (half[None, :] == 0)).astype(np.float32))
    if reverse:
        mats = [m[::-1, ::-1] for m in mats]
        masks = [m[::-1, ::-1] for m in masks]
    dmat = np.concatenate(mats, axis=0)
    return (jnp.asarray(dmat, BF16), jnp.asarray(np.stack(masks), F32),
            jnp.asarray(np.stack(signs), F32), refs)


def _gla_stream(q_ref, k_ref, v_ref, lr_ref, up_ref, bias_ref, dm_ref, mk_ref, sg_ref, s0_ref, *rest,
                C, TB, reverse, refs, part, need_out):
    if need_out:
        o_ref, sf_ref, st_ref, e_s, b_s, oi_s, kv_s = rest
    else:
        sf_ref, st_ref, e_s, b_s, kv_s = rest
    NL = int(np.log2(C))
    nchunk = TB // C
    blk = pl.program_id(2)
    last = 0 if reverse else C - 1
    lr_off = S_LRB if reverse else S_LRF

    if part == "init":
        @pl.when(blk == 0)
        def _():
            st_ref[...] = s0_ref[0, 0]
        return
    if part == "final":
        @pl.when(blk == pl.num_programs(2) - 1)
        def _():
            sf_ref[0, 0] = st_ref[...]
        return

    lr = lr_ref[0][:, lr_off:lr_off + GLA_RANK]
    pre = _dot(lr.astype(BF16), up_ref[...].astype(BF16)) + bias_ref[...]
    g = -_softplus(-pre) * (1.0 / GLA_TAU)
    g_hi = g.astype(BF16)
    g_lo = (g - g_hi.astype(F32)).astype(BF16)
    gs = jnp.concatenate([g_hi, g_lo], axis=1)

    for c in range(nchunk):
        ex = _dot(dm_ref[...], gs[c * C:(c + 1) * C])
        ex = ex[:, :GLA_DK] + ex[:, GLA_DK:]
        b = ex[0:C]
        b_s[c] = b
        if need_out:
            e_s[c, 0:GLA_MXU_LEVELS * C, :] = jnp.exp(ex[C:(1 + GLA_MXU_LEVELS) * C])
            for li, lev in enumerate(range(GLA_MXU_LEVELS, NL)):
                h2 = 2 << lev
                bref = jnp.concatenate([jnp.broadcast_to(b[r:r + 1, :], (h2, GLA_DK)) for r in refs[li]],
                                       axis=0)
                e_s[c, lev * C:(lev + 1) * C, :] = jnp.exp(sg_ref[li] * (b - bref))
        e_s[c, NL * C:(NL + 1) * C, :] = jnp.exp(b[last:last + 1, :] - b)

    for c in range(nchunk):
        rows = slice(c * C, (c + 1) * C)
        if need_out:
            q = q_ref[0, rows, :].astype(F32) * (GLA_DK ** -0.5)
        k = k_ref[0, rows, :].astype(F32)
        v = v_ref[0, rows, :]
        if need_out:
            att = mk_ref[0] * _dot_nt(q.astype(BF16), k.astype(BF16))
            for lev in range(NL):
                e_l = e_s[c, lev * C:(lev + 1) * C, :]
                att = att + mk_ref[lev + 1] * _dot_nt((q * e_l).astype(BF16), (k * e_l).astype(BF16))
            oi_s[rows, :] = _dot(att.astype(BF16), v)
        kv_s[c] = _dot_tn((k * e_s[c, NL * C:(NL + 1) * C, :]).astype(BF16), v)

    order = list(reversed(range(nchunk))) if reverse else list(range(nchunk))
    st = st_ref[...]
    for g0 in range(0, nchunk, GLA_GROUP):
        grp = order[g0:g0 + GLA_GROUP]
        if need_out:
            wcat = jnp.concatenate([st.astype(BF16)] + [kv_s[c].astype(BF16) for c in grp[:-1]], axis=0)
        for i, c in enumerate(grp if need_out else ()):
            rows = slice(c * C, (c + 1) * C)
            q = q_ref[0, rows, :].astype(F32) * (GLA_DK ** -0.5)
            expo = b_s[c]
            pieces = []
            for j in range(i - 1, -2, -1):
                pieces.insert(0, (q * jnp.exp(expo)).astype(BF16))
                if j >= 0:
                    expo = expo + b_s[grp[j], last:last + 1, :]
            lhs = jnp.concatenate(pieces, axis=1)
            o_ref[0, rows, :] = oi_s[rows, :] + _dot(lhs, wcat[:(i + 1) * GLA_DK, :])
        for c in grp:
            dec = jnp.transpose(jnp.broadcast_to(jnp.exp(b_s[c, last:last + 1, :]), (GLA_DK, GLA_DK)))
            st = jnp.concatenate([dec] * (GLA_DV // GLA_DK), axis=1) * st + kv_s[c]
    st_ref[...] = st


GLA_STREAM_IN = 10


def _gla_kernel(*refs, C, TB, lv_refs, need_out):
    n_in = GLA_STREAM_IN
    n_out, n_scr = (2, 5) if need_out else (1, 4)
    ins = [refs[d * n_in:(d + 1) * n_in] for d in range(2)]
    outs = [refs[2 * n_in + d * n_out:2 * n_in + (d + 1) * n_out] for d in range(2)]
    base = 2 * (n_in + n_out)
    scr = [refs[base + d * n_scr:base + (d + 1) * n_scr] for d in range(2)]
    for part in ("init", "body", "final"):
        for d, reverse in enumerate((False, True)):
            _gla_stream(*ins[d], *outs[d], *scr[d], C=C, TB=TB, reverse=reverse, refs=lv_refs[d], part=part,
                        need_out=need_out)


def _gla_bidir(pm, ps, up, bias, s0, *, TB, need_out=True):
    B, L, _ = pm.shape
    C = GLA_CHUNK
    nb = L // TB
    assert C >= (2 << GLA_MXU_LEVELS)
    qb, kb, vb = P_Q // GLA_DK, P_K // GLA_DK, P_V // GLA_DV
    in_specs, args, out_specs, out_shape, scratch, lv_refs = [], [], [], [], [], []
    for d, reverse in enumerate((False, True)):
        dmat, masks, signs, refs = _gla_consts(C, reverse)
        lv_refs.append(refs)
        tmap = (lambda i: nb - 1 - i) if reverse else (lambda i: i)
        in_specs += [
            pl.BlockSpec((1, TB, GLA_DK), lambda b, h, i, tmap=tmap: (b, tmap(i), qb + h)),
            pl.BlockSpec((1, TB, GLA_DK), lambda b, h, i, tmap=tmap: (b, tmap(i), kb + h)),
            pl.BlockSpec((1, TB, GLA_DV), lambda b, h, i, tmap=tmap: (b, tmap(i), vb + h)),
            pl.BlockSpec((1, TB, P_SMALL), lambda b, h, i, tmap=tmap: (b, tmap(i), 0)),
            pl.BlockSpec((GLA_RANK, GLA_DK), lambda b, h, i: (0, h)),
            pl.BlockSpec((1, GLA_DK), lambda b, h, i: (0, h)),
            _const_spec(dmat.shape),
            _const_spec(masks.shape),
            _const_spec(signs.shape),
            pl.BlockSpec((1, 1, GLA_DK, GLA_DV), lambda b, h, i: (b, h, 0, 0)),
        ]
        args += [pm, pm, pm, ps, up[d], bias[d], dmat, masks, signs, s0[d]]
        if need_out:
            out_specs += [pl.BlockSpec((1, TB, GLA_DV), lambda b, h, i, tmap=tmap: (b, tmap(i), h))]
            out_shape += [jax.ShapeDtypeStruct((B, L, GLA_V), F32)]
        out_specs += [pl.BlockSpec((1, 1, GLA_DK, GLA_DV), lambda b, h, i: (b, h, 0, 0))]
        out_shape += [jax.ShapeDtypeStruct((B, GLA_HEADS, GLA_DK, GLA_DV), F32)]
        scratch += [pltpu.VMEM((GLA_DK, GLA_DV), F32),
                    pltpu.VMEM((TB // C, masks.shape[0] * C, GLA_DK), F32),
                    pltpu.VMEM((TB // C, C, GLA_DK), F32)]
        if need_out:
            scratch += [pltpu.VMEM((TB, GLA_DV), F32)]
        scratch += [pltpu.VMEM((TB // C, GLA_DK, GLA_DV), F32)]
    assert len(in_specs) == 2 * GLA_STREAM_IN
    return pl.pallas_call(
        functools.partial(_gla_kernel, C=C, TB=TB, lv_refs=lv_refs, need_out=need_out),
        grid=(B, GLA_HEADS, nb),
        in_specs=in_specs,
        out_specs=out_specs,
        out_shape=out_shape,
        scratch_shapes=scratch,
        compiler_params=_params(("parallel", "parallel", "arbitrary")),
        name="gla",
    )(*args)


CONV_ROWS = 256


def _conv_shift_mats(T, rowlen):
    t = np.arange(T)
    mats = []
    for j in range(SSM_CONV):
        off = j - CONV_LEFT
        if off == 0:
            continue
        src = t + off
        ok = (src // rowlen == t // rowlen) & (src >= 0) & (src < T)
        m = np.zeros((T, T), np.float32)
        m[t[ok], src[ok]] = 1.0
        mats.append(m)
    return jnp.asarray(np.stack(mats), BF16)


def _conv_silu(u, w, b, sh_ref):
    acc = b + u.astype(F32) * w[CONV_LEFT:CONV_LEFT + 1, :]
    taps = [j for j in range(SSM_CONV) if j != CONV_LEFT]
    for i, j in enumerate(taps):
        acc = acc + _dot(sh_ref[i], u) * w[j:j + 1, :]
    return _silu(acc)


def _ssd_kernel(*refs, C, TB, reverse, finalize, conv, emit, lane0, need_out):
    refs = list(refs)
    xs_ref, bm_ref, cm_ref, ps_ref = refs[:4]
    refs = refs[4:]
    if conv:
        wx_ref, wb_ref, wc_ref, bx_ref, bb_ref, bc_ref, sh_ref = refs[:7]
        refs = refs[7:]
    dtb_ref, alog_ref, tri_ref, s0_ref = refs[:4]
    refs = refs[4:]
    if finalize:
        z_ref, yp_ref, dsk_ref, nw_ref = refs[:4]
        refs = refs[4:]
    if need_out:
        y_ref = refs[0]
        refs = refs[1:]
    sf_ref = refs[0]
    refs = refs[1:]
    if emit:
        xo_ref, bo_ref, co_ref = refs[:3]
        refs = refs[3:]
    if need_out:
        st_ref, xc_s, bc_s, cc_s, ya_s = refs
    else:
        st_ref, xc_s, bc_s, cc_s = refs
    nchunk = TB // C
    blk = pl.program_id(1)
    G, HPG, P, N = SSM_GROUPS, SSM_HPG, SSM_HEADDIM, SSM_STATE

    @pl.when(blk == 0)
    def _():
        st_ref[...] = s0_ref[0]

    if conv:
        tc = sh_ref.shape[1]
        for r0 in range(0, TB, tc):
            rs = slice(r0, r0 + tc)
            xc_s[rs, :] = _conv_silu(xs_ref[0, rs, :], wx_ref[...], bx_ref[...], sh_ref).astype(BF16)
            bc_s[rs, :] = _conv_silu(bm_ref[0, rs, :], wb_ref[...], bb_ref[...], sh_ref).astype(BF16)
            cc_s[rs, :] = _conv_silu(cm_ref[0, rs, :], wc_ref[...], bc_ref[...], sh_ref).astype(BF16)
        if emit:
            xo_ref[0] = xc_s[...]
            bo_ref[0] = bc_s[...]
            co_ref[0] = cc_s[...]
    else:
        xc_s[...] = xs_ref[0]
        bc_s[...] = bm_ref[0]
        cc_s[...] = cm_ref[0]

    neg_a = -jnp.exp(alog_ref[...])
    ti = lax.broadcasted_iota(jnp.int32, (C, C), 0)
    si = lax.broadcasted_iota(jnp.int32, (C, C), 1)
    keep = (si >= ti) if reverse else (si <= ti)
    lo = lax.broadcasted_iota(jnp.int32, (1, 2 * P), 1) < P
    last = 0 if reverse else C - 1
    nheads = G * HPG

    for c in (reversed(range(nchunk)) if reverse else range(nchunk)):
        rows = slice(c * C, (c + 1) * C)
        dt = _softplus(ps_ref[0, rows, :] + dtb_ref[...])
        cum = _dot01(tri_ref[...], dt * neg_a) * LOG2E
        cum_t = jnp.transpose(cum)[lane0:lane0 + nheads, :]
        dt_t = jnp.transpose(dt)[lane0:lane0 + nheads, :]
        w_t = (dt_t * jnp.exp2(cum_t[:, last:last + 1] - cum_t)).astype(BF16)
        cdl_t = cum_t - jnp.log2(dt_t)
        dec_all = jnp.exp2(cum[last:last + 1, :])
        for g in range(G):
            bm_g = bc_s[rows, g * N:(g + 1) * N]
            cm_g = cc_s[rows, g * N:(g + 1) * N]
            if need_out:
                cb = _dot_nt(cm_g, bm_g).astype(BF16)
            bm_t = jnp.transpose(bm_g.astype(F32)).astype(BF16)
            for pr in range(HPG // 2):
                lanes = slice(g * SSM_GW + pr * 2 * P, g * SSM_GW + (pr + 1) * 2 * P)
                x_pair = xc_s[rows, lanes]
                s_pair = st_ref[g, :, pr * 2 * P:(pr + 1) * 2 * P]
                s_bf = s_pair.astype(BF16)
                y = None
                ds = None
                decs = []
                for half in range(2):
                    j = g * HPG + pr * 2 + half
                    lane = lane0 + j
                    sel = lo if half == 0 else jnp.logical_not(lo)
                    if need_out:
                        bc_ = jnp.broadcast_to(cum[:, lane:lane + 1], (C, C))
                        m = cb * jnp.exp2(jnp.where(keep, bc_ - cdl_t[j:j + 1, :], NEG_BIG)).astype(BF16)
                        cd = cm_g * jnp.exp2(bc_).astype(BF16)
                        lhs = jnp.concatenate([m, cd], axis=1)
                        rhs = jnp.concatenate([jnp.where(sel, x_pair, jnp.zeros_like(x_pair)),
                                               jnp.where(sel, s_bf, jnp.zeros_like(s_bf))], axis=0)
                        yh = _dot(lhs, rhs)
                    dh = _dot(bm_t * w_t[j:j + 1, :],
                              jnp.where(sel, x_pair, jnp.zeros_like(x_pair)))
                    if need_out:
                        y = yh if y is None else y + yh
                    ds = dh if ds is None else ds + dh
                    decs.append(dec_all[:, lane:lane + 1])
                if need_out:
                    ya_s[rows, lanes] = y
                dec = jnp.where(lo, decs[0], decs[1])
                st_ref[g, :, pr * 2 * P:(pr + 1) * 2 * P] = dec * s_pair + ds

    if finalize:
        for g in range(G):
            lanes = slice(g * SSM_GW, (g + 1) * SSM_GW)
            y = ya_s[:, lanes] + yp_ref[0, :, lanes] + dsk_ref[:, lanes] * xc_s[:, lanes].astype(F32)
            y = y * _silu(z_ref[0, :, lanes].astype(F32))
            y = y * lax.rsqrt(jnp.mean(y * y, axis=-1, keepdims=True) + EPS) * nw_ref[:, lanes]
            y_ref[0, :, lanes] = y.astype(y_ref.dtype)
    elif need_out:
        y_ref[0] = ya_s[...]

    @pl.when(blk == pl.num_programs(1) - 1)
    def _():
        sf_ref[0] = st_ref[...]


def _ssd_pass(src, ps, conv_w, conv_b, dt_bias, a_log, s0, *, reverse, finalize, TB, rowlen, emit=False,
              need_out=True, y_prev=None, z_src=None, d_skip_x=None, norm_w=None):
    (xs_a, xs_o), (bm_a, bm_o), (cm_a, cm_o) = src
    B, L, _ = ps.shape
    C = SSD_CHUNK
    nb = L // TB
    G = SSM_GROUPS
    conv = conv_w is not None
    tmap = (lambda i: nb - 1 - i) if reverse else (lambda i: i)
    idx = np.arange(C)
    tri = (idx[None, :] >= idx[:, None]) if reverse else (idx[None, :] <= idx[:, None])
    tri = jnp.asarray(tri.astype(np.float32), BF16)
    in_specs = [
        pl.BlockSpec((1, TB, SSM_INNER), lambda b, i: (b, tmap(i), xs_o)),
        pl.BlockSpec((1, TB, SSM_BC), lambda b, i: (b, tmap(i), bm_o)),
        pl.BlockSpec((1, TB, SSM_BC), lambda b, i: (b, tmap(i), cm_o)),
        pl.BlockSpec((1, TB, P_SMALL), lambda b, i: (b, tmap(i), 0)),
    ]
    args = [xs_a, bm_a, cm_a, ps]
    if conv:
        tc = min(TB, max(rowlen, CONV_ROWS))
        assert TB % tc == 0 and tc % rowlen == 0
        nx = SSM_INNER // SSM_BC
        in_specs += [
            pl.BlockSpec((SSM_CONV, SSM_INNER), lambda b, i: (0, 0)),
            pl.BlockSpec((SSM_CONV, SSM_BC), lambda b, i: (0, nx)),
            pl.BlockSpec((SSM_CONV, SSM_BC), lambda b, i: (0, nx + 1)),
            pl.BlockSpec((1, SSM_INNER), lambda b, i: (0, 0)),
            pl.BlockSpec((1, SSM_BC), lambda b, i: (0, nx)),
            pl.BlockSpec((1, SSM_BC), lambda b, i: (0, nx + 1)),
            _const_spec((SSM_CONV - 1, tc, tc)),
        ]
        args += [conv_w, conv_w, conv_w, conv_b, conv_b, conv_b, _conv_shift_mats(tc, rowlen)]
    in_specs += [_const_spec((1, P_SMALL)), _const_spec((1, P_SMALL)), _const_spec((C, C)),
                 pl.BlockSpec((1, G, SSM_STATE, SSM_GW), lambda b, i: (b, 0, 0, 0))]
    args += [dt_bias, a_log, tri, s0]
    if finalize:
        z_a, z_o = z_src
        in_specs += [
            pl.BlockSpec((1, TB, SSM_INNER), lambda b, i: (b, tmap(i), z_o)),
            pl.BlockSpec((1, TB, SSM_INNER), lambda b, i: (b, tmap(i), 0)),
            _const_spec((1, SSM_INNER)), _const_spec((1, SSM_INNER)),
        ]
        args += [z_a, y_prev, d_skip_x, norm_w]
    out_specs, out_shape = [], []
    if need_out:
        out_specs += [pl.BlockSpec((1, TB, SSM_INNER), lambda b, i: (b, tmap(i), 0))]
        out_shape += [jax.ShapeDtypeStruct((B, L, SSM_INNER), BF16 if finalize else F32)]
    out_specs += [pl.BlockSpec((1, G, SSM_STATE, SSM_GW), lambda b, i: (b, 0, 0, 0))]
    out_shape += [jax.ShapeDtypeStruct((B, G, SSM_STATE, SSM_GW), F32)]
    if emit:
        out_specs += [pl.BlockSpec((1, TB, SSM_INNER), lambda b, i: (b, tmap(i), 0)),
                      pl.BlockSpec((1, TB, SSM_BC), lambda b, i: (b, tmap(i), 0)),
                      pl.BlockSpec((1, TB, SSM_BC), lambda b, i: (b, tmap(i), 0))]
        out_shape += [jax.ShapeDtypeStruct((B, L, SSM_INNER), BF16),
                      jax.ShapeDtypeStruct((B, L, SSM_BC), BF16),
                      jax.ShapeDtypeStruct((B, L, SSM_BC), BF16)]
    kern = functools.partial(_ssd_kernel, C=C, TB=TB, reverse=reverse, finalize=finalize,
                             conv=conv, emit=emit, lane0=S_DTB if reverse else S_DTF, need_out=need_out)
    return pl.pallas_call(
        kern,
        grid=(B, nb),
        in_specs=in_specs,
        out_specs=out_specs,
        out_shape=out_shape,
        scratch_shapes=[pltpu.VMEM((G, SSM_STATE, SSM_GW), F32),
                        pltpu.VMEM((TB, SSM_INNER), BF16),
                        pltpu.VMEM((TB, SSM_BC), BF16),
                        pltpu.VMEM((TB, SSM_BC), BF16)]
                       + ([pltpu.VMEM((TB, SSM_INNER), F32)] if need_out else []),
        compiler_params=_params(("parallel", "arbitrary")),
        name="ssd_" + ("rev" if reverse else "fwd") + ("_fin" if finalize else ""),
    )(*args)


def _merge_kernel(of_ref, ob_ref, r_ref, gnw_ref, sb_ref, ga_ref, gb_ref, x_ref, g1_ref, wpa_ref, wpb_ref,
                  wout_ref, h_ref):
    o = of_ref[0] + ob_ref[0]
    heads = []
    for h in range(GLA_HEADS):
        oh = o[:, h * GLA_DV:(h + 1) * GLA_DV]
        heads.append(oh * lax.rsqrt(jnp.mean(oh * oh, axis=-1, keepdims=True) + EPS) * gnw_ref[...])
    oa = (jnp.concatenate(heads, axis=1) * _silu(r_ref[0].astype(F32))).astype(BF16)
    ya = _dot(oa, wpa_ref[...])
    yb = _dot(sb_ref[0], wpb_ref[...])
    m = _sigmoid(ga_ref[0].astype(F32)) * ya + _sigmoid(gb_ref[0].astype(F32)) * yb
    mix = _dot(m.astype(BF16), wout_ref[...])
    h_ref[0] = x_ref[0] + g1_ref[0] * mix


def _merge(o_f, o_b, gla_norm_w, s_b, pm, x, g1, w_pa, w_pb, w_out, tm):
    B, L, _ = x.shape
    gab, gbb, rb = P_GA // D_MODEL, P_GB // D_MODEL, P_R // GLA_V
    return pl.pallas_call(
        _merge_kernel,
        grid=(B, L // tm),
        in_specs=[pl.BlockSpec((1, tm, GLA_V), lambda b, i: (b, i, 0)),
                  pl.BlockSpec((1, tm, GLA_V), lambda b, i: (b, i, 0)),
                  pl.BlockSpec((1, tm, GLA_V), lambda b, i: (b, i, rb)),
                  _const_spec((1, GLA_DV)),
                  pl.BlockSpec((1, tm, SSM_INNER), lambda b, i: (b, i, 0)),
                  pl.BlockSpec((1, tm, D_MODEL), lambda b, i: (b, i, gab)),
                  pl.BlockSpec((1, tm, D_MODEL), lambda b, i: (b, i, gbb)),
                  pl.BlockSpec((1, tm, D_MODEL), lambda b, i: (b, i, 0)),
                  pl.BlockSpec((1, 1, D_MODEL), lambda b, i: (b, 0, 0)),
                  _const_spec(w_pa.shape), _const_spec(w_pb.shape), _const_spec(w_out.shape)],
        out_specs=pl.BlockSpec((1, tm, D_MODEL), lambda b, i: (b, i, 0)),
        out_shape=jax.ShapeDtypeStruct((B, L, D_MODEL), F32),
        compiler_params=_params(("parallel", "parallel")),
        name="merge",
    )(o_f, o_b, pm, gla_norm_w, s_b, pm, pm, x, g1, w_pa, w_pb, w_out)


FFN_CHUNK = 256


def _ffn_kernel(h_ref, mod_ref, n2_ref, fw_ref, wg_ref, wu_ref, wd_ref, o_ref):
    h = h_ref[0]
    hn = h * lax.rsqrt(jnp.mean(h * h, axis=-1, keepdims=True) + EPS) * n2_ref[...]
    hn = (hn * (1.0 + mod_ref[0, 1:2, :]) + mod_ref[0, 0:1, :]).astype(BF16)
    acc = jnp.zeros(h.shape, F32)
    for c in range(D_FF // FFN_CHUNK):
        cols = slice(c * FFN_CHUNK, (c + 1) * FFN_CHUNK)
        gt = _dot(hn, wg_ref[:, cols])
        up = _dot(hn, wu_ref[:, cols])
        acc = acc + _dot((_silu(gt) * up).astype(BF16), wd_ref[cols, :])
    h2 = h + mod_ref[0, 2:3, :] * acc
    o_ref[0] = h2 * lax.rsqrt(jnp.mean(h2 * h2, axis=-1, keepdims=True) + EPS) * fw_ref[...]


def _ffn(h, mod, n2w, fw, w_gate, w_up, w_down, tm):
    B, L, _ = h.shape
    return pl.pallas_call(
        _ffn_kernel,
        grid=(B, L // tm),
        in_specs=[pl.BlockSpec((1, tm, D_MODEL), lambda b, i: (b, i, 0)),
                  pl.BlockSpec((1, 3, D_MODEL), lambda b, i: (b, 0, 0)),
                  _const_spec((1, D_MODEL)), _const_spec((1, D_MODEL)),
                  pl.BlockSpec(w_gate.shape, lambda b, i: (0, 0), pipeline_mode=pl.Buffered(1)),
                  pl.BlockSpec(w_up.shape, lambda b, i: (0, 0), pipeline_mode=pl.Buffered(1)),
                  pl.BlockSpec(w_down.shape, lambda b, i: (0, 0), pipeline_mode=pl.Buffered(1))],
        out_specs=pl.BlockSpec((1, tm, D_MODEL), lambda b, i: (b, i, 0)),
        out_shape=jax.ShapeDtypeStruct((B, L, D_MODEL), F32),
        compiler_params=_params(("parallel", "parallel")),
        name="ffn",
    )(h, mod, n2w, fw, w_gate, w_up, w_down)


def _pick_block(L, pref):
    tb = min(L, pref)
    assert L % tb == 0
    return tb


def kernel(x, c, ctx, c_ctx, w_ada, b_ada, norm1_w, w_in, gla_up_f, gla_bias_f, gla_up_b, gla_bias_b,
           gla_norm_w, conv_w, conv_b, dt_bias_f, dt_bias_b, a_log_f, a_log_b, d_skip, ssm_norm_w,
           w_pa, w_pb, w_out, norm2_w, w_gate, w_up, w_down, final_norm_w):
    B, L, D = x.shape
    Lc = ctx.shape[1]
    depth = w_ada.shape[0]
    assert depth == 1 and D == D_MODEL
    assert L % GRID_W == 0 and L % SSD_CHUNK == 0 and Lc % SSD_CHUNK == 0
    lay = 0

    nrow = -(-(B + 1) // 8) * 8
    cc = jnp.zeros((nrow, D), F32).at[:B].set(c).at[B].set(c_ctx)
    ada = _ada(cc, w_ada, b_ada[lay][None, :], lay)
    sh1, sc1, g1, sh2, sc2, g2 = [ada[:, i * D:(i + 1) * D] for i in range(6)]
    mod1 = jnp.stack([sh1[:B], sc1[:B]], axis=1)
    mod1_c = jnp.broadcast_to(jnp.stack([sh1[B], sc1[B]])[None], (B, 2, D))
    mod2 = jnp.stack([sh2[:B], sc2[:B], g2[:B]], axis=1)
    g1_l = g1[:B, None, :]

    wt = jnp.swapaxes(w_in[lay], 0, 1)
    w_main, w_small = _regroup(wt)
    nw1 = norm1_w[lay][None, :]

    at_lanes = lambda p, off: jnp.zeros((1, P_SMALL), F32).at[0, off:off + SSM_HEADS].set(p[lay])
    dtb_f, al_f = at_lanes(dt_bias_f, S_DTF), at_lanes(a_log_f, S_DTF)
    dtb_b, al_b = at_lanes(dt_bias_b, S_DTB), at_lanes(a_log_b, S_DTB)
    cw, cb_ = conv_w[lay], conv_b[lay][None, :]
    up_f, up_b = gla_up_f[lay], gla_up_b[lay]
    bi_f, bi_b = gla_bias_f[lay][None, :], gla_bias_b[lay][None, :]

    gla_zero = jnp.zeros((B, GLA_HEADS, GLA_DK, GLA_DV), F32)
    ssd_zero = jnp.zeros((B, SSM_GROUPS, SSM_STATE, SSM_GW), F32)

    pm_c, ps_c = _inproj(ctx, mod1_c, nw1, w_main, w_small, _pick_block(Lc, INPROJ_TM))
    sg_f, sg_b = _gla_bidir(pm_c, ps_c, (up_f, up_b), (bi_f, bi_b), (gla_zero, gla_zero),
                            TB=_pick_block(Lc, GLA_TB), need_out=False)
    src_c = ((pm_c, P_XS // SSM_INNER), (pm_c, P_BM // SSM_BC), (pm_c, P_CM // SSM_BC))
    ss_f, = _ssd_pass(src_c, ps_c, cw, cb_, dtb_f, al_f, ssd_zero, reverse=False, finalize=False,
                      TB=Lc, rowlen=Lc, need_out=False)
    ss_b, = _ssd_pass(src_c, ps_c, cw, cb_, dtb_b, al_b, ssd_zero, reverse=True, finalize=False,
                      TB=Lc, rowlen=Lc, need_out=False)

    pm, ps = _inproj(x, mod1, nw1, w_main, w_small, _pick_block(L, INPROJ_TM))
    tbg = _pick_block(L, GLA_TB)
    og_f, _, og_b, _ = _gla_bidir(pm, ps, (up_f, up_b), (bi_f, bi_b), (sg_f, sg_b), TB=tbg)
    tbs = _pick_block(L, SSD_TB)
    src = ((pm, P_XS // SSM_INNER), (pm, P_BM // SSM_BC), (pm, P_CM // SSM_BC))
    ys_f, _, xc, bc, cc = _ssd_pass(src, ps, cw, cb_, dtb_f, al_f, ss_f, reverse=False, finalize=False,
                                    TB=tbs, rowlen=GRID_W, emit=True)
    o_b, _ = _ssd_pass(((xc, 0), (bc, 0), (cc, 0)), ps, None, None, dtb_b, al_b, ss_b, reverse=True,
                       finalize=True, TB=tbs, rowlen=GRID_W, y_prev=ys_f, z_src=(pm, P_Z // SSM_INNER),
                       d_skip_x=jnp.repeat(d_skip[lay], SSM_HEADDIM)[None, :],
                       norm_w=ssm_norm_w[lay][None, :])

    h = _merge(og_f, og_b, gla_norm_w[lay][None, :], o_b, pm, x, g1_l, w_pa[lay].astype(BF16),
               w_pb[lay].astype(BF16), w_out[lay].astype(BF16), _pick_block(L, MERGE_TM))
    return _ffn(h, mod2, norm2_w[lay][None, :], final_norm_w[None, :], w_gate[lay].astype(BF16),
                w_up[lay].astype(BF16), w_down[lay].astype(BF16), _pick_block(L, FFN_TM))
```

```python
import functools

import numpy as np
import jax
import jax.numpy as jnp
from jax import lax
from jax.experimental import pallas as pl
from jax.experimental.pallas import tpu as pltpu

F32 = jnp.float32
BF16 = jnp.bfloat16

D_MODEL = 1024
GRID_W = 64
EPS = 1e-6

GLA_HEADS = 4
GLA_DK = 128
GLA_DV = 256
GLA_QK = GLA_HEADS * GLA_DK
GLA_V = GLA_HEADS * GLA_DV
GLA_RANK = 16
GLA_TAU = 16.0

SSM_INNER = 2 * D_MODEL
SSM_HEADDIM = 64
SSM_HEADS = SSM_INNER // SSM_HEADDIM
SSM_GROUPS = 4
SSM_HPG = SSM_HEADS // SSM_GROUPS
SSM_STATE = 128
SSM_BC = SSM_GROUPS * SSM_STATE
SSM_CONV = 4
CONV_LEFT = 2
SSM_GW = SSM_HPG * SSM_HEADDIM

D_FF = ((8 * D_MODEL // 3 + 255) // 256) * 256

_IN_WIDTHS = (GLA_QK, GLA_QK, GLA_V, GLA_V, GLA_RANK, GLA_RANK,
              SSM_INNER, SSM_INNER, SSM_BC, SSM_BC, SSM_HEADS, SSM_HEADS, D_MODEL, D_MODEL)
_IN_OFF = np.concatenate([[0], np.cumsum(_IN_WIDTHS)]).tolist()

P_Z, P_XS = 0, 2048
P_Q, P_K, P_V, P_R = 4096, 4608, 5120, 6144
P_BM, P_CM = 7168, 7680
P_GA, P_GB = 8192, 9216
P_MAIN = 10240
S_LRF, S_LRB, S_DTF, S_DTB = 0, 16, 32, 64
P_SMALL = 128

GLA_CHUNK = 128
SSD_CHUNK = 128

ADA_TN = 1536
INPROJ_TM, INPROJ_TN = 1024, 2560
GLA_TB = 2048
SSD_TB = 512
MERGE_TM = 512
FFN_TM = 512

VMEM_LIMIT = 56 * 1024 * 1024
NEG_BIG = -1e30
LOG2E = 1.4426950408889634


def _sigmoid(x):
    return 1.0 / (1.0 + jnp.exp(-x))


def _silu(x):
    return x * _sigmoid(x)


def _softplus(x):
    return jnp.maximum(x, 0.0) + jnp.log(1.0 + jnp.exp(-jnp.abs(x)))


def _split3(x):
    hi = x.astype(BF16)
    r1 = x - hi.astype(F32)
    mid = r1.astype(BF16)
    lo = (r1 - mid.astype(F32)).astype(BF16)
    return hi, mid, lo


def _dot(a, b):
    return jnp.dot(a, b, preferred_element_type=F32)


def _dot_nt(a, b):
    return lax.dot_general(a, b, (((1,), (1,)), ((), ())), preferred_element_type=F32)


def _dot_tn(a, b):
    return lax.dot_general(a, b, (((0,), (0,)), ((), ())), preferred_element_type=F32)


def _dot01(m01, x):
    hi, mid, lo = _split3(x)
    return _dot(m01, hi) + _dot(m01, mid) + _dot(m01, lo)


def _params(sem):
    return pltpu.CompilerParams(dimension_semantics=sem, vmem_limit_bytes=VMEM_LIMIT)


def _const_spec(shape):
    n = len(shape)
    return pl.BlockSpec(shape, lambda *_: (0,) * n)


def _ada_kernel(c_ref, w_ref, b_ref, o_ref):
    s = _silu(c_ref[...])
    w = w_ref[0]
    s_hi = s.astype(BF16)
    s_lo = (s - s_hi.astype(F32)).astype(BF16)
    w_hi = w.astype(BF16)
    w_lo = (w - w_hi.astype(F32)).astype(BF16)
    o_ref[...] = _dot(s_hi, w_hi) + _dot(s_hi, w_lo) + _dot(s_lo, w_hi) + b_ref[...]


def _ada(cc, w, b, lay):
    rows = cc.shape[0]
    n = w.shape[2]
    tn = ADA_TN
    return pl.pallas_call(
        _ada_kernel,
        grid=(n // tn,),
        in_specs=[pl.BlockSpec((rows, D_MODEL), lambda j: (0, 0)),
                  pl.BlockSpec((1, D_MODEL, tn), lambda j: (lay, 0, j)),
                  pl.BlockSpec((1, tn), lambda j: (0, j))],
        out_specs=pl.BlockSpec((rows, tn), lambda j: (0, j)),
        out_shape=jax.ShapeDtypeStruct((rows, n), F32),
        compiler_params=_params(("arbitrary",)),
        name="ada",
    )(cc, w, b)


REGROUP_ROWS = 1024
REGROUP_ALIGN = 16
_MAIN_RUNS = ((P_Z, _IN_OFF[6], 2 * SSM_INNER), (P_Q, _IN_OFF[0], 2 * GLA_QK + 2 * GLA_V),
              (P_BM, _IN_OFF[8], 2 * SSM_BC), (P_GA, _IN_OFF[12], 2 * D_MODEL))


def _regroup_kernel(wt_ref, lr_ref, dt_ref, wm_ref, ws_ref):
    wm_ref[...] = jnp.transpose(wt_ref[...]).astype(BF16)
    pad = jnp.zeros((P_SMALL - lr_ref.shape[0] - dt_ref.shape[0], lr_ref.shape[1]), F32)
    ws_ref[...] = jnp.transpose(jnp.concatenate([lr_ref[...], dt_ref[...], pad], axis=0)).astype(BF16)


def _regroup(wt):
    _, D = wt.shape
    R = REGROUP_ROWS
    U = REGROUP_ALIGN
    assert all(dst % R == 0 and n % R == 0 and src % U == 0 for dst, src, n in _MAIN_RUNS)

    def src_row(i):
        units = jnp.int32(0)
        for dst, src, n in _MAIN_RUNS:
            inside = (i * R >= dst) & (i * R < dst + n)
            units = jnp.where(inside, (src - dst) // U + i * (R // U), units)
        return units * U

    return pl.pallas_call(
        _regroup_kernel,
        grid=(P_MAIN // R,),
        in_specs=[pl.BlockSpec((pl.Element(R), pl.Element(D)), lambda i: (src_row(i), 0)),
                  pl.BlockSpec((pl.Element(2 * GLA_RANK), pl.Element(D)), lambda i: (_IN_OFF[4], 0)),
                  pl.BlockSpec((pl.Element(2 * SSM_HEADS), pl.Element(D)), lambda i: (_IN_OFF[10], 0))],
        out_specs=[pl.BlockSpec((D, R), lambda i: (0, i)),
                   pl.BlockSpec((D, P_SMALL), lambda i: (0, 0))],
        out_shape=[jax.ShapeDtypeStruct((D, P_MAIN), BF16), jax.ShapeDtypeStruct((D, P_SMALL), BF16)],
        compiler_params=_params(("arbitrary",)),
        name="regroup",
    )(wt, wt, wt)


def _inproj_kernel(x_ref, mod_ref, nw_ref, wm_ref, ws_ref, om_ref, os_ref, xn_ref):
    @pl.when(pl.program_id(2) == 0)
    def _():
        x = x_ref[0]
        y = x * lax.rsqrt(jnp.mean(x * x, axis=-1, keepdims=True) + EPS) * nw_ref[...]
        y = y * (1.0 + mod_ref[0, 1:2, :]) + mod_ref[0, 0:1, :]
        xn = y.astype(BF16)
        xn_ref[...] = xn
        os_ref[0] = _dot(xn, ws_ref[...])

    om_ref[0] = _dot(xn_ref[...], wm_ref[...]).astype(BF16)


def _inproj(x, mod, nw, w_main, w_small, tm):
    B, L, _ = x.shape
    tn = INPROJ_TN
    return pl.pallas_call(
        _inproj_kernel,
        grid=(B, L // tm, P_MAIN // tn),
        in_specs=[pl.BlockSpec((1, tm, D_MODEL), lambda b, i, j: (b, i, 0)),
                  pl.BlockSpec((1, 2, D_MODEL), lambda b, i, j: (b, 0, 0)),
                  pl.BlockSpec((1, D_MODEL), lambda b, i, j: (0, 0)),
                  pl.BlockSpec((D_MODEL, tn), lambda b, i, j: (0, j)),
                  pl.BlockSpec((D_MODEL, P_SMALL), lambda b, i, j: (0, 0))],
        out_specs=[pl.BlockSpec((1, tm, tn), lambda b, i, j: (b, i, j)),
                   pl.BlockSpec((1, tm, P_SMALL), lambda b, i, j: (b, i, 0))],
        out_shape=[jax.ShapeDtypeStruct((B, L, P_MAIN), BF16),
                   jax.ShapeDtypeStruct((B, L, P_SMALL), F32)],
        scratch_shapes=[pltpu.VMEM((tm, D_MODEL), BF16)],
        compiler_params=_params(("parallel", "parallel", "arbitrary")),
        name="inproj",
    )(x, mod, nw, w_main, w_small)


GLA_GROUP = 4


def _gla_consts(C, reverse):
    NL = int(np.log2(C))
    idx = np.arange(C)
    tri = (idx[None, :] <= idx[:, None]).astype(np.float32)
    mats = [tri]
    masks = [np.eye(C, dtype=np.float32)]
    refs, signs = [], []
    for lev in range(NL):
        h = 1 << lev
        blk = idx // (2 * h)
        half = (idx // h) % 2
        ref = blk * 2 * h + h - 1
        if lev > 0:
            sg = np.where(half == 1, 1.0, -1.0).astype(np.float32)
            if reverse:
                ref, sg = (C - 1 - ref)[::-1], sg[::-1]
            refs.append([int(ref[m * 2 * h]) for m in range(C // (2 * h))])
            signs.append(np.broadcast_to(sg[:, None], (C, GLA_DK)))
        masks.append(((blk[:, None] == blk[None, :]) & (half[:, None] == 1)
                      & (half[None, :] == 0)).astype(np.float32))
    if reverse:
        mats = [m[::-1, ::-1] for m in mats]
        masks = [m[::-1, ::-1] for m in masks]
    dmat = np.concatenate(mats, axis=0)
    return (jnp.asarray(dmat, BF16), jnp.asarray(np.stack(masks), F32),
            jnp.asarray(np.stack(signs), F32), refs)


def _gla_stream(q_ref, k_ref, v_ref, lr_ref, up_ref, bias_ref, dm_ref, mk_ref, sg_ref, s0_ref, *rest,
                C, TB, reverse, refs, part, need_out):
    if need_out:
        o_ref, sf_ref, st_ref, e_s, b_s, oi_s, kv_s = rest
    else:
        sf_ref, st_ref, e_s, b_s, kv_s = rest
    NL = int(np.log2(C))
    nchunk = TB // C
    blk = pl.program_id(2)
    last = 0 if reverse else C - 1
    lr_off = S_LRB if reverse else S_LRF

    if part == "init":
        @pl.when(blk == 0)
        def _():
            st_ref[...] = s0_ref[0, 0]
        return
    if part == "final":
        @pl.when(blk == pl.num_programs(2) - 1)
        def _():
            sf_ref[0, 0] = st_ref[...]
        return

    lr = lr_ref[0][:, lr_off:lr_off + GLA_RANK]
    pre = _dot(lr.astype(BF16), up_ref[...].astype(BF16)) + bias_ref[...]
    g = -_softplus(-pre) * (1.0 / GLA_TAU)
    g_hi = g.astype(BF16)
    g_lo = (g - g_hi.astype(F32)).astype(BF16)
    gs = jnp.concatenate([g_hi, g_lo], axis=1)

    parity = lax.broadcasted_iota(jnp.int32, (C, GLA_DK), 0) & 1
    later = parity == (0 if reverse else 1)
    for c in range(nchunk):
        ex = _dot(dm_ref[...], gs[c * C:(c + 1) * C])
        b = ex[:, :GLA_DK] + ex[:, GLA_DK:]
        b_s[c] = b
        if need_out:
            e_s[c, 0:C, :] = jnp.exp(jnp.where(later, g[c * C:(c + 1) * C], 0.0))
            for li, lev in enumerate(range(1, NL)):
                h2 = 2 << lev
                bref = jnp.concatenate([jnp.broadcast_to(b[r:r + 1, :], (h2, GLA_DK)) for r in refs[li]],
                                       axis=0)
                e_s[c, lev * C:(lev + 1) * C, :] = jnp.exp(sg_ref[li] * (b - bref))
        e_s[c, NL * C:(NL + 1) * C, :] = jnp.exp(b[last:last + 1, :] - b)

    for c in range(nchunk):
        rows = slice(c * C, (c + 1) * C)
        if need_out:
            q = q_ref[0, rows, :].astype(F32) * (GLA_DK ** -0.5)
        k = k_ref[0, rows, :].astype(F32)
        v = v_ref[0, rows, :]
        if need_out:
            att = mk_ref[0] * _dot_nt(q.astype(BF16), k.astype(BF16))
            for lev in range(NL):
                e_l = e_s[c, lev * C:(lev + 1) * C, :]
                att = att + mk_ref[lev + 1] * _dot_nt((q * e_l).astype(BF16), (k * e_l).astype(BF16))
            oi_s[rows, :] = _dot(att.astype(BF16), v)
        kv_s[c] = _dot_tn((k * e_s[c, NL * C:(NL + 1) * C, :]).astype(BF16), v)

    order = list(reversed(range(nchunk))) if reverse else list(range(nchunk))
    st = st_ref[...]
    for g0 in range(0, nchunk, GLA_GROUP):
        grp = order[g0:g0 + GLA_GROUP]
        if need_out:
            wcat = jnp.concatenate([st.astype(BF16)] + [kv_s[c].astype(BF16) for c in grp[:-1]], axis=0)
        for i, c in enumerate(grp if need_out else ()):
            rows = slice(c * C, (c + 1) * C)
            q = q_ref[0, rows, :].astype(F32) * (GLA_DK ** -0.5)
            expo = b_s[c]
            pieces = []
            for j in range(i - 1, -2, -1):
                pieces.insert(0, (q * jnp.exp(expo)).astype(BF16))
                if j >= 0:
                    expo = expo + b_s[grp[j], last:last + 1, :]
            lhs = jnp.concatenate(pieces, axis=1)
            o_ref[0, rows, :] = oi_s[rows, :] + _dot(lhs, wcat[:(i + 1) * GLA_DK, :])
        for c in grp:
            dec = jnp.transpose(jnp.broadcast_to(jnp.exp(b_s[c, last:last + 1, :]), (GLA_DK, GLA_DK)))
            st = jnp.concatenate([dec] * (GLA_DV // GLA_DK), axis=1) * st + kv_s[c]
    st_ref[...] = st


GLA_STREAM_IN = 10


def _gla_kernel(*refs, C, TB, lv_refs, need_out):
    n_in = GLA_STREAM_IN
    n_out, n_scr = (2, 5) if need_out else (1, 4)
    ins = [refs[d * n_in:(d + 1) * n_in] for d in range(2)]
    outs = [refs[2 * n_in + d * n_out:2 * n_in + (d + 1) * n_out] for d in range(2)]
    base = 2 * (n_in + n_out)
    scr = [refs[base + d * n_scr:base + (d + 1) * n_scr] for d in range(2)]
    for part in ("init", "body", "final"):
        for d, reverse in enumerate((False, True)):
            _gla_stream(*ins[d], *outs[d], *scr[d], C=C, TB=TB, reverse=reverse, refs=lv_refs[d], part=part,
                        need_out=need_out)


def _gla_bidir(pm, ps, up, bias, s0, *, TB, need_out=True):
    B, L, _ = pm.shape
    C = GLA_CHUNK
    nb = L // TB
    qb, kb, vb = P_Q // GLA_DK, P_K // GLA_DK, P_V // GLA_DV
    in_specs, args, out_specs, out_shape, scratch, lv_refs = [], [], [], [], [], []
    for d, reverse in enumerate((False, True)):
        dmat, masks, signs, refs = _gla_consts(C, reverse)
        lv_refs.append(refs)
        tmap = (lambda i: nb - 1 - i) if reverse else (lambda i: i)
        in_specs += [
            pl.BlockSpec((1, TB, GLA_DK), lambda b, h, i, tmap=tmap: (b, tmap(i), qb + h)),
            pl.BlockSpec((1, TB, GLA_DK), lambda b, h, i, tmap=tmap: (b, tmap(i), kb + h)),
            pl.BlockSpec((1, TB, GLA_DV), lambda b, h, i, tmap=tmap: (b, tmap(i), vb + h)),
            pl.BlockSpec((1, TB, P_SMALL), lambda b, h, i, tmap=tmap: (b, tmap(i), 0)),
            pl.BlockSpec((GLA_RANK, GLA_DK), lambda b, h, i: (0, h)),
            pl.BlockSpec((1, GLA_DK), lambda b, h, i: (0, h)),
            _const_spec(dmat.shape),
            _const_spec(masks.shape),
            _const_spec(signs.shape),
            pl.BlockSpec((1, 1, GLA_DK, GLA_DV), lambda b, h, i: (b, h, 0, 0)),
        ]
        args += [pm, pm, pm, ps, up[d], bias[d], dmat, masks, signs, s0[d]]
        if need_out:
            out_specs += [pl.BlockSpec((1, TB, GLA_DV), lambda b, h, i, tmap=tmap: (b, tmap(i), h))]
            out_shape += [jax.ShapeDtypeStruct((B, L, GLA_V), F32)]
        out_specs += [pl.BlockSpec((1, 1, GLA_DK, GLA_DV), lambda b, h, i: (b, h, 0, 0))]
        out_shape += [jax.ShapeDtypeStruct((B, GLA_HEADS, GLA_DK, GLA_DV), F32)]
        scratch += [pltpu.VMEM((GLA_DK, GLA_DV), F32),
                    pltpu.VMEM((TB // C, masks.shape[0] * C, GLA_DK), F32),
                    pltpu.VMEM((TB // C, C, GLA_DK), F32)]
        if need_out:
            scratch += [pltpu.VMEM((TB, GLA_DV), F32)]
        scratch += [pltpu.VMEM((TB // C, GLA_DK, GLA_DV), F32)]
    assert len(in_specs) == 2 * GLA_STREAM_IN
    return pl.pallas_call(
        functools.partial(_gla_kernel, C=C, TB=TB, lv_refs=lv_refs, need_out=need_out),
        grid=(B, GLA_HEADS, nb),
        in_specs=in_specs,
        out_specs=out_specs,
        out_shape=out_shape,
        scratch_shapes=scratch,
        compiler_params=_params(("parallel", "parallel", "arbitrary")),
        name="gla",
    )(*args)


CONV_ROWS = 256


def _conv_shift_mats(T, rowlen):
    t = np.arange(T)
    mats = []
    for j in range(SSM_CONV):
        off = j - CONV_LEFT
        if off == 0:
            continue
        src = t + off
        ok = (src // rowlen == t // rowlen) & (src >= 0) & (src < T)
        m = np.zeros((T, T), np.float32)
        m[t[ok], src[ok]] = 1.0
        mats.append(m)
    return jnp.asarray(np.stack(mats), BF16)


def _conv_silu(u, w, b, sh_ref):
    acc = b + u.astype(F32) * w[CONV_LEFT:CONV_LEFT + 1, :]
    taps = [j for j in range(SSM_CONV) if j != CONV_LEFT]
    for i, j in enumerate(taps):
        acc = acc + _dot(sh_ref[i], u) * w[j:j + 1, :]
    return _silu(acc)


def _ssd_kernel(*refs, C, TB, reverse, finalize, conv, emit, lane0, need_out):
    refs = list(refs)
    xs_ref, bm_ref, cm_ref, ps_ref = refs[:4]
    refs = refs[4:]
    if conv:
        wx_ref, wb_ref, wc_ref, bx_ref, bb_ref, bc_ref, sh_ref = refs[:7]
        refs = refs[7:]
    dtb_ref, alog_ref, tri_ref, s0_ref = refs[:4]
    refs = refs[4:]
    if finalize:
        z_ref, yp_ref, dsk_ref, nw_ref = refs[:4]
        refs = refs[4:]
    if need_out:
        y_ref = refs[0]
        refs = refs[1:]
    sf_ref = refs[0]
    refs = refs[1:]
    if emit:
        xo_ref, bo_ref, co_ref = refs[:3]
        refs = refs[3:]
    if need_out:
        st_ref, xc_s, bc_s, cc_s, ya_s = refs
    else:
        st_ref, xc_s, bc_s, cc_s = refs
    nchunk = TB // C
    blk = pl.program_id(1)
    G, HPG, P, N = SSM_GROUPS, SSM_HPG, SSM_HEADDIM, SSM_STATE

    @pl.when(blk == 0)
    def _():
        st_ref[...] = s0_ref[0]

    if conv:
        tc = sh_ref.shape[1]
        for r0 in range(0, TB, tc):
            rs = slice(r0, r0 + tc)
            xc_s[rs, :] = _conv_silu(xs_ref[0, rs, :], wx_ref[...], bx_ref[...], sh_ref).astype(BF16)
            bc_s[rs, :] = _conv_silu(bm_ref[0, rs, :], wb_ref[...], bb_ref[...], sh_ref).astype(BF16)
            cc_s[rs, :] = _conv_silu(cm_ref[0, rs, :], wc_ref[...], bc_ref[...], sh_ref).astype(BF16)
        if emit:
            xo_ref[0] = xc_s[...]
            bo_ref[0] = bc_s[...]
            co_ref[0] = cc_s[...]
    else:
        xc_s[...] = xs_ref[0]
        bc_s[...] = bm_ref[0]
        cc_s[...] = cm_ref[0]

    neg_a = -jnp.exp(alog_ref[...])
    ti = lax.broadcasted_iota(jnp.int32, (C, C), 0)
    si = lax.broadcasted_iota(jnp.int32, (C, C), 1)
    keep = (si >= ti) if reverse else (si <= ti)
    lo = lax.broadcasted_iota(jnp.int32, (1, 2 * P), 1) < P
    last = 0 if reverse else C - 1
    nheads = G * HPG

    for c in (reversed(range(nchunk)) if reverse else range(nchunk)):
        rows = slice(c * C, (c + 1) * C)
        dt = _softplus(ps_ref[0, rows, :] + dtb_ref[...])
        cum = _dot01(tri_ref[...], dt * neg_a) * LOG2E
        cum_t = jnp.transpose(cum)[lane0:lane0 + nheads, :]
        dt_t = jnp.transpose(dt)[lane0:lane0 + nheads, :]
        w_t = (dt_t * jnp.exp2(cum_t[:, last:last + 1] - cum_t)).astype(BF16)
        cdl_t = cum_t - jnp.log2(dt_t)
        dec_all = jnp.exp2(cum[last:last + 1, :])
        for g in range(G):
            bm_g = bc_s[rows, g * N:(g + 1) * N]
            cm_g = cc_s[rows, g * N:(g + 1) * N]
            if need_out:
                cb = _dot_nt(cm_g, bm_g).astype(BF16)
            bm_t = jnp.transpose(bm_g.astype(F32)).astype(BF16)
            for pr in range(HPG // 2):
                lanes = slice(g * SSM_GW + pr * 2 * P, g * SSM_GW + (pr + 1) * 2 * P)
                x_pair = xc_s[rows, lanes]
                s_pair = st_ref[g, :, pr * 2 * P:(pr + 1) * 2 * P]
                s_bf = s_pair.astype(BF16)
                y = None
                ds = None
                decs = []
                for half in range(2):
                    j = g * HPG + pr * 2 + half
                    lane = lane0 + j
                    sel = lo if half == 0 else jnp.logical_not(lo)
                    if need_out:
                        bc_ = jnp.broadcast_to(cum[:, lane:lane + 1], (C, C))
                        m = cb * jnp.exp2(jnp.where(keep, bc_ - cdl_t[j:j + 1, :], NEG_BIG)).astype(BF16)
                        cd = cm_g * jnp.exp2(bc_).astype(BF16)
                        lhs = jnp.concatenate([m, cd], axis=1)
                        rhs = jnp.concatenate([jnp.where(sel, x_pair, jnp.zeros_like(x_pair)),
                                               jnp.where(sel, s_bf, jnp.zeros_like(s_bf))], axis=0)
                        yh = _dot(lhs, rhs)
                    dh = _dot(bm_t * w_t[j:j + 1, :],
                              jnp.where(sel, x_pair, jnp.zeros_like(x_pair)))
                    if need_out:
                        y = yh if y is None else y + yh
                    ds = dh if ds is None else ds + dh
                    decs.append(dec_all[:, lane:lane + 1])
                if need_out:
                    ya_s[rows, lanes] = y
                dec = jnp.where(lo, decs[0], decs[1])
                st_ref[g, :, pr * 2 * P:(pr + 1) * 2 * P] = dec * s_pair + ds

    if finalize:
        for g in range(G):
            lanes = slice(g * SSM_GW, (g + 1) * SSM_GW)
            y = ya_s[:, lanes] + yp_ref[0, :, lanes] + dsk_ref[:, lanes] * xc_s[:, lanes].astype(F32)
            y = y * _silu(z_ref[0, :, lanes].astype(F32))
            y = y * lax.rsqrt(jnp.mean(y * y, axis=-1, keepdims=True) + EPS) * nw_ref[:, lanes]
            y_ref[0, :, lanes] = y.astype(y_ref.dtype)
    elif need_out:
        y_ref[0] = ya_s[...]

    @pl.when(blk == pl.num_programs(1) - 1)
    def _():
        sf_ref[0] = st_ref[...]


def _ssd_pass(src, ps, conv_w, conv_b, dt_bias, a_log, s0, *, reverse, finalize, TB, rowlen, emit=False,
              need_out=True, y_prev=None, z_src=None, d_skip_x=None, norm_w=None):
    (xs_a, xs_o), (bm_a, bm_o), (cm_a, cm_o) = src
    B, L, _ = ps.shape
    C = SSD_CHUNK
    nb = L // TB
    G = SSM_GROUPS
    conv = conv_w is not None
    tmap = (lambda i: nb - 1 - i) if reverse else (lambda i: i)
    idx = np.arange(C)
    tri = (idx[None, :] >= idx[:, None]) if reverse else (idx[None, :] <= idx[:, None])
    tri = jnp.asarray(tri.astype(np.float32), BF16)
    in_specs = [
        pl.BlockSpec((1, TB, SSM_INNER), lambda b, i: (b, tmap(i), xs_o)),
        pl.BlockSpec((1, TB, SSM_BC), lambda b, i: (b, tmap(i), bm_o)),
        pl.BlockSpec((1, TB, SSM_BC), lambda b, i: (b, tmap(i), cm_o)),
        pl.BlockSpec((1, TB, P_SMALL), lambda b, i: (b, tmap(i), 0)),
    ]
    args = [xs_a, bm_a, cm_a, ps]
    if conv:
        tc = min(TB, max(rowlen, CONV_ROWS))
        assert TB % tc == 0 and tc % rowlen == 0
        nx = SSM_INNER // SSM_BC
        in_specs += [
            pl.BlockSpec((SSM_CONV, SSM_INNER), lambda b, i: (0, 0)),
            pl.BlockSpec((SSM_CONV, SSM_BC), lambda b, i: (0, nx)),
            pl.BlockSpec((SSM_CONV, SSM_BC), lambda b, i: (0, nx + 1)),
            pl.BlockSpec((1, SSM_INNER), lambda b, i: (0, 0)),
            pl.BlockSpec((1, SSM_BC), lambda b, i: (0, nx)),
            pl.BlockSpec((1, SSM_BC), lambda b, i: (0, nx + 1)),
            _const_spec((SSM_CONV - 1, tc, tc)),
        ]
        args += [conv_w, conv_w, conv_w, conv_b, conv_b, conv_b, _conv_shift_mats(tc, rowlen)]
    in_specs += [_const_spec((1, P_SMALL)), _const_spec((1, P_SMALL)), _const_spec((C, C)),
                 pl.BlockSpec((1, G, SSM_STATE, SSM_GW), lambda b, i: (b, 0, 0, 0))]
    args += [dt_bias, a_log, tri, s0]
    if finalize:
        z_a, z_o = z_src
        in_specs += [
            pl.BlockSpec((1, TB, SSM_INNER), lambda b, i: (b, tmap(i), z_o)),
            pl.BlockSpec((1, TB, SSM_INNER), lambda b, i: (b, tmap(i), 0)),
            _const_spec((1, SSM_INNER)), _const_spec((1, SSM_INNER)),
        ]
        args += [z_a, y_prev, d_skip_x, norm_w]
    out_specs, out_shape = [], []
    if need_out:
        out_specs += [pl.BlockSpec((1, TB, SSM_INNER), lambda b, i: (b, tmap(i), 0))]
        out_shape += [jax.ShapeDtypeStruct((B, L, SSM_INNER), BF16 if finalize else F32)]
    out_specs += [pl.BlockSpec((1, G, SSM_STATE, SSM_GW), lambda b, i: (b, 0, 0, 0))]
    out_shape += [jax.ShapeDtypeStruct((B, G, SSM_STATE, SSM_GW), F32)]
    if emit:
        out_specs += [pl.BlockSpec((1, TB, SSM_INNER), lambda b, i: (b, tmap(i), 0)),
                      pl.BlockSpec((1, TB, SSM_BC), lambda b, i: (b, tmap(i), 0)),
                      pl.BlockSpec((1, TB, SSM_BC), lambda b, i: (b, tmap(i), 0))]
        out_shape += [jax.ShapeDtypeStruct((B, L, SSM_INNER), BF16),
                      jax.ShapeDtypeStruct((B, L, SSM_BC), BF16),
                      jax.ShapeDtypeStruct((B, L, SSM_BC), BF16)]
    kern = functools.partial(_ssd_kernel, C=C, TB=TB, reverse=reverse, finalize=finalize,
                             conv=conv, emit=emit, lane0=S_DTB if reverse else S_DTF, need_out=need_out)
    return pl.pallas_call(
        kern,
        grid=(B, nb),
        in_specs=in_specs,
        out_specs=out_specs,
        out_shape=out_shape,
        scratch_shapes=[pltpu.VMEM((G, SSM_STATE, SSM_GW), F32),
                        pltpu.VMEM((TB, SSM_INNER), BF16),
                        pltpu.VMEM((TB, SSM_BC), BF16),
                        pltpu.VMEM((TB, SSM_BC), BF16)]
                       + ([pltpu.VMEM((TB, SSM_INNER), F32)] if need_out else []),
        compiler_params=_params(("parallel", "arbitrary")),
        name="ssd_" + ("rev" if reverse else "fwd") + ("_fin" if finalize else ""),
    )(*args)


def _merge_kernel(of_ref, ob_ref, r_ref, gnw_ref, sb_ref, ga_ref, gb_ref, x_ref, g1_ref, wpa_ref, wpb_ref,
                  wout_ref, h_ref):
    o = of_ref[0] + ob_ref[0]
    heads = []
    for h in range(GLA_HEADS):
        oh = o[:, h * GLA_DV:(h + 1) * GLA_DV]
        heads.append(oh * lax.rsqrt(jnp.mean(oh * oh, axis=-1, keepdims=True) + EPS) * gnw_ref[...])
    oa = (jnp.concatenate(heads, axis=1) * _silu(r_ref[0].astype(F32))).astype(BF16)
    ya = _dot(oa, wpa_ref[...])
    yb = _dot(sb_ref[0], wpb_ref[...])
    m = _sigmoid(ga_ref[0].astype(F32)) * ya + _sigmoid(gb_ref[0].astype(F32)) * yb
    mix = _dot(m.astype(BF16), wout_ref[...])
    h_ref[0] = x_ref[0] + g1_ref[0] * mix


def _merge(o_f, o_b, gla_norm_w, s_b, pm, x, g1, w_pa, w_pb, w_out, tm):
    B, L, _ = x.shape
    gab, gbb, rb = P_GA // D_MODEL, P_GB // D_MODEL, P_R // GLA_V
    return pl.pallas_call(
        _merge_kernel,
        grid=(B, L // tm),
        in_specs=[pl.BlockSpec((1, tm, GLA_V), lambda b, i: (b, i, 0)),
                  pl.BlockSpec((1, tm, GLA_V), lambda b, i: (b, i, 0)),
                  pl.BlockSpec((1, tm, GLA_V), lambda b, i: (b, i, rb)),
                  _const_spec((1, GLA_DV)),
                  pl.BlockSpec((1, tm, SSM_INNER), lambda b, i: (b, i, 0)),
                  pl.BlockSpec((1, tm, D_MODEL), lambda b, i: (b, i, gab)),
                  pl.BlockSpec((1, tm, D_MODEL), lambda b, i: (b, i, gbb)),
                  pl.BlockSpec((1, tm, D_MODEL), lambda b, i: (b, i, 0)),
                  pl.BlockSpec((1, 1, D_MODEL), lambda b, i: (b, 0, 0)),
                  _const_spec(w_pa.shape), _const_spec(w_pb.shape), _const_spec(w_out.shape)],
        out_specs=pl.BlockSpec((1, tm, D_MODEL), lambda b, i: (b, i, 0)),
        out_shape=jax.ShapeDtypeStruct((B, L, D_MODEL), F32),
        compiler_params=_params(("parallel", "parallel")),
        name="merge",
    )(o_f, o_b, pm, gla_norm_w, s_b, pm, pm, x, g1, w_pa, w_pb, w_out)


FFN_CHUNK = 256


def _ffn_kernel(h_ref, mod_ref, n2_ref, fw_ref, wg_ref, wu_ref, wd_ref, o_ref):
    h = h_ref[0]
    hn = h * lax.rsqrt(jnp.mean(h * h, axis=-1, keepdims=True) + EPS) * n2_ref[...]
    hn = (hn * (1.0 + mod_ref[0, 1:2, :]) + mod_ref[0, 0:1, :]).astype(BF16)
    acc = jnp.zeros(h.shape, F32)
    for c in range(D_FF // FFN_CHUNK):
        cols = slice(c * FFN_CHUNK, (c + 1) * FFN_CHUNK)
        gt = _dot(hn, wg_ref[:, cols])
        up = _dot(hn, wu_ref[:, cols])
        acc = acc + _dot((_silu(gt) * up).astype(BF16), wd_ref[cols, :])
    h2 = h + mod_ref[0, 2:3, :] * acc
    o_ref[0] = h2 * lax.rsqrt(jnp.mean(h2 * h2, axis=-1, keepdims=True) + EPS) * fw_ref[...]


def _ffn(h, mod, n2w, fw, w_gate, w_up, w_down, tm):
    B, L, _ = h.shape
    return pl.pallas_call(
        _ffn_kernel,
        grid=(B, L // tm),
        in_specs=[pl.BlockSpec((1, tm, D_MODEL), lambda b, i: (b, i, 0)),
                  pl.BlockSpec((1, 3, D_MODEL), lambda b, i: (b, 0, 0)),
                  _const_spec((1, D_MODEL)), _const_spec((1, D_MODEL)),
                  pl.BlockSpec(w_gate.shape, lambda b, i: (0, 0), pipeline_mode=pl.Buffered(1)),
                  pl.BlockSpec(w_up.shape, lambda b, i: (0, 0), pipeline_mode=pl.Buffered(1)),
                  pl.BlockSpec(w_down.shape, lambda b, i: (0, 0), pipeline_mode=pl.Buffered(1))],
        out_specs=pl.BlockSpec((1, tm, D_MODEL), lambda b, i: (b, i, 0)),
        out_shape=jax.ShapeDtypeStruct((B, L, D_MODEL), F32),
        compiler_params=_params(("parallel", "parallel")),
        name="ffn",
    )(h, mod, n2w, fw, w_gate, w_up, w_down)


def _pick_block(L, pref):
    tb = min(L, pref)
    assert L % tb == 0
    return tb


def kernel(x, c, ctx, c_ctx, w_ada, b_ada, norm1_w, w_in, gla_up_f, gla_bias_f, gla_up_b, gla_bias_b,
           gla_norm_w, conv_w, conv_b, dt_bias_f, dt_bias_b, a_log_f, a_log_b, d_skip, ssm_norm_w,
           w_pa, w_pb, w_out, norm2_w, w_gate, w_up, w_down, final_norm_w):
    B, L, D = x.shape
    Lc = ctx.shape[1]
    depth = w_ada.shape[0]
    assert depth == 1 and D == D_MODEL
    assert L % GRID_W == 0 and L % SSD_CHUNK == 0 and Lc % SSD_CHUNK == 0
    lay = 0

    nrow = -(-(B + 1) // 8) * 8
    cc = jnp.zeros((nrow, D), F32).at[:B].set(c).at[B].set(c_ctx)
    ada = _ada(cc, w_ada, b_ada[lay][None, :], lay)
    sh1, sc1, g1, sh2, sc2, g2 = [ada[:, i * D:(i + 1) * D] for i in range(6)]
    mod1 = jnp.stack([sh1[:B], sc1[:B]], axis=1)
    mod1_c = jnp.broadcast_to(jnp.stack([sh1[B], sc1[B]])[None], (B, 2, D))
    mod2 = jnp.stack([sh2[:B], sc2[:B], g2[:B]], axis=1)
    g1_l = g1[:B, None, :]

    wt = jnp.swapaxes(w_in[lay], 0, 1)
    w_main, w_small = _regroup(wt)
    nw1 = norm1_w[lay][None, :]

    at_lanes = lambda p, off: jnp.zeros((1, P_SMALL), F32).at[0, off:off + SSM_HEADS].set(p[lay])
    dtb_f, al_f = at_lanes(dt_bias_f, S_DTF), at_lanes(a_log_f, S_DTF)
    dtb_b, al_b = at_lanes(dt_bias_b, S_DTB), at_lanes(a_log_b, S_DTB)
    cw, cb_ = conv_w[lay], conv_b[lay][None, :]
    up_f, up_b = gla_up_f[lay], gla_up_b[lay]
    bi_f, bi_b = gla_bias_f[lay][None, :], gla_bias_b[lay][None, :]

    gla_zero = jnp.zeros((B, GLA_HEADS, GLA_DK, GLA_DV), F32)
    ssd_zero = jnp.zeros((B, SSM_GROUPS, SSM_STATE, SSM_GW), F32)

    pm_c, ps_c = _inproj(ctx, mod1_c, nw1, w_main, w_small, _pick_block(Lc, INPROJ_TM))
    sg_f, sg_b = _gla_bidir(pm_c, ps_c, (up_f, up_b), (bi_f, bi_b), (gla_zero, gla_zero),
                            TB=_pick_block(Lc, GLA_TB), need_out=False)
    src_c = ((pm_c, P_XS // SSM_INNER), (pm_c, P_BM // SSM_BC), (pm_c, P_CM // SSM_BC))
    ss_f, = _ssd_pass(src_c, ps_c, cw, cb_, dtb_f, al_f, ssd_zero, reverse=False, finalize=False,
                      TB=Lc, rowlen=Lc, need_out=False)
    ss_b, = _ssd_pass(src_c, ps_c, cw, cb_, dtb_b, al_b, ssd_zero, reverse=True, finalize=False,
                      TB=Lc, rowlen=Lc, need_out=False)

    pm, ps = _inproj(x, mod1, nw1, w_main, w_small, _pick_block(L, INPROJ_TM))
    tbg = _pick_block(L, GLA_TB)
    og_f, _, og_b, _ = _gla_bidir(pm, ps, (up_f, up_b), (bi_f, bi_b), (sg_f, sg_b), TB=tbg)
    tbs = _pick_block(L, SSD_TB)
    src = ((pm, P_XS // SSM_INNER), (pm, P_BM // SSM_BC), (pm, P_CM // SSM_BC))
    ys_f, _, xc, bc, cc = _ssd_pass(src, ps, cw, cb_, dtb_f, al_f, ss_f, reverse=False, finalize=False,
                                    TB=tbs, rowlen=GRID_W, emit=True)
    o_b, _ = _ssd_pass(((xc, 0), (bc, 0), (cc, 0)), ps, None, None, dtb_b, al_b, ss_b, reverse=True,
                       finalize=True, TB=tbs, rowlen=GRID_W, y_prev=ys_f, z_src=(pm, P_Z // SSM_INNER),
                       d_skip_x=jnp.repeat(d_skip[lay], SSM_HEADDIM)[None, :],
                       norm_w=ssm_norm_w[lay][None, :])

    h = _merge(og_f, og_b, gla_norm_w[lay][None, :], o_b, pm, x, g1_l, w_pa[lay].astype(BF16),
               w_pb[lay].astype(BF16), w_out[lay].astype(BF16), _pick_block(L, MERGE_TM))
    return _ffn(h, mod2, norm2_w[lay][None, :], final_norm_w[None, :], w_gate[lay].astype(BF16),
                w_up[lay].astype(BF16), w_down[lay].astype(BF16), _pick_block(L, FFN_TM))
```

```python
import functools

import numpy as np
import jax
import jax.numpy as jnp
from jax import lax
from jax.experimental import pallas as pl
from jax.experimental.pallas import tpu as pltpu

F32 = jnp.float32
BF16 = jnp.bfloat16

D_MODEL = 1024
GRID_W = 64
EPS = 1e-6

GLA_HEADS = 4
GLA_DK = 128
GLA_DV = 256
GLA_QK = GLA_HEADS * GLA_DK
GLA_V = GLA_HEADS * GLA_DV
GLA_RANK = 16
GLA_TAU = 16.0

SSM_INNER = 2 * D_MODEL
SSM_HEADDIM = 64
SSM_HEADS = SSM_INNER // SSM_HEADDIM
SSM_GROUPS = 4
SSM_HPG = SSM_HEADS // SSM_GROUPS
SSM_STATE = 128
SSM_BC = SSM_GROUPS * SSM_STATE
SSM_CONV = 4
CONV_LEFT = 2
SSM_GW = SSM_HPG * SSM_HEADDIM

D_FF = ((8 * D_MODEL // 3 + 255) // 256) * 256

_IN_WIDTHS = (GLA_QK, GLA_QK, GLA_V, GLA_V, GLA_RANK, GLA_RANK,
              SSM_INNER, SSM_INNER, SSM_BC, SSM_BC, SSM_HEADS, SSM_HEADS, D_MODEL, D_MODEL)
_IN_OFF = np.concatenate([[0], np.cumsum(_IN_WIDTHS)]).tolist()

P_Z, P_XS = 0, 2048
P_Q, P_K, P_V, P_R = 4096, 4608, 5120, 6144
P_BM, P_CM = 7168, 7680
P_GA, P_GB = 8192, 9216
P_MAIN = 10240
S_LRF, S_LRB, S_DTF, S_DTB = 0, 16, 32, 64
P_SMALL = 128

GLA_CHUNK = 128
SSD_CHUNK = 128

ADA_TN = 1536
INPROJ_TM, INPROJ_TN = 1024, 2560
GLA_TB = 2048
SSD_TB = 512
MERGE_TM = 512
FFN_TM = 512

VMEM_LIMIT = 56 * 1024 * 1024
NEG_BIG = -1e30
LOG2E = 1.4426950408889634


def _sigmoid(x):
    return 1.0 / (1.0 + jnp.exp(-x))


def _silu(x):
    return x * _sigmoid(x)


def _softplus(x):
    return jnp.maximum(x, 0.0) + jnp.log(1.0 + jnp.exp(-jnp.abs(x)))


def _split3(x):
    hi = x.astype(BF16)
    r1 = x - hi.astype(F32)
    mid = r1.astype(BF16)
    lo = (r1 - mid.astype(F32)).astype(BF16)
    return hi, mid, lo


def _dot(a, b):
    return jnp.dot(a, b, preferred_element_type=F32)


def _dot_nt(a, b):
    return lax.dot_general(a, b, (((1,), (1,)), ((), ())), preferred_element_type=F32)


def _dot_tn(a, b):
    return lax.dot_general(a, b, (((0,), (0,)), ((), ())), preferred_element_type=F32)


def _dot01(m01, x):
    hi, mid, lo = _split3(x)
    return _dot(m01, hi) + _dot(m01, mid) + _dot(m01, lo)


def _params(sem):
    return pltpu.CompilerParams(dimension_semantics=sem, vmem_limit_bytes=VMEM_LIMIT)


def _const_spec(shape):
    n = len(shape)
    return pl.BlockSpec(shape, lambda *_: (0,) * n)


def _ada_kernel(c_ref, w_ref, b_ref, o_ref):
    s = _silu(c_ref[...])
    w = w_ref[0]
    s_hi = s.astype(BF16)
    s_lo = (s - s_hi.astype(F32)).astype(BF16)
    w_hi = w.astype(BF16)
    w_lo = (w - w_hi.astype(F32)).astype(BF16)
    o_ref[...] = _dot(s_hi, w_hi) + _dot(s_hi, w_lo) + _dot(s_lo, w_hi) + b_ref[...]


def _ada(cc, w, b, lay):
    rows = cc.shape[0]
    n = w.shape[2]
    tn = ADA_TN
    return pl.pallas_call(
        _ada_kernel,
        grid=(n // tn,),
        in_specs=[pl.BlockSpec((rows, D_MODEL), lambda j: (0, 0)),
                  pl.BlockSpec((1, D_MODEL, tn), lambda j: (lay, 0, j)),
                  pl.BlockSpec((1, tn), lambda j: (0, j))],
        out_specs=pl.BlockSpec((rows, tn), lambda j: (0, j)),
        out_shape=jax.ShapeDtypeStruct((rows, n), F32),
        compiler_params=_params(("arbitrary",)),
        name="ada",
    )(cc, w, b)


REGROUP_ROWS = 1024
REGROUP_ALIGN = 16
_MAIN_RUNS = ((P_Z, _IN_OFF[6], 2 * SSM_INNER), (P_Q, _IN_OFF[0], 2 * GLA_QK + 2 * GLA_V),
              (P_BM, _IN_OFF[8], 2 * SSM_BC), (P_GA, _IN_OFF[12], 2 * D_MODEL))


def _regroup_kernel(wt_ref, lr_ref, dt_ref, wm_ref, ws_ref):
    wm_ref[...] = jnp.transpose(wt_ref[...]).astype(BF16)
    pad = jnp.zeros((P_SMALL - lr_ref.shape[0] - dt_ref.shape[0], lr_ref.shape[1]), F32)
    ws_ref[...] = jnp.transpose(jnp.concatenate([lr_ref[...], dt_ref[...], pad], axis=0)).astype(BF16)


def _regroup(wt):
    _, D = wt.shape
    R = REGROUP_ROWS
    U = REGROUP_ALIGN
    assert all(dst % R == 0 and n % R == 0 and src % U == 0 for dst, src, n in _MAIN_RUNS)

    def src_row(i):
        units = jnp.int32(0)
        for dst, src, n in _MAIN_RUNS:
            inside = (i * R >= dst) & (i * R < dst + n)
            units = jnp.where(inside, (src - dst) // U + i * (R // U), units)
        return units * U

    return pl.pallas_call(
        _regroup_kernel,
        grid=(P_MAIN // R,),
        in_specs=[pl.BlockSpec((pl.Element(R), pl.Element(D)), lambda i: (src_row(i), 0)),
                  pl.BlockSpec((pl.Element(2 * GLA_RANK), pl.Element(D)), lambda i: (_IN_OFF[4], 0)),
                  pl.BlockSpec((pl.Element(2 * SSM_HEADS), pl.Element(D)), lambda i: (_IN_OFF[10], 0))],
        out_specs=[pl.BlockSpec((D, R), lambda i: (0, i)),
                   pl.BlockSpec((D, P_SMALL), lambda i: (0, 0))],
        out_shape=[jax.ShapeDtypeStruct((D, P_MAIN), BF16), jax.ShapeDtypeStruct((D, P_SMALL), BF16)],
        compiler_params=_params(("arbitrary",)),
        name="regroup",
    )(wt, wt, wt)


def _inproj_kernel(x_ref, mod_ref, nw_ref, wm_ref, ws_ref, om_ref, os_ref, xn_ref):
    @pl.when(pl.program_id(2) == 0)
    def _():
        x = x_ref[0]
        y = x * lax.rsqrt(jnp.mean(x * x, axis=-1, keepdims=True) + EPS) * nw_ref[...]
        y = y * (1.0 + mod_ref[0, 1:2, :]) + mod_ref[0, 0:1, :]
        xn = y.astype(BF16)
        xn_ref[...] = xn
        os_ref[0] = _dot(xn, ws_ref[...])

    om_ref[0] = _dot(xn_ref[...], wm_ref[...]).astype(BF16)


def _inproj(x, mod, nw, w_main, w_small, tm):
    B, L, _ = x.shape
    tn = INPROJ_TN
    return pl.pallas_call(
        _inproj_kernel,
        grid=(B, L // tm, P_MAIN // tn),
        in_specs=[pl.BlockSpec((1, tm, D_MODEL), lambda b, i, j: (b, i, 0)),
                  pl.BlockSpec((1, 2, D_MODEL), lambda b, i, j: (b, 0, 0)),
                  pl.BlockSpec((1, D_MODEL), lambda b, i, j: (0, 0)),
                  pl.BlockSpec((D_MODEL, tn), lambda b, i, j: (0, j)),
                  pl.BlockSpec((D_MODEL, P_SMALL), lambda b, i, j: (0, 0))],
        out_specs=[pl.BlockSpec((1, tm, tn), lambda b, i, j: (b, i, j)),
                   pl.BlockSpec((1, tm, P_SMALL), lambda b, i, j: (b, i, 0))],
        out_shape=[jax.ShapeDtypeStruct((B, L, P_MAIN), BF16),
                   jax.ShapeDtypeStruct((B, L, P_SMALL), F32)],
        scratch_shapes=[pltpu.VMEM((tm, D_MODEL), BF16)],
        compiler_params=_params(("parallel", "parallel", "arbitrary")),
        name="inproj",
    )(x, mod, nw, w_main, w_small)


GLA_GROUP = 2


def _gla_consts(C, reverse):
    NL = int(np.log2(C))
    idx = np.arange(C)
    tri = (idx[None, :] <= idx[:, None]).astype(np.float32)
    mats = [tri]
    masks = [np.eye(C, dtype=np.float32)]
    refs, signs = [], []
    for lev in range(NL):
        h = 1 << lev
        blk = idx // (2 * h)
        half = (idx // h) % 2
        ref = blk * 2 * h + h - 1
        if lev > 0:
            sg = np.where(half == 1, 1.0, -1.0).astype(np.float32)
            if reverse:
                ref, sg = (C - 1 - ref)[::-1], sg[::-1]
            refs.append([int(ref[m * 2 * h]) for m in range(C // (2 * h))])
            signs.append(np.broadcast_to(sg[:, None], (C, GLA_DK)))
        masks.append(((blk[:, None] == blk[None, :]) & (half[:, None] == 1)
                      & (half[None, :] == 0)).astype(np.float32))
    if reverse:
        mats = [m[::-1, ::-1] for m in mats]
        masks = [m[::-1, ::-1] for m in masks]
    dmat = np.concatenate(mats, axis=0)
    return (jnp.asarray(dmat, BF16), jnp.asarray(np.stack(masks), F32),
            jnp.asarray(np.stack(signs), F32), refs)


def _gla_stream(q_ref, k_ref, v_ref, lr_ref, up_ref, bias_ref, dm_ref, mk_ref, sg_ref, s0_ref, *rest,
                C, TB, reverse, refs, part, need_out):
    if need_out:
        o_ref, sf_ref, st_ref, e_s, b_s, oi_s, kv_s = rest
    else:
        sf_ref, st_ref, e_s, b_s, kv_s = rest
    NL = int(np.log2(C))
    nchunk = TB // C
    blk = pl.program_id(2)
    last = 0 if reverse else C - 1
    lr_off = S_LRB if reverse else S_LRF

    if part == "init":
        @pl.when(blk == 0)
        def _():
            st_ref[...] = s0_ref[0, 0]
        return
    if part == "final":
        @pl.when(blk == pl.num_programs(2) - 1)
        def _():
            sf_ref[0, 0] = st_ref[...]
        return

    lr = lr_ref[0][:, lr_off:lr_off + GLA_RANK]
    pre = _dot(lr.astype(BF16), up_ref[...].astype(BF16)) + bias_ref[...]
    g = -_softplus(-pre) * (1.0 / GLA_TAU)
    g_hi = g.astype(BF16)
    g_lo = (g - g_hi.astype(F32)).astype(BF16)
    gs = jnp.concatenate([g_hi, g_lo], axis=1)

    parity = lax.broadcasted_iota(jnp.int32, (C, GLA_DK), 0) & 1
    later = parity == (0 if reverse else 1)
    for c in range(nchunk):
        ex = _dot(dm_ref[...], gs[c * C:(c + 1) * C])
        b = ex[:, :GLA_DK] + ex[:, GLA_DK:]
        b_s[c] = b
        if need_out:
            e_s[c, 0:C, :] = jnp.exp(jnp.where(later, g[c * C:(c + 1) * C], 0.0))
            for li, lev in enumerate(range(1, NL)):
                h2 = 2 << lev
                bref = jnp.concatenate([jnp.broadcast_to(b[r:r + 1, :], (h2, GLA_DK)) for r in refs[li]],
                                       axis=0)
                e_s[c, lev * C:(lev + 1) * C, :] = jnp.exp(sg_ref[li] * (b - bref))
        e_s[c, NL * C:(NL + 1) * C, :] = jnp.exp(b[last:last + 1, :] - b)

    for c in range(nchunk):
        rows = slice(c * C, (c + 1) * C)
        if need_out:
            q = q_ref[0, rows, :].astype(F32) * (GLA_DK ** -0.5)
        k = k_ref[0, rows, :].astype(F32)
        v = v_ref[0, rows, :]
        if need_out:
            att = mk_ref[0] * _dot_nt(q.astype(BF16), k.astype(BF16))
            for lev in range(NL):
                e_l = e_s[c, lev * C:(lev + 1) * C, :]
                att = att + mk_ref[lev + 1] * _dot_nt((q * e_l).astype(BF16), (k * e_l).astype(BF16))
            oi_s[rows, :] = _dot(att.astype(BF16), v)
        kv_s[c] = _dot_tn((k * e_s[c, NL * C:(NL + 1) * C, :]).astype(BF16), v)

    order = list(reversed(range(nchunk))) if reverse else list(range(nchunk))
    st = st_ref[...]
    for g0 in range(0, nchunk, GLA_GROUP):
        grp = order[g0:g0 + GLA_GROUP]
        if need_out:
            wcat = jnp.concatenate([st.astype(BF16)] + [kv_s[c].astype(BF16) for c in grp[:-1]], axis=0)
        for i, c in enumerate(grp if need_out else ()):
            rows = slice(c * C, (c + 1) * C)
            q = q_ref[0, rows, :].astype(F32) * (GLA_DK ** -0.5)
            expo = b_s[c]
            pieces = []
            for j in range(i - 1, -2, -1):
                pieces.insert(0, (q * jnp.exp(expo)).astype(BF16))
                if j >= 0:
                    expo = expo + b_s[grp[j], last:last + 1, :]
            lhs = jnp.concatenate(pieces, axis=1)
            o_ref[0, rows, :] = oi_s[rows, :] + _dot(lhs, wcat[:(i + 1) * GLA_DK, :])
        for c in grp:
            dec = jnp.transpose(jnp.broadcast_to(jnp.exp(b_s[c, last:last + 1, :]), (GLA_DK, GLA_DK)))
            st = jnp.concatenate([dec] * (GLA_DV // GLA_DK), axis=1) * st + kv_s[c]
    st_ref[...] = st


GLA_STREAM_IN = 10


def _gla_kernel(*refs, C, TB, lv_refs, need_out):
    n_in = GLA_STREAM_IN
    n_out, n_scr = (2, 5) if need_out else (1, 4)
    ins = [refs[d * n_in:(d + 1) * n_in] for d in range(2)]
    outs = [refs[2 * n_in + d * n_out:2 * n_in + (d + 1) * n_out] for d in range(2)]
    base = 2 * (n_in + n_out)
    scr = [refs[base + d * n_scr:base + (d + 1) * n_scr] for d in range(2)]
    for part in ("init", "body", "final"):
        for d, reverse in enumerate((False, True)):
            _gla_stream(*ins[d], *outs[d], *scr[d], C=C, TB=TB, reverse=reverse, refs=lv_refs[d], part=part,
                        need_out=need_out)


def _gla_bidir(pm, ps, up, bias, s0, *, TB, need_out=True):
    B, L, _ = pm.shape
    C = GLA_CHUNK
    nb = L // TB
    qb, kb, vb = P_Q // GLA_DK, P_K // GLA_DK, P_V // GLA_DV
    in_specs, args, out_specs, out_shape, scratch, lv_refs = [], [], [], [], [], []
    for d, reverse in enumerate((False, True)):
        dmat, masks, signs, refs = _gla_consts(C, reverse)
        lv_refs.append(refs)
        tmap = (lambda i: nb - 1 - i) if reverse else (lambda i: i)
        in_specs += [
            pl.BlockSpec((1, TB, GLA_DK), lambda b, h, i, tmap=tmap: (b, tmap(i), qb + h)),
            pl.BlockSpec((1, TB, GLA_DK), lambda b, h, i, tmap=tmap: (b, tmap(i), kb + h)),
            pl.BlockSpec((1, TB, GLA_DV), lambda b, h, i, tmap=tmap: (b, tmap(i), vb + h)),
            pl.BlockSpec((1, TB, P_SMALL), lambda b, h, i, tmap=tmap: (b, tmap(i), 0)),
            pl.BlockSpec((GLA_RANK, GLA_DK), lambda b, h, i: (0, h)),
            pl.BlockSpec((1, GLA_DK), lambda b, h, i: (0, h)),
            _const_spec(dmat.shape),
            _const_spec(masks.shape),
            _const_spec(signs.shape),
            pl.BlockSpec((1, 1, GLA_DK, GLA_DV), lambda b, h, i: (b, h, 0, 0)),
        ]
        args += [pm, pm, pm, ps, up[d], bias[d], dmat, masks, signs, s0[d]]
        if need_out:
            out_specs += [pl.BlockSpec((1, TB, GLA_DV), lambda b, h, i, tmap=tmap: (b, tmap(i), h))]
            out_shape += [jax.ShapeDtypeStruct((B, L, GLA_V), F32)]
        out_specs += [pl.BlockSpec((1, 1, GLA_DK, GLA_DV), lambda b, h, i: (b, h, 0, 0))]
        out_shape += [jax.ShapeDtypeStruct((B, GLA_HEADS, GLA_DK, GLA_DV), F32)]
        scratch += [pltpu.VMEM((GLA_DK, GLA_DV), F32),
                    pltpu.VMEM((TB // C, masks.shape[0] * C, GLA_DK), F32),
                    pltpu.VMEM((TB // C, C, GLA_DK), F32)]
        if need_out:
            scratch += [pltpu.VMEM((TB, GLA_DV), F32)]
        scratch += [pltpu.VMEM((TB // C, GLA_DK, GLA_DV), F32)]
    assert len(in_specs) == 2 * GLA_STREAM_IN
    return pl.pallas_call(
        functools.partial(_gla_kernel, C=C, TB=TB, lv_refs=lv_refs, need_out=need_out),
        grid=(B, GLA_HEADS, nb),
        in_specs=in_specs,
        out_specs=out_specs,
        out_shape=out_shape,
        scratch_shapes=scratch,
        compiler_params=_params(("parallel", "parallel", "arbitrary")),
        name="gla",
    )(*args)


CONV_ROWS = 256


def _conv_shift_mats(T, rowlen):
    t = np.arange(T)
    mats = []
    for j in range(SSM_CONV):
        off = j - CONV_LEFT
        if off == 0:
            continue
        src = t + off
        ok = (src // rowlen == t // rowlen) & (src >= 0) & (src < T)
        m = np.zeros((T, T), np.float32)
        m[t[ok], src[ok]] = 1.0
        mats.append(m)
    return jnp.asarray(np.stack(mats), BF16)


def _conv_silu(u, w, b, sh_ref):
    acc = b + u.astype(F32) * w[CONV_LEFT:CONV_LEFT + 1, :]
    taps = [j for j in range(SSM_CONV) if j != CONV_LEFT]
    for i, j in enumerate(taps):
        acc = acc + _dot(sh_ref[i], u) * w[j:j + 1, :]
    return _silu(acc)


def _ssd_kernel(*refs, C, TB, reverse, finalize, conv, emit, lane0, need_out):
    refs = list(refs)
    xs_ref, bm_ref, cm_ref, ps_ref = refs[:4]
    refs = refs[4:]
    if conv:
        wx_ref, wb_ref, wc_ref, bx_ref, bb_ref, bc_ref, sh_ref = refs[:7]
        refs = refs[7:]
    dtb_ref, alog_ref, tri_ref, s0_ref = refs[:4]
    refs = refs[4:]
    if finalize:
        z_ref, yp_ref, dsk_ref, nw_ref = refs[:4]
        refs = refs[4:]
    if need_out:
        y_ref = refs[0]
        refs = refs[1:]
    sf_ref = refs[0]
    refs = refs[1:]
    if emit:
        xo_ref, bo_ref, co_ref = refs[:3]
        refs = refs[3:]
    if need_out:
        st_ref, xc_s, bc_s, cc_s, ya_s = refs
    else:
        st_ref, xc_s, bc_s, cc_s = refs
    nchunk = TB // C
    blk = pl.program_id(1)
    G, HPG, P, N = SSM_GROUPS, SSM_HPG, SSM_HEADDIM, SSM_STATE

    @pl.when(blk == 0)
    def _():
        st_ref[...] = s0_ref[0]

    if conv:
        tc = sh_ref.shape[1]
        for r0 in range(0, TB, tc):
            rs = slice(r0, r0 + tc)
            xc_s[rs, :] = _conv_silu(xs_ref[0, rs, :], wx_ref[...], bx_ref[...], sh_ref).astype(BF16)
            bc_s[rs, :] = _conv_silu(bm_ref[0, rs, :], wb_ref[...], bb_ref[...], sh_ref).astype(BF16)
            cc_s[rs, :] = _conv_silu(cm_ref[0, rs, :], wc_ref[...], bc_ref[...], sh_ref).astype(BF16)
        if emit:
            xo_ref[0] = xc_s[...]
            bo_ref[0] = bc_s[...]
            co_ref[0] = cc_s[...]
    else:
        xc_s[...] = xs_ref[0]
        bc_s[...] = bm_ref[0]
        cc_s[...] = cm_ref[0]

    neg_a = -jnp.exp(alog_ref[...])
    ti = lax.broadcasted_iota(jnp.int32, (C, C), 0)
    si = lax.broadcasted_iota(jnp.int32, (C, C), 1)
    keep = (si >= ti) if reverse else (si <= ti)
    lo = lax.broadcasted_iota(jnp.int32, (1, 2 * P), 1) < P
    last = 0 if reverse else C - 1
    nheads = G * HPG

    for c in (reversed(range(nchunk)) if reverse else range(nchunk)):
        rows = slice(c * C, (c + 1) * C)
        dt = _softplus(ps_ref[0, rows, :] + dtb_ref[...])
        cum = _dot01(tri_ref[...], dt * neg_a) * LOG2E
        cum_t = jnp.transpose(cum)[lane0:lane0 + nheads, :]
        dt_t = jnp.transpose(dt)[lane0:lane0 + nheads, :]
        w_t = (dt_t * jnp.exp2(cum_t[:, last:last + 1] - cum_t)).astype(BF16)
        cdl_t = cum_t - jnp.log2(dt_t)
        dec_all = jnp.exp2(cum[last:last + 1, :])
        for g in range(G):
            bm_g = bc_s[rows, g * N:(g + 1) * N]
            cm_g = cc_s[rows, g * N:(g + 1) * N]
            if need_out:
                cb = _dot_nt(cm_g, bm_g).astype(BF16)
            bm_t = jnp.transpose(bm_g.astype(F32)).astype(BF16)
            for pr in range(HPG // 2):
                lanes = slice(g * SSM_GW + pr * 2 * P, g * SSM_GW + (pr + 1) * 2 * P)
                x_pair = xc_s[rows, lanes]
                s_pair = st_ref[g, :, pr * 2 * P:(pr + 1) * 2 * P]
                s_bf = s_pair.astype(BF16)
                y = None
                ds = None
                decs = []
                for half in range(2):
                    j = g * HPG + pr * 2 + half
                    lane = lane0 + j
                    sel = lo if half == 0 else jnp.logical_not(lo)
                    if need_out:
                        bc_ = jnp.broadcast_to(cum[:, lane:lane + 1], (C, C))
                        m = cb * jnp.exp2(jnp.where(keep, bc_ - cdl_t[j:j + 1, :], NEG_BIG)).astype(BF16)
                        cd = cm_g * jnp.exp2(bc_).astype(BF16)
                        lhs = jnp.concatenate([m, cd], axis=1)
                        rhs = jnp.concatenate([jnp.where(sel, x_pair, jnp.zeros_like(x_pair)),
                                               jnp.where(sel, s_bf, jnp.zeros_like(s_bf))], axis=0)
                        yh = _dot(lhs, rhs)
                    dh = _dot(bm_t * w_t[j:j + 1, :],
                              jnp.where(sel, x_pair, jnp.zeros_like(x_pair)))
                    if need_out:
                        y = yh if y is None else y + yh
                    ds = dh if ds is None else ds + dh
                    decs.append(dec_all[:, lane:lane + 1])
                if need_out:
                    ya_s[rows, lanes] = y
                dec = jnp.where(lo, decs[0], decs[1])
                st_ref[g, :, pr * 2 * P:(pr + 1) * 2 * P] = dec * s_pair + ds

    if finalize:
        for g in range(G):
            lanes = slice(g * SSM_GW, (g + 1) * SSM_GW)
            y = ya_s[:, lanes] + yp_ref[0, :, lanes] + dsk_ref[:, lanes] * xc_s[:, lanes].astype(F32)
            y = y * _silu(z_ref[0, :, lanes].astype(F32))
            y = y * lax.rsqrt(jnp.mean(y * y, axis=-1, keepdims=True) + EPS) * nw_ref[:, lanes]
            y_ref[0, :, lanes] = y.astype(y_ref.dtype)
    elif need_out:
        y_ref[0] = ya_s[...]

    @pl.when(blk == pl.num_programs(1) - 1)
    def _():
        sf_ref[0] = st_ref[...]


def _ssd_pass(src, ps, conv_w, conv_b, dt_bias, a_log, s0, *, reverse, finalize, TB, rowlen, emit=False,
              need_out=True, y_prev=None, z_src=None, d_skip_x=None, norm_w=None):
    (xs_a, xs_o), (bm_a, bm_o), (cm_a, cm_o) = src
    B, L, _ = ps.shape
    C = SSD_CHUNK
    nb = L // TB
    G = SSM_GROUPS
    conv = conv_w is not None
    tmap = (lambda i: nb - 1 - i) if reverse else (lambda i: i)
    idx = np.arange(C)
    tri = (idx[None, :] >= idx[:, None]) if reverse else (idx[None, :] <= idx[:, None])
    tri = jnp.asarray(tri.astype(np.float32), BF16)
    in_specs = [
        pl.BlockSpec((1, TB, SSM_INNER), lambda b, i: (b, tmap(i), xs_o)),
        pl.BlockSpec((1, TB, SSM_BC), lambda b, i: (b, tmap(i), bm_o)),
        pl.BlockSpec((1, TB, SSM_BC), lambda b, i: (b, tmap(i), cm_o)),
        pl.BlockSpec((1, TB, P_SMALL), lambda b, i: (b, tmap(i), 0)),
    ]
    args = [xs_a, bm_a, cm_a, ps]
    if conv:
        tc = min(TB, max(rowlen, CONV_ROWS))
        assert TB % tc == 0 and tc % rowlen == 0
        nx = SSM_INNER // SSM_BC
        in_specs += [
            pl.BlockSpec((SSM_CONV, SSM_INNER), lambda b, i: (0, 0)),
            pl.BlockSpec((SSM_CONV, SSM_BC), lambda b, i: (0, nx)),
            pl.BlockSpec((SSM_CONV, SSM_BC), lambda b, i: (0, nx + 1)),
            pl.BlockSpec((1, SSM_INNER), lambda b, i: (0, 0)),
            pl.BlockSpec((1, SSM_BC), lambda b, i: (0, nx)),
            pl.BlockSpec((1, SSM_BC), lambda b, i: (0, nx + 1)),
            _const_spec((SSM_CONV - 1, tc, tc)),
        ]
        args += [conv_w, conv_w, conv_w, conv_b, conv_b, conv_b, _conv_shift_mats(tc, rowlen)]
    in_specs += [_const_spec((1, P_SMALL)), _const_spec((1, P_SMALL)), _const_spec((C, C)),
                 pl.BlockSpec((1, G, SSM_STATE, SSM_GW), lambda b, i: (b, 0, 0, 0))]
    args += [dt_bias, a_log, tri, s0]
    if finalize:
        z_a, z_o = z_src
        in_specs += [
            pl.BlockSpec((1, TB, SSM_INNER), lambda b, i: (b, tmap(i), z_o)),
            pl.BlockSpec((1, TB, SSM_INNER), lambda b, i: (b, tmap(i), 0)),
            _const_spec((1, SSM_INNER)), _const_spec((1, SSM_INNER)),
        ]
        args += [z_a, y_prev, d_skip_x, norm_w]
    out_specs, out_shape = [], []
    if need_out:
        out_specs += [pl.BlockSpec((1, TB, SSM_INNER), lambda b, i: (b, tmap(i), 0))]
        out_shape += [jax.ShapeDtypeStruct((B, L, SSM_INNER), BF16 if finalize else F32)]
    out_specs += [pl.BlockSpec((1, G, SSM_STATE, SSM_GW), lambda b, i: (b, 0, 0, 0))]
    out_shape += [jax.ShapeDtypeStruct((B, G, SSM_STATE, SSM_GW), F32)]
    if emit:
        out_specs += [pl.BlockSpec((1, TB, SSM_INNER), lambda b, i: (b, tmap(i), 0)),
                      pl.BlockSpec((1, TB, SSM_BC), lambda b, i: (b, tmap(i), 0)),
                      pl.BlockSpec((1, TB, SSM_BC), lambda b, i: (b, tmap(i), 0))]
        out_shape += [jax.ShapeDtypeStruct((B, L, SSM_INNER), BF16),
                      jax.ShapeDtypeStruct((B, L, SSM_BC), BF16),
                      jax.ShapeDtypeStruct((B, L, SSM_BC), BF16)]
    kern = functools.partial(_ssd_kernel, C=C, TB=TB, reverse=reverse, finalize=finalize,
                             conv=conv, emit=emit, lane0=S_DTB if reverse else S_DTF, need_out=need_out)
    return pl.pallas_call(
        kern,
        grid=(B, nb),
        in_specs=in_specs,
        out_specs=out_specs,
        out_shape=out_shape,
        scratch_shapes=[pltpu.VMEM((G, SSM_STATE, SSM_GW), F32),
                        pltpu.VMEM((TB, SSM_INNER), BF16),
                        pltpu.VMEM((TB, SSM_BC), BF16),
                        pltpu.VMEM((TB, SSM_BC), BF16)]
                       + ([pltpu.VMEM((TB, SSM_INNER), F32)] if need_out else []),
        compiler_params=_params(("parallel", "arbitrary")),
        name="ssd_" + ("rev" if reverse else "fwd") + ("_fin" if finalize else ""),
    )(*args)


def _merge_kernel(of_ref, ob_ref, r_ref, gnw_ref, sb_ref, ga_ref, gb_ref, x_ref, g1_ref, wpa_ref, wpb_ref,
                  wout_ref, h_ref):
    o = of_ref[0] + ob_ref[0]
    heads = []
    for h in range(GLA_HEADS):
        oh = o[:, h * GLA_DV:(h + 1) * GLA_DV]
        heads.append(oh * lax.rsqrt(jnp.mean(oh * oh, axis=-1, keepdims=True) + EPS) * gnw_ref[...])
    oa = (jnp.concatenate(heads, axis=1) * _silu(r_ref[0].astype(F32))).astype(BF16)
    ya = _dot(oa, wpa_ref[...])
    yb = _dot(sb_ref[0], wpb_ref[...])
    m = _sigmoid(ga_ref[0].astype(F32)) * ya + _sigmoid(gb_ref[0].astype(F32)) * yb
    mix = _dot(m.astype(BF16), wout_ref[...])
    h_ref[0] = x_ref[0] + g1_ref[0] * mix


def _merge(o_f, o_b, gla_norm_w, s_b, pm, x, g1, w_pa, w_pb, w_out, tm):
    B, L, _ = x.shape
    gab, gbb, rb = P_GA // D_MODEL, P_GB // D_MODEL, P_R // GLA_V
    return pl.pallas_call(
        _merge_kernel,
        grid=(B, L // tm),
        in_specs=[pl.BlockSpec((1, tm, GLA_V), lambda b, i: (b, i, 0)),
                  pl.BlockSpec((1, tm, GLA_V), lambda b, i: (b, i, 0)),
                  pl.BlockSpec((1, tm, GLA_V), lambda b, i: (b, i, rb)),
                  _const_spec((1, GLA_DV)),
                  pl.BlockSpec((1, tm, SSM_INNER), lambda b, i: (b, i, 0)),
                  pl.BlockSpec((1, tm, D_MODEL), lambda b, i: (b, i, gab)),
                  pl.BlockSpec((1, tm, D_MODEL), lambda b, i: (b, i, gbb)),
                  pl.BlockSpec((1, tm, D_MODEL), lambda b, i: (b, i, 0)),
                  pl.BlockSpec((1, 1, D_MODEL), lambda b, i: (b, 0, 0)),
                  _const_spec(w_pa.shape), _const_spec(w_pb.shape), _const_spec(w_out.shape)],
        out_specs=pl.BlockSpec((1, tm, D_MODEL), lambda b, i: (b, i, 0)),
        out_shape=jax.ShapeDtypeStruct((B, L, D_MODEL), F32),
        compiler_params=_params(("parallel", "parallel")),
        name="merge",
    )(o_f, o_b, pm, gla_norm_w, s_b, pm, pm, x, g1, w_pa, w_pb, w_out)


FFN_CHUNK = 256


def _ffn_kernel(h_ref, mod_ref, n2_ref, fw_ref, wg_ref, wu_ref, wd_ref, o_ref):
    h = h_ref[0]
    hn = h * lax.rsqrt(jnp.mean(h * h, axis=-1, keepdims=True) + EPS) * n2_ref[...]
    hn = (hn * (1.0 + mod_ref[0, 1:2, :]) + mod_ref[0, 0:1, :]).astype(BF16)
    acc = jnp.zeros(h.shape, F32)
    for c in range(D_FF // FFN_CHUNK):
        cols = slice(c * FFN_CHUNK, (c + 1) * FFN_CHUNK)
        gt = _dot(hn, wg_ref[:, cols])
        up = _dot(hn, wu_ref[:, cols])
        acc = acc + _dot((_silu(gt) * up).astype(BF16), wd_ref[cols, :])
    h2 = h + mod_ref[0, 2:3, :] * acc
    o_ref[0] = h2 * lax.rsqrt(jnp.mean(h2 * h2, axis=-1, keepdims=True) + EPS) * fw_ref[...]


def _ffn(h, mod, n2w, fw, w_gate, w_up, w_down, tm):
    B, L, _ = h.shape
    return pl.pallas_call(
        _ffn_kernel,
        grid=(B, L // tm),
        in_specs=[pl.BlockSpec((1, tm, D_MODEL), lambda b, i: (b, i, 0)),
                  pl.BlockSpec((1, 3, D_MODEL), lambda b, i: (b, 0, 0)),
                  _const_spec((1, D_MODEL)), _const_spec((1, D_MODEL)),
                  pl.BlockSpec(w_gate.shape, lambda b, i: (0, 0), pipeline_mode=pl.Buffered(1)),
                  pl.BlockSpec(w_up.shape, lambda b, i: (0, 0), pipeline_mode=pl.Buffered(1)),
                  pl.BlockSpec(w_down.shape, lambda b, i: (0, 0), pipeline_mode=pl.Buffered(1))],
        out_specs=pl.BlockSpec((1, tm, D_MODEL), lambda b, i: (b, i, 0)),
        out_shape=jax.ShapeDtypeStruct((B, L, D_MODEL), F32),
        compiler_params=_params(("parallel", "parallel")),
        name="ffn",
    )(h, mod, n2w, fw, w_gate, w_up, w_down)


def _pick_block(L, pref):
    tb = min(L, pref)
    assert L % tb == 0
    return tb


def kernel(x, c, ctx, c_ctx, w_ada, b_ada, norm1_w, w_in, gla_up_f, gla_bias_f, gla_up_b, gla_bias_b,
           gla_norm_w, conv_w, conv_b, dt_bias_f, dt_bias_b, a_log_f, a_log_b, d_skip, ssm_norm_w,
           w_pa, w_pb, w_out, norm2_w, w_gate, w_up, w_down, final_norm_w):
    B, L, D = x.shape
    Lc = ctx.shape[1]
    depth = w_ada.shape[0]
    assert depth == 1 and D == D_MODEL
    assert L % GRID_W == 0 and L % SSD_CHUNK == 0 and Lc % SSD_CHUNK == 0
    lay = 0

    nrow = -(-(B + 1) // 8) * 8
    cc = jnp.zeros((nrow, D), F32).at[:B].set(c).at[B].set(c_ctx)
    ada = _ada(cc, w_ada, b_ada[lay][None, :], lay)
    sh1, sc1, g1, sh2, sc2, g2 = [ada[:, i * D:(i + 1) * D] for i in range(6)]
    mod1 = jnp.stack([sh1[:B], sc1[:B]], axis=1)
    mod1_c = jnp.broadcast_to(jnp.stack([sh1[B], sc1[B]])[None], (B, 2, D))
    mod2 = jnp.stack([sh2[:B], sc2[:B], g2[:B]], axis=1)
    g1_l = g1[:B, None, :]

    wt = jnp.swapaxes(w_in[lay], 0, 1)
    w_main, w_small = _regroup(wt)
    nw1 = norm1_w[lay][None, :]

    at_lanes = lambda p, off: jnp.zeros((1, P_SMALL), F32).at[0, off:off + SSM_HEADS].set(p[lay])
    dtb_f, al_f = at_lanes(dt_bias_f, S_DTF), at_lanes(a_log_f, S_DTF)
    dtb_b, al_b = at_lanes(dt_bias_b, S_DTB), at_lanes(a_log_b, S_DTB)
    cw, cb_ = conv_w[lay], conv_b[lay][None, :]
    up_f, up_b = gla_up_f[lay], gla_up_b[lay]
    bi_f, bi_b = gla_bias_f[lay][None, :], gla_bias_b[lay][None, :]

    gla_zero = jnp.zeros((B, GLA_HEADS, GLA_DK, GLA_DV), F32)
    ssd_zero = jnp.zeros((B, SSM_GROUPS, SSM_STATE, SSM_GW), F32)

    pm_c, ps_c = _inproj(ctx, mod1_c, nw1, w_main, w_small, _pick_block(Lc, INPROJ_TM))
    sg_f, sg_b = _gla_bidir(pm_c, ps_c, (up_f, up_b), (bi_f, bi_b), (gla_zero, gla_zero),
                            TB=_pick_block(Lc, GLA_TB), need_out=False)
    src_c = ((pm_c, P_XS // SSM_INNER), (pm_c, P_BM // SSM_BC), (pm_c, P_CM // SSM_BC))
    ss_f, = _ssd_pass(src_c, ps_c, cw, cb_, dtb_f, al_f, ssd_zero, reverse=False, finalize=False,
                      TB=Lc, rowlen=Lc, need_out=False)
    ss_b, = _ssd_pass(src_c, ps_c, cw, cb_, dtb_b, al_b, ssd_zero, reverse=True, finalize=False,
                      TB=Lc, rowlen=Lc, need_out=False)

    pm, ps = _inproj(x, mod1, nw1, w_main, w_small, _pick_block(L, INPROJ_TM))
    tbg = _pick_block(L, GLA_TB)
    og_f, _, og_b, _ = _gla_bidir(pm, ps, (up_f, up_b), (bi_f, bi_b), (sg_f, sg_b), TB=tbg)
    tbs = _pick_block(L, SSD_TB)
    src = ((pm, P_XS // SSM_INNER), (pm, P_BM // SSM_BC), (pm, P_CM // SSM_BC))
    ys_f, _, xc, bc, cc = _ssd_pass(src, ps, cw, cb_, dtb_f, al_f, ss_f, reverse=False, finalize=False,
                                    TB=tbs, rowlen=GRID_W, emit=True)
    o_b, _ = _ssd_pass(((xc, 0), (bc, 0), (cc, 0)), ps, None, None, dtb_b, al_b, ss_b, reverse=True,
                       finalize=True, TB=tbs, rowlen=GRID_W, y_prev=ys_f, z_src=(pm, P_Z // SSM_INNER),
                       d_skip_x=jnp.repeat(d_skip[lay], SSM_HEADDIM)[None, :],
                       norm_w=ssm_norm_w[lay][None, :])

    h = _merge(og_f, og_b, gla_norm_w[lay][None, :], o_b, pm, x, g1_l, w_pa[lay].astype(BF16),
               w_pb[lay].astype(BF16), w_out[lay].astype(BF16), _pick_block(L, MERGE_TM))
    return _ffn(h, mod2, norm2_w[lay][None, :], final_norm_w[None, :], w_gate[lay].astype(BF16),
                w_up[lay].astype(BF16), w_down[lay].astype(BF16), _pick_block(L, FFN_TM))
```

```python
import functools

import numpy as np
import jax
import jax.numpy as jnp
from jax import lax
from jax.experimental import pallas as pl
from jax.experimental.pallas import tpu as pltpu

F32 = jnp.float32
BF16 = jnp.bfloat16

D_MODEL = 1024
GRID_W = 64
EPS = 1e-6

GLA_HEADS = 4
GLA_DK = 128
GLA_DV = 256
GLA_QK = GLA_HEADS * GLA_DK
GLA_V = GLA_HEADS * GLA_DV
GLA_RANK = 16
GLA_TAU = 16.0

SSM_INNER = 2 * D_MODEL
SSM_HEADDIM = 64
SSM_HEADS = SSM_INNER // SSM_HEADDIM
SSM_GROUPS = 4
SSM_HPG = SSM_HEADS // SSM_GROUPS
SSM_STATE = 128
SSM_BC = SSM_GROUPS * SSM_STATE
SSM_CONV = 4
CONV_LEFT = 2
SSM_GW = SSM_HPG * SSM_HEADDIM

D_FF = ((8 * D_MODEL // 3 + 255) // 256) * 256

_IN_WIDTHS = (GLA_QK, GLA_QK, GLA_V, GLA_V, GLA_RANK, GLA_RANK,
              SSM_INNER, SSM_INNER, SSM_BC, SSM_BC, SSM_HEADS, SSM_HEADS, D_MODEL, D_MODEL)
_IN_OFF = np.concatenate([[0], np.cumsum(_IN_WIDTHS)]).tolist()

P_Z, P_XS = 0, 2048
P_Q, P_K, P_V, P_R = 4096, 4608, 5120, 6144
P_BM, P_CM = 7168, 7680
P_GA, P_GB = 8192, 9216
P_MAIN = 10240
S_LRF, S_LRB, S_DTF, S_DTB = 0, 16, 32, 64
P_SMALL = 128

GLA_CHUNK = 128
SSD_CHUNK = 128

ADA_TN = 1536
INPROJ_TM, INPROJ_TN = 1024, 2560
GLA_TB = 2048
SSD_TB = 512
MERGE_TM = 512
FFN_TM = 512

VMEM_LIMIT = 56 * 1024 * 1024
NEG_BIG = -1e30
LOG2E = 1.4426950408889634


def _sigmoid(x):
    return 1.0 / (1.0 + jnp.exp(-x))


def _silu(x):
    return x * _sigmoid(x)


def _softplus(x):
    return jnp.maximum(x, 0.0) + jnp.log(1.0 + jnp.exp(-jnp.abs(x)))


def _split3(x):
    hi = x.astype(BF16)
    r1 = x - hi.astype(F32)
    mid = r1.astype(BF16)
    lo = (r1 - mid.astype(F32)).astype(BF16)
    return hi, mid, lo


def _dot(a, b):
    return jnp.dot(a, b, preferred_element_type=F32)


def _dot_nt(a, b):
    return lax.dot_general(a, b, (((1,), (1,)), ((), ())), preferred_element_type=F32)


def _dot_tn(a, b):
    return lax.dot_general(a, b, (((0,), (0,)), ((), ())), preferred_element_type=F32)


def _dot01(m01, x):
    hi, mid, lo = _split3(x)
    return _dot(m01, hi) + _dot(m01, mid) + _dot(m01, lo)


def _params(sem):
    return pltpu.CompilerParams(dimension_semantics=sem, vmem_limit_bytes=VMEM_LIMIT)


def _const_spec(shape):
    n = len(shape)
    return pl.BlockSpec(shape, lambda *_: (0,) * n)


def _ada_kernel(c_ref, w_ref, b_ref, o_ref):
    s = _silu(c_ref[...])
    w = w_ref[0]
    s_hi = s.astype(BF16)
    s_lo = (s - s_hi.astype(F32)).astype(BF16)
    w_hi = w.astype(BF16)
    w_lo = (w - w_hi.astype(F32)).astype(BF16)
    o_ref[...] = _dot(s_hi, w_hi) + _dot(s_hi, w_lo) + _dot(s_lo, w_hi) + b_ref[...]


def _ada(cc, w, b, lay):
    rows = cc.shape[0]
    n = w.shape[2]
    tn = ADA_TN
    return pl.pallas_call(
        _ada_kernel,
        grid=(n // tn,),
        in_specs=[pl.BlockSpec((rows, D_MODEL), lambda j: (0, 0)),
                  pl.BlockSpec((1, D_MODEL, tn), lambda j: (lay, 0, j)),
                  pl.BlockSpec((1, tn), lambda j: (0, j))],
        out_specs=pl.BlockSpec((rows, tn), lambda j: (0, j)),
        out_shape=jax.ShapeDtypeStruct((rows, n), F32),
        compiler_params=_params(("arbitrary",)),
        name="ada",
    )(cc, w, b)


REGROUP_ROWS = 1024
REGROUP_ALIGN = 16
_MAIN_RUNS = ((P_Z, _IN_OFF[6], 2 * SSM_INNER), (P_Q, _IN_OFF[0], 2 * GLA_QK + 2 * GLA_V),
              (P_BM, _IN_OFF[8], 2 * SSM_BC), (P_GA, _IN_OFF[12], 2 * D_MODEL))


def _regroup_kernel(wt_ref, lr_ref, dt_ref, wm_ref, ws_ref):
    wm_ref[...] = jnp.transpose(wt_ref[...]).astype(BF16)
    pad = jnp.zeros((P_SMALL - lr_ref.shape[0] - dt_ref.shape[0], lr_ref.shape[1]), F32)
    ws_ref[...] = jnp.transpose(jnp.concatenate([lr_ref[...], dt_ref[...], pad], axis=0)).astype(BF16)


def _regroup(wt):
    _, D = wt.shape
    R = REGROUP_ROWS
    U = REGROUP_ALIGN
    assert all(dst % R == 0 and n % R == 0 and src % U == 0 for dst, src, n in _MAIN_RUNS)

    def src_row(i):
        units = jnp.int32(0)
        for dst, src, n in _MAIN_RUNS:
            inside = (i * R >= dst) & (i * R < dst + n)
            units = jnp.where(inside, (src - dst) // U + i * (R // U), units)
        return units * U

    return pl.pallas_call(
        _regroup_kernel,
        grid=(P_MAIN // R,),
        in_specs=[pl.BlockSpec((pl.Element(R), pl.Element(D)), lambda i: (src_row(i), 0)),
                  pl.BlockSpec((pl.Element(2 * GLA_RANK), pl.Element(D)), lambda i: (_IN_OFF[4], 0)),
                  pl.BlockSpec((pl.Element(2 * SSM_HEADS), pl.Element(D)), lambda i: (_IN_OFF[10], 0))],
        out_specs=[pl.BlockSpec((D, R), lambda i: (0, i)),
                   pl.BlockSpec((D, P_SMALL), lambda i: (0, 0))],
        out_shape=[jax.ShapeDtypeStruct((D, P_MAIN), BF16), jax.ShapeDtypeStruct((D, P_SMALL), BF16)],
        compiler_params=_params(("arbitrary",)),
        name="regroup",
    )(wt, wt, wt)


def _inproj_kernel(x_ref, mod_ref, nw_ref, wm_ref, ws_ref, om_ref, os_ref, xn_ref):
    @pl.when(pl.program_id(2) == 0)
    def _():
        x = x_ref[0]
        y = x * lax.rsqrt(jnp.mean(x * x, axis=-1, keepdims=True) + EPS) * nw_ref[...]
        y = y * (1.0 + mod_ref[0, 1:2, :]) + mod_ref[0, 0:1, :]
        xn = y.astype(BF16)
        xn_ref[...] = xn
        os_ref[0] = _dot(xn, ws_ref[...])

    om_ref[0] = _dot(xn_ref[...], wm_ref[...]).astype(BF16)


def _inproj(x, mod, nw, w_main, w_small, tm):
    B, L, _ = x.shape
    tn = INPROJ_TN
    return pl.pallas_call(
        _inproj_kernel,
        grid=(B, L // tm, P_MAIN // tn),
        in_specs=[pl.BlockSpec((1, tm, D_MODEL), lambda b, i, j: (b, i, 0)),
                  pl.BlockSpec((1, 2, D_MODEL), lambda b, i, j: (b, 0, 0)),
                  pl.BlockSpec((1, D_MODEL), lambda b, i, j: (0, 0)),
                  pl.BlockSpec((D_MODEL, tn), lambda b, i, j: (0, j)),
                  pl.BlockSpec((D_MODEL, P_SMALL), lambda b, i, j: (0, 0))],
        out_specs=[pl.BlockSpec((1, tm, tn), lambda b, i, j: (b, i, j)),
                   pl.BlockSpec((1, tm, P_SMALL), lambda b, i, j: (b, i, 0))],
        out_shape=[jax.ShapeDtypeStruct((B, L, P_MAIN), BF16),
                   jax.ShapeDtypeStruct((B, L, P_SMALL), F32)],
        scratch_shapes=[pltpu.VMEM((tm, D_MODEL), BF16)],
        compiler_params=_params(("parallel", "parallel", "arbitrary")),
        name="inproj",
    )(x, mod, nw, w_main, w_small)


GLA_GROUP = 2


def _gla_consts(C, reverse):
    NL = int(np.log2(C))
    idx = np.arange(C)
    tri = (idx[None, :] <= idx[:, None]).astype(np.float32)
    mats = [tri]
    masks = [np.eye(C, dtype=np.float32)]
    refs, signs = [], []
    for lev in range(NL):
        h = 1 << lev
        blk = idx // (2 * h)
        half = (idx // h) % 2
        ref = blk * 2 * h + h - 1
        if lev > 0:
            sg = np.where(half == 1, 1.0, -1.0).astype(np.float32)
            if reverse:
                ref, sg = (C - 1 - ref)[::-1], sg[::-1]
            refs.append([int(ref[m * 2 * h]) for m in range(C // (2 * h))])
            signs.append(np.broadcast_to(sg[:, None], (C, GLA_DK)))
        masks.append(((blk[:, None] == blk[None, :]) & (half[:, None] == 1)
                      & (half[None, :] == 0)).astype(np.float32))
    if reverse:
        mats = [m[::-1, ::-1] for m in mats]
        masks = [m[::-1, ::-1] for m in masks]
    dmat = np.concatenate(mats, axis=0)
    return (jnp.asarray(dmat, BF16), jnp.asarray(np.stack(masks), F32),
            jnp.asarray(np.stack(signs), F32), refs)


def _gla_stream(q_ref, k_ref, v_ref, lr_ref, up_ref, bias_ref, dm_ref, mk_ref, sg_ref, s0_ref, *rest,
                C, TB, reverse, refs, part, need_out):
    if need_out:
        o_ref, sf_ref, st_ref, e_s, b_s, oi_s, kv_s = rest
    else:
        sf_ref, st_ref, e_s, b_s, kv_s = rest
    NL = int(np.log2(C))
    nchunk = TB // C
    blk = pl.program_id(2)
    last = 0 if reverse else C - 1
    lr_off = S_LRB if reverse else S_LRF

    if part == "init":
        @pl.when(blk == 0)
        def _():
            st_ref[...] = s0_ref[0, 0]
        return
    if part == "final":
        @pl.when(blk == pl.num_programs(2) - 1)
        def _():
            sf_ref[0, 0] = st_ref[...]
        return

    lr = lr_ref[0][:, lr_off:lr_off + GLA_RANK]
    pre = _dot(lr.astype(BF16), up_ref[...].astype(BF16)) + bias_ref[...]
    g = -_softplus(-pre) * (1.0 / GLA_TAU)
    g_hi = g.astype(BF16)
    g_lo = (g - g_hi.astype(F32)).astype(BF16)
    gs = jnp.concatenate([g_hi, g_lo], axis=1)

    parity = lax.broadcasted_iota(jnp.int32, (C, GLA_DK), 0) & 1
    later = parity == (0 if reverse else 1)
    for c in range(nchunk):
        ex = _dot(dm_ref[...], gs[c * C:(c + 1) * C])
        b = ex[:, :GLA_DK] + ex[:, GLA_DK:]
        b_s[c] = b
        if need_out:
            e_s[c, 0:C, :] = jnp.exp(jnp.where(later, g[c * C:(c + 1) * C], 0.0))
            for li, lev in enumerate(range(1, NL)):
                h2 = 2 << lev
                bref = jnp.concatenate([jnp.broadcast_to(b[r:r + 1, :], (h2, GLA_DK)) for r in refs[li]],
                                       axis=0)
                e_s[c, lev * C:(lev + 1) * C, :] = jnp.exp(sg_ref[li] * (b - bref))
        e_s[c, NL * C:(NL + 1) * C, :] = jnp.exp(b[last:last + 1, :] - b)

    for c in range(nchunk):
        rows = slice(c * C, (c + 1) * C)
        if need_out:
            q = q_ref[0, rows, :].astype(F32) * (GLA_DK ** -0.5)
        k = k_ref[0, rows, :].astype(F32)
        v = v_ref[0, rows, :]
        if need_out:
            att = mk_ref[0] * _dot_nt(q.astype(BF16), k.astype(BF16))
            for lev in range(NL):
                e_l = e_s[c, lev * C:(lev + 1) * C, :]
                att = att + mk_ref[lev + 1] * _dot_nt((q * e_l).astype(BF16), (k * e_l).astype(BF16))
            oi_s[rows, :] = _dot(att.astype(BF16), v)
        kv_s[c] = _dot_tn((k * e_s[c, NL * C:(NL + 1) * C, :]).astype(BF16), v)

    order = list(reversed(range(nchunk))) if reverse else list(range(nchunk))
    st = st_ref[...]
    for g0 in range(0, nchunk, GLA_GROUP):
        grp = order[g0:g0 + GLA_GROUP]
        if need_out:
            wcat = jnp.concatenate([st.astype(BF16)] + [kv_s[c].astype(BF16) for c in grp[:-1]], axis=0)
        for i, c in enumerate(grp if need_out else ()):
            rows = slice(c * C, (c + 1) * C)
            q = q_ref[0, rows, :].astype(F32) * (GLA_DK ** -0.5)
            expo = b_s[c]
            pieces = []
            for j in range(i - 1, -2, -1):
                pieces.insert(0, (q * jnp.exp(expo)).astype(BF16))
                if j >= 0:
                    expo = expo + b_s[grp[j], last:last + 1, :]
            lhs = jnp.concatenate(pieces, axis=1)
            o_ref[0, rows, :] = oi_s[rows, :] + _dot(lhs, wcat[:(i + 1) * GLA_DK, :])
        for c in grp:
            dec = jnp.transpose(jnp.broadcast_to(jnp.exp(b_s[c, last:last + 1, :]), (GLA_DK, GLA_DK)))
            st = jnp.concatenate([dec] * (GLA_DV // GLA_DK), axis=1) * st + kv_s[c]
    st_ref[...] = st


GLA_STREAM_IN = 10


def _gla_kernel(*refs, C, TB, lv_refs, need_out):
    n_in = GLA_STREAM_IN
    n_out, n_scr = (2, 5) if need_out else (1, 4)
    ins = [refs[d * n_in:(d + 1) * n_in] for d in range(2)]
    outs = [refs[2 * n_in + d * n_out:2 * n_in + (d + 1) * n_out] for d in range(2)]
    base = 2 * (n_in + n_out)
    scr = [refs[base + d * n_scr:base + (d + 1) * n_scr] for d in range(2)]
    for part in ("init", "body", "final"):
        for d, reverse in enumerate((False, True)):
            _gla_stream(*ins[d], *outs[d], *scr[d], C=C, TB=TB, reverse=reverse, refs=lv_refs[d], part=part,
                        need_out=need_out)


def _gla_bidir(pm, ps, up, bias, s0, *, TB, need_out=True):
    B, L, _ = pm.shape
    C = GLA_CHUNK
    nb = L // TB
    qb, kb, vb = P_Q // GLA_DK, P_K // GLA_DK, P_V // GLA_DV
    in_specs, args, out_specs, out_shape, scratch, lv_refs = [], [], [], [], [], []
    for d, reverse in enumerate((False, True)):
        dmat, masks, signs, refs = _gla_consts(C, reverse)
        lv_refs.append(refs)
        tmap = (lambda i: nb - 1 - i) if reverse else (lambda i: i)
        in_specs += [
            pl.BlockSpec((1, TB, GLA_DK), lambda b, h, i, tmap=tmap: (b, tmap(i), qb + h)),
            pl.BlockSpec((1, TB, GLA_DK), lambda b, h, i, tmap=tmap: (b, tmap(i), kb + h)),
            pl.BlockSpec((1, TB, GLA_DV), lambda b, h, i, tmap=tmap: (b, tmap(i), vb + h)),
            pl.BlockSpec((1, TB, P_SMALL), lambda b, h, i, tmap=tmap: (b, tmap(i), 0)),
            pl.BlockSpec((GLA_RANK, GLA_DK), lambda b, h, i: (0, h)),
            pl.BlockSpec((1, GLA_DK), lambda b, h, i: (0, h)),
            _const_spec(dmat.shape),
            _const_spec(masks.shape),
            _const_spec(signs.shape),
            pl.BlockSpec((1, 1, GLA_DK, GLA_DV), lambda b, h, i: (b, h, 0, 0)),
        ]
        args += [pm, pm, pm, ps, up[d], bias[d], dmat, masks, signs, s0[d]]
        if need_out:
            out_specs += [pl.BlockSpec((1, TB, GLA_DV), lambda b, h, i, tmap=tmap: (b, tmap(i), h))]
            out_shape += [jax.ShapeDtypeStruct((B, L, GLA_V), F32)]
        out_specs += [pl.BlockSpec((1, 1, GLA_DK, GLA_DV), lambda b, h, i: (b, h, 0, 0))]
        out_shape += [jax.ShapeDtypeStruct((B, GLA_HEADS, GLA_DK, GLA_DV), F32)]
        scratch += [pltpu.VMEM((GLA_DK, GLA_DV), F32),
                    pltpu.VMEM((TB // C, masks.shape[0] * C, GLA_DK), F32),
                    pltpu.VMEM((TB // C, C, GLA_DK), F32)]
        if need_out:
            scratch += [pltpu.VMEM((TB, GLA_DV), F32)]
        scratch += [pltpu.VMEM((TB // C, GLA_DK, GLA_DV), F32)]
    assert len(in_specs) == 2 * GLA_STREAM_IN
    return pl.pallas_call(
        functools.partial(_gla_kernel, C=C, TB=TB, lv_refs=lv_refs, need_out=need_out),
        grid=(B, GLA_HEADS, nb),
        in_specs=in_specs,
        out_specs=out_specs,
        out_shape=out_shape,
        scratch_shapes=scratch,
        compiler_params=_params(("parallel", "parallel", "arbitrary")),
        name="gla",
    )(*args)


CONV_ROWS = 256


def _conv_shift_mats(T, rowlen):
    t = np.arange(T)
    mats = []
    for j in range(SSM_CONV):
        off = j - CONV_LEFT
        if off == 0:
            continue
        src = t + off
        ok = (src // rowlen == t // rowlen) & (src >= 0) & (src < T)
        m = np.zeros((T, T), np.float32)
        m[t[ok], src[ok]] = 1.0
        mats.append(m)
    return jnp.asarray(np.stack(mats), BF16)


def _conv_silu(u, w, b, sh_ref):
    acc = b + u.astype(F32) * w[CONV_LEFT:CONV_LEFT + 1, :]
    taps = [j for j in range(SSM_CONV) if j != CONV_LEFT]
    for i, j in enumerate(taps):
        acc = acc + _dot(sh_ref[i], u) * w[j:j + 1, :]
    return _silu(acc)


def _ssd_kernel(*refs, C, TB, reverse, finalize, conv, emit, lane0, need_out):
    refs = list(refs)
    xs_ref, bm_ref, cm_ref, ps_ref = refs[:4]
    refs = refs[4:]
    if conv:
        wx_ref, wb_ref, wc_ref, bx_ref, bb_ref, bc_ref, sh_ref = refs[:7]
        refs = refs[7:]
    dtb_ref, alog_ref, tri_ref, s0_ref = refs[:4]
    refs = refs[4:]
    if finalize:
        z_ref, yp_ref, dsk_ref, nw_ref = refs[:4]
        refs = refs[4:]
    if need_out:
        y_ref = refs[0]
        refs = refs[1:]
    sf_ref = refs[0]
    refs = refs[1:]
    if emit:
        xo_ref, bo_ref, co_ref = refs[:3]
        refs = refs[3:]
    if need_out:
        st_ref, xc_s, bc_s, cc_s, ya_s = refs
    else:
        st_ref, xc_s, bc_s, cc_s = refs
    nchunk = TB // C
    blk = pl.program_id(1)
    G, HPG, P, N = SSM_GROUPS, SSM_HPG, SSM_HEADDIM, SSM_STATE

    @pl.when(blk == 0)
    def _():
        st_ref[...] = s0_ref[0]

    if conv:
        tc = sh_ref.shape[1]
        for r0 in range(0, TB, tc):
            rs = slice(r0, r0 + tc)
            xc_s[rs, :] = _conv_silu(xs_ref[0, rs, :], wx_ref[...], bx_ref[...], sh_ref).astype(BF16)
            bc_s[rs, :] = _conv_silu(bm_ref[0, rs, :], wb_ref[...], bb_ref[...], sh_ref).astype(BF16)
            cc_s[rs, :] = _conv_silu(cm_ref[0, rs, :], wc_ref[...], bc_ref[...], sh_ref).astype(BF16)
        if emit:
            xo_ref[0] = xc_s[...]
            bo_ref[0] = bc_s[...]
            co_ref[0] = cc_s[...]
    else:
        xc_s[...] = xs_ref[0]
        bc_s[...] = bm_ref[0]
        cc_s[...] = cm_ref[0]

    neg_a = -jnp.exp(alog_ref[...])
    ti = lax.broadcasted_iota(jnp.int32, (C, C), 0)
    si = lax.broadcasted_iota(jnp.int32, (C, C), 1)
    keep = (si >= ti) if reverse else (si <= ti)
    lo = lax.broadcasted_iota(jnp.int32, (1, 2 * P), 1) < P
    last = 0 if reverse else C - 1
    nheads = G * HPG

    for c in (reversed(range(nchunk)) if reverse else range(nchunk)):
        rows = slice(c * C, (c + 1) * C)
        dt = _softplus(ps_ref[0, rows, :] + dtb_ref[...])
        cum = _dot01(tri_ref[...], dt * neg_a) * LOG2E
        cum_t = jnp.transpose(cum)[lane0:lane0 + nheads, :]
        dt_t = jnp.transpose(dt)[lane0:lane0 + nheads, :]
        w_t = (dt_t * jnp.exp2(cum_t[:, last:last + 1] - cum_t)).astype(BF16)
        cdl_t = cum_t - jnp.log2(dt_t)
        dec_all = jnp.exp2(cum[last:last + 1, :])
        for g in range(G):
            bm_g = bc_s[rows, g * N:(g + 1) * N]
            cm_g = cc_s[rows, g * N:(g + 1) * N]
            if need_out:
                cb = _dot_nt(cm_g, bm_g).astype(BF16)
            bm_t = jnp.transpose(bm_g.astype(F32)).astype(BF16)
            for pr in range(HPG // 2):
                lanes = slice(g * SSM_GW + pr * 2 * P, g * SSM_GW + (pr + 1) * 2 * P)
                x_pair = xc_s[rows, lanes]
                s_pair = st_ref[g, :, pr * 2 * P:(pr + 1) * 2 * P]
                s_bf = s_pair.astype(BF16)
                y = None
                ds = None
                decs = []
                for half in range(2):
                    j = g * HPG + pr * 2 + half
                    lane = lane0 + j
                    sel = lo if half == 0 else jnp.logical_not(lo)
                    if need_out:
                        bc_ = jnp.broadcast_to(cum[:, lane:lane + 1], (C, C))
                        m = cb * jnp.exp2(jnp.where(keep, bc_ - cdl_t[j:j + 1, :], NEG_BIG)).astype(BF16)
                        cd = cm_g * jnp.exp2(bc_).astype(BF16)
                        lhs = jnp.concatenate([m, cd], axis=1)
                        rhs = jnp.concatenate([jnp.where(sel, x_pair, jnp.zeros_like(x_pair)),
                                               jnp.where(sel, s_bf, jnp.zeros_like(s_bf))], axis=0)
                        yh = _dot(lhs, rhs)
                    dh = _dot(bm_t * w_t[j:j + 1, :],
                              jnp.where(sel, x_pair, jnp.zeros_like(x_pair)))
                    if need_out:
                        y = yh if y is None else y + yh
                    ds = dh if ds is None else ds + dh
                    decs.append(dec_all[:, lane:lane + 1])
                if need_out:
                    ya_s[rows, lanes] = y
                dec = jnp.where(lo, decs[0], decs[1])
                st_ref[g, :, pr * 2 * P:(pr + 1) * 2 * P] = dec * s_pair + ds

    if finalize:
        for g in range(G):
            lanes = slice(g * SSM_GW, (g + 1) * SSM_GW)
            y = ya_s[:, lanes] + yp_ref[0, :, lanes] + dsk_ref[:, lanes] * xc_s[:, lanes].astype(F32)
            y = y * _silu(z_ref[0, :, lanes].astype(F32))
            y = y * lax.rsqrt(jnp.mean(y * y, axis=-1, keepdims=True) + EPS) * nw_ref[:, lanes]
            y_ref[0, :, lanes] = y.astype(y_ref.dtype)
    elif need_out:
        y_ref[0] = ya_s[...]

    @pl.when(blk == pl.num_programs(1) - 1)
    def _():
        sf_ref[0] = st_ref[...]


def _ssd_pass(src, ps, conv_w, conv_b, dt_bias, a_log, s0, *, reverse, finalize, TB, rowlen, emit=False,
              need_out=True, y_prev=None, z_src=None, d_skip_x=None, norm_w=None):
    (xs_a, xs_o), (bm_a, bm_o), (cm_a, cm_o) = src
    B, L, _ = ps.shape
    C = SSD_CHUNK
    nb = L // TB
    G = SSM_GROUPS
    conv = conv_w is not None
    tmap = (lambda i: nb - 1 - i) if reverse else (lambda i: i)
    idx = np.arange(C)
    tri = (idx[None, :] >= idx[:, None]) if reverse else (idx[None, :] <= idx[:, None])
    tri = jnp.asarray(tri.astype(np.float32), BF16)
    in_specs = [
        pl.BlockSpec((1, TB, SSM_INNER), lambda b, i: (b, tmap(i), xs_o)),
        pl.BlockSpec((1, TB, SSM_BC), lambda b, i: (b, tmap(i), bm_o)),
        pl.BlockSpec((1, TB, SSM_BC), lambda b, i: (b, tmap(i), cm_o)),
        pl.BlockSpec((1, TB, P_SMALL), lambda b, i: (b, tmap(i), 0)),
    ]
    args = [xs_a, bm_a, cm_a, ps]
    if conv:
        tc = min(TB, max(rowlen, CONV_ROWS))
        assert TB % tc == 0 and tc % rowlen == 0
        nx = SSM_INNER // SSM_BC
        in_specs += [
            pl.BlockSpec((SSM_CONV, SSM_INNER), lambda b, i: (0, 0)),
            pl.BlockSpec((SSM_CONV, SSM_BC), lambda b, i: (0, nx)),
            pl.BlockSpec((SSM_CONV, SSM_BC), lambda b, i: (0, nx + 1)),
            pl.BlockSpec((1, SSM_INNER), lambda b, i: (0, 0)),
            pl.BlockSpec((1, SSM_BC), lambda b, i: (0, nx)),
            pl.BlockSpec((1, SSM_BC), lambda b, i: (0, nx + 1)),
            _const_spec((SSM_CONV - 1, tc, tc)),
        ]
        args += [conv_w, conv_w, conv_w, conv_b, conv_b, conv_b, _conv_shift_mats(tc, rowlen)]
    in_specs += [_const_spec((1, P_SMALL)), _const_spec((1, P_SMALL)), _const_spec((C, C)),
                 pl.BlockSpec((1, G, SSM_STATE, SSM_GW), lambda b, i: (b, 0, 0, 0))]
    args += [dt_bias, a_log, tri, s0]
    if finalize:
        z_a, z_o = z_src
        in_specs += [
            pl.BlockSpec((1, TB, SSM_INNER), lambda b, i: (b, tmap(i), z_o)),
            pl.BlockSpec((1, TB, SSM_INNER), lambda b, i: (b, tmap(i), 0)),
            _const_spec((1, SSM_INNER)), _const_spec((1, SSM_INNER)),
        ]
        args += [z_a, y_prev, d_skip_x, norm_w]
    out_specs, out_shape = [], []
    if need_out:
        out_specs += [pl.BlockSpec((1, TB, SSM_INNER), lambda b, i: (b, tmap(i), 0))]
        out_shape += [jax.ShapeDtypeStruct((B, L, SSM_INNER), BF16 if finalize else F32)]
    out_specs += [pl.BlockSpec((1, G, SSM_STATE, SSM_GW), lambda b, i: (b, 0, 0, 0))]
    out_shape += [jax.ShapeDtypeStruct((B, G, SSM_STATE, SSM_GW), F32)]
    if emit:
        out_specs += [pl.BlockSpec((1, TB, SSM_INNER), lambda b, i: (b, tmap(i), 0)),
                      pl.BlockSpec((1, TB, SSM_BC), lambda b, i: (b, tmap(i), 0)),
                      pl.BlockSpec((1, TB, SSM_BC), lambda b, i: (b, tmap(i), 0))]
        out_shape += [jax.ShapeDtypeStruct((B, L, SSM_INNER), BF16),
                      jax.ShapeDtypeStruct((B, L, SSM_BC), BF16),
                      jax.ShapeDtypeStruct((B, L, SSM_BC), BF16)]
    kern = functools.partial(_ssd_kernel, C=C, TB=TB, reverse=reverse, finalize=finalize,
                             conv=conv, emit=emit, lane0=S_DTB if reverse else S_DTF, need_out=need_out)
    return pl.pallas_call(
        kern,
        grid=(B, nb),
        in_specs=in_specs,
        out_specs=out_specs,
        out_shape=out_shape,
        scratch_shapes=[pltpu.VMEM((G, SSM_STATE, SSM_GW), F32),
                        pltpu.VMEM((TB, SSM_INNER), BF16),
                        pltpu.VMEM((TB, SSM_BC), BF16),
                        pltpu.VMEM((TB, SSM_BC), BF16)]
                       + ([pltpu.VMEM((TB, SSM_INNER), F32)] if need_out else []),
        compiler_params=_params(("parallel", "arbitrary")),
        name="ssd_" + ("rev" if reverse else "fwd") + ("_fin" if finalize else ""),
    )(*args)


def _merge_kernel(of_ref, ob_ref, r_ref, gnw_ref, sb_ref, ga_ref, gb_ref, x_ref, g1_ref, wpa_ref, wpb_ref,
                  wout_ref, h_ref):
    o = of_ref[0] + ob_ref[0]
    heads = []
    for h in range(GLA_HEADS):
        oh = o[:, h * GLA_DV:(h + 1) * GLA_DV]
        heads.append(oh * lax.rsqrt(jnp.mean(oh * oh, axis=-1, keepdims=True) + EPS) * gnw_ref[...])
    oa = (jnp.concatenate(heads, axis=1) * _silu(r_ref[0].astype(F32))).astype(BF16)
    ya = _dot(oa, wpa_ref[...])
    yb = _dot(sb_ref[0], wpb_ref[...])
    m = _sigmoid(ga_ref[0].astype(F32)) * ya + _sigmoid(gb_ref[0].astype(F32)) * yb
    mix = _dot(m.astype(BF16), wout_ref[...])
    h_ref[0] = x_ref[0] + g1_ref[0] * mix


def _merge(o_f, o_b, gla_norm_w, s_b, pm, x, g1, w_pa, w_pb, w_out, tm):
    B, L, _ = x.shape
    gab, gbb, rb = P_GA // D_MODEL, P_GB // D_MODEL, P_R // GLA_V
    return pl.pallas_call(
        _merge_kernel,
        grid=(B, L // tm),
        in_specs=[pl.BlockSpec((1, tm, GLA_V), lambda b, i: (b, i, 0)),
                  pl.BlockSpec((1, tm, GLA_V), lambda b, i: (b, i, 0)),
                  pl.BlockSpec((1, tm, GLA_V), lambda b, i: (b, i, rb)),
                  _const_spec((1, GLA_DV)),
                  pl.BlockSpec((1, tm, SSM_INNER), lambda b, i: (b, i, 0)),
                  pl.BlockSpec((1, tm, D_MODEL), lambda b, i: (b, i, gab)),
                  pl.BlockSpec((1, tm, D_MODEL), lambda b, i: (b, i, gbb)),
                  pl.BlockSpec((1, tm, D_MODEL), lambda b, i: (b, i, 0)),
                  pl.BlockSpec((1, 1, D_MODEL), lambda b, i: (b, 0, 0)),
                  _const_spec(w_pa.shape), _const_spec(w_pb.shape), _const_spec(w_out.shape)],
        out_specs=pl.BlockSpec((1, tm, D_MODEL), lambda b, i: (b, i, 0)),
        out_shape=jax.ShapeDtypeStruct((B, L, D_MODEL), F32),
        compiler_params=_params(("parallel", "parallel")),
        name="merge",
    )(o_f, o_b, pm, gla_norm_w, s_b, pm, pm, x, g1, w_pa, w_pb, w_out)


FFN_CHUNK = 256


def _ffn_kernel(h_ref, mod_ref, n2_ref, fw_ref, wg_ref, wu_ref, wd_ref, o_ref):
    h = h_ref[0]
    hn = h * lax.rsqrt(jnp.mean(h * h, axis=-1, keepdims=True) + EPS) * n2_ref[...]
    hn = (hn * (1.0 + mod_ref[0, 1:2, :]) + mod_ref[0, 0:1, :]).astype(BF16)
    acc = jnp.zeros(h.shape, F32)
    for c in range(D_FF // FFN_CHUNK):
        cols = slice(c * FFN_CHUNK, (c + 1) * FFN_CHUNK)
        gt = _dot(hn, wg_ref[:, cols])
        up = _dot(hn, wu_ref[:, cols])
        acc = acc + _dot((_silu(gt) * up).astype(BF16), wd_ref[cols, :])
    h2 = h + mod_ref[0, 2:3, :] * acc
    o_ref[0] = h2 * lax.rsqrt(jnp.mean(h2 * h2, axis=-1, keepdims=True) + EPS) * fw_ref[...]


def _ffn(h, mod, n2w, fw, w_gate, w_up, w_down, tm):
    B, L, _ = h.shape
    return pl.pallas_call(
        _ffn_kernel,
        grid=(B, L // tm),
        in_specs=[pl.BlockSpec((1, tm, D_MODEL), lambda b, i: (b, i, 0)),
                  pl.BlockSpec((1, 3, D_MODEL), lambda b, i: (b, 0, 0)),
                  _const_spec((1, D_MODEL)), _const_spec((1, D_MODEL)),
                  pl.BlockSpec(w_gate.shape, lambda b, i: (0, 0), pipeline_mode=pl.Buffered(1)),
                  pl.BlockSpec(w_up.shape, lambda b, i: (0, 0), pipeline_mode=pl.Buffered(1)),
                  pl.BlockSpec(w_down.shape, lambda b, i: (0, 0), pipeline_mode=pl.Buffered(1))],
        out_specs=pl.BlockSpec((1, tm, D_MODEL), lambda b, i: (b, i, 0)),
        out_shape=jax.ShapeDtypeStruct((B, L, D_MODEL), F32),
        compiler_params=_params(("parallel", "parallel")),
        name="ffn",
    )(h, mod, n2w, fw, w_gate, w_up, w_down)


def _tail_kernel(of_ref, ob_ref, r_ref, gnw_ref, sb_ref, ga_ref, gb_ref, x_ref, g1_ref, wpa_ref, wpb_ref,
                 wout_ref, mod_ref, n2_ref, fw_ref, wg_ref, wu_ref, wd_ref, o_ref, h_s):
    _merge_kernel(of_ref, ob_ref, r_ref, gnw_ref, sb_ref, ga_ref, gb_ref, x_ref, g1_ref, wpa_ref, wpb_ref,
                  wout_ref, h_s)
    _ffn_kernel(h_s, mod_ref, n2_ref, fw_ref, wg_ref, wu_ref, wd_ref, o_ref)


def _tail(o_f, o_b, gla_norm_w, s_b, pm, x, g1, w_pa, w_pb, w_out, mod, n2w, fw, w_gate, w_up, w_down, tm):
    B, L, _ = x.shape
    gab, gbb, rb = P_GA // D_MODEL, P_GB // D_MODEL, P_R // GLA_V
    one = lambda shape: pl.BlockSpec(shape, lambda b, i: (0,) * len(shape), pipeline_mode=pl.Buffered(1))
    return pl.pallas_call(
        _tail_kernel,
        grid=(B, L // tm),
        in_specs=[pl.BlockSpec((1, tm, GLA_V), lambda b, i: (b, i, 0)),
                  pl.BlockSpec((1, tm, GLA_V), lambda b, i: (b, i, 0)),
                  pl.BlockSpec((1, tm, GLA_V), lambda b, i: (b, i, rb)),
                  _const_spec((1, GLA_DV)),
                  pl.BlockSpec((1, tm, SSM_INNER), lambda b, i: (b, i, 0)),
                  pl.BlockSpec((1, tm, D_MODEL), lambda b, i: (b, i, gab)),
                  pl.BlockSpec((1, tm, D_MODEL), lambda b, i: (b, i, gbb)),
                  pl.BlockSpec((1, tm, D_MODEL), lambda b, i: (b, i, 0)),
                  pl.BlockSpec((1, 1, D_MODEL), lambda b, i: (b, 0, 0)),
                  one(w_pa.shape), one(w_pb.shape), one(w_out.shape),
                  pl.BlockSpec((1, 3, D_MODEL), lambda b, i: (b, 0, 0)),
                  _const_spec((1, D_MODEL)), _const_spec((1, D_MODEL)),
                  one(w_gate.shape), one(w_up.shape), one(w_down.shape)],
        out_specs=pl.BlockSpec((1, tm, D_MODEL), lambda b, i: (b, i, 0)),
        out_shape=jax.ShapeDtypeStruct((B, L, D_MODEL), F32),
        scratch_shapes=[pltpu.VMEM((1, tm, D_MODEL), F32)],
        compiler_params=_params(("parallel", "parallel")),
        name="tail",
    )(o_f, o_b, pm, gla_norm_w, s_b, pm, pm, x, g1, w_pa, w_pb, w_out, mod, n2w, fw, w_gate, w_up, w_down)


def _pick_block(L, pref):
    tb = min(L, pref)
    assert L % tb == 0
    return tb


def kernel(x, c, ctx, c_ctx, w_ada, b_ada, norm1_w, w_in, gla_up_f, gla_bias_f, gla_up_b, gla_bias_b,
           gla_norm_w, conv_w, conv_b, dt_bias_f, dt_bias_b, a_log_f, a_log_b, d_skip, ssm_norm_w,
           w_pa, w_pb, w_out, norm2_w, w_gate, w_up, w_down, final_norm_w):
    B, L, D = x.shape
    Lc = ctx.shape[1]
    depth = w_ada.shape[0]
    assert depth == 1 and D == D_MODEL
    assert L % GRID_W == 0 and L % SSD_CHUNK == 0 and Lc % SSD_CHUNK == 0
    lay = 0

    nrow = -(-(B + 1) // 8) * 8
    cc = jnp.zeros((nrow, D), F32).at[:B].set(c).at[B].set(c_ctx)
    ada = _ada(cc, w_ada, b_ada[lay][None, :], lay)
    sh1, sc1, g1, sh2, sc2, g2 = [ada[:, i * D:(i + 1) * D] for i in range(6)]
    mod1 = jnp.stack([sh1[:B], sc1[:B]], axis=1)
    mod1_c = jnp.broadcast_to(jnp.stack([sh1[B], sc1[B]])[None], (B, 2, D))
    mod2 = jnp.stack([sh2[:B], sc2[:B], g2[:B]], axis=1)
    g1_l = g1[:B, None, :]

    wt = jnp.swapaxes(w_in[lay], 0, 1)
    w_main, w_small = _regroup(wt)
    nw1 = norm1_w[lay][None, :]

    at_lanes = lambda p, off: jnp.zeros((1, P_SMALL), F32).at[0, off:off + SSM_HEADS].set(p[lay])
    dtb_f, al_f = at_lanes(dt_bias_f, S_DTF), at_lanes(a_log_f, S_DTF)
    dtb_b, al_b = at_lanes(dt_bias_b, S_DTB), at_lanes(a_log_b, S_DTB)
    cw, cb_ = conv_w[lay], conv_b[lay][None, :]
    up_f, up_b = gla_up_f[lay], gla_up_b[lay]
    bi_f, bi_b = gla_bias_f[lay][None, :], gla_bias_b[lay][None, :]

    gla_zero = jnp.zeros((B, GLA_HEADS, GLA_DK, GLA_DV), F32)
    ssd_zero = jnp.zeros((B, SSM_GROUPS, SSM_STATE, SSM_GW), F32)

    pm_c, ps_c = _inproj(ctx, mod1_c, nw1, w_main, w_small, _pick_block(Lc, INPROJ_TM))
    sg_f, sg_b = _gla_bidir(pm_c, ps_c, (up_f, up_b), (bi_f, bi_b), (gla_zero, gla_zero),
                            TB=_pick_block(Lc, GLA_TB), need_out=False)
    src_c = ((pm_c, P_XS // SSM_INNER), (pm_c, P_BM // SSM_BC), (pm_c, P_CM // SSM_BC))
    ss_f, = _ssd_pass(src_c, ps_c, cw, cb_, dtb_f, al_f, ssd_zero, reverse=False, finalize=False,
                      TB=Lc, rowlen=Lc, need_out=False)
    ss_b, = _ssd_pass(src_c, ps_c, cw, cb_, dtb_b, al_b, ssd_zero, reverse=True, finalize=False,
                      TB=Lc, rowlen=Lc, need_out=False)

    pm, ps = _inproj(x, mod1, nw1, w_main, w_small, _pick_block(L, INPROJ_TM))
    tbg = _pick_block(L, GLA_TB)
    og_f, _, og_b, _ = _gla_bidir(pm, ps, (up_f, up_b), (bi_f, bi_b), (sg_f, sg_b), TB=tbg)
    tbs = _pick_block(L, SSD_TB)
    src = ((pm, P_XS // SSM_INNER), (pm, P_BM // SSM_BC), (pm, P_CM // SSM_BC))
    ys_f, _, xc, bc, cc = _ssd_pass(src, ps, cw, cb_, dtb_f, al_f, ss_f, reverse=False, finalize=False,
                                    TB=tbs, rowlen=GRID_W, emit=True)
    o_b, _ = _ssd_pass(((xc, 0), (bc, 0), (cc, 0)), ps, None, None, dtb_b, al_b, ss_b, reverse=True,
                       finalize=True, TB=tbs, rowlen=GRID_W, y_prev=ys_f, z_src=(pm, P_Z // SSM_INNER),
                       d_skip_x=jnp.repeat(d_skip[lay], SSM_HEADDIM)[None, :],
                       norm_w=ssm_norm_w[lay][None, :])

    return _tail(og_f, og_b, gla_norm_w[lay][None, :], o_b, pm, x, g1_l, w_pa[lay].astype(BF16),
                 w_pb[lay].astype(BF16), w_out[lay].astype(BF16), mod2, norm2_w[lay][None, :],
                 final_norm_w[None, :], w_gate[lay].astype(BF16), w_up[lay].astype(BF16),
                 w_down[lay].astype(BF16), _pick_block(L, MERGE_TM))
```
